```python
import math
import jax, jax.numpy as jnp
from jax import lax
import numpy as np

D_MODEL = 4096
BATCH = 8
SEQ = 4096
DEPTH = 1

MEM_LEN = 256
RWKV_WIDTH = D_MODEL // 2
RWKV_HEAD = 64
RWKV_HEADS = RWKV_WIDTH // RWKV_HEAD
RWKV_DECAY_LORA = 96
RWKV_A_LORA = 96
RWKV_GATE_LORA = 256
RWKV_GN_EPS = 64e-5
RWKV_COLS = 3 * RWKV_WIDTH + RWKV_DECAY_LORA + RWKV_A_LORA + RWKV_GATE_LORA
GDN_WIDTH = D_MODEL // 2
GDN_HEAD = 128
GDN_HEADS = GDN_WIDTH // GDN_HEAD
GDN_CONV = 4
GDN_CHUNK = 64
GDN_COLS = 4 * GDN_WIDTH + 2 * GDN_HEADS
N_IN = RWKV_COLS + GDN_COLS + 2 * D_MODEL
XA_HEADS = 4
XA_HEAD = 128
XA_WIDTH = XA_HEADS * XA_HEAD
D_FF = 4 * D_MODEL
NORM_EPS = 1e-6
L2_EPS = 1e-6

kernel_name = 'hybrid_rwkv7_gdn_memxattn_layer'


def rms_norm(x, gain, eps=NORM_EPS):
    xf = x.astype(jnp.float32)
    y = xf * lax.rsqrt(jnp.mean(xf * xf, axis=-1, keepdims=True) + eps)
    return (y * gain.astype(jnp.float32)).astype(x.dtype)


def l2_normalize(x, eps=L2_EPS):
    xf = x.astype(jnp.float32)
    return xf * lax.rsqrt(jnp.sum(xf * xf, axis=-1, keepdims=True) + eps)


def causal_depthwise_conv(x, w):
    return lax.conv_general_dilated(
        x, w[:, None, :].astype(x.dtype), window_strides=(1,),
        padding=[(w.shape[0] - 1, 0)], dimension_numbers=('NWC', 'WIO', 'NWC'),
        feature_group_count=x.shape[-1])


def rwkv7_recurrence(r, log_w, k, v, a, b):
    def step(S, inp):
        r_t, lw_t, k_t, v_t, a_t, b_t = inp
        sa = jnp.einsum('bhvk,bhk->bhv', S, a_t)
        S = (S * jnp.exp(lw_t)[:, :, None, :] + sa[..., :, None] * b_t[..., None, :]
             + v_t[..., :, None] * k_t[..., None, :])
        return S, jnp.einsum('bhvk,bhk->bhv', S, r_t)
    B, T, H, N = r.shape
    xs = tuple(jnp.moveaxis(t, 1, 0) for t in (r, log_w, k, v, a, b))
    _, y = lax.scan(step, jnp.zeros((B, H, N, N), jnp.float32), xs)
    return jnp.moveaxis(y, 0, 1)


def rwkv7_time_mix(p_rw, shift_mix, w0, w_up, a0, a_up, g_up, k_k, k_a, r_k, gn_w, gn_b):
    B, T, _ = p_rw.shape
    dt = p_rw.dtype
    prev = jnp.pad(p_rw, ((0, 0), (1, 0), (0, 0)))[:, :-1]
    xm = p_rw + (prev - p_rw) * shift_mix
    o1 = RWKV_WIDTH
    o2 = o1 + RWKV_WIDTH
    o3 = o2 + RWKV_WIDTH
    o4 = o3 + RWKV_DECAY_LORA
    o5 = o4 + RWKV_A_LORA
    r, k, v, lw, la, lg = jnp.split(xm, [o1, o2, o3, o4, o5], axis=-1)
    w_raw = -jax.nn.softplus(-(w0 + jnp.tanh(lw) @ w_up)) - 0.5
    log_decay = -jnp.exp(w_raw.astype(jnp.float32))
    a = jax.nn.sigmoid(a0 + la @ a_up)
    gate = jax.nn.sigmoid(lg) @ g_up
    heads = lambda t: t.reshape(B, T, RWKV_HEADS, RWKV_HEAD)
    kk = l2_normalize(heads(k * k_k))
    k = k * (1.0 + (a - 1.0) * k_a)
    r_h, k_h, v_h, a_h = heads(r), heads(k), heads(v), heads(a).astype(jnp.float32)
    f32 = lambda t: t.astype(jnp.float32)
    y = rwkv7_recurrence(f32(r_h), heads(log_decay), f32(k_h), f32(v_h), -kk, kk * a_h)
    mean = jnp.mean(y, axis=-1, keepdims=True)
    var = jnp.mean(jnp.square(y - mean), axis=-1, keepdims=True)
    y = ((y - mean) * lax.rsqrt(var + RWKV_GN_EPS)).reshape(B, T, RWKV_WIDTH)
    y = y * gn_w.astype(jnp.float32) + gn_b.astype(jnp.float32)
    bonus = jnp.sum(r_h * k_h * r_k, axis=-1, keepdims=True) * v_h
    y = y + bonus.reshape(B, T, RWKV_WIDTH).astype(jnp.float32)
    return y.astype(dt) * gate


def gated_delta_chunked(q, k, v, g, beta):
    B, T, H, K = q.shape
    V = v.shape[-1]
    C = GDN_CHUNK
    n = T // C

    def chunks(t):
        return jnp.moveaxis(t.reshape((B, n, C, H) + t.shape[3:]), 3, 1)

    q = chunks(q) * (K ** -0.5)
    k, v, g, beta = chunks(k), chunks(v), chunks(g), chunks(beta)
    gc = jnp.cumsum(g, axis=-1)
    idx = jnp.arange(C)
    causal = idx[:, None] >= idx[None, :]
    strict = idx[:, None] > idx[None, :]
    decay = jnp.where(causal, jnp.exp(jnp.where(causal, gc[..., :, None] - gc[..., None, :], 0.0)), 0.0)
    kb = k * beta[..., None]
    L = jnp.where(strict, jnp.einsum('bhnik,bhnjk->bhnij', kb, k) * decay, 0.0)
    eye = jnp.eye(C, dtype=q.dtype)
    tinv = lax.linalg.triangular_solve(L + eye, jnp.broadcast_to(eye, L.shape),
                                       left_side=True, lower=True, unit_diagonal=True)
    u = jnp.einsum('bhnij,bhnjv->bhniv', tinv, v * beta[..., None])
    w = jnp.einsum('bhnij,bhnjk->bhnik', tinv, kb * jnp.exp(gc)[..., None])
    a_intra = jnp.where(causal, jnp.einsum('bhnik,bhnjk->bhnij', q, k) * decay, 0.0)
    q_dec = q * jnp.exp(gc)[..., None]
    k_dec = k * jnp.exp(gc[..., -1:] - gc)[..., None]
    g_last = jnp.exp(gc[..., -1])

    def step(S, inp):
        qd, kd, u_i, w_i, a_i, gl = inp
        v_new = u_i - jnp.einsum('bhck,bhkv->bhcv', w_i, S)
        o = jnp.einsum('bhck,bhkv->bhcv', qd, S) + jnp.einsum('bhcj,bhjv->bhcv', a_i, v_new)
        S = S * gl[..., None, None] + jnp.einsum('bhck,bhcv->bhkv', kd, v_new)
        return S, o

    xs = tuple(jnp.moveaxis(t, 2, 0) for t in (q_dec, k_dec, u, w, a_intra, g_last))
    _, o = lax.scan(step, jnp.zeros((B, H, K, V), jnp.float32), xs)
    o = jnp.moveaxis(o, 0, 2)
    return jnp.moveaxis(o, 1, 3).reshape(B, T, H, V)


def gated_deltanet_mix(p_qkv, p_z, p_beta, p_alpha, conv_w, a_log, dt_bias, norm_w):
    B, T, _ = p_qkv.shape
    dt = p_qkv.dtype
    qkv = jax.nn.silu(causal_depthwise_conv(p_qkv, conv_w))
    q, k, v = jnp.split(qkv, 3, axis=-1)
    heads = lambda t: t.reshape(B, T, GDN_HEADS, GDN_HEAD)
    q = l2_normalize(heads(q))
    k = l2_normalize(heads(k))
    v = heads(v).astype(jnp.float32)
    beta = jax.nn.sigmoid(p_beta.astype(jnp.float32))
    g = -jnp.exp(a_log.astype(jnp.float32)) * jax.nn.softplus(
        p_alpha.astype(jnp.float32) + dt_bias.astype(jnp.float32))
    o = gated_delta_chunked(q, k, v, g, beta)
    o = o * lax.rsqrt(jnp.mean(o * o, axis=-1, keepdims=True) + NORM_EPS) * norm_w.astype(jnp.float32)
    o = o * jax.nn.silu(heads(p_z).astype(jnp.float32))
    return o.reshape(B, T, GDN_WIDTH).astype(dt)


def hybrid_mixer(u, w_in, shift_mix, w0, w_up, a0, a_up, g_up, k_k, k_a, r_k, gn_w, gn_b,
                 conv_w, a_log, dt_bias, gdn_norm_w, w_br_rwkv, w_br_gdn, w_out):
    p = u @ w_in
    o1 = RWKV_COLS
    o2 = o1 + 3 * GDN_WIDTH
    o3 = o2 + GDN_WIDTH
    o4 = o3 + GDN_HEADS
    o5 = o4 + GDN_HEADS
    o6 = o5 + D_MODEL
    p_rw, p_qkv, p_z, p_beta, p_alpha, p_gate_rw, p_gate_gdn = jnp.split(
        p, [o1, o2, o3, o4, o5, o6], axis=-1)
    y_rw = rwkv7_time_mix(p_rw, shift_mix, w0, w_up, a0, a_up, g_up, k_k, k_a, r_k, gn_w, gn_b)
    y_gdn = gated_deltanet_mix(p_qkv, p_z, p_beta, p_alpha, conv_w, a_log, dt_bias, gdn_norm_w)
    merged = (jax.nn.sigmoid(p_gate_rw) * (y_rw @ w_br_rwkv)
              + jax.nn.sigmoid(p_gate_gdn) * (y_gdn @ w_br_gdn))
    return merged @ w_out


def memory_cross_attention(c, m, w_q, w_kv, w_o):
    B, T, _ = c.shape
    M = m.shape[1]
    q = (c @ w_q).reshape(B, T, XA_HEADS, XA_HEAD)
    k, v = jnp.split(m @ w_kv, 2, axis=-1)
    k = k.reshape(B, M, XA_HEADS, XA_HEAD)
    v = v.reshape(B, M, XA_HEADS, XA_HEAD)
    s = jnp.einsum('bthd,bmhd->bhtm', q, k).astype(jnp.float32) * (XA_HEAD ** -0.5)
    pr = jax.nn.softmax(s, axis=-1).astype(v.dtype)
    o = jnp.einsum('bhtm,bmhd->bthd', pr, v).reshape(B, T, XA_WIDTH)
    return o @ w_o


def squared_relu_mlp(f, w_up, w_down):
    return jnp.square(jax.nn.relu(f @ w_up)) @ w_down


def _fwd_setup_inputs(seed: int = 0) -> dict:
    key = jax.random.key(seed)
    ks = iter(jax.random.split(key, 48))
    Ld = DEPTH

    def nrm(shape, scale):
        return scale * jax.random.normal(next(ks), shape, jnp.float32)

    def unif(shape, lo, hi):
        return jax.random.uniform(next(ks), shape, jnp.float32, lo, hi)

    def gain(width):
        return 1.0 + nrm((Ld, width), 0.02)

    dt = jnp.exp(unif((Ld, GDN_HEADS), math.log(1e-3), math.log(1e-1)))
    dt_bias = dt + jnp.log(-jnp.expm1(-dt))
    return {
        'x': nrm((BATCH, SEQ, D_MODEL), 1.0),
        'mem': nrm((BATCH, MEM_LEN, D_MODEL), 1.0),
        'mix_norm_pre': gain(D_MODEL),
        'mix_norm_post': gain(D_MODEL),
        'w_in': nrm((Ld, D_MODEL, N_IN), D_MODEL ** -0.5),
        'rwkv_shift_mix': unif((Ld, RWKV_COLS), 0.0, 1.0),
        'rwkv_w0': unif((Ld, RWKV_WIDTH), -6.0, -1.0),
        'rwkv_w_up': nrm((Ld, RWKV_DECAY_LORA, RWKV_WIDTH), 0.5 * RWKV_DECAY_LORA ** -0.5),
        'rwkv_a0': nrm((Ld, RWKV_WIDTH), 0.1),
        'rwkv_a_up': nrm((Ld, RWKV_A_LORA, RWKV_WIDTH), RWKV_A_LORA ** -0.5),
        'rwkv_g_up': nrm((Ld, RWKV_GATE_LORA, RWKV_WIDTH), RWKV_GATE_LORA ** -0.5),
        'rwkv_k_k': 0.85 + nrm((Ld, RWKV_WIDTH), 0.02),
        'rwkv_k_a': 1.0 + nrm((Ld, RWKV_WIDTH), 0.02),
        'rwkv_r_k': -0.04 + nrm((Ld, RWKV_HEADS, RWKV_HEAD), 0.05),
        'rwkv_gn_w': gain(RWKV_WIDTH),
        'rwkv_gn_b': nrm((Ld, RWKV_WIDTH), 0.02),
        'gdn_conv_w': nrm((Ld, GDN_CONV, 3 * GDN_WIDTH), GDN_CONV ** -0.5),
        'gdn_a_log': jnp.log(unif((Ld, GDN_HEADS), 1.0, 16.0)),
        'gdn_dt_bias': dt_bias,
        'gdn_norm_w': gain(GDN_HEAD),
        'w_branch_rwkv': nrm((Ld, RWKV_WIDTH, D_MODEL), RWKV_WIDTH ** -0.5),
        'w_branch_gdn': nrm((Ld, GDN_WIDTH, D_MODEL), GDN_WIDTH ** -0.5),
        'w_mix_out': nrm((Ld, D_MODEL, D_MODEL), D_MODEL ** -0.5),
        'xa_norm_pre': gain(D_MODEL),
        'xa_norm_mem': gain(D_MODEL),
        'xa_norm_post': gain(D_MODEL),
        'xa_w_q': nrm((Ld, D_MODEL, XA_WIDTH), D_MODEL ** -0.5),
        'xa_w_kv': nrm((Ld, D_MODEL, 2 * XA_WIDTH), D_MODEL ** -0.5),
        'xa_w_o': nrm((Ld, XA_WIDTH, D_MODEL), XA_WIDTH ** -0.5),
        'mlp_norm_pre': gain(D_MODEL),
        'mlp_norm_post': gain(D_MODEL),
        'mlp_w_up': nrm((Ld, D_MODEL, D_FF), D_MODEL ** -0.5),
        'mlp_w_down': nrm((Ld, D_FF, D_MODEL), D_FF ** -0.5),
    }


def _fwd_reference(x, mem, mix_norm_pre, mix_norm_post, w_in,
              rwkv_shift_mix, rwkv_w0, rwkv_w_up, rwkv_a0, rwkv_a_up, rwkv_g_up,
              rwkv_k_k, rwkv_k_a, rwkv_r_k, rwkv_gn_w, rwkv_gn_b,
              gdn_conv_w, gdn_a_log, gdn_dt_bias, gdn_norm_w,
              w_branch_rwkv, w_branch_gdn, w_mix_out,
              xa_norm_pre, xa_norm_mem, xa_norm_post, xa_w_q, xa_w_kv, xa_w_o,
              mlp_norm_pre, mlp_norm_post, mlp_w_up, mlp_w_down):
    h = x
    for l in range(DEPTH):
        u = rms_norm(h, mix_norm_pre[l])
        y = hybrid_mixer(u, w_in[l], rwkv_shift_mix[l], rwkv_w0[l], rwkv_w_up[l], rwkv_a0[l],
                         rwkv_a_up[l], rwkv_g_up[l], rwkv_k_k[l], rwkv_k_a[l], rwkv_r_k[l],
                         rwkv_gn_w[l], rwkv_gn_b[l], gdn_conv_w[l], gdn_a_log[l], gdn_dt_bias[l],
                         gdn_norm_w[l], w_branch_rwkv[l], w_branch_gdn[l], w_mix_out[l])
        h = h + rms_norm(y, mix_norm_post[l])
        c = rms_norm(h, xa_norm_pre[l])
        m = rms_norm(mem, xa_norm_mem[l])
        h = h + rms_norm(memory_cross_attention(c, m, xa_w_q[l], xa_w_kv[l], xa_w_o[l]), xa_norm_post[l])
        f = rms_norm(h, mlp_norm_pre[l])
        h = h + rms_norm(squared_relu_mlp(f, mlp_w_up[l], mlp_w_down[l]), mlp_norm_post[l])
    return h


import jax as _jax
import jax.numpy as _jnp

TWIN_FORMAT = 'train_step'
FWD_PARAMS = ['x', 'mem', 'mix_norm_pre', 'mix_norm_post', 'w_in', 'rwkv_shift_mix', 'rwkv_w0', 'rwkv_w_up', 'rwkv_a0', 'rwkv_a_up', 'rwkv_g_up', 'rwkv_k_k', 'rwkv_k_a', 'rwkv_r_k', 'rwkv_gn_w', 'rwkv_gn_b', 'gdn_conv_w', 'gdn_a_log', 'gdn_dt_bias', 'gdn_norm_w', 'w_branch_rwkv', 'w_branch_gdn', 'w_mix_out', 'xa_norm_pre', 'xa_norm_mem', 'xa_norm_post', 'xa_w_q', 'xa_w_kv', 'xa_w_o', 'mlp_norm_pre', 'mlp_norm_post', 'mlp_w_up', 'mlp_w_down']
TWIN_WEIGHTS = ['mix_norm_pre', 'mix_norm_post', 'w_in', 'rwkv_shift_mix', 'rwkv_w0', 'rwkv_w_up', 'rwkv_a0', 'rwkv_a_up', 'rwkv_g_up', 'rwkv_k_k', 'rwkv_k_a', 'rwkv_r_k', 'rwkv_gn_w', 'rwkv_gn_b', 'gdn_conv_w', 'gdn_a_log', 'gdn_dt_bias', 'gdn_norm_w', 'w_branch_rwkv', 'w_branch_gdn', 'w_mix_out', 'xa_norm_pre', 'xa_norm_mem', 'xa_norm_post', 'xa_w_q', 'xa_w_kv', 'xa_w_o', 'mlp_norm_pre', 'mlp_norm_post', 'mlp_w_up', 'mlp_w_down']
TWIN_DIFF_INPUT = 'x'
TWIN_INPUTS = ['x', 'mem', 'mix_norm_pre', 'mix_norm_post', 'w_in', 'rwkv_shift_mix', 'rwkv_w0', 'rwkv_w_up', 'rwkv_a0', 'rwkv_a_up', 'rwkv_g_up', 'rwkv_k_k', 'rwkv_k_a', 'rwkv_r_k', 'rwkv_gn_w', 'rwkv_gn_b', 'gdn_conv_w', 'gdn_a_log', 'gdn_dt_bias', 'gdn_norm_w', 'w_branch_rwkv', 'w_branch_gdn', 'w_mix_out', 'xa_norm_pre', 'xa_norm_mem', 'xa_norm_post', 'xa_w_q', 'xa_w_kv', 'xa_w_o', 'mlp_norm_pre', 'mlp_norm_post', 'mlp_w_up', 'mlp_w_down', 'loss_target', 'm_mix_norm_pre', 'm_mix_norm_post', 'm_w_in', 'm_rwkv_shift_mix', 'm_rwkv_w0', 'm_rwkv_w_up', 'm_rwkv_a0', 'm_rwkv_a_up', 'm_rwkv_g_up', 'm_rwkv_k_k', 'm_rwkv_k_a', 'm_rwkv_r_k', 'm_rwkv_gn_w', 'm_rwkv_gn_b', 'm_gdn_conv_w', 'm_gdn_a_log', 'm_gdn_dt_bias', 'm_gdn_norm_w', 'm_w_branch_rwkv', 'm_w_branch_gdn', 'm_w_mix_out', 'm_xa_norm_pre', 'm_xa_norm_mem', 'm_xa_norm_post', 'm_xa_w_q', 'm_xa_w_kv', 'm_xa_w_o', 'm_mlp_norm_pre', 'm_mlp_norm_post', 'm_mlp_w_up', 'm_mlp_w_down', 'v_mix_norm_pre', 'v_mix_norm_post', 'v_w_in', 'v_rwkv_shift_mix', 'v_rwkv_w0', 'v_rwkv_w_up', 'v_rwkv_a0', 'v_rwkv_a_up', 'v_rwkv_g_up', 'v_rwkv_k_k', 'v_rwkv_k_a', 'v_rwkv_r_k', 'v_rwkv_gn_w', 'v_rwkv_gn_b', 'v_gdn_conv_w', 'v_gdn_a_log', 'v_gdn_dt_bias', 'v_gdn_norm_w', 'v_w_branch_rwkv', 'v_w_branch_gdn', 'v_w_mix_out', 'v_xa_norm_pre', 'v_xa_norm_mem', 'v_xa_norm_post', 'v_xa_w_q', 'v_xa_w_kv', 'v_xa_w_o', 'v_mlp_norm_pre', 'v_mlp_norm_post', 'v_mlp_w_up', 'v_mlp_w_down']
TWIN_OUTPUTS = ['loss', 'grad_x', 'grad_mix_norm_pre', 'grad_mix_norm_post', 'grad_w_in', 'grad_rwkv_shift_mix', 'grad_rwkv_w0', 'grad_rwkv_w_up', 'grad_rwkv_a0', 'grad_rwkv_a_up', 'grad_rwkv_g_up', 'grad_rwkv_k_k', 'grad_rwkv_k_a', 'grad_rwkv_r_k', 'grad_rwkv_gn_w', 'grad_rwkv_gn_b', 'grad_gdn_conv_w', 'grad_gdn_a_log', 'grad_gdn_dt_bias', 'grad_gdn_norm_w', 'grad_w_branch_rwkv', 'grad_w_branch_gdn', 'grad_w_mix_out', 'grad_xa_norm_pre', 'grad_xa_norm_mem', 'grad_xa_norm_post', 'grad_xa_w_q', 'grad_xa_w_kv', 'grad_xa_w_o', 'grad_mlp_norm_pre', 'grad_mlp_norm_post', 'grad_mlp_w_up', 'grad_mlp_w_down', 'delta_mix_norm_pre', 'delta_mix_norm_post', 'delta_w_in', 'delta_rwkv_shift_mix', 'delta_rwkv_w0', 'delta_rwkv_w_up', 'delta_rwkv_a0', 'delta_rwkv_a_up', 'delta_rwkv_g_up', 'delta_rwkv_k_k', 'delta_rwkv_k_a', 'delta_rwkv_r_k', 'delta_rwkv_gn_w', 'delta_rwkv_gn_b', 'delta_gdn_conv_w', 'delta_gdn_a_log', 'delta_gdn_dt_bias', 'delta_gdn_norm_w', 'delta_w_branch_rwkv', 'delta_w_branch_gdn', 'delta_w_mix_out', 'delta_xa_norm_pre', 'delta_xa_norm_mem', 'delta_xa_norm_post', 'delta_xa_w_q', 'delta_xa_w_kv', 'delta_xa_w_o', 'delta_mlp_norm_pre', 'delta_mlp_norm_post', 'delta_mlp_w_up', 'delta_mlp_w_down', 'new_m_mix_norm_pre', 'new_m_mix_norm_post', 'new_m_w_in', 'new_m_rwkv_shift_mix', 'new_m_rwkv_w0', 'new_m_rwkv_w_up', 'new_m_rwkv_a0', 'new_m_rwkv_a_up', 'new_m_rwkv_g_up', 'new_m_rwkv_k_k', 'new_m_rwkv_k_a', 'new_m_rwkv_r_k', 'new_m_rwkv_gn_w', 'new_m_rwkv_gn_b', 'new_m_gdn_conv_w', 'new_m_gdn_a_log', 'new_m_gdn_dt_bias', 'new_m_gdn_norm_w', 'new_m_w_branch_rwkv', 'new_m_w_branch_gdn', 'new_m_w_mix_out', 'new_m_xa_norm_pre', 'new_m_xa_norm_mem', 'new_m_xa_norm_post', 'new_m_xa_w_q', 'new_m_xa_w_kv', 'new_m_xa_w_o', 'new_m_mlp_norm_pre', 'new_m_mlp_norm_post', 'new_m_mlp_w_up', 'new_m_mlp_w_down', 'new_v_mix_norm_pre', 'new_v_mix_norm_post', 'new_v_w_in', 'new_v_rwkv_shift_mix', 'new_v_rwkv_w0', 'new_v_rwkv_w_up', 'new_v_rwkv_a0', 'new_v_rwkv_a_up', 'new_v_rwkv_g_up', 'new_v_rwkv_k_k', 'new_v_rwkv_k_a', 'new_v_rwkv_r_k', 'new_v_rwkv_gn_w', 'new_v_rwkv_gn_b', 'new_v_gdn_conv_w', 'new_v_gdn_a_log', 'new_v_gdn_dt_bias', 'new_v_gdn_norm_w', 'new_v_w_branch_rwkv', 'new_v_w_branch_gdn', 'new_v_w_mix_out', 'new_v_xa_norm_pre', 'new_v_xa_norm_mem', 'new_v_xa_norm_post', 'new_v_xa_w_q', 'new_v_xa_w_kv', 'new_v_xa_w_o', 'new_v_mlp_norm_pre', 'new_v_mlp_norm_post', 'new_v_mlp_w_up', 'new_v_mlp_w_down']
TWIN_LEAF_KINDS = {'loss': 'loss', 'grad_x': 'grad_x', 'grad_mix_norm_pre': 'grad_w', 'grad_mix_norm_post': 'grad_w', 'grad_w_in': 'grad_w', 'grad_rwkv_shift_mix': 'grad_w', 'grad_rwkv_w0': 'grad_w', 'grad_rwkv_w_up': 'grad_w', 'grad_rwkv_a0': 'grad_w', 'grad_rwkv_a_up': 'grad_w', 'grad_rwkv_g_up': 'grad_w', 'grad_rwkv_k_k': 'grad_w', 'grad_rwkv_k_a': 'grad_w', 'grad_rwkv_r_k': 'grad_w', 'grad_rwkv_gn_w': 'grad_w', 'grad_rwkv_gn_b': 'grad_w', 'grad_gdn_conv_w': 'grad_w', 'grad_gdn_a_log': 'grad_w', 'grad_gdn_dt_bias': 'grad_w', 'grad_gdn_norm_w': 'grad_w', 'grad_w_branch_rwkv': 'grad_w', 'grad_w_branch_gdn': 'grad_w', 'grad_w_mix_out': 'grad_w', 'grad_xa_norm_pre': 'grad_w', 'grad_xa_norm_mem': 'grad_w', 'grad_xa_norm_post': 'grad_w', 'grad_xa_w_q': 'grad_w', 'grad_xa_w_kv': 'grad_w', 'grad_xa_w_o': 'grad_w', 'grad_mlp_norm_pre': 'grad_w', 'grad_mlp_norm_post': 'grad_w', 'grad_mlp_w_up': 'grad_w', 'grad_mlp_w_down': 'grad_w', 'delta_mix_norm_pre': 'delta_w', 'delta_mix_norm_post': 'delta_w', 'delta_w_in': 'delta_w', 'delta_rwkv_shift_mix': 'delta_w', 'delta_rwkv_w0': 'delta_w', 'delta_rwkv_w_up': 'delta_w', 'delta_rwkv_a0': 'delta_w', 'delta_rwkv_a_up': 'delta_w', 'delta_rwkv_g_up': 'delta_w', 'delta_rwkv_k_k': 'delta_w', 'delta_rwkv_k_a': 'delta_w', 'delta_rwkv_r_k': 'delta_w', 'delta_rwkv_gn_w': 'delta_w', 'delta_rwkv_gn_b': 'delta_w', 'delta_gdn_conv_w': 'delta_w', 'delta_gdn_a_log': 'delta_w', 'delta_gdn_dt_bias': 'delta_w', 'delta_gdn_norm_w': 'delta_w', 'delta_w_branch_rwkv': 'delta_w', 'delta_w_branch_gdn': 'delta_w', 'delta_w_mix_out': 'delta_w', 'delta_xa_norm_pre': 'delta_w', 'delta_xa_norm_mem': 'delta_w', 'delta_xa_norm_post': 'delta_w', 'delta_xa_w_q': 'delta_w', 'delta_xa_w_kv': 'delta_w', 'delta_xa_w_o': 'delta_w', 'delta_mlp_norm_pre': 'delta_w', 'delta_mlp_norm_post': 'delta_w', 'delta_mlp_w_up': 'delta_w', 'delta_mlp_w_down': 'delta_w', 'new_m_mix_norm_pre': 'new_m', 'new_m_mix_norm_post': 'new_m', 'new_m_w_in': 'new_m', 'new_m_rwkv_shift_mix': 'new_m', 'new_m_rwkv_w0': 'new_m', 'new_m_rwkv_w_up': 'new_m', 'new_m_rwkv_a0': 'new_m', 'new_m_rwkv_a_up': 'new_m', 'new_m_rwkv_g_up': 'new_m', 'new_m_rwkv_k_k': 'new_m', 'new_m_rwkv_k_a': 'new_m', 'new_m_rwkv_r_k': 'new_m', 'new_m_rwkv_gn_w': 'new_m', 'new_m_rwkv_gn_b': 'new_m', 'new_m_gdn_conv_w': 'new_m', 'new_m_gdn_a_log': 'new_m', 'new_m_gdn_dt_bias': 'new_m', 'new_m_gdn_norm_w': 'new_m', 'new_m_w_branch_rwkv': 'new_m', 'new_m_w_branch_gdn': 'new_m', 'new_m_w_mix_out': 'new_m', 'new_m_xa_norm_pre': 'new_m', 'new_m_xa_norm_mem': 'new_m', 'new_m_xa_norm_post': 'new_m', 'new_m_xa_w_q': 'new_m', 'new_m_xa_w_kv': 'new_m', 'new_m_xa_w_o': 'new_m', 'new_m_mlp_norm_pre': 'new_m', 'new_m_mlp_norm_post': 'new_m', 'new_m_mlp_w_up': 'new_m', 'new_m_mlp_w_down': 'new_m', 'new_v_mix_norm_pre': 'new_v', 'new_v_mix_norm_post': 'new_v', 'new_v_w_in': 'new_v', 'new_v_rwkv_shift_mix': 'new_v', 'new_v_rwkv_w0': 'new_v', 'new_v_rwkv_w_up': 'new_v', 'new_v_rwkv_a0': 'new_v', 'new_v_rwkv_a_up': 'new_v', 'new_v_rwkv_g_up': 'new_v', 'new_v_rwkv_k_k': 'new_v', 'new_v_rwkv_k_a': 'new_v', 'new_v_rwkv_r_k': 'new_v', 'new_v_rwkv_gn_w': 'new_v', 'new_v_rwkv_gn_b': 'new_v', 'new_v_gdn_conv_w': 'new_v', 'new_v_gdn_a_log': 'new_v', 'new_v_gdn_dt_bias': 'new_v', 'new_v_gdn_norm_w': 'new_v', 'new_v_w_branch_rwkv': 'new_v', 'new_v_w_branch_gdn': 'new_v', 'new_v_w_mix_out': 'new_v', 'new_v_xa_norm_pre': 'new_v', 'new_v_xa_norm_mem': 'new_v', 'new_v_xa_norm_post': 'new_v', 'new_v_xa_w_q': 'new_v', 'new_v_xa_w_kv': 'new_v', 'new_v_xa_w_o': 'new_v', 'new_v_mlp_norm_pre': 'new_v', 'new_v_mlp_norm_post': 'new_v', 'new_v_mlp_w_up': 'new_v', 'new_v_mlp_w_down': 'new_v'}


def _forward(args):
    return _fwd_reference(*[args[k] for k in FWD_PARAMS])


def _output_shape():
    out = _jax.eval_shape(lambda: _forward(_fwd_setup_inputs(0)))
    return out.shape, out.dtype

N_MICROBATCH = 1
ADAM_LR = 0.001
ADAM_B1 = 0.9
ADAM_B2 = 0.999
ADAM_EPS = 1e-08
ADAM_WD = 0.01
ADAM_STEP = 10
PER_EXAMPLE_BATCH_AXIS = {'x': 0, 'mem': 0, 'loss_target': 0}
SHARED_INPUTS = []
_WEIGHT_DTYPES = {'mix_norm_pre': _jnp.float32, 'mix_norm_post': _jnp.float32, 'w_in': _jnp.float32, 'rwkv_shift_mix': _jnp.float32, 'rwkv_w0': _jnp.float32, 'rwkv_w_up': _jnp.float32, 'rwkv_a0': _jnp.float32, 'rwkv_a_up': _jnp.float32, 'rwkv_g_up': _jnp.float32, 'rwkv_k_k': _jnp.float32, 'rwkv_k_a': _jnp.float32, 'rwkv_r_k': _jnp.float32, 'rwkv_gn_w': _jnp.float32, 'rwkv_gn_b': _jnp.float32, 'gdn_conv_w': _jnp.float32, 'gdn_a_log': _jnp.float32, 'gdn_dt_bias': _jnp.float32, 'gdn_norm_w': _jnp.float32, 'w_branch_rwkv': _jnp.float32, 'w_branch_gdn': _jnp.float32, 'w_mix_out': _jnp.float32, 'xa_norm_pre': _jnp.float32, 'xa_norm_mem': _jnp.float32, 'xa_norm_post': _jnp.float32, 'xa_w_q': _jnp.float32, 'xa_w_kv': _jnp.float32, 'xa_w_o': _jnp.float32, 'mlp_norm_pre': _jnp.float32, 'mlp_norm_post': _jnp.float32, 'mlp_w_up': _jnp.float32, 'mlp_w_down': _jnp.float32}
MOMENT_SCALE = {'mix_norm_pre': 2.462567e-01, 'mix_norm_post': 8.003413e+00, 'w_in': 1.043329e-01, 'rwkv_shift_mix': 2.189603e-01, 'rwkv_w0': 5.370407e-02, 'rwkv_w_up': 6.165176e-03, 'rwkv_a0': 5.117416e-02, 'rwkv_a_up': 4.594948e-02, 'rwkv_g_up': 1.398005e-01, 'rwkv_k_k': 1.662903e-01, 'rwkv_k_a': 1.436250e-01, 'rwkv_r_k': 2.912628e-01, 'rwkv_gn_w': 1.468050e-01, 'rwkv_gn_b': 3.670646e+00, 'gdn_conv_w': 2.709648e-01, 'gdn_a_log': 9.177557e-01, 'gdn_dt_bias': 9.136937e-01, 'gdn_norm_w': 3.387025e+00, 'w_branch_rwkv': 1.026006e-01, 'w_branch_gdn': 5.421852e-01, 'w_mix_out': 5.037405e-01, 'xa_norm_pre': 1.328281e-01, 'xa_norm_mem': 1.199225e+00, 'xa_norm_post': 8.364702e+00, 'xa_w_q': 3.954293e-01, 'xa_w_kv': 2.483764e+00, 'xa_w_o': 1.344992e+00, 'mlp_norm_pre': 5.827691e-01, 'mlp_norm_post': 8.301833e+00, 'mlp_w_up': 2.878619e-01, 'mlp_w_down': 1.279091e+00}


def _to_microbatches(a, axis):
    t = _jnp.moveaxis(a, axis, 0)
    t = t.reshape((N_MICROBATCH, t.shape[0] // N_MICROBATCH) + t.shape[1:])
    return _jnp.moveaxis(t, 1, axis + 1)


def setup_inputs(seed: int = 0) -> dict:
    inp = _fwd_setup_inputs(seed)
    key = _jax.random.fold_in(_jax.random.key(seed), 7919)
    shape, _ = _output_shape()
    out = dict(inp)
    out["loss_target"] = _jax.random.normal(_jax.random.fold_in(key, 0), shape, _jnp.float32)
    for i, name in enumerate(TWIN_WEIGHTS):
        w = inp[name].astype(_jnp.float32)
        if MOMENT_SCALE is None:
            s = _jnp.sqrt(_jnp.mean(_jnp.square(w)) + 1e-30)
        else:
            s = MOMENT_SCALE[name]
        km, kv = _jax.random.split(_jax.random.fold_in(key, i + 1))
        out[name] = w
        out["m_" + name] = s * _jax.random.normal(km, w.shape, _jnp.float32)
        out["v_" + name] = (s * s) * _jax.random.uniform(kv, w.shape, _jnp.float32, 0.5, 1.5)
    if N_MICROBATCH > 1:
        for name, axis in PER_EXAMPLE_BATCH_AXIS.items():
            out[name] = _to_microbatches(out[name], axis)
    return {'x': out['x'], 'mem': out['mem'], 'mix_norm_pre': out['mix_norm_pre'], 'mix_norm_post': out['mix_norm_post'], 'w_in': out['w_in'], 'rwkv_shift_mix': out['rwkv_shift_mix'], 'rwkv_w0': out['rwkv_w0'], 'rwkv_w_up': out['rwkv_w_up'], 'rwkv_a0': out['rwkv_a0'], 'rwkv_a_up': out['rwkv_a_up'], 'rwkv_g_up': out['rwkv_g_up'], 'rwkv_k_k': out['rwkv_k_k'], 'rwkv_k_a': out['rwkv_k_a'], 'rwkv_r_k': out['rwkv_r_k'], 'rwkv_gn_w': out['rwkv_gn_w'], 'rwkv_gn_b': out['rwkv_gn_b'], 'gdn_conv_w': out['gdn_conv_w'], 'gdn_a_log': out['gdn_a_log'], 'gdn_dt_bias': out['gdn_dt_bias'], 'gdn_norm_w': out['gdn_norm_w'], 'w_branch_rwkv': out['w_branch_rwkv'], 'w_branch_gdn': out['w_branch_gdn'], 'w_mix_out': out['w_mix_out'], 'xa_norm_pre': out['xa_norm_pre'], 'xa_norm_mem': out['xa_norm_mem'], 'xa_norm_post': out['xa_norm_post'], 'xa_w_q': out['xa_w_q'], 'xa_w_kv': out['xa_w_kv'], 'xa_w_o': out['xa_w_o'], 'mlp_norm_pre': out['mlp_norm_pre'], 'mlp_norm_post': out['mlp_norm_post'], 'mlp_w_up': out['mlp_w_up'], 'mlp_w_down': out['mlp_w_down'], 'loss_target': out['loss_target'], 'm_mix_norm_pre': out['m_mix_norm_pre'], 'm_mix_norm_post': out['m_mix_norm_post'], 'm_w_in': out['m_w_in'], 'm_rwkv_shift_mix': out['m_rwkv_shift_mix'], 'm_rwkv_w0': out['m_rwkv_w0'], 'm_rwkv_w_up': out['m_rwkv_w_up'], 'm_rwkv_a0': out['m_rwkv_a0'], 'm_rwkv_a_up': out['m_rwkv_a_up'], 'm_rwkv_g_up': out['m_rwkv_g_up'], 'm_rwkv_k_k': out['m_rwkv_k_k'], 'm_rwkv_k_a': out['m_rwkv_k_a'], 'm_rwkv_r_k': out['m_rwkv_r_k'], 'm_rwkv_gn_w': out['m_rwkv_gn_w'], 'm_rwkv_gn_b': out['m_rwkv_gn_b'], 'm_gdn_conv_w': out['m_gdn_conv_w'], 'm_gdn_a_log': out['m_gdn_a_log'], 'm_gdn_dt_bias': out['m_gdn_dt_bias'], 'm_gdn_norm_w': out['m_gdn_norm_w'], 'm_w_branch_rwkv': out['m_w_branch_rwkv'], 'm_w_branch_gdn': out['m_w_branch_gdn'], 'm_w_mix_out': out['m_w_mix_out'], 'm_xa_norm_pre': out['m_xa_norm_pre'], 'm_xa_norm_mem': out['m_xa_norm_mem'], 'm_xa_norm_post': out['m_xa_norm_post'], 'm_xa_w_q': out['m_xa_w_q'], 'm_xa_w_kv': out['m_xa_w_kv'], 'm_xa_w_o': out['m_xa_w_o'], 'm_mlp_norm_pre': out['m_mlp_norm_pre'], 'm_mlp_norm_post': out['m_mlp_norm_post'], 'm_mlp_w_up': out['m_mlp_w_up'], 'm_mlp_w_down': out['m_mlp_w_down'], 'v_mix_norm_pre': out['v_mix_norm_pre'], 'v_mix_norm_post': out['v_mix_norm_post'], 'v_w_in': out['v_w_in'], 'v_rwkv_shift_mix': out['v_rwkv_shift_mix'], 'v_rwkv_w0': out['v_rwkv_w0'], 'v_rwkv_w_up': out['v_rwkv_w_up'], 'v_rwkv_a0': out['v_rwkv_a0'], 'v_rwkv_a_up': out['v_rwkv_a_up'], 'v_rwkv_g_up': out['v_rwkv_g_up'], 'v_rwkv_k_k': out['v_rwkv_k_k'], 'v_rwkv_k_a': out['v_rwkv_k_a'], 'v_rwkv_r_k': out['v_rwkv_r_k'], 'v_rwkv_gn_w': out['v_rwkv_gn_w'], 'v_rwkv_gn_b': out['v_rwkv_gn_b'], 'v_gdn_conv_w': out['v_gdn_conv_w'], 'v_gdn_a_log': out['v_gdn_a_log'], 'v_gdn_dt_bias': out['v_gdn_dt_bias'], 'v_gdn_norm_w': out['v_gdn_norm_w'], 'v_w_branch_rwkv': out['v_w_branch_rwkv'], 'v_w_branch_gdn': out['v_w_branch_gdn'], 'v_w_mix_out': out['v_w_mix_out'], 'v_xa_norm_pre': out['v_xa_norm_pre'], 'v_xa_norm_mem': out['v_xa_norm_mem'], 'v_xa_norm_post': out['v_xa_norm_post'], 'v_xa_w_q': out['v_xa_w_q'], 'v_xa_w_kv': out['v_xa_w_kv'], 'v_xa_w_o': out['v_xa_w_o'], 'v_mlp_norm_pre': out['v_mlp_norm_pre'], 'v_mlp_norm_post': out['v_mlp_norm_post'], 'v_mlp_w_up': out['v_mlp_w_up'], 'v_mlp_w_down': out['v_mlp_w_down']}


def _loss(weights, diff, rest, loss_target):
    with _jax.named_scope("forward"):
        args = {**rest, TWIN_DIFF_INPUT: diff, **{k: w.astype(_WEIGHT_DTYPES[k]) for k, w in weights.items()}}
        y = _forward(args)
    with _jax.named_scope("loss_head"):
        err = _jnp.square(y.astype(_jnp.float32) - loss_target)
        return 0.5 * _jnp.sum(_jnp.mean(err, axis=-1)) if err.ndim else 0.5 * err


def _adamw(w, g, m, v):
    m = ADAM_B1 * m + (1.0 - ADAM_B1) * g
    v = ADAM_B2 * v + (1.0 - ADAM_B2) * _jnp.square(g)
    m_hat = m / (1.0 - ADAM_B1 ** ADAM_STEP)
    v_hat = v / (1.0 - ADAM_B2 ** ADAM_STEP)
    delta = -ADAM_LR * (m_hat / (_jnp.sqrt(v_hat) + ADAM_EPS) + ADAM_WD * w)
    return delta, m, v


def reference(x, mem, mix_norm_pre, mix_norm_post, w_in, rwkv_shift_mix, rwkv_w0, rwkv_w_up, rwkv_a0, rwkv_a_up, rwkv_g_up, rwkv_k_k, rwkv_k_a, rwkv_r_k, rwkv_gn_w, rwkv_gn_b, gdn_conv_w, gdn_a_log, gdn_dt_bias, gdn_norm_w, w_branch_rwkv, w_branch_gdn, w_mix_out, xa_norm_pre, xa_norm_mem, xa_norm_post, xa_w_q, xa_w_kv, xa_w_o, mlp_norm_pre, mlp_norm_post, mlp_w_up, mlp_w_down, loss_target, m_mix_norm_pre, m_mix_norm_post, m_w_in, m_rwkv_shift_mix, m_rwkv_w0, m_rwkv_w_up, m_rwkv_a0, m_rwkv_a_up, m_rwkv_g_up, m_rwkv_k_k, m_rwkv_k_a, m_rwkv_r_k, m_rwkv_gn_w, m_rwkv_gn_b, m_gdn_conv_w, m_gdn_a_log, m_gdn_dt_bias, m_gdn_norm_w, m_w_branch_rwkv, m_w_branch_gdn, m_w_mix_out, m_xa_norm_pre, m_xa_norm_mem, m_xa_norm_post, m_xa_w_q, m_xa_w_kv, m_xa_w_o, m_mlp_norm_pre, m_mlp_norm_post, m_mlp_w_up, m_mlp_w_down, v_mix_norm_pre, v_mix_norm_post, v_w_in, v_rwkv_shift_mix, v_rwkv_w0, v_rwkv_w_up, v_rwkv_a0, v_rwkv_a_up, v_rwkv_g_up, v_rwkv_k_k, v_rwkv_k_a, v_rwkv_r_k, v_rwkv_gn_w, v_rwkv_gn_b, v_gdn_conv_w, v_gdn_a_log, v_gdn_dt_bias, v_gdn_norm_w, v_w_branch_rwkv, v_w_branch_gdn, v_w_mix_out, v_xa_norm_pre, v_xa_norm_mem, v_xa_norm_post, v_xa_w_q, v_xa_w_kv, v_xa_w_o, v_mlp_norm_pre, v_mlp_norm_post, v_mlp_w_up, v_mlp_w_down):
    given = dict(x=x, mem=mem, mix_norm_pre=mix_norm_pre, mix_norm_post=mix_norm_post, w_in=w_in, rwkv_shift_mix=rwkv_shift_mix, rwkv_w0=rwkv_w0, rwkv_w_up=rwkv_w_up, rwkv_a0=rwkv_a0, rwkv_a_up=rwkv_a_up, rwkv_g_up=rwkv_g_up, rwkv_k_k=rwkv_k_k, rwkv_k_a=rwkv_k_a, rwkv_r_k=rwkv_r_k, rwkv_gn_w=rwkv_gn_w, rwkv_gn_b=rwkv_gn_b, gdn_conv_w=gdn_conv_w, gdn_a_log=gdn_a_log, gdn_dt_bias=gdn_dt_bias, gdn_norm_w=gdn_norm_w, w_branch_rwkv=w_branch_rwkv, w_branch_gdn=w_branch_gdn, w_mix_out=w_mix_out, xa_norm_pre=xa_norm_pre, xa_norm_mem=xa_norm_mem, xa_norm_post=xa_norm_post, xa_w_q=xa_w_q, xa_w_kv=xa_w_kv, xa_w_o=xa_w_o, mlp_norm_pre=mlp_norm_pre, mlp_norm_post=mlp_norm_post, mlp_w_up=mlp_w_up, mlp_w_down=mlp_w_down, loss_target=loss_target, m_mix_norm_pre=m_mix_norm_pre, m_mix_norm_post=m_mix_norm_post, m_w_in=m_w_in, m_rwkv_shift_mix=m_rwkv_shift_mix, m_rwkv_w0=m_rwkv_w0, m_rwkv_w_up=m_rwkv_w_up, m_rwkv_a0=m_rwkv_a0, m_rwkv_a_up=m_rwkv_a_up, m_rwkv_g_up=m_rwkv_g_up, m_rwkv_k_k=m_rwkv_k_k, m_rwkv_k_a=m_rwkv_k_a, m_rwkv_r_k=m_rwkv_r_k, m_rwkv_gn_w=m_rwkv_gn_w, m_rwkv_gn_b=m_rwkv_gn_b, m_gdn_conv_w=m_gdn_conv_w, m_gdn_a_log=m_gdn_a_log, m_gdn_dt_bias=m_gdn_dt_bias, m_gdn_norm_w=m_gdn_norm_w, m_w_branch_rwkv=m_w_branch_rwkv, m_w_branch_gdn=m_w_branch_gdn, m_w_mix_out=m_w_mix_out, m_xa_norm_pre=m_xa_norm_pre, m_xa_norm_mem=m_xa_norm_mem, m_xa_norm_post=m_xa_norm_post, m_xa_w_q=m_xa_w_q, m_xa_w_kv=m_xa_w_kv, m_xa_w_o=m_xa_w_o, m_mlp_norm_pre=m_mlp_norm_pre, m_mlp_norm_post=m_mlp_norm_post, m_mlp_w_up=m_mlp_w_up, m_mlp_w_down=m_mlp_w_down, v_mix_norm_pre=v_mix_norm_pre, v_mix_norm_post=v_mix_norm_post, v_w_in=v_w_in, v_rwkv_shift_mix=v_rwkv_shift_mix, v_rwkv_w0=v_rwkv_w0, v_rwkv_w_up=v_rwkv_w_up, v_rwkv_a0=v_rwkv_a0, v_rwkv_a_up=v_rwkv_a_up, v_rwkv_g_up=v_rwkv_g_up, v_rwkv_k_k=v_rwkv_k_k, v_rwkv_k_a=v_rwkv_k_a, v_rwkv_r_k=v_rwkv_r_k, v_rwkv_gn_w=v_rwkv_gn_w, v_rwkv_gn_b=v_rwkv_gn_b, v_gdn_conv_w=v_gdn_conv_w, v_gdn_a_log=v_gdn_a_log, v_gdn_dt_bias=v_gdn_dt_bias, v_gdn_norm_w=v_gdn_norm_w, v_w_branch_rwkv=v_w_branch_rwkv, v_w_branch_gdn=v_w_branch_gdn, v_w_mix_out=v_w_mix_out, v_xa_norm_pre=v_xa_norm_pre, v_xa_norm_mem=v_xa_norm_mem, v_xa_norm_post=v_xa_norm_post, v_xa_w_q=v_xa_w_q, v_xa_w_kv=v_xa_w_kv, v_xa_w_o=v_xa_w_o, v_mlp_norm_pre=v_mlp_norm_pre, v_mlp_norm_post=v_mlp_norm_post, v_mlp_w_up=v_mlp_w_up, v_mlp_w_down=v_mlp_w_down)
    weights = {n: given[n] for n in TWIN_WEIGHTS}
    shared = {n: given[n] for n in SHARED_INPUTS}
    per_example = {n: given[n] for n in ['x', 'mem']}
    grad_fn = _jax.value_and_grad(_loss, argnums=(0, 1))

    def one_microbatch(ex, loss_target):
        ex = dict(ex)
        diff = ex.pop(TWIN_DIFF_INPUT)
        return grad_fn(weights, diff, {**shared, **ex}, loss_target)

    if N_MICROBATCH == 1:
        loss, (grad_w, grad_x) = one_microbatch(per_example, given["loss_target"])
    else:
        def body(carry, xs):
            loss_sum, grad_sum = carry
            l_k, (gw_k, gx_k) = one_microbatch(xs[0], xs[1])
            with _jax.named_scope("update"):
                return (loss_sum + l_k, _jax.tree.map(_jnp.add, grad_sum, gw_k)), gx_k

        init = (_jnp.zeros((), _jnp.float32), _jax.tree.map(_jnp.zeros_like, weights))
        (loss, grad_w), grad_x = _jax.lax.scan(body, init, (per_example, given["loss_target"]))
    with _jax.named_scope("update"):
        delta_w, new_m, new_v = {}, {}, {}
        for n in TWIN_WEIGHTS:
            delta_w[n], new_m[n], new_v[n] = _adamw(weights[n], grad_w[n], given["m_" + n], given["v_" + n])
    return (loss, grad_x, *[grad_w[n] for n in TWIN_WEIGHTS], *[delta_w[n] for n in TWIN_WEIGHTS],
            *[new_m[n] for n in TWIN_WEIGHTS], *[new_v[n] for n in TWIN_WEIGHTS])
```

```python
import functools
import math

import jax
import jax.numpy as jnp
from jax import lax
from jax.experimental import pallas as pl
from jax.experimental.pallas import tpu as pltpu

F32 = jnp.float32
BF16 = jnp.bfloat16
MESH = pl.DeviceIdType.MESH

LANES = 128
VMEM_LIMIT = 56 * 1024 * 1024
CHUNK = 64
TIME_BLOCK = 1024

NORM_EPS = 1e-6
L2_EPS = 1e-6
RWKV_GN_EPS = 64e-5
RWKV_HEAD = 64
GDN_HEAD = 128
XA_HEAD = 128
ADAM_LR, ADAM_B1, ADAM_B2, ADAM_EPS, ADAM_WD, ADAM_STEP = 0.001, 0.9, 0.999, 1e-08, 0.01, 10

WEIGHTS = ['mix_norm_pre', 'mix_norm_post', 'w_in', 'rwkv_shift_mix', 'rwkv_w0', 'rwkv_w_up', 'rwkv_a0', 'rwkv_a_up',
           'rwkv_g_up', 'rwkv_k_k', 'rwkv_k_a', 'rwkv_r_k', 'rwkv_gn_w', 'rwkv_gn_b', 'gdn_conv_w', 'gdn_a_log',
           'gdn_dt_bias', 'gdn_norm_w', 'w_branch_rwkv', 'w_branch_gdn', 'w_mix_out', 'xa_norm_pre', 'xa_norm_mem',
           'xa_norm_post', 'xa_w_q', 'xa_w_kv', 'xa_w_o', 'mlp_norm_pre', 'mlp_norm_post', 'mlp_w_up', 'mlp_w_down']
COL_SHARDED = ['w_in', 'rwkv_w_up', 'rwkv_a_up', 'rwkv_g_up', 'gdn_conv_w', 'w_branch_rwkv', 'w_branch_gdn', 'xa_w_o',
               'mlp_w_up']
ROW_SHARDED = ['w_mix_out', 'xa_w_q', 'xa_w_kv', 'mlp_w_down']
SHARDED = COL_SHARDED + ROW_SHARDED
REPLICATED = [n for n in WEIGHTS if n not in SHARDED]


def _pcall(body, **kw):
    return pl.pallas_call(body, **kw)


def _params(sem=None):
    return pltpu.CompilerParams(dimension_semantics=sem, vmem_limit_bytes=VMEM_LIMIT)


def _pick(n, pref, mult=LANES):
    t = (min(n, pref) // mult) * mult
    while t >= mult:
        if n % t == 0:
            return t
        t -= mult
    return n


def _dg(a, b, ca, cb):
    return lax.dot_general(a, b, (((ca,), (cb,)), ((), ())), preferred_element_type=F32)


def _split(x):
    hi = x.astype(BF16)
    lo = (x - hi.astype(F32)).astype(BF16)
    return hi, lo


def _dot_raw(a, b, ca, cb, mode):
    if mode == '1':
        return _dg(a.astype(BF16), b.astype(BF16), ca, cb)
    if mode == 'cb':
        ah, al = _split(a)
        bb = b.astype(BF16)
        return _dg(ah, bb, ca, cb) + _dg(al, bb, ca, cb)
    if mode == 'ca':
        bh, bl = _split(b)
        ab = a.astype(BF16)
        return _dg(ab, bh, ca, cb) + _dg(ab, bl, ca, cb)
    ah, al = _split(a)
    bh, bl = _split(b)
    return _dg(ah, bh, ca, cb) + (_dg(ah, bl, ca, cb) + _dg(al, bh, ca, cb))


@functools.partial(jax.custom_vjp, nondiff_argnums=(2, 3, 4))
def _dot_diff(a, b, ca, cb, mode):
    return _dot_raw(a, b, ca, cb, mode)


def _dot_diff_fwd(a, b, ca, cb, mode):
    return _dot_raw(a, b, ca, cb, mode), (a, b)


def _dot_diff_bwd(ca, cb, mode, res, ct):
    a, b = res
    if mode == 'ca':
        da = jnp.zeros_like(a)
    else:
        bmode = {'1': '1', '3': '3', 'cb': 'cb'}[mode]
        if ca == 1:
            da = _dot_raw(ct, b, 1, 1 if cb == 0 else 0, bmode)
        else:
            da = _dot_raw(b, ct, 1 if cb == 0 else 0, 1, 'ca' if bmode == 'cb' else bmode)
    if mode == 'cb':
        db = jnp.zeros_like(b)
    else:
        amode = {'1': '1', '3': '3', 'ca': 'ca'}[mode]
        if cb == 0:
            db = _dot_raw(a, ct, 0 if ca == 1 else 1, 0, amode)
        else:
            db = _dot_raw(ct, a, 0, 0 if ca == 1 else 1, 'cb' if amode == 'ca' else amode)
    return da, db


_dot_diff.defvjp(_dot_diff_fwd, _dot_diff_bwd)


class _Ops:
    def __init__(self, diff):
        self.diff = diff

    def dot(self, a, b, ca=1, cb=0, mode='1'):
        if self.diff:
            return _dot_diff(a, b, ca, cb, mode)
        return _dot_raw(a, b, ca, cb, mode)


_RAW = _Ops(False)
_DIFF = _Ops(True)


def _sigmoid(x):
    return 1.0 / (1.0 + jnp.exp(-x))


def _softplus(x):
    return jnp.maximum(x, 0.0) + jnp.log(1.0 + jnp.exp(-jnp.abs(x)))


def _rms(x, g, eps=NORM_EPS):
    return x * lax.rsqrt(jnp.mean(x * x, axis=-1, keepdims=True) + eps) * g


def _head_matrix(width, head):
    ch = lax.broadcasted_iota(jnp.int32, (width, LANES), 0)
    hh = lax.broadcasted_iota(jnp.int32, (width, LANES), 1)
    return (lax.shift_right_logical(ch, int(math.log2(head))) == hh).astype(BF16)


def _headsum(ops, x, head):
    e = _head_matrix(x.shape[-1], head)
    s = ops.dot(x, e, 1, 0, 'cb')
    return ops.dot(s, e, 1, 1, 'cb')


def _f_rms(ops, x, g):
    return (_rms(x, g),)


def _f_resid(ops, h, y, g_post, g_next):
    h1 = h + _rms(y, g_post)
    return h1, _rms(h1, g_next)


def _f_final(ops, h, y, tgt, g_post):
    e = h + _rms(y, g_post) - tgt
    return (0.5 * jnp.mean(e * e, axis=-1, keepdims=True),)


def _f_act(ops, up):
    r = jnp.maximum(up, 0.0)
    return (r * r,)


def _f_merge(ops, g_rw, g_gdn, b_rw, b_gdn):
    return (_sigmoid(g_rw) * b_rw + _sigmoid(g_gdn) * b_gdn,)


def _f_rwkv_pre(ops, xr, xk, xv, xlw, xla, xlg, w0, a0, k_k, k_a, w_up, a_up, g_up):
    z = w0 + ops.dot(jnp.tanh(xlw), w_up)
    lwd = -jnp.exp(-_softplus(-z) - 0.5)
    a = _sigmoid(a0 + ops.dot(xla, a_up))
    gate = ops.dot(_sigmoid(xlg), g_up)
    kk = xk * k_k
    kk = kk * lax.rsqrt(_headsum(ops, kk * kk, RWKV_HEAD) + L2_EPS)
    k2 = xk * (1.0 + (a - 1.0) * k_a)
    return xr, lwd, k2, xv, -kk, kk * a, gate


def _f_rwkv_post(ops, y, r, k2, v, gate, r_k, gn_w, gn_b):
    inv = 1.0 / RWKV_HEAD
    yc = y - _headsum(ops, y, RWKV_HEAD) * inv
    var = _headsum(ops, yc * yc, RWKV_HEAD) * inv
    yn = yc * lax.rsqrt(var + RWKV_GN_EPS) * gn_w + gn_b
    bonus = _headsum(ops, r * k2 * r_k, RWKV_HEAD) * v
    return ((yn + bonus) * gate,)


def _f_gdn_pre(ops, qc, kc, vc, ba, alog_p, dt_p, *, heads):
    lane = lax.broadcasted_iota(jnp.int32, (1, LANES), 1)
    beta = _sigmoid(ba)
    g = -jnp.exp(alog_p) * _softplus(ba + dt_p)
    gb = jnp.where(lane < heads, beta, g)
    width = qc.shape[-1]
    row = lax.broadcasted_iota(jnp.int32, (LANES, width), 0)
    col_head = lax.shift_right_logical(lax.broadcasted_iota(jnp.int32, (LANES, width), 1), int(math.log2(GDN_HEAD)))
    beta_x = ops.dot(gb, (row == col_head).astype(BF16), 1, 0, 'cb')
    g_x = ops.dot(gb, (row == col_head + heads).astype(BF16), 1, 0, 'cb')
    qn = qc * lax.rsqrt(_headsum(ops, qc * qc, GDN_HEAD) + L2_EPS)
    kn = kc * lax.rsqrt(_headsum(ops, kc * kc, GDN_HEAD) + L2_EPS)
    return qn, kn, vc, beta_x, g_x


def _f_gdn_post(ops, o, z, nw):
    ms = _headsum(ops, o * o, GDN_HEAD) * (1.0 / GDN_HEAD)
    return (o * lax.rsqrt(ms + NORM_EPS) * nw * (z * _sigmoid(z)),)


def _f_xattn(ops, q, k, v):
    width = q.shape[-1]
    lane = lax.broadcasted_iota(jnp.int32, (1, width), 1)
    out = jnp.zeros_like(q)
    for h in range(width // XA_HEAD):
        m = ((lane >= h * XA_HEAD) & (lane < (h + 1) * XA_HEAD)).astype(F32)
        s = ops.dot(q * m, k, 1, 1) * (XA_HEAD ** -0.5)
        s = s - jnp.max(s, axis=-1, keepdims=True)
        e = jnp.exp(s)
        pr = e / jnp.sum(e, axis=-1, keepdims=True)
        out = out + ops.dot(pr, v) * m
    return (out,)


def _inv_unit_lower(ops, a):
    n = a.shape[0]
    ii = lax.broadcasted_iota(jnp.int32, (n, n), 0)
    jj = lax.broadcasted_iota(jnp.int32, (n, n), 1)
    eye = (ii == jj).astype(F32)
    p = eye + a
    q = a
    k = 2
    while k < n:
        q = ops.dot(q, q, 1, 0, '3')
        p = p + ops.dot(p, q, 1, 0, '3')
        k *= 2
    return p


def _rwkv_chunk(ops, s, r, lw, k, v, a, b):
    c = r.shape[0]
    ii = lax.broadcasted_iota(jnp.int32, (c, c), 0)
    jj = lax.broadcasted_iota(jnp.int32, (c, c), 1)
    incl = ii >= jj
    strict = ii > jj
    cum = ops.dot(incl.astype(BF16), lw, 1, 0, 'ca')
    tot = ops.dot(jnp.ones((LANES, c), BF16), lw, 1, 0, 'ca')
    w_in = jnp.exp(cum)
    w_inv = jnp.exp(-cum)
    at = a * jnp.exp(cum - lw)
    bt = b * w_inv
    kt = k * w_inv
    rt = r * w_in
    lane = lax.broadcasted_iota(jnp.int32, (1, LANES), 1)
    masks = [(lane < RWKV_HEAD).astype(F32), (lane >= RWKV_HEAD).astype(F32)]
    base = ops.dot(at, s, 1, 1, '3')
    y = ops.dot(rt, s, 1, 1, '3')
    u = jnp.zeros_like(r)
    for m in masks:
        atm = at * m
        a_ab = jnp.where(strict, ops.dot(atm, bt, 1, 1, '3'), 0.0)
        a_ak = jnp.where(strict, ops.dot(atm, kt, 1, 1, '3'), 0.0)
        t = _inv_unit_lower(ops, a_ab)
        u = u + ops.dot(t, base + ops.dot(a_ak, v, 1, 0, '3'), 1, 0, '3') * m
    for m in masks:
        rtm = rt * m
        a_rb = jnp.where(incl, ops.dot(rtm, bt, 1, 1, '3'), 0.0)
        a_rk = jnp.where(incl, ops.dot(rtm, kt, 1, 1, '3'), 0.0)
        y = y + (ops.dot(a_rb, u, 1, 0, '3') + ops.dot(a_rk, v, 1, 0, '3')) * m
    bi = lax.broadcasted_iota(jnp.int32, (LANES, LANES), 0) >= RWKV_HEAD
    bj = lax.broadcasted_iota(jnp.int32, (LANES, LANES), 1) >= RWKV_HEAD
    upd = jnp.where(bi == bj, ops.dot(u, bt, 0, 0, '3') + ops.dot(v, kt, 0, 0, '3'), 0.0)
    return (s + upd) * jnp.exp(tot), y


def _gdn_chunk(ops, s, q, k, v, beta, g):
    c = q.shape[0]
    ii = lax.broadcasted_iota(jnp.int32, (c, c), 0)
    jj = lax.broadcasted_iota(jnp.int32, (c, c), 1)
    causal = ii >= jj
    strict = ii > jj
    q = q * (GDN_HEAD ** -0.5)
    gc = ops.dot(causal.astype(BF16), g, 1, 0, 'ca')
    ones = jnp.ones((c, LANES), BF16)
    gc_col = ops.dot(gc, ones, 1, 1, 'cb') * (1.0 / LANES)
    gc_row = ops.dot(ones, gc, 1, 1, 'ca') * (1.0 / LANES)
    decay = jnp.where(causal, jnp.exp(jnp.where(causal, gc_col - gc_row, 0.0)), 0.0)
    kb = k * beta
    lmat = jnp.where(strict, ops.dot(kb, k, 1, 1, '3') * decay, 0.0)
    tinv = _inv_unit_lower(ops, -lmat)
    egc = jnp.exp(gc)
    u = ops.dot(tinv, v * beta, 1, 0, '3')
    w = ops.dot(tinv, kb * egc, 1, 0, '3')
    a_intra = jnp.where(causal, ops.dot(q, k, 1, 1, '3') * decay, 0.0)
    g_tot = ops.dot(jnp.ones((c, c), BF16), g, 1, 0, 'ca')
    k_dec = k * jnp.exp(g_tot - gc)
    v_new = u - ops.dot(w, s, 1, 0, '3')
    o = ops.dot(q * egc, s, 1, 0, '3') + ops.dot(a_intra, v_new, 1, 0, '3')
    g_full = ops.dot(jnp.ones((LANES, c), BF16), g, 1, 0, 'ca')
    s_new = s * jnp.exp(g_full) + ops.dot(k_dec, v_new, 0, 0, '3')
    return s_new, o


def _mm(name, a, b, ta=False, tb=False, out_dtype=F32, tm=1024, tn=1024, tk=512):
    m = a.shape[1] if ta else a.shape[0]
    kd = a.shape[0] if ta else a.shape[1]
    n = b.shape[0] if tb else b.shape[1]
    assert kd == (b.shape[1] if tb else b.shape[0]), (name, a.shape, b.shape)
    tm, tn, tk = _pick(m, tm), _pick(n, tn), _pick(kd, tk)
    nk = kd // tk

    def body(a_ref, b_ref, o_ref, acc):
        kk = pl.program_id(2)

        @pl.when(kk == 0)
        def _():
            acc[...] = jnp.zeros_like(acc)

        acc[...] += _dg(a_ref[...].astype(BF16), b_ref[...].astype(BF16), 0 if ta else 1, 1 if tb else 0)

        @pl.when(kk == nk - 1)
        def _():
            o_ref[...] = acc[...].astype(o_ref.dtype)

    a_spec = pl.BlockSpec((tk, tm), lambda i, j, k: (k, i)) if ta else pl.BlockSpec((tm, tk), lambda i, j, k: (i, k))
    b_spec = pl.BlockSpec((tn, tk), lambda i, j, k: (j, k)) if tb else pl.BlockSpec((tk, tn), lambda i, j, k: (k, j))
    return _pcall(
        body, name=name, grid=(m // tm, n // tn, nk), in_specs=[a_spec, b_spec],
        out_specs=pl.BlockSpec((tm, tn), lambda i, j, k: (i, j)), out_shape=jax.ShapeDtypeStruct((m, n), out_dtype),
        scratch_shapes=[pltpu.VMEM((tm, tn), F32)],
        compiler_params=_params(("parallel", "parallel", "arbitrary")))(a, b)


def _row_spec(tm, width, cb):
    return pl.BlockSpec((tm, width), lambda i: (i, cb))


def _rows_fwd(name, f, rows, params, outs, tm):
    t = rows[0][0].shape[0]
    tm = min(tm, t)
    n_in = len(rows) + len(params)
    kept = [o for o in outs if o is not None]

    def body(*refs):
        vals = f(_RAW, *[r[...].astype(F32) for r in refs[:n_in]])
        o_refs = refs[n_in:]
        j = 0
        for val, o in zip(vals, outs):
            if o is not None:
                o_refs[j][...] = val.astype(o_refs[j].dtype)
                j += 1

    in_specs = [_row_spec(tm, w, cb) for (_, w, cb) in rows]
    in_specs += [pl.BlockSpec(p.shape, lambda i: (0, 0)) for p in params]
    res = _pcall(
        body, name=name, grid=(t // tm,), in_specs=in_specs,
        out_specs=[_row_spec(tm, w, 0) for (w, _) in kept],
        out_shape=[jax.ShapeDtypeStruct((t, w), dt) for (w, dt) in kept],
        compiler_params=_params(("parallel",)))(*[r[0] for r in rows], *params)
    return res


def _rows_bwd(name, f, rows, params, cts, tm, row_out, out_widths, adds=(), loss=False):
    t = rows[0][0].shape[0]
    tm = min(tm, t)
    n_r, n_p = len(rows), len(params)
    ct_flat = [c for cl in cts for c in cl]
    n_ct, n_add = len(ct_flat), len(adds)
    n_out = len(out_widths)

    def body(*refs):
        i = pl.program_id(0)
        ins = [r[...].astype(F32) for r in refs[:n_r + n_p]]
        ct_refs = refs[n_r + n_p:n_r + n_p + n_ct]
        add_refs = refs[n_r + n_p + n_ct:n_r + n_p + n_ct + n_add]
        o_refs = refs[n_r + n_p + n_ct + n_add:]
        outs, vjp = jax.vjp(lambda *a: f(_DIFF, *a), *ins)
        ctv, pos = [], 0
        for o, cl in zip(outs, cts):
            if loss:
                ctv.append(jnp.ones_like(o))
            elif not cl:
                ctv.append(jnp.zeros_like(o))
            else:
                acc = ct_refs[pos][...].astype(F32)
                for q in range(1, len(cl)):
                    acc = acc + ct_refs[pos + q][...].astype(F32)
                ctv.append(acc)
            pos += len(cl)
        grads = vjp(tuple(ctv))
        for ri, ro in enumerate(row_out):
            if ro is not None:
                oi, off = ro
                val = grads[ri]
                for ai, (aoi, _) in enumerate(adds):
                    if aoi == oi:
                        val = val + add_refs[ai][...].astype(F32)
                o_refs[oi][:, off:off + val.shape[1]] = val
        p_refs = o_refs[n_out:n_out + n_p]

        @pl.when(i == 0)
        def _():
            for pr in p_refs:
                pr[...] = jnp.zeros_like(pr)
            if loss:
                o_refs[n_out + n_p][...] = jnp.zeros((8, LANES), F32)

        for pi, pr in enumerate(p_refs):
            pr[...] += grads[n_r + pi]
        if loss:
            o_refs[n_out + n_p][...] += jnp.sum(outs[0])

    in_specs = [_row_spec(tm, w, cb) for (_, w, cb) in rows]
    in_specs += [pl.BlockSpec(p.shape, lambda i: (0, 0)) for p in params]
    in_specs += [_row_spec(tm, w, cb) for (_, w, cb) in ct_flat]
    in_specs += [_row_spec(tm, w, cb) for (_, (_, w, cb)) in adds]
    out_specs = [_row_spec(tm, w, 0) for w in out_widths]
    out_specs += [pl.BlockSpec(p.shape, lambda i: (0, 0)) for p in params]
    out_shape = [jax.ShapeDtypeStruct((t, w), F32) for w in out_widths]
    out_shape += [jax.ShapeDtypeStruct(p.shape, F32) for p in params]
    if loss:
        out_specs.append(pl.BlockSpec((8, LANES), lambda i: (0, 0)))
        out_shape.append(jax.ShapeDtypeStruct((8, LANES), F32))
    res = _pcall(
        body, name=name, grid=(t // tm,), in_specs=in_specs, out_specs=out_specs, out_shape=out_shape,
        compiler_params=_params(("arbitrary",)))(
            *[r[0] for r in rows], *params, *[c[0] for c in ct_flat], *[a[1][0] for a in adds])
    return res[:n_out], res[n_out:n_out + n_p], res[n_out + n_p:]


def _shift_down(x, k):
    row = lax.broadcasted_iota(jnp.int32, x.shape, 0)
    return jnp.where(row >= k, pltpu.roll(x, k, 0), 0.0)


def _shift_up(x, k):
    t = x.shape[0]
    row = lax.broadcasted_iota(jnp.int32, x.shape, 0)
    return jnp.where(row < t - k, pltpu.roll(x, t - k, 0), 0.0)


def _tokshift_fwd(p, mix, n_main, lora_cb):
    t = p.shape[0]
    nblk = mix.shape[1] // LANES

    def src(j):
        return (0, jnp.where(j < n_main, j, j - n_main + lora_cb))

    def body(p_ref, mix_ref, o_ref):
        xv = p_ref[...]
        o_ref[...] = xv + (_shift_down(xv, 1) - xv) * mix_ref[...]

    return _pcall(
        body, name="tokshift_fwd", grid=(nblk,),
        in_specs=[pl.BlockSpec((t, LANES), src), pl.BlockSpec((1, LANES), lambda j: (0, j))],
        out_specs=pl.BlockSpec((t, LANES), lambda j: (0, j)),
        out_shape=jax.ShapeDtypeStruct((t, mix.shape[1]), F32), compiler_params=_params(("parallel",)))(p, mix)


def _tokshift_bwd(dxm, p, mix, n_main, lora_cb):
    t = p.shape[0]
    nblk = mix.shape[1] // LANES

    def src(j):
        return (0, jnp.where(j < n_main, j, j - n_main + lora_cb))

    def body(d_ref, p_ref, mix_ref, dp_ref, dmix_ref):
        xv, d, mx = p_ref[...], d_ref[...], mix_ref[...]
        dp_ref[...] = d * (1.0 - mx) + _shift_up(d * mx, 1)
        dmix_ref[...] = jnp.sum(d * (_shift_down(xv, 1) - xv), axis=0, keepdims=True)

    return _pcall(
        body, name="tokshift_bwd", grid=(nblk,),
        in_specs=[pl.BlockSpec((t, LANES), lambda j: (0, j)), pl.BlockSpec((t, LANES), src),
                  pl.BlockSpec((1, LANES), lambda j: (0, j))],
        out_specs=[pl.BlockSpec((t, LANES), lambda j: (0, j)), pl.BlockSpec((1, LANES), lambda j: (0, j))],
        out_shape=[jax.ShapeDtypeStruct((t, mix.shape[1]), F32), jax.ShapeDtypeStruct((1, mix.shape[1]), F32)],
        compiler_params=_params(("parallel",)))(dxm, p, mix)


def _conv_pre(xv, w):
    return (w[3:4, :] * xv + w[2:3, :] * _shift_down(xv, 1) + w[1:2, :] * _shift_down(xv, 2)
            + w[0:1, :] * _shift_down(xv, 3))


def _conv_fwd(p, w8, cb0):
    t = p.shape[0]
    width = w8.shape[1]

    def body(p_ref, w_ref, o_ref):
        c = _conv_pre(p_ref[...], w_ref[...])
        o_ref[...] = c * _sigmoid(c)

    return _pcall(
        body, name="conv_fwd", grid=(width // LANES,),
        in_specs=[pl.BlockSpec((t, LANES), lambda j: (0, j + cb0)), pl.BlockSpec((8, LANES), lambda j: (0, j))],
        out_specs=pl.BlockSpec((t, LANES), lambda j: (0, j)),
        out_shape=jax.ShapeDtypeStruct((t, width), F32), compiler_params=_params(("parallel",)))(p, w8)


def _conv_bwd(dy, p, w8, cb0):
    t = p.shape[0]
    width = w8.shape[1]

    def body(d_ref, p_ref, w_ref, dx_ref, dw_ref):
        xv, w = p_ref[...], w_ref[...]
        c = _conv_pre(xv, w)
        sg = _sigmoid(c)
        dc = d_ref[...] * (sg * (1.0 + c * (1.0 - sg)))
        dx_ref[...] = (w[3:4, :] * dc + w[2:3, :] * _shift_up(dc, 1) + w[1:2, :] * _shift_up(dc, 2)
                       + w[0:1, :] * _shift_up(dc, 3))
        row = lax.broadcasted_iota(jnp.int32, (8, LANES), 0)
        dw = jnp.zeros((8, LANES), F32)
        for j in range(4):
            sj = jnp.sum(dc * (xv if j == 3 else _shift_down(xv, 3 - j)), axis=0, keepdims=True)
            dw = dw + jnp.where(row == j, sj, 0.0)
        dw_ref[...] = dw

    return _pcall(
        body, name="conv_bwd", grid=(width // LANES,),
        in_specs=[pl.BlockSpec((t, LANES), lambda j: (0, j)), pl.BlockSpec((t, LANES), lambda j: (0, j + cb0)),
                  pl.BlockSpec((8, LANES), lambda j: (0, j))],
        out_specs=[pl.BlockSpec((t, LANES), lambda j: (0, j)), pl.BlockSpec((8, LANES), lambda j: (0, j))],
        out_shape=[jax.ShapeDtypeStruct((t, width), F32), jax.ShapeDtypeStruct((8, width), F32)],
        compiler_params=_params(("parallel",)))(dy, p, w8)


def _rec_fwd(name, chunk_fn, ins, nblocks):
    t = ins[0][0].shape[0]
    tb = min(TIME_BLOCK, t)
    nt, ncb = t // tb, tb // CHUNK
    n_in = len(ins)

    def body(*refs):
        in_refs, y_ref, sall_ref, s_scr = refs[:n_in], refs[n_in], refs[n_in + 1], refs[n_in + 2]

        @pl.when(pl.program_id(1) == 0)
        def _():
            s_scr[...] = jnp.zeros_like(s_scr)

        def step(ci, s):
            sl = pl.ds(pl.multiple_of(ci * CHUNK, CHUNK), CHUNK)
            sall_ref[0, ci] = s
            s1, y = chunk_fn(_RAW, s, *[r[sl, :] for r in in_refs])
            y_ref[sl, :] = y
            return s1

        s_scr[...] = lax.fori_loop(0, ncb, step, s_scr[...])

    in_specs = [pl.BlockSpec((tb, LANES), functools.partial(lambda h, tt, c0: (tt, c0 + h), c0=c0)) for (_, c0) in ins]
    return _pcall(
        body, name=name, grid=(nblocks, nt), in_specs=in_specs,
        out_specs=[pl.BlockSpec((tb, LANES), lambda h, tt: (tt, h)),
                   pl.BlockSpec((1, ncb, LANES, LANES), lambda h, tt: (h, tt, 0, 0))],
        out_shape=[jax.ShapeDtypeStruct((t, nblocks * LANES), F32),
                   jax.ShapeDtypeStruct((nblocks, t // CHUNK, LANES, LANES), F32)],
        scratch_shapes=[pltpu.VMEM((LANES, LANES), F32)],
        compiler_params=_params(("parallel", "arbitrary")))(*[a for (a, _) in ins])


def _rec_bwd(name, chunk_fn, ins, dy, sall, nblocks):
    t = ins[0][0].shape[0]
    tb = min(TIME_BLOCK, t)
    nt, ncb = t // tb, tb // CHUNK
    n_in = len(ins)

    def body(*refs):
        in_refs, dy_ref, sall_ref = refs[:n_in], refs[n_in], refs[n_in + 1]
        d_refs, ds_scr = refs[n_in + 2:2 * n_in + 2], refs[2 * n_in + 2]

        @pl.when(pl.program_id(1) == 0)
        def _():
            ds_scr[...] = jnp.zeros_like(ds_scr)

        def step(j, ds):
            ci = ncb - 1 - j
            sl = pl.ds(pl.multiple_of(ci * CHUNK, CHUNK), CHUNK)
            _, vjp = jax.vjp(lambda *a: chunk_fn(_DIFF, *a), sall_ref[0, ci], *[r[sl, :] for r in in_refs])
            grads = vjp((ds, dy_ref[sl, :]))
            for dr, gval in zip(d_refs, grads[1:]):
                dr[sl, :] = gval
            return grads[0]

        ds_scr[...] = lax.fori_loop(0, ncb, step, ds_scr[...])

    in_specs = [pl.BlockSpec((tb, LANES), functools.partial(lambda h, tt, c0: (nt - 1 - tt, c0 + h), c0=c0))
                for (_, c0) in ins]
    in_specs.append(pl.BlockSpec((tb, LANES), lambda h, tt: (nt - 1 - tt, h)))
    in_specs.append(pl.BlockSpec((1, ncb, LANES, LANES), lambda h, tt: (h, nt - 1 - tt, 0, 0)))
    return _pcall(
        body, name=name, grid=(nblocks, nt), in_specs=in_specs,
        out_specs=[pl.BlockSpec((tb, LANES), lambda h, tt: (nt - 1 - tt, h)) for _ in ins],
        out_shape=[jax.ShapeDtypeStruct((t, nblocks * LANES), F32) for _ in ins],
        scratch_shapes=[pltpu.VMEM((LANES, LANES), F32)],
        compiler_params=_params(("parallel", "arbitrary")))(*[a for (a, _) in ins], dy, sall)


def _pick_rows(r, c):
    tr = max(8, ((1 << 18) // c) // 8 * 8)
    while tr > 8 and r % tr:
        tr -= 8
    return tr if r % tr == 0 else r


def _sum4(name, r4):
    _, r, c = r4.shape
    tr = _pick_rows(r, c)

    def body(r_ref, o_ref):
        o_ref[...] = ((r_ref[3] + r_ref[0]) + r_ref[1]) + r_ref[2]

    return _pcall(
        body, name=name, grid=(r // tr,), in_specs=[pl.BlockSpec((4, tr, c), lambda i: (0, i, 0))],
        out_specs=pl.BlockSpec((tr, c), lambda i: (i, 0)), out_shape=jax.ShapeDtypeStruct((r, c), F32),
        compiler_params=_params(("parallel",)))(r4)


def _adam_math(w, g, m, v):
    m = ADAM_B1 * m + (1.0 - ADAM_B1) * g
    v = ADAM_B2 * v + (1.0 - ADAM_B2) * (g * g)
    m_hat = m / (1.0 - ADAM_B1 ** ADAM_STEP)
    v_hat = v / (1.0 - ADAM_B2 ** ADAM_STEP)
    return -ADAM_LR * (m_hat / (jnp.sqrt(v_hat) + ADAM_EPS) + ADAM_WD * w), m, v


def _adam_big(name, w, m, v, ga, gb):
    r, c = w.shape
    tr = _pick_rows(r, c)

    def body(w_ref, m_ref, v_ref, ga_ref, gb_ref, g_ref, d_ref, mo_ref, vo_ref):
        g = ga_ref[...] + gb_ref[...]
        g_ref[...] = g
        d_ref[...], mo_ref[...], vo_ref[...] = _adam_math(w_ref[...], g, m_ref[...], v_ref[...])

    spec = pl.BlockSpec((tr, c), lambda i: (i, 0))
    return _pcall(
        body, name=name, grid=(r // tr,), in_specs=[spec] * 5, out_specs=[spec] * 4,
        out_shape=[jax.ShapeDtypeStruct((r, c), F32)] * 4, compiler_params=_params(("parallel",)))(w, m, v, ga, gb)


def _adam_small(w, m, v, g8):
    shape = w.shape

    def body(w_ref, m_ref, v_ref, g8_ref, g_ref, d_ref, mo_ref, vo_ref):
        g = g8_ref[0]
        for dev in range(1, 8):
            g = g + g8_ref[dev]
        g_ref[...] = g
        d_ref[...], mo_ref[...], vo_ref[...] = _adam_math(w_ref[...], g, m_ref[...], v_ref[...])

    return _pcall(
        body, name="adam_small", out_shape=[jax.ShapeDtypeStruct(shape, F32)] * 4,
        compiler_params=_params())(w, m, v, g8)


ANY = pl.BlockSpec(memory_space=pl.ANY)


def _place():
    x, y, c = lax.axis_index("x"), lax.axis_index("y"), lax.axis_index("c")
    return x, y, c, [(1 - x, y), (x, 1 - y), (1 - x, 1 - y)]


def _remote(src, dst, send_sems, recv_sems, k, dev):
    return pltpu.make_async_remote_copy(src_ref=src, dst_ref=dst, send_sem=send_sems.at[k], recv_sem=recv_sems.at[k],
                                        device_id=dev, device_id_type=MESH)


def _gather4(name, local):
    def body(src, out, send_sems, recv_sems, local_sem):
        x, y, c, chips = _place()
        me = 2 * x + y
        own = pltpu.make_async_copy(src, out.at[me], local_sem)
        own.start()
        sends = [_remote(src, out.at[me], send_sems, recv_sems, k, (px, py, c)) for k, (px, py) in enumerate(chips)]
        for cp in sends:
            cp.start()
        for k, (px, py) in enumerate(chips):
            _remote(src, out.at[2 * px + py], send_sems, recv_sems, k, (px, py, c)).wait_recv()
        for cp in sends:
            cp.wait_send()
        own.wait()

    return _pcall(
        body, name=name, in_specs=[ANY], out_specs=ANY,
        out_shape=jax.ShapeDtypeStruct((4,) + local.shape, local.dtype),
        scratch_shapes=[pltpu.SemaphoreType.DMA((3,)), pltpu.SemaphoreType.DMA((3,)), pltpu.SemaphoreType.DMA(())],
    )(local)


def _exchange3(name, g4):
    def body(src, out, send_sems, recv_sems, local_sem):
        x, y, c, chips = _place()
        me = 2 * x + y
        own = pltpu.make_async_copy(src.at[me], out.at[3], local_sem)
        own.start()
        sends = [_remote(src.at[2 * px + py], out.at[k], send_sems, recv_sems, k, (px, py, c))
                 for k, (px, py) in enumerate(chips)]
        for cp in sends:
            cp.start()
        for k, (px, py) in enumerate(chips):
            _remote(src.at[me], out.at[k], send_sems, recv_sems, k, (px, py, c)).wait_recv()
        for cp in sends:
            cp.wait_send()
        own.wait()

    return _pcall(
        body, name=name, in_specs=[ANY], out_specs=ANY, out_shape=jax.ShapeDtypeStruct(g4.shape, g4.dtype),
        scratch_shapes=[pltpu.SemaphoreType.DMA((3,)), pltpu.SemaphoreType.DMA((3,)), pltpu.SemaphoreType.DMA(())],
    )(g4)


def _swap_sibling(name, part):
    def body(src, out, send_sems, recv_sems):
        x, y, c, _ = _place()
        cp = _remote(src, out, send_sems, recv_sems, 0, (x, y, 1 - c))
        cp.start()
        cp.wait_recv()
        cp.wait_send()

    return _pcall(
        body, name=name, in_specs=[ANY], out_specs=ANY, out_shape=jax.ShapeDtypeStruct(part.shape, part.dtype),
        scratch_shapes=[pltpu.SemaphoreType.DMA((1,)), pltpu.SemaphoreType.DMA((1,))])(part)


def _gather8(flat):
    def body(src, out, send_sems, recv_sems, local_sem):
        x, y, c, _ = _place()
        own = pltpu.make_async_copy(src, out.at[4 * x + 2 * y + c], local_sem)
        own.start()
        peers = []
        for mask in range(1, 8):
            px = x ^ ((mask >> 2) & 1)
            py = y ^ ((mask >> 1) & 1)
            pc = c ^ (mask & 1)
            peers.append((px, py, pc))
        sends = [_remote(src, out.at[4 * x + 2 * y + c], send_sems, recv_sems, k, dev) for k, dev in enumerate(peers)]
        for cp in sends:
            cp.start()
        for k, (px, py, pc) in enumerate(peers):
            _remote(src, out.at[4 * px + 2 * py + pc], send_sems, recv_sems, k, (px, py, pc)).wait_recv()
        for cp in sends:
            cp.wait_send()
        own.wait()

    return _pcall(
        body, name="gather8_small", in_specs=[ANY], out_specs=ANY,
        out_shape=jax.ShapeDtypeStruct((8,) + flat.shape, flat.dtype),
        scratch_shapes=[pltpu.SemaphoreType.DMA((7,)), pltpu.SemaphoreType.DMA((7,)), pltpu.SemaphoreType.DMA(())],
    )(flat)


def _pad_cols(a, width):
    return jnp.pad(a, ((0, 0), (0, width - a.shape[1])))


def _col_shards(full):
    r, c = full.shape
    return full.reshape(r, 4, c // 4).transpose(1, 0, 2)


def kernel(x, mem, mix_norm_pre, mix_norm_post, w_in, rwkv_shift_mix, rwkv_w0, rwkv_w_up, rwkv_a0, rwkv_a_up, rwkv_g_up, rwkv_k_k, rwkv_k_a, rwkv_r_k, rwkv_gn_w, rwkv_gn_b, gdn_conv_w, gdn_a_log, gdn_dt_bias, gdn_norm_w, w_branch_rwkv, w_branch_gdn, w_mix_out, xa_norm_pre, xa_norm_mem, xa_norm_post, xa_w_q, xa_w_kv, xa_w_o, mlp_norm_pre, mlp_norm_post, mlp_w_up, mlp_w_down, loss_target, m_mix_norm_pre, m_mix_norm_post, m_w_in, m_rwkv_shift_mix, m_rwkv_w0, m_rwkv_w_up, m_rwkv_a0, m_rwkv_a_up, m_rwkv_g_up, m_rwkv_k_k, m_rwkv_k_a, m_rwkv_r_k, m_rwkv_gn_w, m_rwkv_gn_b, m_gdn_conv_w, m_gdn_a_log, m_gdn_dt_bias, m_gdn_norm_w, m_w_branch_rwkv, m_w_branch_gdn, m_w_mix_out, m_xa_norm_pre, m_xa_norm_mem, m_xa_norm_post, m_xa_w_q, m_xa_w_kv, m_xa_w_o, m_mlp_norm_pre, m_mlp_norm_post, m_mlp_w_up, m_mlp_w_down, v_mix_norm_pre, v_mix_norm_post, v_w_in, v_rwkv_shift_mix, v_rwkv_w0, v_rwkv_w_up, v_rwkv_a0, v_rwkv_a_up, v_rwkv_g_up, v_rwkv_k_k, v_rwkv_k_a, v_rwkv_r_k, v_rwkv_gn_w, v_rwkv_gn_b, v_gdn_conv_w, v_gdn_a_log, v_gdn_dt_bias, v_gdn_norm_w, v_w_branch_rwkv, v_w_branch_gdn, v_w_mix_out, v_xa_norm_pre, v_xa_norm_mem, v_xa_norm_post, v_xa_w_q, v_xa_w_kv, v_xa_w_o, v_mlp_norm_pre, v_mlp_norm_post, v_mlp_w_up, v_mlp_w_down):
    given = dict(locals())
    W = {n: given[n] if given[n].ndim == 2 else given[n][0] for n in WEIGHTS}
    Mo = {n: given['m_' + n].reshape(W[n].shape) for n in WEIGHTS}
    Vo = {n: given['v_' + n].reshape(W[n].shape) for n in WEIGHTS}
    xs, mems, tgt = x[0], mem[0], loss_target[0]
    T, D = xs.shape
    RW = W['rwkv_w0'].shape[-1]
    GW = W['gdn_conv_w'].shape[-1] * 4 // 3
    NH = W['gdn_a_log'].shape[-1]
    LW, LA, LG = W['rwkv_w_up'].shape[0], W['rwkv_a_up'].shape[0], W['rwkv_g_up'].shape[0]
    XW = W['xa_w_q'].shape[-1]
    assert LW <= LANES and LA <= LANES and LG % LANES == 0 and 2 * NH <= LANES

    full = {}
    for n in SHARDED:
        wl = W[n] if n == 'gdn_conv_w' else W[n].astype(BF16)
        g = _gather4("gather_" + n, wl)
        if n in COL_SHARDED:
            full[n] = jnp.concatenate([g[0], g[1], g[2], g[3]], axis=1)
        else:
            full[n] = g.reshape((4 * g.shape[1],) + g.shape[2:])

    RC = 3 * RW + LW + LA + LG
    o_z_raw = RC + 3 * GW
    o_b_raw = o_z_raw + GW
    o_grw_raw = o_b_raw + 2 * NH
    O_GQ = 3 * RW
    O_GRW = O_GQ + 3 * GW
    O_GGDN = O_GRW + D
    O_Z = O_GGDN + D
    O_LW = O_Z + GW
    O_LA = O_LW + LANES
    O_LG = O_LA + LANES
    O_BA = O_LG + LG
    END = O_BA + LANES
    NP = -(-END // 1024) * 1024
    XM = 3 * RW + 2 * LANES + LG
    assert O_GRW % D == 0 and O_Z % GW == 0 and O_LG % LG == 0 and (3 * RW + 2 * LANES) % LG == 0

    def pad_layout(raw, lead):
        return jnp.concatenate([
            raw[:, :3 * RW], raw[:, RC:RC + 3 * GW], raw[:, o_grw_raw:o_grw_raw + 2 * D], raw[:, o_z_raw:o_z_raw + GW],
            _pad_cols(raw[:, 3 * RW:3 * RW + LW], LANES), _pad_cols(raw[:, 3 * RW + LW:3 * RW + LW + LA], LANES),
            raw[:, 3 * RW + LW + LA:RC], _pad_cols(raw[:, o_b_raw:o_b_raw + 2 * NH], LANES),
            jnp.zeros((lead, NP - END), raw.dtype)], axis=1)

    def unpad_layout(pd):
        return jnp.concatenate([
            pd[:, :3 * RW], pd[:, O_LW:O_LW + LW], pd[:, O_LA:O_LA + LA], pd[:, O_LG:O_LG + LG],
            pd[:, O_GQ:O_GQ + 3 * GW], pd[:, O_Z:O_Z + GW], pd[:, O_BA:O_BA + 2 * NH], pd[:, O_GRW:O_GRW + 2 * D]], axis=1)

    def xm_layout(raw):
        return jnp.concatenate([raw[:, :3 * RW], _pad_cols(raw[:, 3 * RW:3 * RW + LW], LANES),
                                _pad_cols(raw[:, 3 * RW + LW:3 * RW + LW + LA], LANES), raw[:, 3 * RW + LW + LA:]], axis=1)

    def xm_unlayout(pd):
        return jnp.concatenate([pd[:, :3 * RW], pd[:, 3 * RW:3 * RW + LW], pd[:, 3 * RW + LANES:3 * RW + LANES + LA],
                                pd[:, 3 * RW + 2 * LANES:]], axis=1)

    w_p = pad_layout(full['w_in'], D)
    mix_x = xm_layout(W['rwkv_shift_mix'])
    w_up_p = jnp.pad(full['rwkv_w_up'], ((0, LANES - LW), (0, 0)))
    a_up_p = jnp.pad(full['rwkv_a_up'], ((0, LANES - LA), (0, 0)))
    g_up = full['rwkv_g_up']
    conv8 = jnp.pad(full['gdn_conv_w'], ((0, 4), (0, 0)))
    r_k = W['rwkv_r_k'].reshape(1, RW)
    alog_p = jnp.pad(W['gdn_a_log'], ((0, 0), (NH, LANES - 2 * NH)))
    dt_p = jnp.pad(W['gdn_dt_bias'], ((0, 0), (NH, LANES - 2 * NH)))
    nw_t = jnp.tile(W['gdn_norm_w'], (1, GW // GDN_HEAD))
    rwkv_par = [W['rwkv_w0'], W['rwkv_a0'], W['rwkv_k_k'], W['rwkv_k_a'], w_up_p, a_up_p, g_up]
    n_main, lora_cb = 3 * RW // LANES, O_LW // LANES
    f_gdn_pre = functools.partial(_f_gdn_pre, heads=NH)

    (u,) = _rows_fwd("rms_pre", _f_rms, [(xs, D, 0)], [W['mix_norm_pre']], [(D, BF16)], 256)
    p = _mm("mm_in", u, w_p)
    xm = _tokshift_fwd(p, mix_x, n_main, lora_cb)
    pre_rows = [(xm, RW, 0), (xm, RW, 1), (xm, RW, 2), (xm, LANES, n_main), (xm, LANES, n_main + 1),
                (xm, LG, (3 * RW + 2 * LANES) // LG)]
    lwd, k2, av, bv, gate = _rows_fwd("rwkv_pre", _f_rwkv_pre, pre_rows, rwkv_par,
                                      [None, (RW, F32), (RW, F32), None, (RW, F32), (RW, F32), (RW, F32)], 128)
    rec_rw = [(xm, 0), (lwd, 0), (k2, 0), (xm, 2 * RW // LANES), (av, 0), (bv, 0)]
    y_rec, s_rw = _rec_fwd("rwkv_rec_fwd", _rwkv_chunk, rec_rw, RW // LANES)
    post_rows = [(y_rec, RW, 0), (xm, RW, 0), (k2, RW, 0), (xm, RW, 2), (gate, RW, 0)]
    post_par = [r_k, W['rwkv_gn_w'], W['rwkv_gn_b']]
    (y_rw,) = _rows_fwd("rwkv_post", _f_rwkv_post, post_rows, post_par, [(RW, BF16)], 128)

    qkvc = _conv_fwd(p, conv8, O_GQ // LANES)
    gpre_rows = [(qkvc, GW, 0), (qkvc, GW, 1), (qkvc, GW, 2), (p, LANES, O_BA // LANES)]
    qn, kn, beta_x, g_x = _rows_fwd("gdn_pre", f_gdn_pre, gpre_rows, [alog_p, dt_p],
                                    [(GW, F32), (GW, F32), None, (GW, F32), (GW, F32)], 128)
    rec_gd = [(qn, 0), (kn, 0), (qkvc, 2 * GW // LANES), (beta_x, 0), (g_x, 0)]
    o_gd, s_gd = _rec_fwd("gdn_rec_fwd", _gdn_chunk, rec_gd, GW // LANES)
    gpost_rows = [(o_gd, GW, 0), (p, GW, O_Z // GW)]
    (y_gdn,) = _rows_fwd("gdn_post", _f_gdn_post, gpost_rows, [nw_t], [(GW, BF16)], 128)

    br_rw = _mm("mm_br_rw", y_rw, full['w_branch_rwkv'])
    br_gdn = _mm("mm_br_gdn", y_gdn, full['w_branch_gdn'])
    merge_rows = [(p, D, O_GRW // D), (p, D, O_GGDN // D), (br_rw, D, 0), (br_gdn, D, 0)]
    (merged,) = _rows_fwd("merge", _f_merge, merge_rows, [], [(D, BF16)], 128)
    ymix = _mm("mm_mix_out", merged, full['w_mix_out'])
    r1_par = [W['mix_norm_post'], W['xa_norm_pre']]
    h1, cq = _rows_fwd("resid1", _f_resid, [(xs, D, 0), (ymix, D, 0)], r1_par, [(D, F32), (D, BF16)], 128)
    q = _mm("mm_xa_q", cq, full['xa_w_q'])
    (mn,) = _rows_fwd("mem_norm", _f_rms, [(mems, D, 0)], [W['xa_norm_mem']], [(D, BF16)], 128)
    kv = _mm("mm_xa_kv", mn, full['xa_w_kv'])
    k_x, v_x = kv[:, :XW], kv[:, XW:]
    (o_x,) = _rows_fwd("xattn", _f_xattn, [(q, XW, 0)], [k_x, v_x], [(XW, BF16)], 256)
    xo = _mm("mm_xa_o", o_x, full['xa_w_o'])
    r2_par = [W['xa_norm_post'], W['mlp_norm_pre']]
    h2, fm = _rows_fwd("resid2", _f_resid, [(h1, D, 0), (xo, D, 0)], r2_par, [(D, F32), (D, BF16)], 128)
    up = _mm("mm_mlp_up", fm, full['mlp_w_up'])
    DFF = up.shape[1]
    (act,) = _rows_fwd("mlp_act", _f_act, [(up, DFF, 0)], [], [(DFF, BF16)], 64)
    down = _mm("mm_mlp_down", act, full['mlp_w_down'])

    G = {}
    (dh2, ddown), (G['mlp_norm_post'],), (loss_acc,) = _rows_bwd(
        "final", _f_final, [(h2, D, 0), (down, D, 0), (tgt, D, 0)], [W['mlp_norm_post']], [[]], 128,
        [(0, 0), (1, 0), None], [D, D], loss=True)
    d_act = _mm("mm_d_act", ddown, full['mlp_w_down'], tb=True)
    dw_down = _mm("mm_dw_down", act, ddown, ta=True)
    (d_up,), _, _ = _rows_bwd("mlp_act_bwd", _f_act, [(up, DFF, 0)], [], [[(d_act, DFF, 0)]], 32, [(0, 0)], [DFF])
    d_fm = _mm("mm_d_fm", d_up, full['mlp_w_up'], tb=True)
    dw_up = _mm("mm_dw_up", fm, d_up, ta=True)
    (dh1, dxo), (G['xa_norm_post'], G['mlp_norm_pre']), _ = _rows_bwd(
        "resid2_bwd", _f_resid, [(h1, D, 0), (xo, D, 0)], r2_par, [[(dh2, D, 0)], [(d_fm, D, 0)]], 128,
        [(0, 0), (1, 0)], [D, D])
    d_ox = _mm("mm_d_ox", dxo, full['xa_w_o'], tb=True)
    dw_o = _mm("mm_dw_o", o_x, dxo, ta=True)
    (dq,), (dk_x, dv_x), _ = _rows_bwd("xattn_bwd", _f_xattn, [(q, XW, 0)], [k_x, v_x], [[(d_ox, XW, 0)]], 128,
                                       [(0, 0)], [XW])
    d_cq = _mm("mm_d_cq", dq, full['xa_w_q'], tb=True)
    dw_q = _mm("mm_dw_q", cq, dq, ta=True)
    dkv = jnp.concatenate([dk_x, dv_x], axis=1)
    d_mn = _mm("mm_d_mn", dkv, full['xa_w_kv'], tb=True)
    dw_kv = _mm("mm_dw_kv", mn, dkv, ta=True)
    _, (G['xa_norm_mem'],), _ = _rows_bwd("mem_norm_bwd", _f_rms, [(mems, D, 0)], [W['xa_norm_mem']],
                                          [[(d_mn, D, 0)]], 128, [None], [])
    (dx_a, dymix), (G['mix_norm_post'], G['xa_norm_pre']), _ = _rows_bwd(
        "resid1_bwd", _f_resid, [(xs, D, 0), (ymix, D, 0)], r1_par, [[(dh1, D, 0)], [(d_cq, D, 0)]], 128,
        [(0, 0), (1, 0)], [D, D])
    d_merged = _mm("mm_d_merged", dymix, full['w_mix_out'], tb=True)
    dw_out = _mm("mm_dw_out", merged, dymix, ta=True)
    (d_grw, d_ggdn, d_brw, d_bgdn), _, _ = _rows_bwd(
        "merge_bwd", _f_merge, merge_rows, [], [[(d_merged, D, 0)]], 128, [(0, 0), (1, 0), (2, 0), (3, 0)], [D, D, D, D])
    d_yrw = _mm("mm_d_yrw", d_brw, full['w_branch_rwkv'], tb=True)
    dw_brw = _mm("mm_dw_brw", y_rw, d_brw, ta=True)
    d_ygdn = _mm("mm_d_ygdn", d_bgdn, full['w_branch_gdn'], tb=True)
    dw_bgdn = _mm("mm_dw_bgdn", y_gdn, d_bgdn, ta=True)

    (d_y, d_r1, d_k2a, d_v1, d_gate), (d_rk, G['rwkv_gn_w'], G['rwkv_gn_b']), _ = _rows_bwd(
        "rwkv_post_bwd", _f_rwkv_post, post_rows, post_par, [[(d_yrw, RW, 0)]], 64,
        [(0, 0), (1, 0), (2, 0), (3, 0), (4, 0)], [RW] * 5)
    d_r2, d_lwd, d_k2b, d_v2, d_av, d_bv = _rec_bwd("rwkv_rec_bwd", _rwkv_chunk, rec_rw, d_y, s_rw, RW // LANES)
    pre_cts = [[(d_r1, RW, 0), (d_r2, RW, 0)], [(d_lwd, RW, 0)], [(d_k2a, RW, 0), (d_k2b, RW, 0)],
               [(d_v1, RW, 0), (d_v2, RW, 0)], [(d_av, RW, 0)], [(d_bv, RW, 0)], [(d_gate, RW, 0)]]
    pre_out = [(0, 0), (0, RW), (0, 2 * RW), (0, 3 * RW), (0, 3 * RW + LANES), (0, 3 * RW + 2 * LANES)]
    (d_xm,), pre_pg, _ = _rows_bwd("rwkv_pre_bwd", _f_rwkv_pre, pre_rows, rwkv_par, pre_cts, 64, pre_out, [XM])
    G['rwkv_w0'], G['rwkv_a0'], G['rwkv_k_k'], G['rwkv_k_a'], d_wup_p, d_aup_p, d_gup = pre_pg
    d_prw, d_mix_x = _tokshift_bwd(d_xm, p, mix_x, n_main, lora_cb)

    (d_ogd, d_z), (d_nw_t,), _ = _rows_bwd("gdn_post_bwd", _f_gdn_post, gpost_rows, [nw_t], [[(d_ygdn, GW, 0)]], 64,
                                           [(0, 0), (1, 0)], [GW, GW])
    d_qn, d_kn, d_vg, d_betax, d_gx = _rec_bwd("gdn_rec_bwd", _gdn_chunk, rec_gd, d_ogd, s_gd, GW // LANES)
    gpre_cts = [[(d_qn, GW, 0)], [(d_kn, GW, 0)], [(d_vg, GW, 0)], [(d_betax, GW, 0)], [(d_gx, GW, 0)]]
    (d_qkvc, d_ba), (d_alog_p, d_dt_p), _ = _rows_bwd(
        "gdn_pre_bwd", f_gdn_pre, gpre_rows, [alog_p, dt_p], gpre_cts, 64,
        [(0, 0), (0, GW), (0, 2 * GW), (1, 0)], [3 * GW, LANES])
    d_pqkv, d_conv8 = _conv_bwd(d_qkvc, p, conv8, O_GQ // LANES)

    dp = jnp.concatenate([d_prw[:, :3 * RW], d_pqkv, d_grw, d_ggdn, d_z, d_prw[:, 3 * RW:], d_ba,
                          jnp.zeros((T, NP - END), F32)], axis=1)
    d_u = _mm("mm_d_u", dp, w_p, tb=True)
    dw_p = _mm("mm_dw_in", u, dp, ta=True)
    (grad_x,), (G['mix_norm_pre'],), _ = _rows_bwd(
        "rms_pre_bwd", _f_rms, [(xs, D, 0)], [W['mix_norm_pre']], [[(d_u, D, 0)]], 128, [(0, 0)], [D],
        adds=[(0, (dx_a, D, 0))])

    G['rwkv_shift_mix'] = xm_unlayout(d_mix_x)
    G['rwkv_r_k'] = d_rk
    G['gdn_a_log'] = d_alog_p[:, NH:2 * NH]
    G['gdn_dt_bias'] = d_dt_p[:, NH:2 * NH]
    G['gdn_norm_w'] = jnp.sum(d_nw_t.reshape(GW // GDN_HEAD, GDN_HEAD), axis=0, keepdims=True)

    G4 = {
        'w_in': _col_shards(unpad_layout(dw_p)),
        'rwkv_w_up': _col_shards(d_wup_p[:LW]), 'rwkv_a_up': _col_shards(d_aup_p[:LA]), 'rwkv_g_up': _col_shards(d_gup),
        'gdn_conv_w': _col_shards(d_conv8[:4]),
        'w_branch_rwkv': _col_shards(dw_brw), 'w_branch_gdn': _col_shards(dw_bgdn), 'xa_w_o': _col_shards(dw_o),
        'mlp_w_up': _col_shards(dw_up),
        'w_mix_out': dw_out.reshape(4, D // 4, D), 'xa_w_q': dw_q.reshape(4, D // 4, XW),
        'xa_w_kv': dw_kv.reshape(4, D // 4, 2 * XW), 'mlp_w_down': dw_down.reshape(4, DFF // 4, D),
    }

    res = {}
    for n in SHARDED:
        r4 = _exchange3("exchange_" + n, G4[n])
        part = _sum4("sum4_" + n, r4)
        other = _swap_sibling("swap_" + n, part)
        res[n] = _adam_big("adam_" + n, W[n], Mo[n], Vo[n], part, other)

    sizes = [-(-W[n].size // LANES) * LANES for n in REPLICATED]
    total = -(-sum(sizes) // (8 * LANES)) * (8 * LANES)

    def pack(vals):
        parts = [jnp.pad(vv.reshape(-1), (0, sz - vv.size)) for vv, sz in zip(vals, sizes)]
        flat = jnp.concatenate(parts)
        return jnp.pad(flat, (0, total - flat.shape[0])).reshape(8, total // 8)

    g8 = _gather8(pack([G[n] for n in REPLICATED]))
    small = _adam_small(pack([W[n] for n in REPLICATED]), pack([Mo[n] for n in REPLICATED]),
                        pack([Vo[n] for n in REPLICATED]), g8)
    small = [s.reshape(-1) for s in small]
    off = 0
    for n, sz in zip(REPLICATED, sizes):
        res[n] = [s[off:off + W[n].size].reshape(W[n].shape) for s in small]
        off += sz

    loss = lax.psum(loss_acc[0, 0], ("x", "y", "c"))
    outs = [loss, grad_x[None]]
    for j in range(4):
        outs += [res[n][j].reshape(given[n].shape) for n in WEIGHTS]
    return tuple(outs)
```

```python
import functools
import math

import jax
import jax.numpy as jnp
from jax import lax
from jax.experimental import pallas as pl
from jax.experimental.pallas import tpu as pltpu

F32 = jnp.float32
BF16 = jnp.bfloat16
MESH = pl.DeviceIdType.MESH

LANES = 128
VMEM_LIMIT = 56 * 1024 * 1024
CHUNK = 64
TIME_BLOCK = 1024
REC_GROUP = 2
REC_MODE = '1'
INV_MODE = '3'

NORM_EPS = 1e-6
L2_EPS = 1e-6
RWKV_GN_EPS = 64e-5
RWKV_HEAD = 64
GDN_HEAD = 128
XA_HEAD = 128
ADAM_LR, ADAM_B1, ADAM_B2, ADAM_EPS, ADAM_WD, ADAM_STEP = 0.001, 0.9, 0.999, 1e-08, 0.01, 10

WEIGHTS = ['mix_norm_pre', 'mix_norm_post', 'w_in', 'rwkv_shift_mix', 'rwkv_w0', 'rwkv_w_up', 'rwkv_a0', 'rwkv_a_up',
           'rwkv_g_up', 'rwkv_k_k', 'rwkv_k_a', 'rwkv_r_k', 'rwkv_gn_w', 'rwkv_gn_b', 'gdn_conv_w', 'gdn_a_log',
           'gdn_dt_bias', 'gdn_norm_w', 'w_branch_rwkv', 'w_branch_gdn', 'w_mix_out', 'xa_norm_pre', 'xa_norm_mem',
           'xa_norm_post', 'xa_w_q', 'xa_w_kv', 'xa_w_o', 'mlp_norm_pre', 'mlp_norm_post', 'mlp_w_up', 'mlp_w_down']
COL_SHARDED = ['w_in', 'rwkv_w_up', 'rwkv_a_up', 'rwkv_g_up', 'gdn_conv_w', 'w_branch_rwkv', 'w_branch_gdn', 'xa_w_o',
               'mlp_w_up']
ROW_SHARDED = ['w_mix_out', 'xa_w_q', 'xa_w_kv', 'mlp_w_down']
SHARDED = COL_SHARDED + ROW_SHARDED
REPLICATED = [n for n in WEIGHTS if n not in SHARDED]


def _pcall(body, **kw):
    return pl.pallas_call(body, **kw)


def _params(sem=None):
    return pltpu.CompilerParams(dimension_semantics=sem, vmem_limit_bytes=VMEM_LIMIT)


def _pick(n, pref, mult=LANES):
    t = (min(n, pref) // mult) * mult
    while t >= mult:
        if n % t == 0:
            return t
        t -= mult
    return n


def _dg(a, b, ca, cb):
    return lax.dot_general(a, b, (((ca,), (cb,)), ((), ())), preferred_element_type=F32)


def _split(x):
    hi = x.astype(BF16)
    lo = (x - hi.astype(F32)).astype(BF16)
    return hi, lo


def _dot_raw(a, b, ca, cb, mode):
    if mode == '1':
        return _dg(a.astype(BF16), b.astype(BF16), ca, cb)
    if mode == 'cb':
        ah, al = _split(a)
        bb = b.astype(BF16)
        return _dg(ah, bb, ca, cb) + _dg(al, bb, ca, cb)
    if mode == 'ca':
        bh, bl = _split(b)
        ab = a.astype(BF16)
        return _dg(ab, bh, ca, cb) + _dg(ab, bl, ca, cb)
    ah, al = _split(a)
    bh, bl = _split(b)
    return _dg(ah, bh, ca, cb) + (_dg(ah, bl, ca, cb) + _dg(al, bh, ca, cb))


@functools.partial(jax.custom_vjp, nondiff_argnums=(2, 3, 4))
def _dot_diff(a, b, ca, cb, mode):
    return _dot_raw(a, b, ca, cb, mode)


def _dot_diff_fwd(a, b, ca, cb, mode):
    return _dot_raw(a, b, ca, cb, mode), (a, b)


def _dot_diff_bwd(ca, cb, mode, res, ct):
    a, b = res
    if mode == 'ca':
        da = jnp.zeros_like(a)
    else:
        bmode = {'1': '1', '3': '3', 'cb': 'cb'}[mode]
        if ca == 1:
            da = _dot_raw(ct, b, 1, 1 if cb == 0 else 0, bmode)
        else:
            da = _dot_raw(b, ct, 1 if cb == 0 else 0, 1, 'ca' if bmode == 'cb' else bmode)
    if mode == 'cb':
        db = jnp.zeros_like(b)
    else:
        amode = {'1': '1', '3': '3', 'ca': 'ca'}[mode]
        if cb == 0:
            db = _dot_raw(a, ct, 0 if ca == 1 else 1, 0, amode)
        else:
            db = _dot_raw(ct, a, 0, 0 if ca == 1 else 1, 'cb' if amode == 'ca' else amode)
    return da, db


_dot_diff.defvjp(_dot_diff_fwd, _dot_diff_bwd)


class _Ops:
    def __init__(self, diff):
        self.diff = diff

    def dot(self, a, b, ca=1, cb=0, mode='1'):
        if self.diff:
            return _dot_diff(a, b, ca, cb, mode)
        return _dot_raw(a, b, ca, cb, mode)


_RAW = _Ops(False)
_DIFF = _Ops(True)


def _sigmoid(x):
    return 1.0 / (1.0 + jnp.exp(-x))


def _softplus(x):
    return jnp.maximum(x, 0.0) + jnp.log(1.0 + jnp.exp(-jnp.abs(x)))


def _rms(x, g, eps=NORM_EPS):
    return x * lax.rsqrt(jnp.mean(x * x, axis=-1, keepdims=True) + eps) * g


def _head_matrix(width, head):
    ch = lax.broadcasted_iota(jnp.int32, (width, LANES), 0)
    hh = lax.broadcasted_iota(jnp.int32, (width, LANES), 1)
    return (lax.shift_right_logical(ch, int(math.log2(head))) == hh).astype(BF16)


def _headsum(ops, x, head):
    e = _head_matrix(x.shape[-1], head)
    s = ops.dot(x, e, 1, 0, 'cb')
    return ops.dot(s, e, 1, 1, 'cb')


def _f_rms(ops, x, g):
    return (_rms(x, g),)


def _f_resid(ops, h, y, g_post, g_next):
    h1 = h + _rms(y, g_post)
    return h1, _rms(h1, g_next)


def _f_final(ops, h, y, tgt, g_post):
    e = h + _rms(y, g_post) - tgt
    return (0.5 * jnp.mean(e * e, axis=-1, keepdims=True),)


def _f_act(ops, up):
    r = jnp.maximum(up, 0.0)
    return (r * r,)


def _f_merge(ops, g_rw, g_gdn, b_rw, b_gdn):
    return (_sigmoid(g_rw) * b_rw + _sigmoid(g_gdn) * b_gdn,)


def _f_rwkv_pre(ops, xr, xk, xv, xlw, xla, xlg, w0, a0, k_k, k_a, w_up, a_up, g_up):
    z = w0 + ops.dot(jnp.tanh(xlw), w_up)
    lwd = -jnp.exp(-_softplus(-z) - 0.5)
    a = _sigmoid(a0 + ops.dot(xla, a_up))
    gate = ops.dot(_sigmoid(xlg), g_up)
    kk = xk * k_k
    kk = kk * lax.rsqrt(_headsum(ops, kk * kk, RWKV_HEAD) + L2_EPS)
    k2 = xk * (1.0 + (a - 1.0) * k_a)
    return xr, lwd, k2, xv, -kk, kk * a, gate


def _f_rwkv_post(ops, y, r, k2, v, gate, r_k, gn_w, gn_b):
    inv = 1.0 / RWKV_HEAD
    yc = y - _headsum(ops, y, RWKV_HEAD) * inv
    var = _headsum(ops, yc * yc, RWKV_HEAD) * inv
    yn = yc * lax.rsqrt(var + RWKV_GN_EPS) * gn_w + gn_b
    bonus = _headsum(ops, r * k2 * r_k, RWKV_HEAD) * v
    return ((yn + bonus) * gate,)


def _f_gdn_pre(ops, qc, kc, vc, ba, alog_p, dt_p, *, heads):
    lane = lax.broadcasted_iota(jnp.int32, (1, LANES), 1)
    beta = _sigmoid(ba)
    g = -jnp.exp(alog_p) * _softplus(ba + dt_p)
    gb = jnp.where(lane < heads, beta, g)
    width = qc.shape[-1]
    row = lax.broadcasted_iota(jnp.int32, (LANES, width), 0)
    col_head = lax.shift_right_logical(lax.broadcasted_iota(jnp.int32, (LANES, width), 1), int(math.log2(GDN_HEAD)))
    beta_x = ops.dot(gb, (row == col_head).astype(BF16), 1, 0, 'cb')
    g_x = ops.dot(gb, (row == col_head + heads).astype(BF16), 1, 0, 'cb')
    qn = qc * lax.rsqrt(_headsum(ops, qc * qc, GDN_HEAD) + L2_EPS)
    kn = kc * lax.rsqrt(_headsum(ops, kc * kc, GDN_HEAD) + L2_EPS)
    return qn, kn, vc, beta_x, g_x


def _f_gdn_post(ops, o, z, nw):
    ms = _headsum(ops, o * o, GDN_HEAD) * (1.0 / GDN_HEAD)
    return (o * lax.rsqrt(ms + NORM_EPS) * nw * (z * _sigmoid(z)),)


def _f_xattn(ops, q, k, v):
    width = q.shape[-1]
    lane = lax.broadcasted_iota(jnp.int32, (1, width), 1)
    out = jnp.zeros_like(q)
    for h in range(width // XA_HEAD):
        m = ((lane >= h * XA_HEAD) & (lane < (h + 1) * XA_HEAD)).astype(F32)
        s = ops.dot(q * m, k, 1, 1) * (XA_HEAD ** -0.5)
        s = s - jnp.max(s, axis=-1, keepdims=True)
        e = jnp.exp(s)
        pr = e / jnp.sum(e, axis=-1, keepdims=True)
        out = out + ops.dot(pr, v) * m
    return (out,)


def _inv_unit_lower(ops, a):
    n = a.shape[0]
    ii = lax.broadcasted_iota(jnp.int32, (n, n), 0)
    jj = lax.broadcasted_iota(jnp.int32, (n, n), 1)
    eye = (ii == jj).astype(F32)
    p = eye + a
    q = a
    k = 2
    while k < n:
        q = ops.dot(q, q, 1, 0, INV_MODE)
        p = p + ops.dot(p, q, 1, 0, INV_MODE)
        k *= 2
    return p


def _rwkv_chunk(ops, s, r, lw, k, v, a, b):
    c = r.shape[0]
    ii = lax.broadcasted_iota(jnp.int32, (c, c), 0)
    jj = lax.broadcasted_iota(jnp.int32, (c, c), 1)
    incl = ii >= jj
    strict = ii > jj
    cum = ops.dot(incl.astype(BF16), lw, 1, 0, 'ca')
    tot = ops.dot(jnp.ones((LANES, c), BF16), lw, 1, 0, 'ca')
    w_in = jnp.exp(cum)
    w_inv = jnp.exp(-cum)
    at = a * jnp.exp(cum - lw)
    bt = b * w_inv
    kt = k * w_inv
    rt = r * w_in
    lane = lax.broadcasted_iota(jnp.int32, (1, LANES), 1)
    masks = [(lane < RWKV_HEAD).astype(F32), (lane >= RWKV_HEAD).astype(F32)]
    base = ops.dot(at, s, 1, 1, REC_MODE)
    y = ops.dot(rt, s, 1, 1, REC_MODE)
    u = jnp.zeros_like(r)
    for m in masks:
        atm = at * m
        a_ab = jnp.where(strict, ops.dot(atm, bt, 1, 1, REC_MODE), 0.0)
        a_ak = jnp.where(strict, ops.dot(atm, kt, 1, 1, REC_MODE), 0.0)
        t = _inv_unit_lower(ops, a_ab)
        u = u + ops.dot(t, base + ops.dot(a_ak, v, 1, 0, REC_MODE), 1, 0, REC_MODE) * m
    for m in masks:
        rtm = rt * m
        a_rb = jnp.where(incl, ops.dot(rtm, bt, 1, 1, REC_MODE), 0.0)
        a_rk = jnp.where(incl, ops.dot(rtm, kt, 1, 1, REC_MODE), 0.0)
        y = y + (ops.dot(a_rb, u, 1, 0, REC_MODE) + ops.dot(a_rk, v, 1, 0, REC_MODE)) * m
    bi = lax.broadcasted_iota(jnp.int32, (LANES, LANES), 0) >= RWKV_HEAD
    bj = lax.broadcasted_iota(jnp.int32, (LANES, LANES), 1) >= RWKV_HEAD
    upd = jnp.where(bi == bj, ops.dot(u, bt, 0, 0, REC_MODE) + ops.dot(v, kt, 0, 0, REC_MODE), 0.0)
    return (s + upd) * jnp.exp(tot), y


def _gdn_chunk(ops, s, q, k, v, beta, g):
    c = q.shape[0]
    ii = lax.broadcasted_iota(jnp.int32, (c, c), 0)
    jj = lax.broadcasted_iota(jnp.int32, (c, c), 1)
    causal = ii >= jj
    strict = ii > jj
    q = q * (GDN_HEAD ** -0.5)
    gc = ops.dot(causal.astype(BF16), g, 1, 0, 'ca')
    ones = jnp.ones((c, LANES), BF16)
    gc_col = ops.dot(gc, ones, 1, 1, 'cb') * (1.0 / LANES)
    gc_row = ops.dot(ones, gc, 1, 1, 'ca') * (1.0 / LANES)
    decay = jnp.where(causal, jnp.exp(jnp.where(causal, gc_col - gc_row, 0.0)), 0.0)
    kb = k * beta
    lmat = jnp.where(strict, ops.dot(kb, k, 1, 1, REC_MODE) * decay, 0.0)
    tinv = _inv_unit_lower(ops, -lmat)
    egc = jnp.exp(gc)
    u = ops.dot(tinv, v * beta, 1, 0, REC_MODE)
    w = ops.dot(tinv, kb * egc, 1, 0, REC_MODE)
    a_intra = jnp.where(causal, ops.dot(q, k, 1, 1, REC_MODE) * decay, 0.0)
    g_tot = ops.dot(jnp.ones((c, c), BF16), g, 1, 0, 'ca')
    k_dec = k * jnp.exp(g_tot - gc)
    v_new = u - ops.dot(w, s, 1, 0, REC_MODE)
    o = ops.dot(q * egc, s, 1, 0, REC_MODE) + ops.dot(a_intra, v_new, 1, 0, REC_MODE)
    g_full = ops.dot(jnp.ones((LANES, c), BF16), g, 1, 0, 'ca')
    s_new = s * jnp.exp(g_full) + ops.dot(k_dec, v_new, 0, 0, REC_MODE)
    return s_new, o


def _mm(name, a, b, ta=False, tb=False, out_dtype=F32, tm=1024, tn=1024, tk=2048):
    m = a.shape[1] if ta else a.shape[0]
    kd = a.shape[0] if ta else a.shape[1]
    n = b.shape[0] if tb else b.shape[1]
    assert kd == (b.shape[1] if tb else b.shape[0]), (name, a.shape, b.shape)
    tm, tn, tk = _pick(m, tm), _pick(n, tn), _pick(kd, tk)
    nk = kd // tk

    def body(a_ref, b_ref, o_ref, acc):
        kk = pl.program_id(2)
        part = _dg(a_ref[...].astype(BF16), b_ref[...].astype(BF16), 0 if ta else 1, 1 if tb else 0)
        if nk == 1:
            o_ref[...] = part.astype(o_ref.dtype)
            return

        @pl.when(kk == 0)
        def _():
            acc[...] = part

        @pl.when((kk > 0) & (kk < nk - 1))
        def _():
            acc[...] += part

        @pl.when(kk == nk - 1)
        def _():
            o_ref[...] = (acc[...] + part).astype(o_ref.dtype)

    a_spec = pl.BlockSpec((tk, tm), lambda i, j, k: (k, i)) if ta else pl.BlockSpec((tm, tk), lambda i, j, k: (i, k))
    b_spec = pl.BlockSpec((tn, tk), lambda i, j, k: (j, k)) if tb else pl.BlockSpec((tk, tn), lambda i, j, k: (k, j))
    return _pcall(
        body, name=name, grid=(m // tm, n // tn, nk), in_specs=[a_spec, b_spec],
        out_specs=pl.BlockSpec((tm, tn), lambda i, j, k: (i, j)), out_shape=jax.ShapeDtypeStruct((m, n), out_dtype),
        scratch_shapes=[pltpu.VMEM((tm, tn), F32)],
        compiler_params=_params(("parallel", "parallel", "arbitrary")))(a, b)


def _row_spec(tm, width, cb):
    return pl.BlockSpec((tm, width), lambda i: (i, cb))


def _rows_fwd(name, f, rows, params, outs, tm):
    t = rows[0][0].shape[0]
    tm = min(tm, t)
    n_in = len(rows) + len(params)
    kept = [o for o in outs if o is not None]

    def body(*refs):
        vals = f(_RAW, *[r[...].astype(F32) for r in refs[:n_in]])
        o_refs = refs[n_in:]
        j = 0
        for val, o in zip(vals, outs):
            if o is not None:
                o_refs[j][...] = val.astype(o_refs[j].dtype)
                j += 1

    in_specs = [_row_spec(tm, w, cb) for (_, w, cb) in rows]
    in_specs += [pl.BlockSpec(p.shape, lambda i: (0, 0)) for p in params]
    res = _pcall(
        body, name=name, grid=(t // tm,), in_specs=in_specs,
        out_specs=[_row_spec(tm, w, 0) for (w, _) in kept],
        out_shape=[jax.ShapeDtypeStruct((t, w), dt) for (w, dt) in kept],
        compiler_params=_params(("parallel",)))(*[r[0] for r in rows], *params)
    return res


def _rows_bwd(name, f, rows, params, cts, tm, row_out, out_widths, adds=(), loss=False, out_dtypes=None):
    t = rows[0][0].shape[0]
    tm = min(tm, t)
    n_r, n_p = len(rows), len(params)
    ct_flat = [c for cl in cts for c in cl]
    n_ct, n_add = len(ct_flat), len(adds)
    n_out = len(out_widths)

    def body(*refs):
        i = pl.program_id(0)
        ins = [r[...].astype(F32) for r in refs[:n_r + n_p]]
        ct_refs = refs[n_r + n_p:n_r + n_p + n_ct]
        add_refs = refs[n_r + n_p + n_ct:n_r + n_p + n_ct + n_add]
        o_refs = refs[n_r + n_p + n_ct + n_add:]
        outs, vjp = jax.vjp(lambda *a: f(_DIFF, *a), *ins)
        ctv, pos = [], 0
        for o, cl in zip(outs, cts):
            if loss:
                ctv.append(jnp.ones_like(o))
            elif not cl:
                ctv.append(jnp.zeros_like(o))
            else:
                acc = ct_refs[pos][...].astype(F32)
                for q in range(1, len(cl)):
                    acc = acc + ct_refs[pos + q][...].astype(F32)
                ctv.append(acc)
            pos += len(cl)
        grads = vjp(tuple(ctv))
        for ri, ro in enumerate(row_out):
            if ro is not None:
                oi, off = ro
                val = grads[ri]
                for ai, (aoi, _) in enumerate(adds):
                    if aoi == oi:
                        val = val + add_refs[ai][...].astype(F32)
                o_refs[oi][:, off:off + val.shape[1]] = val.astype(o_refs[oi].dtype)
        p_refs = o_refs[n_out:n_out + n_p]

        @pl.when(i == 0)
        def _():
            for pr in p_refs:
                pr[...] = jnp.zeros_like(pr)
            if loss:
                o_refs[n_out + n_p][...] = jnp.zeros((8, LANES), F32)

        for pi, pr in enumerate(p_refs):
            pr[...] += grads[n_r + pi]
        if loss:
            o_refs[n_out + n_p][...] += jnp.sum(outs[0])

    in_specs = [_row_spec(tm, w, cb) for (_, w, cb) in rows]
    in_specs += [pl.BlockSpec(p.shape, lambda i: (0, 0)) for p in params]
    in_specs += [_row_spec(tm, w, cb) for (_, w, cb) in ct_flat]
    in_specs += [_row_spec(tm, w, cb) for (_, (_, w, cb)) in adds]
    out_specs = [_row_spec(tm, w, 0) for w in out_widths]
    out_specs += [pl.BlockSpec(p.shape, lambda i: (0, 0)) for p in params]
    out_dtypes = out_dtypes or [F32] * n_out
    out_shape = [jax.ShapeDtypeStruct((t, w), dt) for w, dt in zip(out_widths, out_dtypes)]
    out_shape += [jax.ShapeDtypeStruct(p.shape, F32) for p in params]
    if loss:
        out_specs.append(pl.BlockSpec((8, LANES), lambda i: (0, 0)))
        out_shape.append(jax.ShapeDtypeStruct((8, LANES), F32))
    res = _pcall(
        body, name=name, grid=(t // tm,), in_specs=in_specs, out_specs=out_specs, out_shape=out_shape,
        compiler_params=_params(("arbitrary",)))(
            *[r[0] for r in rows], *params, *[c[0] for c in ct_flat], *[a[1][0] for a in adds])
    return res[:n_out], res[n_out:n_out + n_p], res[n_out + n_p:]


def _shift_down(x, k):
    row = lax.broadcasted_iota(jnp.int32, x.shape, 0)
    return jnp.where(row >= k, pltpu.roll(x, k, 0), 0.0)


def _shift_up(x, k):
    t = x.shape[0]
    row = lax.broadcasted_iota(jnp.int32, x.shape, 0)
    return jnp.where(row < t - k, pltpu.roll(x, t - k, 0), 0.0)


def _tokshift_fwd(p, mix, n_main, lora_cb):
    t = p.shape[0]
    nblk = mix.shape[1] // LANES

    def src(j):
        return (0, jnp.where(j < n_main, j, j - n_main + lora_cb))

    def body(p_ref, mix_ref, o_ref):
        xv = p_ref[...]
        o_ref[...] = xv + (_shift_down(xv, 1) - xv) * mix_ref[...]

    return _pcall(
        body, name="tokshift_fwd", grid=(nblk,),
        in_specs=[pl.BlockSpec((t, LANES), src), pl.BlockSpec((1, LANES), lambda j: (0, j))],
        out_specs=pl.BlockSpec((t, LANES), lambda j: (0, j)),
        out_shape=jax.ShapeDtypeStruct((t, mix.shape[1]), F32), compiler_params=_params(("parallel",)))(p, mix)


def _tokshift_bwd(dxm, p, mix, n_main, lora_cb):
    t = p.shape[0]
    nblk = mix.shape[1] // LANES

    def src(j):
        return (0, jnp.where(j < n_main, j, j - n_main + lora_cb))

    def body(d_ref, p_ref, mix_ref, dp_ref, dmix_ref):
        xv, d, mx = p_ref[...], d_ref[...], mix_ref[...]
        dp_ref[...] = (d * (1.0 - mx) + _shift_up(d * mx, 1)).astype(dp_ref.dtype)
        dmix_ref[...] = jnp.sum(d * (_shift_down(xv, 1) - xv), axis=0, keepdims=True)

    return _pcall(
        body, name="tokshift_bwd", grid=(nblk,),
        in_specs=[pl.BlockSpec((t, LANES), lambda j: (0, j)), pl.BlockSpec((t, LANES), src),
                  pl.BlockSpec((1, LANES), lambda j: (0, j))],
        out_specs=[pl.BlockSpec((t, LANES), lambda j: (0, j)), pl.BlockSpec((1, LANES), lambda j: (0, j))],
        out_shape=[jax.ShapeDtypeStruct((t, mix.shape[1]), BF16), jax.ShapeDtypeStruct((1, mix.shape[1]), F32)],
        compiler_params=_params(("parallel",)))(dxm, p, mix)


def _conv_pre(xv, w):
    return (w[3:4, :] * xv + w[2:3, :] * _shift_down(xv, 1) + w[1:2, :] * _shift_down(xv, 2)
            + w[0:1, :] * _shift_down(xv, 3))


def _conv_fwd(p, w8, cb0):
    t = p.shape[0]
    width = w8.shape[1]

    def body(p_ref, w_ref, o_ref):
        c = _conv_pre(p_ref[...], w_ref[...])
        o_ref[...] = c * _sigmoid(c)

    return _pcall(
        body, name="conv_fwd", grid=(width // LANES,),
        in_specs=[pl.BlockSpec((t, LANES), lambda j: (0, j + cb0)), pl.BlockSpec((8, LANES), lambda j: (0, j))],
        out_specs=pl.BlockSpec((t, LANES), lambda j: (0, j)),
        out_shape=jax.ShapeDtypeStruct((t, width), F32), compiler_params=_params(("parallel",)))(p, w8)


def _conv_bwd(dy, p, w8, cb0):
    t = p.shape[0]
    width = w8.shape[1]

    def body(d_ref, p_ref, w_ref, dx_ref, dw_ref):
        xv, w = p_ref[...], w_ref[...]
        c = _conv_pre(xv, w)
        sg = _sigmoid(c)
        dc = d_ref[...] * (sg * (1.0 + c * (1.0 - sg)))
        dx_ref[...] = (w[3:4, :] * dc + w[2:3, :] * _shift_up(dc, 1) + w[1:2, :] * _shift_up(dc, 2)
                       + w[0:1, :] * _shift_up(dc, 3)).astype(dx_ref.dtype)
        row =lax.broadcasted_iota(jnp.int32, (8, LANES), 0)
        dw = jnp.zeros((8, LANES), F32)
        for j in range(4):
            sj = jnp.sum(dc * (xv if j == 3 else _shift_down(xv, 3 - j)), axis=0, keepdims=True)
            dw = dw + jnp.where(row == j, sj, 0.0)
        dw_ref[...] = dw

    return _pcall(
        body, name="conv_bwd", grid=(width // LANES,),
        in_specs=[pl.BlockSpec((t, LANES), lambda j: (0, j)), pl.BlockSpec((t, LANES), lambda j: (0, j + cb0)),
                  pl.BlockSpec((8, LANES), lambda j: (0, j))],
        out_specs=[pl.BlockSpec((t, LANES), lambda j: (0, j)), pl.BlockSpec((8, LANES), lambda j: (0, j))],
        out_shape=[jax.ShapeDtypeStruct((t, width), BF16), jax.ShapeDtypeStruct((8, width), F32)],
        compiler_params=_params(("parallel",)))(dy, p, w8)


def _rec_fwd(name, chunk_fn, ins, nblocks):
    t = ins[0][0].shape[0]
    tb = min(TIME_BLOCK, t)
    nt, ncb = t // tb, tb // CHUNK
    n_in = len(ins)
    grp = REC_GROUP if nblocks % REC_GROUP == 0 and all(c0 % REC_GROUP == 0 for (_, c0) in ins) else 1
    width = grp * LANES

    def body(*refs):
        in_refs, y_ref, sall_ref, s_scr = refs[:n_in], refs[n_in], refs[n_in + 1], refs[n_in + 2]

        @pl.when(pl.program_id(1) == 0)
        def _():
            s_scr[...] = jnp.zeros_like(s_scr)

        def step(ci, states):
            sl = pl.ds(pl.multiple_of(ci * CHUNK, CHUNK), CHUNK)
            lanes = [slice(gi * LANES, (gi + 1) * LANES) for gi in range(grp)]
            loaded = [[r[sl, ln] for r in in_refs] for ln in lanes]
            outs = [chunk_fn(_RAW, s, *vals) for s, vals in zip(states, loaded)]
            for gi in range(grp):
                sall_ref[gi, ci] = states[gi]
                y_ref[sl, lanes[gi]] = outs[gi][1]
            return tuple(o[0] for o in outs)

        final = lax.fori_loop(0, ncb, step, tuple(s_scr[gi] for gi in range(grp)))
        for gi in range(grp):
            s_scr[gi] = final[gi]

    in_specs = [pl.BlockSpec((tb, width), functools.partial(lambda h, tt, cb: (tt, cb + h), cb=c0 // grp))
                for (_, c0) in ins]
    return _pcall(
        body, name=name, grid=(nblocks // grp, nt), in_specs=in_specs,
        out_specs=[pl.BlockSpec((tb, width), lambda h, tt: (tt, h)),
                   pl.BlockSpec((grp, ncb, LANES, LANES), lambda h, tt: (h, tt, 0, 0))],
        out_shape=[jax.ShapeDtypeStruct((t, nblocks * LANES), F32),
                   jax.ShapeDtypeStruct((nblocks, t // CHUNK, LANES, LANES), F32)],
        scratch_shapes=[pltpu.VMEM((grp, LANES, LANES), F32)],
        compiler_params=_params(("parallel", "arbitrary")))(*[a for (a, _) in ins])


def _rec_bwd(name, chunk_fn, ins, dy, sall, nblocks):
    t = ins[0][0].shape[0]
    tb = min(TIME_BLOCK, t)
    nt, ncb = t // tb, tb // CHUNK
    n_in = len(ins)
    grp = REC_GROUP if nblocks % REC_GROUP == 0 and all(c0 % REC_GROUP == 0 for (_, c0) in ins) else 1
    width = grp * LANES

    def body(*refs):
        in_refs, dy_ref, sall_ref = refs[:n_in], refs[n_in], refs[n_in + 1]
        d_refs, ds_scr = refs[n_in + 2:2 * n_in + 2], refs[2 * n_in + 2]

        @pl.when(pl.program_id(1) == 0)
        def _():
            ds_scr[...] = jnp.zeros_like(ds_scr)

        def step(j, dstates):
            ci = ncb - 1 - j
            sl = pl.ds(pl.multiple_of(ci * CHUNK, CHUNK), CHUNK)
            lanes = [slice(gi * LANES, (gi + 1) * LANES) for gi in range(grp)]
            loaded = [[sall_ref[gi, ci]] + [r[sl, ln] for r in in_refs] + [dy_ref[sl, ln]]
                      for gi, ln in enumerate(lanes)]
            grads = []
            for ds, vals in zip(dstates, loaded):
                _, vjp = jax.vjp(lambda *a: chunk_fn(_DIFF, *a), *vals[:-1])
                grads.append(vjp((ds, vals[-1])))
            for gi in range(grp):
                for dr, gval in zip(d_refs, grads[gi][1:]):
                    dr[sl, lanes[gi]] = gval
            return tuple(g[0] for g in grads)

        final = lax.fori_loop(0, ncb, step, tuple(ds_scr[gi] for gi in range(grp)))
        for gi in range(grp):
            ds_scr[gi] = final[gi]

    in_specs = [pl.BlockSpec((tb, width), functools.partial(lambda h, tt, cb: (nt - 1 - tt, cb + h), cb=c0 // grp))
                for (_, c0) in ins]
    in_specs.append(pl.BlockSpec((tb, width), lambda h, tt: (nt - 1 - tt, h)))
    in_specs.append(pl.BlockSpec((grp, ncb, LANES, LANES), lambda h, tt: (h, nt - 1 - tt, 0, 0)))
    return _pcall(
        body, name=name, grid=(nblocks // grp, nt), in_specs=in_specs,
        out_specs=[pl.BlockSpec((tb, width), lambda h, tt: (nt - 1 - tt, h)) for _ in ins],
        out_shape=[jax.ShapeDtypeStruct((t, nblocks * LANES), F32) for _ in ins],
        scratch_shapes=[pltpu.VMEM((grp, LANES, LANES), F32)],
        compiler_params=_params(("parallel", "arbitrary")))(*[a for (a, _) in ins], dy, sall)


def _pick_rows(r, c):
    tr = max(8, ((1 << 18) // c) // 8 * 8)
    while tr > 8 and r % tr:
        tr -= 8
    return tr if r % tr == 0 else r


def _sum4(name, r4):
    _, r, c = r4.shape
    tr = _pick_rows(r, c)

    def body(r_ref, o_ref):
        o_ref[...] = ((r_ref[3].astype(F32) + r_ref[0].astype(F32)) + r_ref[1].astype(F32)) + r_ref[2].astype(F32)

    return _pcall(
        body, name=name, grid=(r // tr,), in_specs=[pl.BlockSpec((4, tr, c), lambda i: (0, i, 0))],
        out_specs=pl.BlockSpec((tr, c), lambda i: (i, 0)), out_shape=jax.ShapeDtypeStruct((r, c), F32),
        compiler_params=_params(("parallel",)))(r4)


def _adam_math(w, g, m, v):
    m = ADAM_B1 * m + (1.0 - ADAM_B1) * g
    v = ADAM_B2 * v + (1.0 - ADAM_B2) * (g * g)
    m_hat = m / (1.0 - ADAM_B1 ** ADAM_STEP)
    v_hat = v / (1.0 - ADAM_B2 ** ADAM_STEP)
    return -ADAM_LR * (m_hat / (jnp.sqrt(v_hat) + ADAM_EPS) + ADAM_WD * w), m, v


def _adam_big(name, w, m, v, ga, gb):
    r, c = w.shape
    tr = _pick_rows(r, c)

    def body(w_ref, m_ref, v_ref, ga_ref, gb_ref, g_ref, d_ref, mo_ref, vo_ref):
        g = ga_ref[...] + gb_ref[...]
        g_ref[...] = g
        d_ref[...], mo_ref[...], vo_ref[...] = _adam_math(w_ref[...], g, m_ref[...], v_ref[...])

    spec = pl.BlockSpec((tr, c), lambda i: (i, 0))
    return _pcall(
        body, name=name, grid=(r // tr,), in_specs=[spec] * 5, out_specs=[spec] * 4,
        out_shape=[jax.ShapeDtypeStruct((r, c), F32)] * 4, compiler_params=_params(("parallel",)))(w, m, v, ga, gb)


def _adam_small(w, m, v, g8):
    shape = w.shape

    def body(w_ref, m_ref, v_ref, g8_ref, g_ref, d_ref, mo_ref, vo_ref):
        g = g8_ref[0]
        for dev in range(1, 8):
            g = g + g8_ref[dev]
        g_ref[...] = g
        d_ref[...], mo_ref[...], vo_ref[...] = _adam_math(w_ref[...], g, m_ref[...], v_ref[...])

    return _pcall(
        body, name="adam_small", out_shape=[jax.ShapeDtypeStruct(shape, F32)] * 4,
        compiler_params=_params())(w, m, v, g8)


ANY = pl.BlockSpec(memory_space=pl.ANY)


def _place():
    x, y, c = lax.axis_index("x"), lax.axis_index("y"), lax.axis_index("c")
    return x, y, c, [(1 - x, y), (x, 1 - y), (1 - x, 1 - y)]


def _remote(src, dst, send_sems, recv_sems, k, dev):
    return pltpu.make_async_remote_copy(src_ref=src, dst_ref=dst, send_sem=send_sems.at[k], recv_sem=recv_sems.at[k],
                                        device_id=dev, device_id_type=MESH)


def _gather4(name, local):
    def body(src, out, send_sems, recv_sems, local_sem):
        x, y, c, chips = _place()
        me = 2 * x + y
        own = pltpu.make_async_copy(src, out.at[me], local_sem)
        own.start()
        sends = [_remote(src, out.at[me], send_sems, recv_sems, k, (px, py, c)) for k, (px, py) in enumerate(chips)]
        for cp in sends:
            cp.start()
        for k, (px, py) in enumerate(chips):
            _remote(src, out.at[2 * px + py], send_sems, recv_sems, k, (px, py, c)).wait_recv()
        for cp in sends:
            cp.wait_send()
        own.wait()

    return _pcall(
        body, name=name, in_specs=[ANY], out_specs=ANY,
        out_shape=jax.ShapeDtypeStruct((4,) + local.shape, local.dtype),
        scratch_shapes=[pltpu.SemaphoreType.DMA((3,)), pltpu.SemaphoreType.DMA((3,)), pltpu.SemaphoreType.DMA(())],
    )(local)


def _exchange3(name, g4):
    def body(src, out, send_sems, recv_sems, local_sem):
        x, y, c, chips = _place()
        me = 2 * x + y
        own = pltpu.make_async_copy(src.at[me], out.at[3], local_sem)
        own.start()
        sends = [_remote(src.at[2 * px + py], out.at[k], send_sems, recv_sems, k, (px, py, c))
                 for k, (px, py) in enumerate(chips)]
        for cp in sends:
            cp.start()
        for k, (px, py) in enumerate(chips):
            _remote(src.at[me], out.at[k], send_sems, recv_sems, k, (px, py, c)).wait_recv()
        for cp in sends:
            cp.wait_send()
        own.wait()

    return _pcall(
        body, name=name, in_specs=[ANY], out_specs=ANY, out_shape=jax.ShapeDtypeStruct(g4.shape, g4.dtype),
        scratch_shapes=[pltpu.SemaphoreType.DMA((3,)), pltpu.SemaphoreType.DMA((3,)), pltpu.SemaphoreType.DMA(())],
    )(g4)


def _swap_sibling(name, part):
    def body(src, out, send_sems, recv_sems):
        x, y, c, _ = _place()
        cp = _remote(src, out, send_sems, recv_sems, 0, (x, y, 1 - c))
        cp.start()
        cp.wait_recv()
        cp.wait_send()

    return _pcall(
        body, name=name, in_specs=[ANY], out_specs=ANY, out_shape=jax.ShapeDtypeStruct(part.shape, part.dtype),
        scratch_shapes=[pltpu.SemaphoreType.DMA((1,)), pltpu.SemaphoreType.DMA((1,))])(part)


def _gather8(flat):
    def body(src, out, send_sems, recv_sems, local_sem):
        x, y, c, _ = _place()
        own = pltpu.make_async_copy(src, out.at[4 * x + 2 * y + c], local_sem)
        own.start()
        peers = []
        for mask in range(1, 8):
            px = x ^ ((mask >> 2) & 1)
            py = y ^ ((mask >> 1) & 1)
            pc = c ^ (mask & 1)
            peers.append((px, py, pc))
        sends = [_remote(src, out.at[4 * x + 2 * y + c], send_sems, recv_sems, k, dev) for k, dev in enumerate(peers)]
        for cp in sends:
            cp.start()
        for k, (px, py, pc) in enumerate(peers):
            _remote(src, out.at[4 * px + 2 * py + pc], send_sems, recv_sems, k, (px, py, pc)).wait_recv()
        for cp in sends:
            cp.wait_send()
        own.wait()

    return _pcall(
        body, name="gather8_small", in_specs=[ANY], out_specs=ANY,
        out_shape=jax.ShapeDtypeStruct((8,) + flat.shape, flat.dtype),
        scratch_shapes=[pltpu.SemaphoreType.DMA((7,)), pltpu.SemaphoreType.DMA((7,)), pltpu.SemaphoreType.DMA(())],
    )(flat)


def _pad_cols(a, width):
    return jnp.pad(a, ((0, 0), (0, width - a.shape[1])))


def _col_shards(full):
    r, c = full.shape
    return full.reshape(r, 4, c // 4).transpose(1, 0, 2)


def kernel(x, mem, mix_norm_pre, mix_norm_post, w_in, rwkv_shift_mix, rwkv_w0, rwkv_w_up, rwkv_a0, rwkv_a_up, rwkv_g_up, rwkv_k_k, rwkv_k_a, rwkv_r_k, rwkv_gn_w, rwkv_gn_b, gdn_conv_w, gdn_a_log, gdn_dt_bias, gdn_norm_w, w_branch_rwkv, w_branch_gdn, w_mix_out, xa_norm_pre, xa_norm_mem, xa_norm_post, xa_w_q, xa_w_kv, xa_w_o, mlp_norm_pre, mlp_norm_post, mlp_w_up, mlp_w_down, loss_target, m_mix_norm_pre, m_mix_norm_post, m_w_in, m_rwkv_shift_mix, m_rwkv_w0, m_rwkv_w_up, m_rwkv_a0, m_rwkv_a_up, m_rwkv_g_up, m_rwkv_k_k, m_rwkv_k_a, m_rwkv_r_k, m_rwkv_gn_w, m_rwkv_gn_b, m_gdn_conv_w, m_gdn_a_log, m_gdn_dt_bias, m_gdn_norm_w, m_w_branch_rwkv, m_w_branch_gdn, m_w_mix_out, m_xa_norm_pre, m_xa_norm_mem, m_xa_norm_post, m_xa_w_q, m_xa_w_kv, m_xa_w_o, m_mlp_norm_pre, m_mlp_norm_post, m_mlp_w_up, m_mlp_w_down, v_mix_norm_pre, v_mix_norm_post, v_w_in, v_rwkv_shift_mix, v_rwkv_w0, v_rwkv_w_up, v_rwkv_a0, v_rwkv_a_up, v_rwkv_g_up, v_rwkv_k_k, v_rwkv_k_a, v_rwkv_r_k, v_rwkv_gn_w, v_rwkv_gn_b, v_gdn_conv_w, v_gdn_a_log, v_gdn_dt_bias, v_gdn_norm_w, v_w_branch_rwkv, v_w_branch_gdn, v_w_mix_out, v_xa_norm_pre, v_xa_norm_mem, v_xa_norm_post, v_xa_w_q, v_xa_w_kv, v_xa_w_o, v_mlp_norm_pre, v_mlp_norm_post, v_mlp_w_up, v_mlp_w_down):
    given = dict(locals())
    W = {n: given[n] if given[n].ndim == 2 else given[n][0] for n in WEIGHTS}
    Mo = {n: given['m_' + n].reshape(W[n].shape) for n in WEIGHTS}
    Vo = {n: given['v_' + n].reshape(W[n].shape) for n in WEIGHTS}
    xs, mems, tgt = x[0], mem[0], loss_target[0]
    T, D = xs.shape
    RW = W['rwkv_w0'].shape[-1]
    GW = W['gdn_conv_w'].shape[-1] * 4 // 3
    NH = W['gdn_a_log'].shape[-1]
    LW, LA, LG = W['rwkv_w_up'].shape[0], W['rwkv_a_up'].shape[0], W['rwkv_g_up'].shape[0]
    XW = W['xa_w_q'].shape[-1]
    assert LW <= LANES and LA <= LANES and LG % LANES == 0 and 2 * NH <= LANES

    full = {}
    for n in SHARDED:
        wl = W[n] if n == 'gdn_conv_w' else W[n].astype(BF16)
        g = _gather4("gather_" + n, wl)
        if n in COL_SHARDED:
            full[n] = jnp.concatenate([g[0], g[1], g[2], g[3]], axis=1)
        else:
            full[n] = g.reshape((4 * g.shape[1],) + g.shape[2:])

    RC = 3 * RW + LW + LA + LG
    o_z_raw = RC + 3 * GW
    o_b_raw = o_z_raw + GW
    o_grw_raw = o_b_raw + 2 * NH
    O_GQ = 3 * RW
    O_GRW = O_GQ + 3 * GW
    O_GGDN = O_GRW + D
    O_Z = O_GGDN + D
    O_LW = O_Z + GW
    O_LA = O_LW + LANES
    O_LG = O_LA + LANES
    O_BA = O_LG + LG
    END = O_BA + LANES
    NP = -(-END // 1024) * 1024
    XM = 3 * RW + 2 * LANES + LG
    assert O_GRW % D == 0 and O_Z % GW == 0 and O_LG % LG == 0 and (3 * RW + 2 * LANES) % LG == 0

    def pad_layout(raw, lead):
        return jnp.concatenate([
            raw[:, :3 * RW], raw[:, RC:RC + 3 * GW], raw[:, o_grw_raw:o_grw_raw + 2 * D], raw[:, o_z_raw:o_z_raw + GW],
            _pad_cols(raw[:, 3 * RW:3 * RW + LW], LANES), _pad_cols(raw[:, 3 * RW + LW:3 * RW + LW + LA], LANES),
            raw[:, 3 * RW + LW + LA:RC], _pad_cols(raw[:, o_b_raw:o_b_raw + 2 * NH], LANES),
            jnp.zeros((lead, NP - END), raw.dtype)], axis=1)

    def unpad_layout(pd):
        return jnp.concatenate([
            pd[:, :3 * RW], pd[:, O_LW:O_LW + LW], pd[:, O_LA:O_LA + LA], pd[:, O_LG:O_LG + LG],
            pd[:, O_GQ:O_GQ + 3 * GW], pd[:, O_Z:O_Z + GW], pd[:, O_BA:O_BA + 2 * NH], pd[:, O_GRW:O_GRW + 2 * D]], axis=1)

    def xm_layout(raw):
        return jnp.concatenate([raw[:, :3 * RW], _pad_cols(raw[:, 3 * RW:3 * RW + LW], LANES),
                                _pad_cols(raw[:, 3 * RW + LW:3 * RW + LW + LA], LANES), raw[:, 3 * RW + LW + LA:]], axis=1)

    def xm_unlayout(pd):
        return jnp.concatenate([pd[:, :3 * RW], pd[:, 3 * RW:3 * RW + LW], pd[:, 3 * RW + LANES:3 * RW + LANES + LA],
                                pd[:, 3 * RW + 2 * LANES:]], axis=1)

    w_p = pad_layout(full['w_in'], D)
    mix_x = xm_layout(W['rwkv_shift_mix'])
    w_up_p = jnp.pad(full['rwkv_w_up'], ((0, LANES - LW), (0, 0)))
    a_up_p = jnp.pad(full['rwkv_a_up'], ((0, LANES - LA), (0, 0)))
    g_up = full['rwkv_g_up']
    conv8 = jnp.pad(full['gdn_conv_w'], ((0, 4), (0, 0)))
    r_k = W['rwkv_r_k'].reshape(1, RW)
    alog_p = jnp.pad(W['gdn_a_log'], ((0, 0), (NH, LANES - 2 * NH)))
    dt_p = jnp.pad(W['gdn_dt_bias'], ((0, 0), (NH, LANES - 2 * NH)))
    nw_t = jnp.tile(W['gdn_norm_w'], (1, GW // GDN_HEAD))
    rwkv_par = [W['rwkv_w0'], W['rwkv_a0'], W['rwkv_k_k'], W['rwkv_k_a'], w_up_p, a_up_p, g_up]
    n_main, lora_cb = 3 * RW // LANES, O_LW // LANES
    f_gdn_pre = functools.partial(_f_gdn_pre, heads=NH)

    (u,) = _rows_fwd("rms_pre", _f_rms, [(xs, D, 0)], [W['mix_norm_pre']], [(D, BF16)], 256)
    p = _mm("mm_in", u, w_p)
    xm = _tokshift_fwd(p, mix_x, n_main, lora_cb)
    pre_rows = [(xm, RW, 0), (xm, RW, 1), (xm, RW, 2), (xm, LANES, n_main), (xm, LANES, n_main + 1),
                (xm, LG, (3 * RW + 2 * LANES) // LG)]
    lwd, k2, av, bv, gate = _rows_fwd("rwkv_pre", _f_rwkv_pre, pre_rows, rwkv_par,
                                      [None, (RW, F32), (RW, F32), None, (RW, F32), (RW, F32), (RW, F32)], 128)
    rec_rw = [(xm, 0), (lwd, 0), (k2, 0), (xm, 2 * RW // LANES), (av, 0), (bv, 0)]
    y_rec, s_rw = _rec_fwd("rwkv_rec_fwd", _rwkv_chunk, rec_rw, RW // LANES)
    post_rows = [(y_rec, RW, 0), (xm, RW, 0), (k2, RW, 0), (xm, RW, 2), (gate, RW, 0)]
    post_par = [r_k, W['rwkv_gn_w'], W['rwkv_gn_b']]
    (y_rw,) = _rows_fwd("rwkv_post", _f_rwkv_post, post_rows, post_par, [(RW, BF16)], 128)

    qkvc = _conv_fwd(p, conv8, O_GQ // LANES)
    gpre_rows = [(qkvc, GW, 0), (qkvc, GW, 1), (qkvc, GW, 2), (p, LANES, O_BA // LANES)]
    qn, kn, beta_x, g_x = _rows_fwd("gdn_pre", f_gdn_pre, gpre_rows, [alog_p, dt_p],
                                    [(GW, F32), (GW, F32), None, (GW, F32), (GW, F32)], 128)
    rec_gd = [(qn, 0), (kn, 0), (qkvc, 2 * GW // LANES), (beta_x, 0), (g_x, 0)]
    o_gd, s_gd = _rec_fwd("gdn_rec_fwd", _gdn_chunk, rec_gd, GW // LANES)
    gpost_rows = [(o_gd, GW, 0), (p, GW, O_Z // GW)]
    (y_gdn,) = _rows_fwd("gdn_post", _f_gdn_post, gpost_rows, [nw_t], [(GW, BF16)], 128)

    br_rw = _mm("mm_br_rw", y_rw, full['w_branch_rwkv'])
    br_gdn = _mm("mm_br_gdn", y_gdn, full['w_branch_gdn'])
    merge_rows = [(p, D, O_GRW // D), (p, D, O_GGDN // D), (br_rw, D, 0), (br_gdn, D, 0)]
    (merged,) = _rows_fwd("merge", _f_merge, merge_rows, [], [(D, BF16)], 128)
    ymix = _mm("mm_mix_out", merged, full['w_mix_out'])
    r1_par = [W['mix_norm_post'], W['xa_norm_pre']]
    h1, cq = _rows_fwd("resid1", _f_resid, [(xs, D, 0), (ymix, D, 0)], r1_par, [(D, F32), (D, BF16)], 128)
    q = _mm("mm_xa_q", cq, full['xa_w_q'])
    (mn,) = _rows_fwd("mem_norm", _f_rms, [(mems, D, 0)], [W['xa_norm_mem']], [(D, BF16)], 128)
    kv = _mm("mm_xa_kv", mn, full['xa_w_kv'])
    k_x, v_x = kv[:, :XW], kv[:, XW:]
    (o_x,) = _rows_fwd("xattn", _f_xattn, [(q, XW, 0)], [k_x, v_x], [(XW, BF16)], 256)
    xo = _mm("mm_xa_o", o_x, full['xa_w_o'])
    r2_par = [W['xa_norm_post'], W['mlp_norm_pre']]
    h2, fm = _rows_fwd("resid2", _f_resid, [(h1, D, 0), (xo, D, 0)], r2_par, [(D, F32), (D, BF16)], 128)
    up = _mm("mm_mlp_up", fm, full['mlp_w_up'])
    DFF = up.shape[1]
    (act,) = _rows_fwd("mlp_act", _f_act, [(up, DFF, 0)], [], [(DFF, BF16)], 64)
    down = _mm("mm_mlp_down", act, full['mlp_w_down'])

    G = {}
    (dh2, ddown), (G['mlp_norm_post'],), (loss_acc,) = _rows_bwd(
        "final", _f_final, [(h2, D, 0), (down, D, 0), (tgt, D, 0)], [W['mlp_norm_post']], [[]], 128,
        [(0, 0), (1, 0), None], [D, D], loss=True, out_dtypes=[F32, BF16])
    d_act = _mm("mm_d_act", ddown, full['mlp_w_down'], tb=True)
    dw_down = _mm("mm_dw_down", act, ddown, ta=True)
    (d_up,), _, _ = _rows_bwd("mlp_act_bwd", _f_act, [(up, DFF, 0)], [], [[(d_act, DFF, 0)]], 32, [(0, 0)], [DFF],
                              out_dtypes=[BF16])
    d_fm = _mm("mm_d_fm", d_up, full['mlp_w_up'], tb=True)
    dw_up = _mm("mm_dw_up", fm, d_up, ta=True)
    (dh1, dxo), (G['xa_norm_post'], G['mlp_norm_pre']), _ = _rows_bwd(
        "resid2_bwd", _f_resid, [(h1, D, 0), (xo, D, 0)], r2_par, [[(dh2, D, 0)], [(d_fm, D, 0)]], 128,
        [(0, 0), (1, 0)], [D, D], out_dtypes=[F32, BF16])
    d_ox = _mm("mm_d_ox", dxo, full['xa_w_o'], tb=True)
    dw_o = _mm("mm_dw_o", o_x, dxo, ta=True)
    (dq,), (dk_x, dv_x), _ = _rows_bwd("xattn_bwd", _f_xattn, [(q, XW, 0)], [k_x, v_x], [[(d_ox, XW, 0)]], 128,
                                       [(0, 0)], [XW], out_dtypes=[BF16])
    d_cq = _mm("mm_d_cq", dq, full['xa_w_q'], tb=True)
    dw_q = _mm("mm_dw_q", cq, dq, ta=True)
    dkv = jnp.concatenate([dk_x, dv_x], axis=1)
    d_mn = _mm("mm_d_mn", dkv, full['xa_w_kv'], tb=True)
    dw_kv = _mm("mm_dw_kv", mn, dkv, ta=True)
    _, (G['xa_norm_mem'],), _ = _rows_bwd("mem_norm_bwd", _f_rms, [(mems, D, 0)], [W['xa_norm_mem']],
                                          [[(d_mn, D, 0)]], 128, [None], [])
    (dx_a, dymix), (G['mix_norm_post'], G['xa_norm_pre']), _ = _rows_bwd(
        "resid1_bwd", _f_resid, [(xs, D, 0), (ymix, D, 0)], r1_par, [[(dh1, D, 0)], [(d_cq, D, 0)]], 128,
        [(0, 0), (1, 0)], [D, D], out_dtypes=[F32, BF16])
    d_merged = _mm("mm_d_merged", dymix, full['w_mix_out'], tb=True)
    dw_out = _mm("mm_dw_out", merged, dymix, ta=True)
    (d_grw, d_ggdn, d_brw, d_bgdn), _, _ = _rows_bwd(
        "merge_bwd", _f_merge, merge_rows, [], [[(d_merged, D, 0)]], 128, [(0, 0), (1, 0), (2, 0), (3, 0)], [D, D, D, D],
        out_dtypes=[BF16] * 4)
    d_yrw = _mm("mm_d_yrw", d_brw, full['w_branch_rwkv'], tb=True)
    dw_brw = _mm("mm_dw_brw", y_rw, d_brw, ta=True)
    d_ygdn = _mm("mm_d_ygdn", d_bgdn, full['w_branch_gdn'], tb=True)
    dw_bgdn = _mm("mm_dw_bgdn", y_gdn, d_bgdn, ta=True)

    (d_y, d_r1, d_k2a, d_v1, d_gate), (d_rk, G['rwkv_gn_w'], G['rwkv_gn_b']), _ = _rows_bwd(
        "rwkv_post_bwd", _f_rwkv_post, post_rows, post_par, [[(d_yrw, RW, 0)]], 64,
        [(0, 0), (1, 0), (2, 0), (3, 0), (4, 0)], [RW] * 5)
    d_r2, d_lwd, d_k2b, d_v2, d_av, d_bv = _rec_bwd("rwkv_rec_bwd", _rwkv_chunk, rec_rw, d_y, s_rw, RW // LANES)
    pre_cts = [[(d_r1, RW, 0), (d_r2, RW, 0)], [(d_lwd, RW, 0)], [(d_k2a, RW, 0), (d_k2b, RW, 0)],
               [(d_v1, RW, 0), (d_v2, RW, 0)], [(d_av, RW, 0)], [(d_bv, RW, 0)], [(d_gate, RW, 0)]]
    pre_out = [(0, 0), (0, RW), (0, 2 * RW), (0, 3 * RW), (0, 3 * RW + LANES), (0, 3 * RW + 2 * LANES)]
    (d_xm,), pre_pg, _ = _rows_bwd("rwkv_pre_bwd", _f_rwkv_pre, pre_rows, rwkv_par, pre_cts, 64, pre_out, [XM])
    G['rwkv_w0'], G['rwkv_a0'], G['rwkv_k_k'], G['rwkv_k_a'], d_wup_p, d_aup_p, d_gup = pre_pg
    d_prw, d_mix_x = _tokshift_bwd(d_xm, p, mix_x, n_main, lora_cb)

    (d_ogd, d_z), (d_nw_t,), _ = _rows_bwd("gdn_post_bwd", _f_gdn_post, gpost_rows, [nw_t], [[(d_ygdn, GW, 0)]], 64,
                                           [(0, 0), (1, 0)], [GW, GW], out_dtypes=[F32, BF16])
    d_qn, d_kn, d_vg, d_betax, d_gx = _rec_bwd("gdn_rec_bwd", _gdn_chunk, rec_gd, d_ogd, s_gd, GW // LANES)
    gpre_cts = [[(d_qn, GW, 0)], [(d_kn, GW, 0)], [(d_vg, GW, 0)], [(d_betax, GW, 0)], [(d_gx, GW, 0)]]
    (d_qkvc, d_ba), (d_alog_p, d_dt_p), _ = _rows_bwd(
        "gdn_pre_bwd", f_gdn_pre, gpre_rows, [alog_p, dt_p], gpre_cts, 64,
        [(0, 0), (0, GW), (0, 2 * GW), (1, 0)], [3 * GW, LANES], out_dtypes=[F32, BF16])
    d_pqkv, d_conv8 = _conv_bwd(d_qkvc, p, conv8, O_GQ // LANES)

    dp = jnp.concatenate([d_prw[:, :3 * RW], d_pqkv, d_grw, d_ggdn, d_z, d_prw[:, 3 * RW:], d_ba,
                          jnp.zeros((T, NP - END), BF16)], axis=1)
    d_u = _mm("mm_d_u", dp, w_p, tb=True)
    dw_p = _mm("mm_dw_in", u, dp, ta=True)
    (grad_x,), (G['mix_norm_pre'],), _ = _rows_bwd(
        "rms_pre_bwd", _f_rms, [(xs, D, 0)], [W['mix_norm_pre']], [[(d_u, D, 0)]], 128, [(0, 0)], [D],
        adds=[(0, (dx_a, D, 0))])

    G['rwkv_shift_mix'] = xm_unlayout(d_mix_x)
    G['rwkv_r_k'] = d_rk
    G['gdn_a_log'] = d_alog_p[:, NH:2 * NH]
    G['gdn_dt_bias'] = d_dt_p[:, NH:2 * NH]
    G['gdn_norm_w'] = jnp.sum(d_nw_t.reshape(GW // GDN_HEAD, GDN_HEAD), axis=0, keepdims=True)

    G4 = {
        'w_in': _col_shards(unpad_layout(dw_p)),
        'rwkv_w_up': _col_shards(d_wup_p[:LW]), 'rwkv_a_up': _col_shards(d_aup_p[:LA]), 'rwkv_g_up': _col_shards(d_gup),
        'gdn_conv_w': _col_shards(d_conv8[:4]),
        'w_branch_rwkv': _col_shards(dw_brw), 'w_branch_gdn': _col_shards(dw_bgdn), 'xa_w_o': _col_shards(dw_o),
        'mlp_w_up': _col_shards(dw_up),
        'w_mix_out': dw_out.reshape(4, D // 4, D), 'xa_w_q': dw_q.reshape(4, D // 4, XW),
        'xa_w_kv': dw_kv.reshape(4, D // 4, 2 * XW), 'mlp_w_down': dw_down.reshape(4, DFF // 4, D),
    }

    res = {}
    for n in SHARDED:
        r4 = _exchange3("exchange_" + n, G4[n].astype(BF16))
        part = _sum4("sum4_" + n, r4)
        other = _swap_sibling("swap_" + n, part)
        res[n] = _adam_big("adam_" + n, W[n], Mo[n], Vo[n], part, other)

    sizes = [-(-W[n].size // LANES) * LANES for n in REPLICATED]
    total = -(-sum(sizes) // (8 * LANES)) * (8 * LANES)

    def pack(vals):
        parts = [jnp.pad(vv.reshape(-1), (0, sz - vv.size)) for vv, sz in zip(vals, sizes)]
        flat = jnp.concatenate(parts)
        return jnp.pad(flat, (0, total - flat.shape[0])).reshape(8, total // 8)

    g8 = _gather8(pack([G[n] for n in REPLICATED]))
    small = _adam_small(pack([W[n] for n in REPLICATED]), pack([Mo[n] for n in REPLICATED]),
                        pack([Vo[n] for n in REPLICATED]), g8)
    small = [s.reshape(-1) for s in small]
    off = 0
    for n, sz in zip(REPLICATED, sizes):
        res[n] = [s[off:off + W[n].size].reshape(W[n].shape) for s in small]
        off += sz

    loss = lax.psum(loss_acc[0, 0], ("x", "y", "c"))
    outs = [loss, grad_x[None]]
    for j in range(4):
        outs += [res[n][j].reshape(given[n].shape) for n in WEIGHTS]
    return tuple(outs)
```

```python
import functools
import math

import jax
import jax.numpy as jnp
from jax import lax
from jax.experimental import pallas as pl
from jax.experimental.pallas import tpu as pltpu

F32 = jnp.float32
BF16 = jnp.bfloat16
MESH = pl.DeviceIdType.MESH

LANES = 128
VMEM_LIMIT = 56 * 1024 * 1024
CHUNK = 64
TIME_BLOCK = 1024
REC_GROUP = 2
REC_MODE = '1'
INV_MODE = '3'

NORM_EPS = 1e-6
L2_EPS = 1e-6
RWKV_GN_EPS = 64e-5
RWKV_HEAD = 64
GDN_HEAD = 128
XA_HEAD = 128
ADAM_LR, ADAM_B1, ADAM_B2, ADAM_EPS, ADAM_WD, ADAM_STEP = 0.001, 0.9, 0.999, 1e-08, 0.01, 10

WEIGHTS = ['mix_norm_pre', 'mix_norm_post', 'w_in', 'rwkv_shift_mix', 'rwkv_w0', 'rwkv_w_up', 'rwkv_a0', 'rwkv_a_up',
           'rwkv_g_up', 'rwkv_k_k', 'rwkv_k_a', 'rwkv_r_k', 'rwkv_gn_w', 'rwkv_gn_b', 'gdn_conv_w', 'gdn_a_log',
           'gdn_dt_bias', 'gdn_norm_w', 'w_branch_rwkv', 'w_branch_gdn', 'w_mix_out', 'xa_norm_pre', 'xa_norm_mem',
           'xa_norm_post', 'xa_w_q', 'xa_w_kv', 'xa_w_o', 'mlp_norm_pre', 'mlp_norm_post', 'mlp_w_up', 'mlp_w_down']
COL_SHARDED = ['w_in', 'rwkv_w_up', 'rwkv_a_up', 'rwkv_g_up', 'gdn_conv_w', 'w_branch_rwkv', 'w_branch_gdn', 'xa_w_o',
               'mlp_w_up']
ROW_SHARDED = ['w_mix_out', 'xa_w_q', 'xa_w_kv', 'mlp_w_down']
SHARDED = COL_SHARDED + ROW_SHARDED
SMALL_SHARDED = ['rwkv_w_up', 'rwkv_a_up', 'rwkv_g_up', 'gdn_conv_w']
BIG_SHARDED = [n for n in SHARDED if n not in SMALL_SHARDED]
REPLICATED = [n for n in WEIGHTS if n not in SHARDED]


ANY = pl.BlockSpec(memory_space=pl.ANY)
_PENDING = []


def _pcall(body, **kw):
    deps = tuple(_PENDING) if "in_specs" in kw else ()
    if not deps:
        return pl.pallas_call(body, **kw)
    del _PENDING[:]
    n_in = len(kw["in_specs"])
    kw["in_specs"] = list(kw["in_specs"]) + [ANY] * len(deps)

    def with_deps(*refs):
        return body(*refs[:n_in], *refs[n_in + len(deps):])

    call = pl.pallas_call(with_deps, **kw)
    return lambda *args: call(*args, *deps)


def _params(sem=None):
    return pltpu.CompilerParams(dimension_semantics=sem, vmem_limit_bytes=VMEM_LIMIT)


def _pick(n, pref, mult=LANES):
    t = (min(n, pref) // mult) * mult
    while t >= mult:
        if n % t == 0:
            return t
        t -= mult
    return n


def _dg(a, b, ca, cb):
    return lax.dot_general(a, b, (((ca,), (cb,)), ((), ())), preferred_element_type=F32)


def _split(x):
    hi = x.astype(BF16)
    lo = (x - hi.astype(F32)).astype(BF16)
    return hi, lo


def _dot_raw(a, b, ca, cb, mode):
    if mode == '1':
        return _dg(a.astype(BF16), b.astype(BF16), ca, cb)
    if mode == 'cb':
        ah, al = _split(a)
        bb = b.astype(BF16)
        return _dg(ah, bb, ca, cb) + _dg(al, bb, ca, cb)
    if mode == 'ca':
        bh, bl = _split(b)
        ab = a.astype(BF16)
        return _dg(ab, bh, ca, cb) + _dg(ab, bl, ca, cb)
    ah, al = _split(a)
    bh, bl = _split(b)
    return _dg(ah, bh, ca, cb) + (_dg(ah, bl, ca, cb) + _dg(al, bh, ca, cb))


@functools.partial(jax.custom_vjp, nondiff_argnums=(2, 3, 4))
def _dot_diff(a, b, ca, cb, mode):
    return _dot_raw(a, b, ca, cb, mode)


def _dot_diff_fwd(a, b, ca, cb, mode):
    return _dot_raw(a, b, ca, cb, mode), (a, b)


def _dot_diff_bwd(ca, cb, mode, res, ct):
    a, b = res
    if mode == 'ca':
        da = jnp.zeros_like(a)
    else:
        bmode = {'1': '1', '3': '3', 'cb': 'cb'}[mode]
        if ca == 1:
            da = _dot_raw(ct, b, 1, 1 if cb == 0 else 0, bmode)
        else:
            da = _dot_raw(b, ct, 1 if cb == 0 else 0, 1, 'ca' if bmode == 'cb' else bmode)
    if mode == 'cb':
        db = jnp.zeros_like(b)
    else:
        amode = {'1': '1', '3': '3', 'ca': 'ca'}[mode]
        if cb == 0:
            db = _dot_raw(a, ct, 0 if ca == 1 else 1, 0, amode)
        else:
            db = _dot_raw(ct, a, 0, 0 if ca == 1 else 1, 'cb' if amode == 'ca' else amode)
    return da, db


_dot_diff.defvjp(_dot_diff_fwd, _dot_diff_bwd)


class _Ops:
    def __init__(self, diff):
        self.diff = diff

    def dot(self, a, b, ca=1, cb=0, mode='1'):
        if self.diff:
            return _dot_diff(a, b, ca, cb, mode)
        return _dot_raw(a, b, ca, cb, mode)


_RAW = _Ops(False)
_DIFF = _Ops(True)


def _sigmoid(x):
    return 1.0 / (1.0 + jnp.exp(-x))


def _softplus(x):
    return jnp.maximum(x, 0.0) + jnp.log(1.0 + jnp.exp(-jnp.abs(x)))


def _rms(x, g, eps=NORM_EPS):
    return x * lax.rsqrt(jnp.mean(x * x, axis=-1, keepdims=True) + eps) * g


def _head_matrix(width, head):
    ch = lax.broadcasted_iota(jnp.int32, (width, LANES), 0)
    hh = lax.broadcasted_iota(jnp.int32, (width, LANES), 1)
    return (lax.shift_right_logical(ch, int(math.log2(head))) == hh).astype(BF16)


def _headsum(ops, x, head):
    e = _head_matrix(x.shape[-1], head)
    s = ops.dot(x, e, 1, 0, 'cb')
    return ops.dot(s, e, 1, 1, 'cb')


def _f_rms(ops, x, g):
    return (_rms(x, g),)


def _f_resid(ops, h, y, g_post, g_next):
    h1 = h + _rms(y, g_post)
    return h1, _rms(h1, g_next)


def _f_final(ops, h, y, tgt, g_post):
    e = h + _rms(y, g_post) - tgt
    return (0.5 * jnp.mean(e * e, axis=-1, keepdims=True),)


def _f_act(ops, up):
    r = jnp.maximum(up, 0.0)
    return (r * r,)


def _f_merge(ops, g_rw, g_gdn, b_rw, b_gdn):
    return (_sigmoid(g_rw) * b_rw + _sigmoid(g_gdn) * b_gdn,)


def _f_rwkv_pre(ops, xr, xk, xv, xlw, xla, xlg, w0, a0, k_k, k_a, w_up, a_up, g_up):
    z = w0 + ops.dot(jnp.tanh(xlw), w_up)
    lwd = -jnp.exp(-_softplus(-z) - 0.5)
    a = _sigmoid(a0 + ops.dot(xla, a_up))
    gate = ops.dot(_sigmoid(xlg), g_up)
    kk = xk * k_k
    kk = kk * lax.rsqrt(_headsum(ops, kk * kk, RWKV_HEAD) + L2_EPS)
    k2 = xk * (1.0 + (a - 1.0) * k_a)
    return xr, lwd, k2, xv, -kk, kk * a, gate


def _f_rwkv_post(ops, y, r, k2, v, gate, r_k, gn_w, gn_b):
    inv = 1.0 / RWKV_HEAD
    yc = y - _headsum(ops, y, RWKV_HEAD) * inv
    var = _headsum(ops, yc * yc, RWKV_HEAD) * inv
    yn = yc * lax.rsqrt(var + RWKV_GN_EPS) * gn_w + gn_b
    bonus = _headsum(ops, r * k2 * r_k, RWKV_HEAD) * v
    return ((yn + bonus) * gate,)


def _f_gdn_pre(ops, qc, kc, vc, ba, alog_p, dt_p, *, heads):
    lane = lax.broadcasted_iota(jnp.int32, (1, LANES), 1)
    beta = _sigmoid(ba)
    g = -jnp.exp(alog_p) * _softplus(ba + dt_p)
    gb = jnp.where(lane < heads, beta, g)
    width = qc.shape[-1]
    row = lax.broadcasted_iota(jnp.int32, (LANES, width), 0)
    col_head = lax.shift_right_logical(lax.broadcasted_iota(jnp.int32, (LANES, width), 1), int(math.log2(GDN_HEAD)))
    beta_x = ops.dot(gb, (row == col_head).astype(BF16), 1, 0, 'cb')
    g_x = ops.dot(gb, (row == col_head + heads).astype(BF16), 1, 0, 'cb')
    qn = qc * lax.rsqrt(_headsum(ops, qc * qc, GDN_HEAD) + L2_EPS)
    kn = kc * lax.rsqrt(_headsum(ops, kc * kc, GDN_HEAD) + L2_EPS)
    return qn, kn, vc, beta_x, g_x


def _f_gdn_post(ops, o, z, nw):
    ms = _headsum(ops, o * o, GDN_HEAD) * (1.0 / GDN_HEAD)
    return (o * lax.rsqrt(ms + NORM_EPS) * nw * (z * _sigmoid(z)),)


def _f_xattn(ops, q, k, v):
    width = q.shape[-1]
    lane = lax.broadcasted_iota(jnp.int32, (1, width), 1)
    out = jnp.zeros_like(q)
    for h in range(width // XA_HEAD):
        m = ((lane >= h * XA_HEAD) & (lane < (h + 1) * XA_HEAD)).astype(F32)
        s = ops.dot(q * m, k, 1, 1) * (XA_HEAD ** -0.5)
        s = s - jnp.max(s, axis=-1, keepdims=True)
        e = jnp.exp(s)
        pr = e / jnp.sum(e, axis=-1, keepdims=True)
        out = out + ops.dot(pr, v) * m
    return (out,)


def _inv_unit_lower(ops, a):
    n = a.shape[0]
    ii = lax.broadcasted_iota(jnp.int32, (n, n), 0)
    jj = lax.broadcasted_iota(jnp.int32, (n, n), 1)
    eye = (ii == jj).astype(F32)
    p = eye + a
    q = a
    k = 2
    while k < n:
        q = ops.dot(q, q, 1, 0, INV_MODE)
        p = p + ops.dot(p, q, 1, 0, INV_MODE)
        k *= 2
    return p


def _rwkv_chunk(ops, s, r, lw, k, v, a, b):
    c = r.shape[0]
    ii = lax.broadcasted_iota(jnp.int32, (c, c), 0)
    jj = lax.broadcasted_iota(jnp.int32, (c, c), 1)
    incl = ii >= jj
    strict = ii > jj
    cum = ops.dot(incl.astype(BF16), lw, 1, 0, 'ca')
    tot = ops.dot(jnp.ones((LANES, c), BF16), lw, 1, 0, 'ca')
    w_in = jnp.exp(cum)
    w_inv = jnp.exp(-cum)
    at = a * jnp.exp(cum - lw)
    bt = b * w_inv
    kt = k * w_inv
    rt = r * w_in
    lane = lax.broadcasted_iota(jnp.int32, (1, LANES), 1)
    masks = [(lane < RWKV_HEAD).astype(F32), (lane >= RWKV_HEAD).astype(F32)]
    base = ops.dot(at, s, 1, 1, REC_MODE)
    y = ops.dot(rt, s, 1, 1, REC_MODE)
    u = jnp.zeros_like(r)
    for m in masks:
        atm = at * m
        a_ab = jnp.where(strict, ops.dot(atm, bt, 1, 1, REC_MODE), 0.0)
        a_ak = jnp.where(strict, ops.dot(atm, kt, 1, 1, REC_MODE), 0.0)
        t = _inv_unit_lower(ops, a_ab)
        u = u + ops.dot(t, base + ops.dot(a_ak, v, 1, 0, REC_MODE), 1, 0, REC_MODE) * m
    for m in masks:
        rtm = rt * m
        a_rb = jnp.where(incl, ops.dot(rtm, bt, 1, 1, REC_MODE), 0.0)
        a_rk = jnp.where(incl, ops.dot(rtm, kt, 1, 1, REC_MODE), 0.0)
        y = y + (ops.dot(a_rb, u, 1, 0, REC_MODE) + ops.dot(a_rk, v, 1, 0, REC_MODE)) * m
    bi = lax.broadcasted_iota(jnp.int32, (LANES, LANES), 0) >= RWKV_HEAD
    bj = lax.broadcasted_iota(jnp.int32, (LANES, LANES), 1) >= RWKV_HEAD
    upd = jnp.where(bi == bj, ops.dot(u, bt, 0, 0, REC_MODE) + ops.dot(v, kt, 0, 0, REC_MODE), 0.0)
    return (s + upd) * jnp.exp(tot), y


def _gdn_chunk(ops, s, q, k, v, beta, g):
    c = q.shape[0]
    ii = lax.broadcasted_iota(jnp.int32, (c, c), 0)
    jj = lax.broadcasted_iota(jnp.int32, (c, c), 1)
    causal = ii >= jj
    strict = ii > jj
    q = q * (GDN_HEAD ** -0.5)
    gc = ops.dot(causal.astype(BF16), g, 1, 0, 'ca')
    ones = jnp.ones((c, LANES), BF16)
    gc_col = ops.dot(gc, ones, 1, 1, 'cb') * (1.0 / LANES)
    gc_row = ops.dot(ones, gc, 1, 1, 'ca') * (1.0 / LANES)
    decay = jnp.where(causal, jnp.exp(jnp.where(causal, gc_col - gc_row, 0.0)), 0.0)
    kb = k * beta
    lmat = jnp.where(strict, ops.dot(kb, k, 1, 1, REC_MODE) * decay, 0.0)
    tinv = _inv_unit_lower(ops, -lmat)
    egc = jnp.exp(gc)
    u = ops.dot(tinv, v * beta, 1, 0, REC_MODE)
    w = ops.dot(tinv, kb * egc, 1, 0, REC_MODE)
    a_intra = jnp.where(causal, ops.dot(q, k, 1, 1, REC_MODE) * decay, 0.0)
    g_tot = ops.dot(jnp.ones((c, c), BF16), g, 1, 0, 'ca')
    k_dec = k * jnp.exp(g_tot - gc)
    v_new = u - ops.dot(w, s, 1, 0, REC_MODE)
    o = ops.dot(q * egc, s, 1, 0, REC_MODE) + ops.dot(a_intra, v_new, 1, 0, REC_MODE)
    g_full = ops.dot(jnp.ones((LANES, c), BF16), g, 1, 0, 'ca')
    s_new = s * jnp.exp(g_full) + ops.dot(k_dec, v_new, 0, 0, REC_MODE)
    return s_new, o


def _mm(name, a, b, ta=False, tb=False, out_dtype=F32, tm=1024, tn=1024, tk=2048):
    m = a.shape[1] if ta else a.shape[0]
    kd = a.shape[0] if ta else a.shape[1]
    n = b.shape[0] if tb else b.shape[1]
    assert kd == (b.shape[1] if tb else b.shape[0]), (name, a.shape, b.shape)
    tm, tn, tk = _pick(m, tm), _pick(n, tn), _pick(kd, tk)
    nk = kd // tk

    def body(a_ref, b_ref, o_ref, acc):
        kk = pl.program_id(2)
        part = _dg(a_ref[...].astype(BF16), b_ref[...].astype(BF16), 0 if ta else 1, 1 if tb else 0)
        if nk == 1:
            o_ref[...] = part.astype(o_ref.dtype)
            return

        @pl.when(kk == 0)
        def _():
            acc[...] = part

        @pl.when((kk > 0) & (kk < nk - 1))
        def _():
            acc[...] += part

        @pl.when(kk == nk - 1)
        def _():
            o_ref[...] = (acc[...] + part).astype(o_ref.dtype)

    a_spec = pl.BlockSpec((tk, tm), lambda i, j, k: (k, i)) if ta else pl.BlockSpec((tm, tk), lambda i, j, k: (i, k))
    b_spec = pl.BlockSpec((tn, tk), lambda i, j, k: (j, k)) if tb else pl.BlockSpec((tk, tn), lambda i, j, k: (k, j))
    return _pcall(
        body, name=name, grid=(m // tm, n // tn, nk), in_specs=[a_spec, b_spec],
        out_specs=pl.BlockSpec((tm, tn), lambda i, j, k: (i, j)), out_shape=jax.ShapeDtypeStruct((m, n), out_dtype),
        scratch_shapes=[pltpu.VMEM((tm, tn), F32)],
        compiler_params=_params(("parallel", "parallel", "arbitrary")))(a, b)


def _row_spec(tm, width, cb):
    return pl.BlockSpec((tm, width), lambda i: (i, cb))


def _rows_fwd(name, f, rows, params, outs, tm):
    t = rows[0][0].shape[0]
    tm = min(tm, t)
    n_in = len(rows) + len(params)
    kept = [o for o in outs if o is not None]

    def body(*refs):
        vals = f(_RAW, *[r[...].astype(F32) for r in refs[:n_in]])
        o_refs = refs[n_in:]
        j = 0
        for val, o in zip(vals, outs):
            if o is not None:
                o_refs[j][...] = val.astype(o_refs[j].dtype)
                j += 1

    in_specs = [_row_spec(tm, w, cb) for (_, w, cb) in rows]
    in_specs += [pl.BlockSpec(p.shape, lambda i: (0, 0)) for p in params]
    res = _pcall(
        body, name=name, grid=(t // tm,), in_specs=in_specs,
        out_specs=[_row_spec(tm, w, 0) for (w, _) in kept],
        out_shape=[jax.ShapeDtypeStruct((t, w), dt) for (w, dt) in kept],
        compiler_params=_params(("parallel",)))(*[r[0] for r in rows], *params)
    return res


def _rows_bwd(name, f, rows, params, cts, tm, row_out, out_widths, adds=(), loss=False, out_dtypes=None):
    t = rows[0][0].shape[0]
    tm = min(tm, t)
    n_r, n_p = len(rows), len(params)
    ct_flat = [c for cl in cts for c in cl]
    n_ct, n_add = len(ct_flat), len(adds)
    n_out = len(out_widths)

    def body(*refs):
        i = pl.program_id(0)
        ins = [r[...].astype(F32) for r in refs[:n_r + n_p]]
        ct_refs = refs[n_r + n_p:n_r + n_p + n_ct]
        add_refs = refs[n_r + n_p + n_ct:n_r + n_p + n_ct + n_add]
        o_refs = refs[n_r + n_p + n_ct + n_add:]
        outs, vjp = jax.vjp(lambda *a: f(_DIFF, *a), *ins)
        ctv, pos = [], 0
        for o, cl in zip(outs, cts):
            if loss:
                ctv.append(jnp.ones_like(o))
            elif not cl:
                ctv.append(jnp.zeros_like(o))
            else:
                acc = ct_refs[pos][...].astype(F32)
                for q in range(1, len(cl)):
                    acc = acc + ct_refs[pos + q][...].astype(F32)
                ctv.append(acc)
            pos += len(cl)
        grads = vjp(tuple(ctv))
        for ri, ro in enumerate(row_out):
            if ro is not None:
                oi, off = ro
                val = grads[ri]
                for ai, (aoi, _) in enumerate(adds):
                    if aoi == oi:
                        val = val + add_refs[ai][...].astype(F32)
                o_refs[oi][:, off:off + val.shape[1]] = val.astype(o_refs[oi].dtype)
        p_refs = o_refs[n_out:n_out + n_p]

        @pl.when(i == 0)
        def _():
            for pr in p_refs:
                pr[...] = jnp.zeros_like(pr)
            if loss:
                o_refs[n_out + n_p][...] = jnp.zeros((8, LANES), F32)

        for pi, pr in enumerate(p_refs):
            pr[...] += grads[n_r + pi]
        if loss:
            o_refs[n_out + n_p][...] += jnp.sum(outs[0])

    in_specs = [_row_spec(tm, w, cb) for (_, w, cb) in rows]
    in_specs += [pl.BlockSpec(p.shape, lambda i: (0, 0)) for p in params]
    in_specs += [_row_spec(tm, w, cb) for (_, w, cb) in ct_flat]
    in_specs += [_row_spec(tm, w, cb) for (_, (_, w, cb)) in adds]
    out_specs = [_row_spec(tm, w, 0) for w in out_widths]
    out_specs += [pl.BlockSpec(p.shape, lambda i: (0, 0)) for p in params]
    out_dtypes = out_dtypes or [F32] * n_out
    out_shape = [jax.ShapeDtypeStruct((t, w), dt) for w, dt in zip(out_widths, out_dtypes)]
    out_shape += [jax.ShapeDtypeStruct(p.shape, F32) for p in params]
    if loss:
        out_specs.append(pl.BlockSpec((8, LANES), lambda i: (0, 0)))
        out_shape.append(jax.ShapeDtypeStruct((8, LANES), F32))
    res = _pcall(
        body, name=name, grid=(t // tm,), in_specs=in_specs, out_specs=out_specs, out_shape=out_shape,
        compiler_params=_params(("arbitrary",)))(
            *[r[0] for r in rows], *params, *[c[0] for c in ct_flat], *[a[1][0] for a in adds])
    return res[:n_out], res[n_out:n_out + n_p], res[n_out + n_p:]


def _shift_down(x, k):
    row = lax.broadcasted_iota(jnp.int32, x.shape, 0)
    return jnp.where(row >= k, pltpu.roll(x, k, 0), 0.0)


def _shift_up(x, k):
    t = x.shape[0]
    row = lax.broadcasted_iota(jnp.int32, x.shape, 0)
    return jnp.where(row < t - k, pltpu.roll(x, t - k, 0), 0.0)


def _tokshift_fwd(p, mix, n_main, lora_cb):
    t = p.shape[0]
    nblk = mix.shape[1] // LANES

    def src(j):
        return (0, jnp.where(j < n_main, j, j - n_main + lora_cb))

    def body(p_ref, mix_ref, o_ref):
        xv = p_ref[...]
        o_ref[...] = xv + (_shift_down(xv, 1) - xv) * mix_ref[...]

    return _pcall(
        body, name="tokshift_fwd", grid=(nblk,),
        in_specs=[pl.BlockSpec((t, LANES), src), pl.BlockSpec((1, LANES), lambda j: (0, j))],
        out_specs=pl.BlockSpec((t, LANES), lambda j: (0, j)),
        out_shape=jax.ShapeDtypeStruct((t, mix.shape[1]), F32), compiler_params=_params(("parallel",)))(p, mix)


def _tokshift_bwd(dxm, p, mix, n_main, lora_cb):
    t = p.shape[0]
    nblk = mix.shape[1] // LANES

    def src(j):
        return (0, jnp.where(j < n_main, j, j - n_main + lora_cb))

    def body(d_ref, p_ref, mix_ref, dp_ref, dmix_ref):
        xv, d, mx = p_ref[...], d_ref[...], mix_ref[...]
        dp_ref[...] = (d * (1.0 - mx) + _shift_up(d * mx, 1)).astype(dp_ref.dtype)
        dmix_ref[...] = jnp.sum(d * (_shift_down(xv, 1) - xv), axis=0, keepdims=True)

    return _pcall(
        body, name="tokshift_bwd", grid=(nblk,),
        in_specs=[pl.BlockSpec((t, LANES), lambda j: (0, j)), pl.BlockSpec((t, LANES), src),
                  pl.BlockSpec((1, LANES), lambda j: (0, j))],
        out_specs=[pl.BlockSpec((t, LANES), lambda j: (0, j)), pl.BlockSpec((1, LANES), lambda j: (0, j))],
        out_shape=[jax.ShapeDtypeStruct((t, mix.shape[1]), BF16), jax.ShapeDtypeStruct((1, mix.shape[1]), F32)],
        compiler_params=_params(("parallel",)))(dxm, p, mix)


def _conv_pre(xv, w):
    return (w[3:4, :] * xv + w[2:3, :] * _shift_down(xv, 1) + w[1:2, :] * _shift_down(xv, 2)
            + w[0:1, :] * _shift_down(xv, 3))


def _conv_fwd(p, w8, cb0):
    t = p.shape[0]
    width = w8.shape[1]

    def body(p_ref, w_ref, o_ref):
        c = _conv_pre(p_ref[...], w_ref[...])
        o_ref[...] = c * _sigmoid(c)

    return _pcall(
        body, name="conv_fwd", grid=(width // LANES,),
        in_specs=[pl.BlockSpec((t, LANES), lambda j: (0, j + cb0)), pl.BlockSpec((8, LANES), lambda j: (0, j))],
        out_specs=pl.BlockSpec((t, LANES), lambda j: (0, j)),
        out_shape=jax.ShapeDtypeStruct((t, width), F32), compiler_params=_params(("parallel",)))(p, w8)


def _conv_bwd(dy, p, w8, cb0):
    t = p.shape[0]
    width = w8.shape[1]

    def body(d_ref, p_ref, w_ref, dx_ref, dw_ref):
        xv, w = p_ref[...], w_ref[...]
        c = _conv_pre(xv, w)
        sg = _sigmoid(c)
        dc = d_ref[...] * (sg * (1.0 + c * (1.0 - sg)))
        dx_ref[...] = (w[3:4, :] * dc + w[2:3, :] * _shift_up(dc, 1) + w[1:2, :] * _shift_up(dc, 2)
                       + w[0:1, :] * _shift_up(dc, 3)).astype(dx_ref.dtype)
        row =lax.broadcasted_iota(jnp.int32, (8, LANES), 0)
        dw = jnp.zeros((8, LANES), F32)
        for j in range(4):
            sj = jnp.sum(dc * (xv if j == 3 else _shift_down(xv, 3 - j)), axis=0, keepdims=True)
            dw = dw + jnp.where(row == j, sj, 0.0)
        dw_ref[...] = dw

    return _pcall(
        body, name="conv_bwd", grid=(width // LANES,),
        in_specs=[pl.BlockSpec((t, LANES), lambda j: (0, j)), pl.BlockSpec((t, LANES), lambda j: (0, j + cb0)),
                  pl.BlockSpec((8, LANES), lambda j: (0, j))],
        out_specs=[pl.BlockSpec((t, LANES), lambda j: (0, j)), pl.BlockSpec((8, LANES), lambda j: (0, j))],
        out_shape=[jax.ShapeDtypeStruct((t, width), BF16), jax.ShapeDtypeStruct((8, width), F32)],
        compiler_params=_params(("parallel",)))(dy, p, w8)


def _rec_fwd(name, chunk_fn, ins, nblocks):
    t = ins[0][0].shape[0]
    tb = min(TIME_BLOCK, t)
    nt, ncb = t // tb, tb // CHUNK
    n_in = len(ins)
    grp = REC_GROUP if nblocks % REC_GROUP == 0 and all(c0 % REC_GROUP == 0 for (_, c0) in ins) else 1
    width = grp * LANES

    def body(*refs):
        in_refs, y_ref, sall_ref, s_scr = refs[:n_in], refs[n_in], refs[n_in + 1], refs[n_in + 2]

        @pl.when(pl.program_id(1) == 0)
        def _():
            s_scr[...] = jnp.zeros_like(s_scr)

        def step(ci, states):
            sl = pl.ds(pl.multiple_of(ci * CHUNK, CHUNK), CHUNK)
            lanes = [slice(gi * LANES, (gi + 1) * LANES) for gi in range(grp)]
            loaded = [[r[sl, ln] for r in in_refs] for ln in lanes]
            outs = [chunk_fn(_RAW, s, *vals) for s, vals in zip(states, loaded)]
            for gi in range(grp):
                sall_ref[gi, ci] = states[gi]
                y_ref[sl, lanes[gi]] = outs[gi][1]
            return tuple(o[0] for o in outs)

        final = lax.fori_loop(0, ncb, step, tuple(s_scr[gi] for gi in range(grp)))
        for gi in range(grp):
            s_scr[gi] = final[gi]

    in_specs = [pl.BlockSpec((tb, width), functools.partial(lambda h, tt, cb: (tt, cb + h), cb=c0 // grp))
                for (_, c0) in ins]
    return _pcall(
        body, name=name, grid=(nblocks // grp, nt), in_specs=in_specs,
        out_specs=[pl.BlockSpec((tb, width), lambda h, tt: (tt, h)),
                   pl.BlockSpec((grp, ncb, LANES, LANES), lambda h, tt: (h, tt, 0, 0))],
        out_shape=[jax.ShapeDtypeStruct((t, nblocks * LANES), F32),
                   jax.ShapeDtypeStruct((nblocks, t // CHUNK, LANES, LANES), F32)],
        scratch_shapes=[pltpu.VMEM((grp, LANES, LANES), F32)],
        compiler_params=_params(("parallel", "arbitrary")))(*[a for (a, _) in ins])


def _rec_bwd(name, chunk_fn, ins, dy, sall, nblocks):
    t = ins[0][0].shape[0]
    tb = min(TIME_BLOCK, t)
    nt, ncb = t // tb, tb // CHUNK
    n_in = len(ins)
    grp = REC_GROUP if nblocks % REC_GROUP == 0 and all(c0 % REC_GROUP == 0 for (_, c0) in ins) else 1
    width = grp * LANES

    def body(*refs):
        in_refs, dy_ref, sall_ref = refs[:n_in], refs[n_in], refs[n_in + 1]
        d_refs, ds_scr = refs[n_in + 2:2 * n_in + 2], refs[2 * n_in + 2]

        @pl.when(pl.program_id(1) == 0)
        def _():
            ds_scr[...] = jnp.zeros_like(ds_scr)

        def step(j, dstates):
            ci = ncb - 1 - j
            sl = pl.ds(pl.multiple_of(ci * CHUNK, CHUNK), CHUNK)
            lanes = [slice(gi * LANES, (gi + 1) * LANES) for gi in range(grp)]
            loaded = [[sall_ref[gi, ci]] + [r[sl, ln] for r in in_refs] + [dy_ref[sl, ln]]
                      for gi, ln in enumerate(lanes)]
            grads = []
            for ds, vals in zip(dstates, loaded):
                _, vjp = jax.vjp(lambda *a: chunk_fn(_DIFF, *a), *vals[:-1])
                grads.append(vjp((ds, vals[-1])))
            for gi in range(grp):
                for dr, gval in zip(d_refs, grads[gi][1:]):
                    dr[sl, lanes[gi]] = gval
            return tuple(g[0] for g in grads)

        final = lax.fori_loop(0, ncb, step, tuple(ds_scr[gi] for gi in range(grp)))
        for gi in range(grp):
            ds_scr[gi] = final[gi]

    in_specs = [pl.BlockSpec((tb, width), functools.partial(lambda h, tt, cb: (nt - 1 - tt, cb + h), cb=c0 // grp))
                for (_, c0) in ins]
    in_specs.append(pl.BlockSpec((tb, width), lambda h, tt: (nt - 1 - tt, h)))
    in_specs.append(pl.BlockSpec((grp, ncb, LANES, LANES), lambda h, tt: (h, nt - 1 - tt, 0, 0)))
    return _pcall(
        body, name=name, grid=(nblocks // grp, nt), in_specs=in_specs,
        out_specs=[pl.BlockSpec((tb, width), lambda h, tt: (nt - 1 - tt, h)) for _ in ins],
        out_shape=[jax.ShapeDtypeStruct((t, nblocks * LANES), F32) for _ in ins],
        scratch_shapes=[pltpu.VMEM((grp, LANES, LANES), F32)],
        compiler_params=_params(("parallel", "arbitrary")))(*[a for (a, _) in ins], dy, sall)


def _pick_rows(r, c):
    tr = max(8, ((1 << 18) // c) // 8 * 8)
    while tr > 8 and r % tr:
        tr -= 8
    return tr if r % tr == 0 else r


def _sum4(name, own, r3):
    _, r, c = r3.shape
    tr = _pick_rows(r, c)

    def body(own_ref, r_ref, o_ref):
        o_ref[...] = ((own_ref[...].astype(F32) + r_ref[0].astype(F32)) + r_ref[1].astype(F32)) + r_ref[2].astype(F32)

    return _pcall(
        body, name=name, grid=(r // tr,),
        in_specs=[pl.BlockSpec((tr, c), lambda i: (i, 0)), pl.BlockSpec((3, tr, c), lambda i: (0, i, 0))],
        out_specs=pl.BlockSpec((tr, c), lambda i: (i, 0)), out_shape=jax.ShapeDtypeStruct((r, c), F32),
        compiler_params=_params(("parallel",)))(own, r3)


def _adam_math(w, g, m, v):
    m = ADAM_B1 * m + (1.0 - ADAM_B1) * g
    v = ADAM_B2 * v + (1.0 - ADAM_B2) * (g * g)
    m_hat = m / (1.0 - ADAM_B1 ** ADAM_STEP)
    v_hat = v / (1.0 - ADAM_B2 ** ADAM_STEP)
    return -ADAM_LR * (m_hat / (jnp.sqrt(v_hat) + ADAM_EPS) + ADAM_WD * w), m, v


def _adam_big(name, w, m, v, ga, gb):
    r, c = w.shape
    tr = _pick_rows(r, c)

    def body(w_ref, m_ref, v_ref, ga_ref, gb_ref, g_ref, d_ref, mo_ref, vo_ref):
        g = ga_ref[...] + gb_ref[...]
        g_ref[...] = g
        d_ref[...], mo_ref[...], vo_ref[...] = _adam_math(w_ref[...], g, m_ref[...], v_ref[...])

    spec = pl.BlockSpec((tr, c), lambda i: (i, 0))
    return _pcall(
        body, name=name, grid=(r // tr,), in_specs=[spec] * 5, out_specs=[spec] * 4,
        out_shape=[jax.ShapeDtypeStruct((r, c), F32)] * 4, compiler_params=_params(("parallel",)))(w, m, v, ga, gb)


def _adam_small(w, m, v, g8):
    shape = w.shape

    def body(w_ref, m_ref, v_ref, g8_ref, g_ref, d_ref, mo_ref, vo_ref):
        g = g8_ref[0]
        for dev in range(1, 8):
            g = g + g8_ref[dev]
        g_ref[...] = g
        d_ref[...], mo_ref[...], vo_ref[...] = _adam_math(w_ref[...], g, m_ref[...], v_ref[...])

    return _pcall(
        body, name="adam_small", out_shape=[jax.ShapeDtypeStruct(shape, F32)] * 4,
        compiler_params=_params())(w, m, v, g8)


ANY = pl.BlockSpec(memory_space=pl.ANY)


def _place():
    x, y, c = lax.axis_index("x"), lax.axis_index("y"), lax.axis_index("c")
    return x, y, c, [(1 - x, y), (x, 1 - y), (1 - x, 1 - y)]


def _remote(src, dst, send_sems, recv_sems, k, dev):
    return pltpu.make_async_remote_copy(src_ref=src, dst_ref=dst, send_sem=send_sems.at[k], recv_sem=recv_sems.at[k],
                                        device_id=dev, device_id_type=MESH)


HBM = pl.BlockSpec(memory_space=pltpu.HBM)
SEM = pl.BlockSpec(memory_space=pltpu.SEMAPHORE)


def _send3_start(name, src, land_shape, gather):
    def body(src_ref, land_ref, send_sem, recv_sem, src_thru, land_thru, token):
        x, y, c, chips = _place()
        for k, (px, py) in enumerate(chips):
            s_k = src_ref if gather else src_ref.at[2 * px + py]
            d_k = land_ref.at[2 * x + y] if gather else land_ref.at[k]
            pltpu.make_async_remote_copy(src_ref=s_k, dst_ref=d_k, send_sem=send_sem, recv_sem=recv_sem,
                                         device_id=(px, py, c), device_id_type=MESH).start()
        token[...] = jnp.zeros_like(token)

    return pl.pallas_call(
        body, name=name,
        out_shape=(pltpu.SemaphoreType.DMA(()), pltpu.SemaphoreType.DMA(()), pltpu.HBM(src.shape, src.dtype),
                   pltpu.HBM(land_shape, src.dtype), jax.ShapeDtypeStruct((8, LANES), F32)),
        in_specs=(HBM, HBM), out_specs=(SEM, SEM, HBM, HBM, pl.BlockSpec(memory_space=pltpu.VMEM)),
        input_output_aliases={0: 2, 1: 3},
        compiler_params=pltpu.CompilerParams(has_side_effects=pltpu.SideEffectType.DATAFLOW_SIDE_EFFECTING),
    )(pltpu.with_memory_space_constraint(src, pltpu.HBM),
      pltpu.with_memory_space_constraint(lax.empty(land_shape, src.dtype), pltpu.HBM))


def _send3_wait(name, started, after):
    send_sem, recv_sem, src_thru, land_thru, _ = started

    def body(src_ref, land_ref, send_sem, recv_sem, after_ref, src_dead, got_ref):
        x, y, c, chips = _place()
        three = land_ref.at[pl.ds(0, 3)]
        cp = pltpu.make_async_remote_copy(src_ref=three, dst_ref=three, send_sem=send_sem, recv_sem=recv_sem,
                                          device_id=(chips[0][0], chips[0][1], c), device_id_type=MESH)
        cp.wait_send()
        cp.wait_recv()

    return pl.pallas_call(
        body, name=name,
        out_shape=(pltpu.HBM(src_thru.shape, src_thru.dtype), pltpu.HBM(land_thru.shape, land_thru.dtype)),
        in_specs=(HBM, HBM, SEM, SEM, ANY), out_specs=(HBM, HBM), input_output_aliases={0: 0, 1: 1},
        compiler_params=pltpu.CompilerParams(has_side_effects=pltpu.SideEffectType.DATAFLOW_SIDE_EFFECTING),
    )(src_thru, land_thru, send_sem, recv_sem, after)


def _swap_sibling(name, part):
    def body(src, out, send_sems, recv_sems):
        x, y, c, _ = _place()
        cp = _remote(src, out, send_sems, recv_sems, 0, (x, y, 1 - c))
        cp.start()
        cp.wait_recv()
        cp.wait_send()

    return _pcall(
        body, name=name, in_specs=[ANY], out_specs=ANY, out_shape=jax.ShapeDtypeStruct(part.shape, part.dtype),
        scratch_shapes=[pltpu.SemaphoreType.DMA((1,)), pltpu.SemaphoreType.DMA((1,))])(part)


def _gather8(flat):
    def body(src, out, send_sems, recv_sems, local_sem):
        x, y, c, _ = _place()
        own = pltpu.make_async_copy(src, out.at[4 * x + 2 * y + c], local_sem)
        own.start()
        peers = []
        for mask in range(1, 8):
            px = x ^ ((mask >> 2) & 1)
            py = y ^ ((mask >> 1) & 1)
            pc = c ^ (mask & 1)
            peers.append((px, py, pc))
        sends = [_remote(src, out.at[4 * x + 2 * y + c], send_sems, recv_sems, k, dev) for k, dev in enumerate(peers)]
        for cp in sends:
            cp.start()
        for k, (px, py, pc) in enumerate(peers):
            _remote(src, out.at[4 * px + 2 * py + pc], send_sems, recv_sems, k, (px, py, pc)).wait_recv()
        for cp in sends:
            cp.wait_send()
        own.wait()

    return _pcall(
        body, name="gather8_small", in_specs=[ANY], out_specs=ANY,
        out_shape=jax.ShapeDtypeStruct((8,) + flat.shape, flat.dtype),
        scratch_shapes=[pltpu.SemaphoreType.DMA((7,)), pltpu.SemaphoreType.DMA((7,)), pltpu.SemaphoreType.DMA(())],
    )(flat)


def _pad_cols(a, width):
    return jnp.pad(a, ((0, 0), (0, width - a.shape[1])))


def _col_shards(full):
    r, c = full.shape
    return full.reshape(r, 4, c // 4).transpose(1, 0, 2)


def kernel(x, mem, mix_norm_pre, mix_norm_post, w_in, rwkv_shift_mix, rwkv_w0, rwkv_w_up, rwkv_a0, rwkv_a_up, rwkv_g_up, rwkv_k_k, rwkv_k_a, rwkv_r_k, rwkv_gn_w, rwkv_gn_b, gdn_conv_w, gdn_a_log, gdn_dt_bias, gdn_norm_w, w_branch_rwkv, w_branch_gdn, w_mix_out, xa_norm_pre, xa_norm_mem, xa_norm_post, xa_w_q, xa_w_kv, xa_w_o, mlp_norm_pre, mlp_norm_post, mlp_w_up, mlp_w_down, loss_target, m_mix_norm_pre, m_mix_norm_post, m_w_in, m_rwkv_shift_mix, m_rwkv_w0, m_rwkv_w_up, m_rwkv_a0, m_rwkv_a_up, m_rwkv_g_up, m_rwkv_k_k, m_rwkv_k_a, m_rwkv_r_k, m_rwkv_gn_w, m_rwkv_gn_b, m_gdn_conv_w, m_gdn_a_log, m_gdn_dt_bias, m_gdn_norm_w, m_w_branch_rwkv, m_w_branch_gdn, m_w_mix_out, m_xa_norm_pre, m_xa_norm_mem, m_xa_norm_post, m_xa_w_q, m_xa_w_kv, m_xa_w_o, m_mlp_norm_pre, m_mlp_norm_post, m_mlp_w_up, m_mlp_w_down, v_mix_norm_pre, v_mix_norm_post, v_w_in, v_rwkv_shift_mix, v_rwkv_w0, v_rwkv_w_up, v_rwkv_a0, v_rwkv_a_up, v_rwkv_g_up, v_rwkv_k_k, v_rwkv_k_a, v_rwkv_r_k, v_rwkv_gn_w, v_rwkv_gn_b, v_gdn_conv_w, v_gdn_a_log, v_gdn_dt_bias, v_gdn_norm_w, v_w_branch_rwkv, v_w_branch_gdn, v_w_mix_out, v_xa_norm_pre, v_xa_norm_mem, v_xa_norm_post, v_xa_w_q, v_xa_w_kv, v_xa_w_o, v_mlp_norm_pre, v_mlp_norm_post, v_mlp_w_up, v_mlp_w_down):
    given = dict(locals())
    W = {n: given[n] if given[n].ndim == 2 else given[n][0] for n in WEIGHTS}
    Mo = {n: given['m_' + n].reshape(W[n].shape) for n in WEIGHTS}
    Vo = {n: given['v_' + n].reshape(W[n].shape) for n in WEIGHTS}
    xs, mems, tgt = x[0], mem[0], loss_target[0]
    T, D = xs.shape
    RW = W['rwkv_w0'].shape[-1]
    GW = W['gdn_conv_w'].shape[-1] * 4 // 3
    NH = W['gdn_a_log'].shape[-1]
    LW, LA, LG = W['rwkv_w_up'].shape[0], W['rwkv_a_up'].shape[0], W['rwkv_g_up'].shape[0]
    XW = W['xa_w_q'].shape[-1]
    assert LW <= LANES and LA <= LANES and LG % LANES == 0 and 2 * NH <= LANES

    del _PENDING[:]
    me = 2 * lax.axis_index("x") + lax.axis_index("y")
    small_sizes = [W[n].size for n in SMALL_SHARDED]
    small_len = -(-sum(small_sizes) // 1024) * 1024

    def pack_small(vals):
        flat = jnp.concatenate([vv.reshape(-1).astype(F32) for vv in vals])
        return jnp.pad(flat, (0, small_len - flat.shape[0])).reshape(small_len // 1024, 1024)

    def unpack_small(packed):
        flat, out, off = packed.reshape(-1), [], 0
        for n, sz in zip(SMALL_SHARDED, small_sizes):
            out.append(flat[off:off + sz].reshape(W[n].shape))
            off += sz
        return out

    gstarted = {}
    for n in ['w_in', 'small'] + [b for b in BIG_SHARDED if b != 'w_in']:
        local = pack_small([W[s] for s in SMALL_SHARDED]) if n == 'small' else W[n].astype(BF16)
        gstarted[n] = _send3_start("gather_start_" + n, local, (4,) + local.shape, True)
        _PENDING.append(gstarted[n][4])

    def gathered(n, after):
        src, land = _send3_wait("gather_wait_" + n, gstarted[n], after)
        return lax.dynamic_update_index_in_dim(land, src, me, 0)

    def gathered_full(n, after):
        g = gathered(n, after)
        if n in COL_SHARDED:
            return jnp.concatenate([g[0], g[1], g[2], g[3]], axis=1)
        return g.reshape((4 * g.shape[1],) + g.shape[2:])

    RC = 3 * RW + LW + LA + LG
    o_z_raw = RC + 3 * GW
    o_b_raw = o_z_raw + GW
    o_grw_raw = o_b_raw + 2 * NH
    O_GQ = 3 * RW
    O_GRW = O_GQ + 3 * GW
    O_GGDN = O_GRW + D
    O_Z = O_GGDN + D
    O_LW = O_Z + GW
    O_LA = O_LW + LANES
    O_LG = O_LA + LANES
    O_BA = O_LG + LG
    END = O_BA + LANES
    NP = -(-END // 1024) * 1024
    XM = 3 * RW + 2 * LANES + LG
    assert O_GRW % D == 0 and O_Z % GW == 0 and O_LG % LG == 0 and (3 * RW + 2 * LANES) % LG == 0

    def pad_layout(raw, lead):
        return jnp.concatenate([
            raw[:, :3 * RW], raw[:, RC:RC + 3 * GW], raw[:, o_grw_raw:o_grw_raw + 2 * D], raw[:, o_z_raw:o_z_raw + GW],
            _pad_cols(raw[:, 3 * RW:3 * RW + LW], LANES), _pad_cols(raw[:, 3 * RW + LW:3 * RW + LW + LA], LANES),
            raw[:, 3 * RW + LW + LA:RC], _pad_cols(raw[:, o_b_raw:o_b_raw + 2 * NH], LANES),
            jnp.zeros((lead, NP - END), raw.dtype)], axis=1)

    def unpad_layout(pd):
        return jnp.concatenate([
            pd[:, :3 * RW], pd[:, O_LW:O_LW + LW], pd[:, O_LA:O_LA + LA], pd[:, O_LG:O_LG + LG],
            pd[:, O_GQ:O_GQ + 3 * GW], pd[:, O_Z:O_Z + GW], pd[:, O_BA:O_BA + 2 * NH], pd[:, O_GRW:O_GRW + 2 * D]], axis=1)

    def xm_layout(raw):
        return jnp.concatenate([raw[:, :3 * RW], _pad_cols(raw[:, 3 * RW:3 * RW + LW], LANES),
                                _pad_cols(raw[:, 3 * RW + LW:3 * RW + LW + LA], LANES), raw[:, 3 * RW + LW + LA:]], axis=1)

    def xm_unlayout(pd):
        return jnp.concatenate([pd[:, :3 * RW], pd[:, 3 * RW:3 * RW + LW], pd[:, 3 * RW + LANES:3 * RW + LANES + LA],
                                pd[:, 3 * RW + 2 * LANES:]], axis=1)

    mix_x = xm_layout(W['rwkv_shift_mix'])
    r_k = W['rwkv_r_k'].reshape(1, RW)
    alog_p = jnp.pad(W['gdn_a_log'], ((0, 0), (NH, LANES - 2 * NH)))
    dt_p = jnp.pad(W['gdn_dt_bias'], ((0, 0), (NH, LANES - 2 * NH)))
    nw_t = jnp.tile(W['gdn_norm_w'], (1, GW // GDN_HEAD))
    n_main, lora_cb = 3 * RW // LANES, O_LW // LANES
    f_gdn_pre = functools.partial(_f_gdn_pre, heads=NH)

    (u,) = _rows_fwd("rms_pre", _f_rms, [(xs, D, 0)], [W['mix_norm_pre']], [(D, BF16)], 256)
    full = {'w_in': gathered_full('w_in', u)}
    w_p = pad_layout(full['w_in'], D)
    p = _mm("mm_in", u, w_p)
    small4 = gathered('small', p)
    small_parts = [unpack_small(small4[s]) for s in range(4)]
    for j, n in enumerate(SMALL_SHARDED):
        full[n] = jnp.concatenate([small_parts[s][j] for s in range(4)], axis=1)
    w_up_p = jnp.pad(full['rwkv_w_up'], ((0, LANES - LW), (0, 0)))
    a_up_p = jnp.pad(full['rwkv_a_up'], ((0, LANES - LA), (0, 0)))
    g_up = full['rwkv_g_up']
    conv8 = jnp.pad(full['gdn_conv_w'], ((0, 4), (0, 0)))
    rwkv_par = [W['rwkv_w0'], W['rwkv_a0'], W['rwkv_k_k'], W['rwkv_k_a'], w_up_p, a_up_p, g_up]
    xm = _tokshift_fwd(p, mix_x, n_main, lora_cb)
    pre_rows = [(xm, RW, 0), (xm, RW, 1), (xm, RW, 2), (xm, LANES, n_main), (xm, LANES, n_main + 1),
                (xm, LG, (3 * RW + 2 * LANES) // LG)]
    lwd, k2, av, bv, gate = _rows_fwd("rwkv_pre", _f_rwkv_pre, pre_rows, rwkv_par,
                                      [None, (RW, F32), (RW, F32), None, (RW, F32), (RW, F32), (RW, F32)], 128)
    rec_rw = [(xm, 0), (lwd, 0), (k2, 0), (xm, 2 * RW // LANES), (av, 0), (bv, 0)]
    y_rec, s_rw = _rec_fwd("rwkv_rec_fwd", _rwkv_chunk, rec_rw, RW // LANES)
    post_rows = [(y_rec, RW, 0), (xm, RW, 0), (k2, RW, 0), (xm, RW, 2), (gate, RW, 0)]
    post_par = [r_k, W['rwkv_gn_w'], W['rwkv_gn_b']]
    (y_rw,) = _rows_fwd("rwkv_post", _f_rwkv_post, post_rows, post_par, [(RW, BF16)], 128)

    qkvc = _conv_fwd(p, conv8, O_GQ // LANES)
    gpre_rows = [(qkvc, GW, 0), (qkvc, GW, 1), (qkvc, GW, 2), (p, LANES, O_BA // LANES)]
    qn, kn, beta_x, g_x = _rows_fwd("gdn_pre", f_gdn_pre, gpre_rows, [alog_p, dt_p],
                                    [(GW, F32), (GW, F32), None, (GW, F32), (GW, F32)], 128)
    rec_gd = [(qn, 0), (kn, 0), (qkvc, 2 * GW // LANES), (beta_x, 0), (g_x, 0)]
    o_gd, s_gd = _rec_fwd("gdn_rec_fwd", _gdn_chunk, rec_gd, GW // LANES)
    gpost_rows = [(o_gd, GW, 0), (p, GW, O_Z // GW)]
    (y_gdn,) = _rows_fwd("gdn_post", _f_gdn_post, gpost_rows, [nw_t], [(GW, BF16)], 128)

    full['w_branch_rwkv'] = gathered_full('w_branch_rwkv', y_rw)
    br_rw = _mm("mm_br_rw", y_rw, full['w_branch_rwkv'])
    full['w_branch_gdn'] = gathered_full('w_branch_gdn', y_gdn)
    br_gdn = _mm("mm_br_gdn", y_gdn, full['w_branch_gdn'])
    merge_rows = [(p, D, O_GRW // D), (p, D, O_GGDN // D), (br_rw, D, 0), (br_gdn, D, 0)]
    (merged,) = _rows_fwd("merge", _f_merge, merge_rows, [], [(D, BF16)], 128)
    full['w_mix_out'] = gathered_full('w_mix_out', merged)
    ymix = _mm("mm_mix_out", merged, full['w_mix_out'])
    r1_par = [W['mix_norm_post'], W['xa_norm_pre']]
    h1, cq = _rows_fwd("resid1", _f_resid, [(xs, D, 0), (ymix, D, 0)], r1_par, [(D, F32), (D, BF16)], 128)
    full['xa_w_q'] = gathered_full('xa_w_q', cq)
    q = _mm("mm_xa_q", cq, full['xa_w_q'])
    (mn,) = _rows_fwd("mem_norm", _f_rms, [(mems, D, 0)], [W['xa_norm_mem']], [(D, BF16)], 128)
    full['xa_w_kv'] = gathered_full('xa_w_kv', mn)
    kv = _mm("mm_xa_kv", mn, full['xa_w_kv'])
    k_x, v_x = kv[:, :XW], kv[:, XW:]
    (o_x,) = _rows_fwd("xattn", _f_xattn, [(q, XW, 0)], [k_x, v_x], [(XW, BF16)], 256)
    full['xa_w_o'] = gathered_full('xa_w_o', o_x)
    xo = _mm("mm_xa_o", o_x, full['xa_w_o'])
    r2_par = [W['xa_norm_post'], W['mlp_norm_pre']]
    h2, fm = _rows_fwd("resid2", _f_resid, [(h1, D, 0), (xo, D, 0)], r2_par, [(D, F32), (D, BF16)], 128)
    full['mlp_w_up'] = gathered_full('mlp_w_up', fm)
    up = _mm("mm_mlp_up", fm, full['mlp_w_up'])
    DFF = up.shape[1]
    (act,) = _rows_fwd("mlp_act", _f_act, [(up, DFF, 0)], [], [(DFF, BF16)], 64)
    full['mlp_w_down'] = gathered_full('mlp_w_down', act)
    down = _mm("mm_mlp_down", act, full['mlp_w_down'])

    G = {}
    xstarted = {}

    def xstart(n, g4):
        own = lax.dynamic_index_in_dim(g4, me, 0, keepdims=False)
        xstarted[n] = (_send3_start("grad_start_" + n, g4, (3,) + g4.shape[1:], False), own)
        _PENDING.append(xstarted[n][0][4])

    (dh2, ddown), (G['mlp_norm_post'],), (loss_acc,) = _rows_bwd(
        "final", _f_final, [(h2, D, 0), (down, D, 0), (tgt, D, 0)], [W['mlp_norm_post']], [[]], 128,
        [(0, 0), (1, 0), None], [D, D], loss=True, out_dtypes=[F32, BF16])
    d_act = _mm("mm_d_act", ddown, full['mlp_w_down'], tb=True)
    dw_down = _mm("mm_dw_down", act, ddown, ta=True)
    xstart('mlp_w_down', dw_down.astype(BF16).reshape(4, DFF // 4, D))
    (d_up,), _, _ = _rows_bwd("mlp_act_bwd", _f_act, [(up, DFF, 0)], [], [[(d_act, DFF, 0)]], 32, [(0, 0)], [DFF],
                              out_dtypes=[BF16])
    d_fm = _mm("mm_d_fm", d_up, full['mlp_w_up'], tb=True)
    dw_up = _mm("mm_dw_up", fm, d_up, ta=True)
    xstart('mlp_w_up', _col_shards(dw_up.astype(BF16)))
    (dh1, dxo), (G['xa_norm_post'], G['mlp_norm_pre']), _ = _rows_bwd(
        "resid2_bwd", _f_resid, [(h1, D, 0), (xo, D, 0)], r2_par, [[(dh2, D, 0)], [(d_fm, D, 0)]], 128,
        [(0, 0), (1, 0)], [D, D], out_dtypes=[F32, BF16])
    d_ox = _mm("mm_d_ox", dxo, full['xa_w_o'], tb=True)
    dw_o = _mm("mm_dw_o", o_x, dxo, ta=True)
    xstart('xa_w_o', _col_shards(dw_o.astype(BF16)))
    (dq,), (dk_x, dv_x), _ = _rows_bwd("xattn_bwd", _f_xattn, [(q, XW, 0)], [k_x, v_x], [[(d_ox, XW, 0)]], 128,
                                       [(0, 0)], [XW], out_dtypes=[BF16])
    d_cq = _mm("mm_d_cq", dq, full['xa_w_q'], tb=True)
    dw_q = _mm("mm_dw_q", cq, dq, ta=True)
    xstart('xa_w_q', dw_q.astype(BF16).reshape(4, D // 4, XW))
    dkv = jnp.concatenate([dk_x, dv_x], axis=1)
    d_mn = _mm("mm_d_mn", dkv, full['xa_w_kv'], tb=True)
    dw_kv = _mm("mm_dw_kv", mn, dkv, ta=True)
    xstart('xa_w_kv', dw_kv.astype(BF16).reshape(4, D // 4, 2 * XW))
    _, (G['xa_norm_mem'],), _ = _rows_bwd("mem_norm_bwd", _f_rms, [(mems, D, 0)], [W['xa_norm_mem']],
                                          [[(d_mn, D, 0)]], 128, [None], [])
    (dx_a, dymix), (G['mix_norm_post'], G['xa_norm_pre']), _ = _rows_bwd(
        "resid1_bwd", _f_resid, [(xs, D, 0), (ymix, D, 0)], r1_par, [[(dh1, D, 0)], [(d_cq, D, 0)]], 128,
        [(0, 0), (1, 0)], [D, D], out_dtypes=[F32, BF16])
    d_merged = _mm("mm_d_merged", dymix, full['w_mix_out'], tb=True)
    dw_out = _mm("mm_dw_out", merged, dymix, ta=True)
    xstart('w_mix_out', dw_out.astype(BF16).reshape(4, D // 4, D))
    (d_grw, d_ggdn, d_brw, d_bgdn), _, _ = _rows_bwd(
        "merge_bwd", _f_merge, merge_rows, [], [[(d_merged, D, 0)]], 128, [(0, 0), (1, 0), (2, 0), (3, 0)], [D, D, D, D],
        out_dtypes=[BF16] * 4)
    d_yrw = _mm("mm_d_yrw", d_brw, full['w_branch_rwkv'], tb=True)
    dw_brw = _mm("mm_dw_brw", y_rw, d_brw, ta=True)
    xstart('w_branch_rwkv', _col_shards(dw_brw.astype(BF16)))
    d_ygdn = _mm("mm_d_ygdn", d_bgdn, full['w_branch_gdn'], tb=True)
    dw_bgdn = _mm("mm_dw_bgdn", y_gdn, d_bgdn, ta=True)
    xstart('w_branch_gdn', _col_shards(dw_bgdn.astype(BF16)))

    (d_y, d_r1, d_k2a, d_v1, d_gate), (d_rk, G['rwkv_gn_w'], G['rwkv_gn_b']), _ = _rows_bwd(
        "rwkv_post_bwd", _f_rwkv_post, post_rows, post_par, [[(d_yrw, RW, 0)]], 64,
        [(0, 0), (1, 0), (2, 0), (3, 0), (4, 0)], [RW] * 5)
    d_r2, d_lwd, d_k2b, d_v2, d_av, d_bv = _rec_bwd("rwkv_rec_bwd", _rwkv_chunk, rec_rw, d_y, s_rw, RW // LANES)
    pre_cts = [[(d_r1, RW, 0), (d_r2, RW, 0)], [(d_lwd, RW, 0)], [(d_k2a, RW, 0), (d_k2b, RW, 0)],
               [(d_v1, RW, 0), (d_v2, RW, 0)], [(d_av, RW, 0)], [(d_bv, RW, 0)], [(d_gate, RW, 0)]]
    pre_out = [(0, 0), (0, RW), (0, 2 * RW), (0, 3 * RW), (0, 3 * RW + LANES), (0, 3 * RW + 2 * LANES)]
    (d_xm,), pre_pg, _ = _rows_bwd("rwkv_pre_bwd", _f_rwkv_pre, pre_rows, rwkv_par, pre_cts, 64, pre_out, [XM])
    G['rwkv_w0'], G['rwkv_a0'], G['rwkv_k_k'], G['rwkv_k_a'], d_wup_p, d_aup_p, d_gup = pre_pg
    d_prw, d_mix_x = _tokshift_bwd(d_xm, p, mix_x, n_main, lora_cb)

    (d_ogd, d_z), (d_nw_t,), _ = _rows_bwd("gdn_post_bwd", _f_gdn_post, gpost_rows, [nw_t], [[(d_ygdn, GW, 0)]], 64,
                                           [(0, 0), (1, 0)], [GW, GW], out_dtypes=[F32, BF16])
    d_qn, d_kn, d_vg, d_betax, d_gx = _rec_bwd("gdn_rec_bwd", _gdn_chunk, rec_gd, d_ogd, s_gd, GW // LANES)
    gpre_cts = [[(d_qn, GW, 0)], [(d_kn, GW, 0)], [(d_vg, GW, 0)], [(d_betax, GW, 0)], [(d_gx, GW, 0)]]
    (d_qkvc, d_ba), (d_alog_p, d_dt_p), _ = _rows_bwd(
        "gdn_pre_bwd", f_gdn_pre, gpre_rows, [alog_p, dt_p], gpre_cts, 64,
        [(0, 0), (0, GW), (0, 2 * GW), (1, 0)], [3 * GW, LANES], out_dtypes=[F32, BF16])
    d_pqkv, d_conv8 = _conv_bwd(d_qkvc, p, conv8, O_GQ // LANES)

    dp = jnp.concatenate([d_prw[:, :3 * RW], d_pqkv, d_grw, d_ggdn, d_z, d_prw[:, 3 * RW:], d_ba,
                          jnp.zeros((T, NP - END), BF16)], axis=1)
    d_u = _mm("mm_d_u", dp, w_p, tb=True)
    small_g = [_col_shards(d_wup_p[:LW]), _col_shards(d_aup_p[:LA]), _col_shards(d_gup), _col_shards(d_conv8[:4])]
    xstart('small', jnp.stack([pack_small([g[s] for g in small_g]) for s in range(4)]))
    dw_p = _mm("mm_dw_in", u, dp, ta=True)
    xstart('w_in', _col_shards(unpad_layout(dw_p.astype(BF16))))
    (grad_x,), (G['mix_norm_pre'],), _ = _rows_bwd(
        "rms_pre_bwd", _f_rms, [(xs, D, 0)], [W['mix_norm_pre']], [[(d_u, D, 0)]], 128, [(0, 0)], [D],
        adds=[(0, (dx_a, D, 0))])

    G['rwkv_shift_mix'] = xm_unlayout(d_mix_x)
    G['rwkv_r_k'] = d_rk
    G['gdn_a_log'] = d_alog_p[:, NH:2 * NH]
    G['gdn_dt_bias'] = d_dt_p[:, NH:2 * NH]
    G['gdn_norm_w'] = jnp.sum(d_nw_t.reshape(GW // GDN_HEAD, GDN_HEAD), axis=0, keepdims=True)

    res = {}
    for n in xstarted:
        started, own = xstarted[n]
        _, land = _send3_wait("grad_wait_" + n, started, grad_x)
        part = _sum4("sum4_" + n, own, land)
        other = _swap_sibling("swap_" + n, part)
        if n == 'small':
            for sn, pa, pb in zip(SMALL_SHARDED, unpack_small(part), unpack_small(other)):
                res[sn] = _adam_big("adam_" + sn, W[sn], Mo[sn], Vo[sn], pa, pb)
        else:
            res[n] = _adam_big("adam_" + n, W[n], Mo[n], Vo[n], part, other)

    sizes = [-(-W[n].size // LANES) * LANES for n in REPLICATED]
    total = -(-sum(sizes) // (8 * LANES)) * (8 * LANES)

    def pack(vals):
        parts = [jnp.pad(vv.reshape(-1), (0, sz - vv.size)) for vv, sz in zip(vals, sizes)]
        flat = jnp.concatenate(parts)
        return jnp.pad(flat, (0, total - flat.shape[0])).reshape(8, total // 8)

    g8 = _gather8(pack([G[n] for n in REPLICATED]))
    small = _adam_small(pack([W[n] for n in REPLICATED]), pack([Mo[n] for n in REPLICATED]),
                        pack([Vo[n] for n in REPLICATED]), g8)
    small = [s.reshape(-1) for s in small]
    off = 0
    for n, sz in zip(REPLICATED, sizes):
        res[n] = [s[off:off + W[n].size].reshape(W[n].shape) for s in small]
        off += sz

    loss = lax.psum(loss_acc[0, 0], ("x", "y", "c"))
    outs = [loss, grad_x[None]]
    for j in range(4):
        outs += [res[n][j].reshape(given[n].shape) for n in WEIGHTS]
    return tuple(outs)
```

```python
import functools
import math

import jax
import jax.numpy as jnp
from jax import lax
from jax.experimental import pallas as pl
from jax.experimental.pallas import tpu as pltpu

F32 = jnp.float32
BF16 = jnp.bfloat16
MESH = pl.DeviceIdType.MESH

LANES = 128
VMEM_LIMIT = 56 * 1024 * 1024
CHUNK = 64
TIME_BLOCK = 256
REC_GROUP = 8
REC_MODE = '1'
INV_MODE = '3'

NORM_EPS = 1e-6
L2_EPS = 1e-6
RWKV_GN_EPS = 64e-5
RWKV_HEAD = 64
GDN_HEAD = 128
XA_HEAD = 128
ADAM_LR, ADAM_B1, ADAM_B2, ADAM_EPS, ADAM_WD, ADAM_STEP = 0.001, 0.9, 0.999, 1e-08, 0.01, 10

WEIGHTS = ['mix_norm_pre', 'mix_norm_post', 'w_in', 'rwkv_shift_mix', 'rwkv_w0', 'rwkv_w_up', 'rwkv_a0', 'rwkv_a_up',
           'rwkv_g_up', 'rwkv_k_k', 'rwkv_k_a', 'rwkv_r_k', 'rwkv_gn_w', 'rwkv_gn_b', 'gdn_conv_w', 'gdn_a_log',
           'gdn_dt_bias', 'gdn_norm_w', 'w_branch_rwkv', 'w_branch_gdn', 'w_mix_out', 'xa_norm_pre', 'xa_norm_mem',
           'xa_norm_post', 'xa_w_q', 'xa_w_kv', 'xa_w_o', 'mlp_norm_pre', 'mlp_norm_post', 'mlp_w_up', 'mlp_w_down']
COL_SHARDED = ['w_in', 'rwkv_w_up', 'rwkv_a_up', 'rwkv_g_up', 'gdn_conv_w', 'w_branch_rwkv', 'w_branch_gdn', 'xa_w_o',
               'mlp_w_up']
ROW_SHARDED = ['w_mix_out', 'xa_w_q', 'xa_w_kv', 'mlp_w_down']
SHARDED = COL_SHARDED + ROW_SHARDED
SMALL_SHARDED = ['rwkv_w_up', 'rwkv_a_up', 'rwkv_g_up', 'gdn_conv_w']
BIG_SHARDED = [n for n in SHARDED if n not in SMALL_SHARDED]
REPLICATED = [n for n in WEIGHTS if n not in SHARDED]


ANY = pl.BlockSpec(memory_space=pl.ANY)
_PENDING = []


def _pcall(body, **kw):
    deps = tuple(_PENDING) if "in_specs" in kw else ()
    if not deps:
        return pl.pallas_call(body, **kw)
    del _PENDING[:]
    n_in = len(kw["in_specs"])
    kw["in_specs"] = list(kw["in_specs"]) + [ANY] * len(deps)

    def with_deps(*refs):
        return body(*refs[:n_in], *refs[n_in + len(deps):])

    call = pl.pallas_call(with_deps, **kw)
    return lambda *args: call(*args, *deps)


def _params(sem=None):
    return pltpu.CompilerParams(dimension_semantics=sem, vmem_limit_bytes=VMEM_LIMIT)


def _pick(n, pref, mult=LANES):
    t = (min(n, pref) // mult) * mult
    while t >= mult:
        if n % t == 0:
            return t
        t -= mult
    return n


def _dg(a, b, ca, cb):
    if a.ndim == 3:
        return lax.dot_general(a, b, (((ca + 1,), (cb + 1,)), ((0,), (0,))), preferred_element_type=F32)
    return lax.dot_general(a, b, (((ca,), (cb,)), ((), ())), preferred_element_type=F32)


def _split(x):
    hi = x.astype(BF16)
    lo = (x - hi.astype(F32)).astype(BF16)
    return hi, lo


def _dot_raw(a, b, ca, cb, mode):
    if mode == '1':
        return _dg(a.astype(BF16), b.astype(BF16), ca, cb)
    if mode == 'cb':
        ah, al = _split(a)
        bb = b.astype(BF16)
        return _dg(ah, bb, ca, cb) + _dg(al, bb, ca, cb)
    if mode == 'ca':
        bh, bl = _split(b)
        ab = a.astype(BF16)
        return _dg(ab, bh, ca, cb) + _dg(ab, bl, ca, cb)
    ah, al = _split(a)
    bh, bl = _split(b)
    return _dg(ah, bh, ca, cb) + (_dg(ah, bl, ca, cb) + _dg(al, bh, ca, cb))


@functools.partial(jax.custom_vjp, nondiff_argnums=(2, 3, 4))
def _dot_diff(a, b, ca, cb, mode):
    return _dot_raw(a, b, ca, cb, mode)


def _dot_diff_fwd(a, b, ca, cb, mode):
    return _dot_raw(a, b, ca, cb, mode), (a, b)


def _dot_diff_bwd(ca, cb, mode, res, ct):
    a, b = res
    if mode == 'ca':
        da = jnp.zeros_like(a)
    else:
        bmode = {'1': '1', '3': '3', 'cb': 'cb'}[mode]
        if ca == 1:
            da = _dot_raw(ct, b, 1, 1 if cb == 0 else 0, bmode)
        else:
            da = _dot_raw(b, ct, 1 if cb == 0 else 0, 1, 'ca' if bmode == 'cb' else bmode)
    if mode == 'cb':
        db = jnp.zeros_like(b)
    else:
        amode = {'1': '1', '3': '3', 'ca': 'ca'}[mode]
        if cb == 0:
            db = _dot_raw(a, ct, 0 if ca == 1 else 1, 0, amode)
        else:
            db = _dot_raw(ct, a, 0, 0 if ca == 1 else 1, 'cb' if amode == 'ca' else amode)
    return da, db


_dot_diff.defvjp(_dot_diff_fwd, _dot_diff_bwd)


class _Ops:
    def __init__(self, diff):
        self.diff = diff

    def dot(self, a, b, ca=1, cb=0, mode='1'):
        if self.diff:
            return _dot_diff(a, b, ca, cb, mode)
        return _dot_raw(a, b, ca, cb, mode)


_RAW = _Ops(False)
_DIFF = _Ops(True)


def _sigmoid(x):
    return 1.0 / (1.0 + jnp.exp(-x))


def _softplus(x):
    return jnp.maximum(x, 0.0) + jnp.log(1.0 + jnp.exp(-jnp.abs(x)))


def _rms(x, g, eps=NORM_EPS):
    return x * lax.rsqrt(jnp.mean(x * x, axis=-1, keepdims=True) + eps) * g


def _head_matrix(width, head):
    ch = lax.broadcasted_iota(jnp.int32, (width, LANES), 0)
    hh = lax.broadcasted_iota(jnp.int32, (width, LANES), 1)
    return (lax.shift_right_logical(ch, int(math.log2(head))) == hh).astype(BF16)


def _headsum(ops, x, head):
    e = _head_matrix(x.shape[-1], head)
    s = ops.dot(x, e, 1, 0, 'cb')
    return ops.dot(s, e, 1, 1, 'cb')


def _f_rms(ops, x, g):
    return (_rms(x, g),)


def _f_resid(ops, h, y, g_post, g_next):
    h1 = h + _rms(y, g_post)
    return h1, _rms(h1, g_next)


def _f_final(ops, h, y, tgt, g_post):
    e = h + _rms(y, g_post) - tgt
    return (0.5 * jnp.mean(e * e, axis=-1, keepdims=True),)


def _f_act(ops, up):
    r = jnp.maximum(up, 0.0)
    return (r * r,)


def _f_merge(ops, g_rw, g_gdn, b_rw, b_gdn):
    return (_sigmoid(g_rw) * b_rw + _sigmoid(g_gdn) * b_gdn,)


def _f_rwkv_pre(ops, xr, xk, xv, xlw, xla, xlg, w0, a0, k_k, k_a, w_up, a_up, g_up):
    z = w0 + ops.dot(jnp.tanh(xlw), w_up)
    lwd = -jnp.exp(-_softplus(-z) - 0.5)
    a = _sigmoid(a0 + ops.dot(xla, a_up))
    gate = ops.dot(_sigmoid(xlg), g_up)
    kk = xk * k_k
    kk = kk * lax.rsqrt(_headsum(ops, kk * kk, RWKV_HEAD) + L2_EPS)
    k2 = xk * (1.0 + (a - 1.0) * k_a)
    return xr, lwd, k2, xv, -kk, kk * a, gate


def _f_rwkv_post(ops, y, r, k2, v, gate, r_k, gn_w, gn_b):
    inv = 1.0 / RWKV_HEAD
    yc = y - _headsum(ops, y, RWKV_HEAD) * inv
    var = _headsum(ops, yc * yc, RWKV_HEAD) * inv
    yn = yc * lax.rsqrt(var + RWKV_GN_EPS) * gn_w + gn_b
    bonus = _headsum(ops, r * k2 * r_k, RWKV_HEAD) * v
    return ((yn + bonus) * gate,)


def _f_gdn_pre(ops, qc, kc, vc, ba, alog_p, dt_p, *, heads):
    lane = lax.broadcasted_iota(jnp.int32, (1, LANES), 1)
    beta = _sigmoid(ba)
    g = -jnp.exp(alog_p) * _softplus(ba + dt_p)
    gb = jnp.where(lane < heads, beta, g)
    width = qc.shape[-1]
    row = lax.broadcasted_iota(jnp.int32, (LANES, width), 0)
    col_head = lax.shift_right_logical(lax.broadcasted_iota(jnp.int32, (LANES, width), 1), int(math.log2(GDN_HEAD)))
    beta_x = ops.dot(gb, (row == col_head).astype(BF16), 1, 0, 'cb')
    g_x = ops.dot(gb, (row == col_head + heads).astype(BF16), 1, 0, 'cb')
    qn = qc * lax.rsqrt(_headsum(ops, qc * qc, GDN_HEAD) + L2_EPS)
    kn = kc * lax.rsqrt(_headsum(ops, kc * kc, GDN_HEAD) + L2_EPS)
    return qn, kn, vc, beta_x, g_x


def _f_gdn_post(ops, o, z, nw):
    ms = _headsum(ops, o * o, GDN_HEAD) * (1.0 / GDN_HEAD)
    return (o * lax.rsqrt(ms + NORM_EPS) * nw * (z * _sigmoid(z)),)


def _f_xattn(ops, q, k, v):
    width = q.shape[-1]
    lane = lax.broadcasted_iota(jnp.int32, (1, width), 1)
    out = jnp.zeros_like(q)
    for h in range(width // XA_HEAD):
        m = ((lane >= h * XA_HEAD) & (lane < (h + 1) * XA_HEAD)).astype(F32)
        s = ops.dot(q * m, k, 1, 1) * (XA_HEAD ** -0.5)
        s = s - jnp.max(s, axis=-1, keepdims=True)
        e = jnp.exp(s)
        pr = e / jnp.sum(e, axis=-1, keepdims=True)
        out = out + ops.dot(pr, v) * m
    return (out,)


def _inv_unit_lower(ops, a):
    n = a.shape[-1]
    ii = lax.broadcasted_iota(jnp.int32, (n, n), 0)
    jj = lax.broadcasted_iota(jnp.int32, (n, n), 1)
    p = (ii == jj).astype(F32) + a
    q = a
    k = 2
    while k < n:
        q = ops.dot(q, q, 1, 0, INV_MODE)
        p = p + ops.dot(p, q, 1, 0, INV_MODE)
        k *= 2
    return p


def _const(mask, batch):
    return jnp.broadcast_to(mask.astype(BF16), (batch,) + mask.shape)


def _rwkv_chunk(ops, s, r, lw, k, v, a, b):
    g, c, _ = r.shape
    ii = lax.broadcasted_iota(jnp.int32, (c, c), 0)
    jj = lax.broadcasted_iota(jnp.int32, (c, c), 1)
    incl = ii >= jj
    strict = ii > jj
    cum = ops.dot(_const(incl, g), lw, 1, 0, 'ca')
    tot = ops.dot(jnp.ones((g, LANES, c), BF16), lw, 1, 0, 'ca')
    w_inv = jnp.exp(-cum)
    at = a * jnp.exp(cum - lw)
    bt = b * w_inv
    kt = k * w_inv
    rt = r * jnp.exp(cum)
    lane = lax.broadcasted_iota(jnp.int32, (2, 1, 1, LANES), 3)
    head = lax.broadcasted_iota(jnp.int32, (2, 1, 1, LANES), 0)
    hm = ((lane >= RWKV_HEAD) == (head == 1)).astype(F32)

    def per_head(x):
        return (x[None] * hm).reshape(2 * g, c, x.shape[-1])

    def both(x):
        return jnp.broadcast_to(x[None], (2,) + x.shape).reshape((2 * g,) + x.shape[1:])

    def merge(x2):
        return jnp.sum(x2.reshape(2, g, c, LANES) * hm, axis=0)

    bt2, kt2, v2 = both(bt), both(kt), both(v)
    atm, rtm = per_head(at), per_head(rt)
    base = ops.dot(at, s, 1, 1, REC_MODE)
    a_ab = jnp.where(strict, ops.dot(atm, bt2, 1, 1, REC_MODE), 0.0)
    a_ak = jnp.where(strict, ops.dot(atm, kt2, 1, 1, REC_MODE), 0.0)
    a_rb = jnp.where(incl, ops.dot(rtm, bt2, 1, 1, REC_MODE), 0.0)
    a_rk = jnp.where(incl, ops.dot(rtm, kt2, 1, 1, REC_MODE), 0.0)
    t = _inv_unit_lower(ops, a_ab)
    u = merge(ops.dot(t, both(base) + ops.dot(a_ak, v2, 1, 0, REC_MODE), 1, 0, REC_MODE))
    y = ops.dot(rt, s, 1, 1, REC_MODE) + merge(ops.dot(a_rb, both(u), 1, 0, REC_MODE) + ops.dot(a_rk, v2, 1, 0, REC_MODE))
    bi = lax.broadcasted_iota(jnp.int32, (LANES, LANES), 0) >= RWKV_HEAD
    bj = lax.broadcasted_iota(jnp.int32, (LANES, LANES), 1) >= RWKV_HEAD
    upd = jnp.where(bi == bj, ops.dot(u, bt, 0, 0, REC_MODE) + ops.dot(v, kt, 0, 0, REC_MODE), 0.0)
    return (s + upd) * jnp.exp(tot), y


def _gdn_chunk(ops, s, q, k, v, beta, g):
    nb, c, _ = q.shape
    ii = lax.broadcasted_iota(jnp.int32, (c, c), 0)
    jj = lax.broadcasted_iota(jnp.int32, (c, c), 1)
    causal = ii >= jj
    strict = ii > jj
    q = q * (GDN_HEAD ** -0.5)
    gc = ops.dot(_const(causal, nb), g, 1, 0, 'ca')
    ones = jnp.ones((nb, c, LANES), BF16)
    gc_col = ops.dot(gc, ones, 1, 1, 'cb') * (1.0 / LANES)
    gc_row = ops.dot(ones, gc, 1, 1, 'ca') * (1.0 / LANES)
    decay = jnp.where(causal, jnp.exp(jnp.where(causal, gc_col - gc_row, 0.0)), 0.0)
    kb = k * beta
    lmat = jnp.where(strict, ops.dot(kb, k, 1, 1, REC_MODE) * decay, 0.0)
    tinv = _inv_unit_lower(ops, -lmat)
    egc = jnp.exp(gc)
    u = ops.dot(tinv, v * beta, 1, 0, REC_MODE)
    w = ops.dot(tinv, kb * egc, 1, 0, REC_MODE)
    a_intra = jnp.where(causal, ops.dot(q, k, 1, 1, REC_MODE) * decay, 0.0)
    g_tot = ops.dot(jnp.ones((nb, c, c), BF16), g, 1, 0, 'ca')
    k_dec = k * jnp.exp(g_tot - gc)
    v_new = u - ops.dot(w, s, 1, 0, REC_MODE)
    o = ops.dot(q * egc, s, 1, 0, REC_MODE) + ops.dot(a_intra, v_new, 1, 0, REC_MODE)
    g_full = ops.dot(jnp.ones((nb, LANES, c), BF16), g, 1, 0, 'ca')
    s_new = s * jnp.exp(g_full) + ops.dot(k_dec, v_new, 0, 0, REC_MODE)
    return s_new, o


def _mm(name, a, b, ta=False, tb=False, out_dtype=F32, tm=1024, tn=1024, tk=2048):
    m = a.shape[1] if ta else a.shape[0]
    kd = a.shape[0] if ta else a.shape[1]
    n = b.shape[0] if tb else b.shape[1]
    assert kd == (b.shape[1] if tb else b.shape[0]), (name, a.shape, b.shape)
    tm, tn, tk = _pick(m, tm), _pick(n, tn), _pick(kd, tk)
    nk = kd // tk

    def body(a_ref, b_ref, o_ref, acc):
        kk = pl.program_id(2)
        part = _dg(a_ref[...].astype(BF16), b_ref[...].astype(BF16), 0 if ta else 1, 1 if tb else 0)
        if nk == 1:
            o_ref[...] = part.astype(o_ref.dtype)
            return

        @pl.when(kk == 0)
        def _():
            acc[...] = part

        @pl.when((kk > 0) & (kk < nk - 1))
        def _():
            acc[...] += part

        @pl.when(kk == nk - 1)
        def _():
            o_ref[...] = (acc[...] + part).astype(o_ref.dtype)

    a_spec = pl.BlockSpec((tk, tm), lambda i, j, k: (k, i)) if ta else pl.BlockSpec((tm, tk), lambda i, j, k: (i, k))
    b_spec = pl.BlockSpec((tn, tk), lambda i, j, k: (j, k)) if tb else pl.BlockSpec((tk, tn), lambda i, j, k: (k, j))
    return _pcall(
        body, name=name, grid=(m // tm, n // tn, nk), in_specs=[a_spec, b_spec],
        out_specs=pl.BlockSpec((tm, tn), lambda i, j, k: (i, j)), out_shape=jax.ShapeDtypeStruct((m, n), out_dtype),
        scratch_shapes=[pltpu.VMEM((tm, tn), F32)],
        compiler_params=_params(("parallel", "parallel", "arbitrary")))(a, b)


def _row_spec(tm, width, cb):
    return pl.BlockSpec((tm, width), lambda i: (i, cb))


def _rows_fwd(name, f, rows, params, outs, tm):
    t = rows[0][0].shape[0]
    tm = min(tm, t)
    n_in = len(rows) + len(params)
    kept = [o for o in outs if o is not None]

    def body(*refs):
        vals = f(_RAW, *[r[...].astype(F32) for r in refs[:n_in]])
        o_refs = refs[n_in:]
        j = 0
        for val, o in zip(vals, outs):
            if o is not None:
                o_refs[j][...] = val.astype(o_refs[j].dtype)
                j += 1

    in_specs = [_row_spec(tm, w, cb) for (_, w, cb) in rows]
    in_specs += [pl.BlockSpec(p.shape, lambda i: (0, 0)) for p in params]
    res = _pcall(
        body, name=name, grid=(t // tm,), in_specs=in_specs,
        out_specs=[_row_spec(tm, w, 0) for (w, _) in kept],
        out_shape=[jax.ShapeDtypeStruct((t, w), dt) for (w, dt) in kept],
        compiler_params=_params(("parallel",)))(*[r[0] for r in rows], *params)
    return res


def _rows_bwd(name, f, rows, params, cts, tm, row_out, out_widths, adds=(), loss=False, out_dtypes=None):
    t = rows[0][0].shape[0]
    tm = min(tm, t)
    n_r, n_p = len(rows), len(params)
    ct_flat = [c for cl in cts for c in cl]
    n_ct, n_add = len(ct_flat), len(adds)
    n_out = len(out_widths)

    def body(*refs):
        i = pl.program_id(0)
        ins = [r[...].astype(F32) for r in refs[:n_r + n_p]]
        ct_refs = refs[n_r + n_p:n_r + n_p + n_ct]
        add_refs = refs[n_r + n_p + n_ct:n_r + n_p + n_ct + n_add]
        o_refs = refs[n_r + n_p + n_ct + n_add:]
        outs, vjp = jax.vjp(lambda *a: f(_DIFF, *a), *ins)
        ctv, pos = [], 0
        for o, cl in zip(outs, cts):
            if loss:
                ctv.append(jnp.ones_like(o))
            elif not cl:
                ctv.append(jnp.zeros_like(o))
            else:
                acc = ct_refs[pos][...].astype(F32)
                for q in range(1, len(cl)):
                    acc = acc + ct_refs[pos + q][...].astype(F32)
                ctv.append(acc)
            pos += len(cl)
        grads = vjp(tuple(ctv))
        for ri, ro in enumerate(row_out):
            if ro is not None:
                oi, off = ro
                val = grads[ri]
                for ai, (aoi, _) in enumerate(adds):
                    if aoi == oi:
                        val = val + add_refs[ai][...].astype(F32)
                o_refs[oi][:, off:off + val.shape[1]] = val.astype(o_refs[oi].dtype)
        p_refs = o_refs[n_out:n_out + n_p]

        @pl.when(i == 0)
        def _():
            for pr in p_refs:
                pr[...] = jnp.zeros_like(pr)
            if loss:
                o_refs[n_out + n_p][...] = jnp.zeros((8, LANES), F32)

        for pi, pr in enumerate(p_refs):
            pr[...] += grads[n_r + pi]
        if loss:
            o_refs[n_out + n_p][...] += jnp.sum(outs[0])

    in_specs = [_row_spec(tm, w, cb) for (_, w, cb) in rows]
    in_specs += [pl.BlockSpec(p.shape, lambda i: (0, 0)) for p in params]
    in_specs += [_row_spec(tm, w, cb) for (_, w, cb) in ct_flat]
    in_specs += [_row_spec(tm, w, cb) for (_, (_, w, cb)) in adds]
    out_specs = [_row_spec(tm, w, 0) for w in out_widths]
    out_specs += [pl.BlockSpec(p.shape, lambda i: (0, 0)) for p in params]
    out_dtypes = out_dtypes or [F32] * n_out
    out_shape = [jax.ShapeDtypeStruct((t, w), dt) for w, dt in zip(out_widths, out_dtypes)]
    out_shape += [jax.ShapeDtypeStruct(p.shape, F32) for p in params]
    if loss:
        out_specs.append(pl.BlockSpec((8, LANES), lambda i: (0, 0)))
        out_shape.append(jax.ShapeDtypeStruct((8, LANES), F32))
    res = _pcall(
        body, name=name, grid=(t // tm,), in_specs=in_specs, out_specs=out_specs, out_shape=out_shape,
        compiler_params=_params(("arbitrary",)))(
            *[r[0] for r in rows], *params, *[c[0] for c in ct_flat], *[a[1][0] for a in adds])
    return res[:n_out], res[n_out:n_out + n_p], res[n_out + n_p:]


def _shift_down(x, k):
    row = lax.broadcasted_iota(jnp.int32, x.shape, 0)
    return jnp.where(row >= k, pltpu.roll(x, k, 0), 0.0)


def _shift_up(x, k):
    t = x.shape[0]
    row = lax.broadcasted_iota(jnp.int32, x.shape, 0)
    return jnp.where(row < t - k, pltpu.roll(x, t - k, 0), 0.0)


def _tokshift_fwd(p, mix, n_main, lora_cb):
    t = p.shape[0]
    nblk = mix.shape[1] // LANES

    def src(j):
        return (0, jnp.where(j < n_main, j, j - n_main + lora_cb))

    def body(p_ref, mix_ref, o_ref):
        xv = p_ref[...]
        o_ref[...] = xv + (_shift_down(xv, 1) - xv) * mix_ref[...]

    return _pcall(
        body, name="tokshift_fwd", grid=(nblk,),
        in_specs=[pl.BlockSpec((t, LANES), src), pl.BlockSpec((1, LANES), lambda j: (0, j))],
        out_specs=pl.BlockSpec((t, LANES), lambda j: (0, j)),
        out_shape=jax.ShapeDtypeStruct((t, mix.shape[1]), F32), compiler_params=_params(("parallel",)))(p, mix)


def _tokshift_bwd(dxm, p, mix, n_main, lora_cb):
    t = p.shape[0]
    nblk = mix.shape[1] // LANES

    def src(j):
        return (0, jnp.where(j < n_main, j, j - n_main + lora_cb))

    def body(d_ref, p_ref, mix_ref, dp_ref, dmix_ref):
        xv, d, mx = p_ref[...], d_ref[...], mix_ref[...]
        dp_ref[...] = (d * (1.0 - mx) + _shift_up(d * mx, 1)).astype(dp_ref.dtype)
        dmix_ref[...] = jnp.sum(d * (_shift_down(xv, 1) - xv), axis=0, keepdims=True)

    return _pcall(
        body, name="tokshift_bwd", grid=(nblk,),
        in_specs=[pl.BlockSpec((t, LANES), lambda j: (0, j)), pl.BlockSpec((t, LANES), src),
                  pl.BlockSpec((1, LANES), lambda j: (0, j))],
        out_specs=[pl.BlockSpec((t, LANES), lambda j: (0, j)), pl.BlockSpec((1, LANES), lambda j: (0, j))],
        out_shape=[jax.ShapeDtypeStruct((t, mix.shape[1]), BF16), jax.ShapeDtypeStruct((1, mix.shape[1]), F32)],
        compiler_params=_params(("parallel",)))(dxm, p, mix)


def _conv_pre(xv, w):
    return (w[3:4, :] * xv + w[2:3, :] * _shift_down(xv, 1) + w[1:2, :] * _shift_down(xv, 2)
            + w[0:1, :] * _shift_down(xv, 3))


def _conv_fwd(p, w8, cb0):
    t = p.shape[0]
    width = w8.shape[1]

    def body(p_ref, w_ref, o_ref):
        c = _conv_pre(p_ref[...], w_ref[...])
        o_ref[...] = c * _sigmoid(c)

    return _pcall(
        body, name="conv_fwd", grid=(width // LANES,),
        in_specs=[pl.BlockSpec((t, LANES), lambda j: (0, j + cb0)), pl.BlockSpec((8, LANES), lambda j: (0, j))],
        out_specs=pl.BlockSpec((t, LANES), lambda j: (0, j)),
        out_shape=jax.ShapeDtypeStruct((t, width), F32), compiler_params=_params(("parallel",)))(p, w8)


def _conv_bwd(dy, p, w8, cb0):
    t = p.shape[0]
    width = w8.shape[1]

    def body(d_ref, p_ref, w_ref, dx_ref, dw_ref):
        xv, w = p_ref[...], w_ref[...]
        c = _conv_pre(xv, w)
        sg = _sigmoid(c)
        dc = d_ref[...] * (sg * (1.0 + c * (1.0 - sg)))
        dx_ref[...] = (w[3:4, :] * dc + w[2:3, :] * _shift_up(dc, 1) + w[1:2, :] * _shift_up(dc, 2)
                       + w[0:1, :] * _shift_up(dc, 3)).astype(dx_ref.dtype)
        row =lax.broadcasted_iota(jnp.int32, (8, LANES), 0)
        dw = jnp.zeros((8, LANES), F32)
        for j in range(4):
            sj = jnp.sum(dc * (xv if j == 3 else _shift_down(xv, 3 - j)), axis=0, keepdims=True)
            dw = dw + jnp.where(row == j, sj, 0.0)
        dw_ref[...] = dw

    return _pcall(
        body, name="conv_bwd", grid=(width // LANES,),
        in_specs=[pl.BlockSpec((t, LANES), lambda j: (0, j)), pl.BlockSpec((t, LANES), lambda j: (0, j + cb0)),
                  pl.BlockSpec((8, LANES), lambda j: (0, j))],
        out_specs=[pl.BlockSpec((t, LANES), lambda j: (0, j)), pl.BlockSpec((8, LANES), lambda j: (0, j))],
        out_shape=[jax.ShapeDtypeStruct((t, width), BF16), jax.ShapeDtypeStruct((8, width), F32)],
        compiler_params=_params(("parallel",)))(dy, p, w8)


def _rec_fwd(name, chunk_fn, ins, nblocks):
    t = ins[0][0].shape[0]
    tb = min(TIME_BLOCK, t)
    nt, ncb = t // tb, tb // CHUNK
    n_in = len(ins)
    grp = REC_GROUP if nblocks % REC_GROUP == 0 and all(c0 % REC_GROUP == 0 for (_, c0) in ins) else 1
    width = grp * LANES

    def body(*refs):
        in_refs, y_ref, sall_ref, s_scr = refs[:n_in], refs[n_in], refs[n_in + 1], refs[n_in + 2]

        @pl.when(pl.program_id(1) == 0)
        def _():
            s_scr[...] = jnp.zeros_like(s_scr)

        def step(ci, s):
            sl = pl.ds(pl.multiple_of(ci * CHUNK, CHUNK), CHUNK)
            lanes = [slice(gi * LANES, (gi + 1) * LANES) for gi in range(grp)]
            s1, y = chunk_fn(_RAW, s, *[jnp.stack([r[sl, ln] for ln in lanes]) for r in in_refs])
            for gi, ln in enumerate(lanes):
                sall_ref[gi, ci] = s[gi]
                y_ref[sl, ln] = y[gi]
            return s1

        s_scr[...] = lax.fori_loop(0, ncb, step, s_scr[...])

    in_specs = [pl.BlockSpec((tb, width), functools.partial(lambda h, tt, cb: (tt, cb + h), cb=c0 // grp))
                for (_, c0) in ins]
    return _pcall(
        body, name=name, grid=(nblocks // grp, nt), in_specs=in_specs,
        out_specs=[pl.BlockSpec((tb, width), lambda h, tt: (tt, h)),
                   pl.BlockSpec((grp, ncb, LANES, LANES), lambda h, tt: (h, tt, 0, 0))],
        out_shape=[jax.ShapeDtypeStruct((t, nblocks * LANES), F32),
                   jax.ShapeDtypeStruct((nblocks, t // CHUNK, LANES, LANES), F32)],
        scratch_shapes=[pltpu.VMEM((grp, LANES, LANES), F32)],
        compiler_params=_params(("parallel", "arbitrary")))(*[a for (a, _) in ins])


def _rec_bwd(name, chunk_fn, ins, dy, sall, nblocks):
    t = ins[0][0].shape[0]
    tb = min(TIME_BLOCK, t)
    nt, ncb = t // tb, tb // CHUNK
    n_in = len(ins)
    grp = REC_GROUP if nblocks % REC_GROUP == 0 and all(c0 % REC_GROUP == 0 for (_, c0) in ins) else 1
    width = grp * LANES

    def body(*refs):
        in_refs, dy_ref, sall_ref = refs[:n_in], refs[n_in], refs[n_in + 1]
        d_refs, ds_scr = refs[n_in + 2:2 * n_in + 2], refs[2 * n_in + 2]

        @pl.when(pl.program_id(1) == 0)
        def _():
            ds_scr[...] = jnp.zeros_like(ds_scr)

        def step(j, ds):
            ci = ncb - 1 - j
            sl = pl.ds(pl.multiple_of(ci * CHUNK, CHUNK), CHUNK)
            lanes = [slice(gi * LANES, (gi + 1) * LANES) for gi in range(grp)]
            s0 = jnp.stack([sall_ref[gi, ci] for gi in range(grp)])
            vals = [jnp.stack([r[sl, ln] for ln in lanes]) for r in in_refs]
            _, vjp = jax.vjp(lambda *a: chunk_fn(_DIFF, *a), s0, *vals)
            grads = vjp((ds, jnp.stack([dy_ref[sl, ln] for ln in lanes])))
            for dr, gval in zip(d_refs, grads[1:]):
                for gi, ln in enumerate(lanes):
                    dr[sl, ln] = gval[gi]
            return grads[0]

        ds_scr[...] = lax.fori_loop(0, ncb, step, ds_scr[...])

    in_specs = [pl.BlockSpec((tb, width), functools.partial(lambda h, tt, cb: (nt - 1 - tt, cb + h), cb=c0 // grp))
                for (_, c0) in ins]
    in_specs.append(pl.BlockSpec((tb, width), lambda h, tt: (nt - 1 - tt, h)))
    in_specs.append(pl.BlockSpec((grp, ncb, LANES, LANES), lambda h, tt: (h, nt - 1 - tt, 0, 0)))
    return _pcall(
        body, name=name, grid=(nblocks // grp, nt), in_specs=in_specs,
        out_specs=[pl.BlockSpec((tb, width), lambda h, tt: (nt - 1 - tt, h)) for _ in ins],
        out_shape=[jax.ShapeDtypeStruct((t, nblocks * LANES), F32) for _ in ins],
        scratch_shapes=[pltpu.VMEM((grp, LANES, LANES), F32)],
        compiler_params=_params(("parallel", "arbitrary")))(*[a for (a, _) in ins], dy, sall)


def _pick_rows(r, c):
    tr = max(8, ((1 << 18) // c) // 8 * 8)
    while tr > 8 and r % tr:
        tr -= 8
    return tr if r % tr == 0 else r


def _sum4(name, own, r3):
    _, r, c = r3.shape
    tr = _pick_rows(r, c)

    def body(own_ref, r_ref, o_ref):
        o_ref[...] = ((own_ref[...].astype(F32) + r_ref[0].astype(F32)) + r_ref[1].astype(F32)) + r_ref[2].astype(F32)

    return _pcall(
        body, name=name, grid=(r // tr,),
        in_specs=[pl.BlockSpec((tr, c), lambda i: (i, 0)), pl.BlockSpec((3, tr, c), lambda i: (0, i, 0))],
        out_specs=pl.BlockSpec((tr, c), lambda i: (i, 0)), out_shape=jax.ShapeDtypeStruct((r, c), F32),
        compiler_params=_params(("parallel",)))(own, r3)


def _adam_math(w, g, m, v):
    m = ADAM_B1 * m + (1.0 - ADAM_B1) * g
    v = ADAM_B2 * v + (1.0 - ADAM_B2) * (g * g)
    m_hat = m / (1.0 - ADAM_B1 ** ADAM_STEP)
    v_hat = v / (1.0 - ADAM_B2 ** ADAM_STEP)
    return -ADAM_LR * (m_hat / (jnp.sqrt(v_hat) + ADAM_EPS) + ADAM_WD * w), m, v


def _adam_big(name, w, m, v, ga, gb):
    r, c = w.shape
    tr = _pick_rows(r, c)

    def body(w_ref, m_ref, v_ref, ga_ref, gb_ref, g_ref, d_ref, mo_ref, vo_ref):
        g = ga_ref[...] + gb_ref[...]
        g_ref[...] = g
        d_ref[...], mo_ref[...], vo_ref[...] = _adam_math(w_ref[...], g, m_ref[...], v_ref[...])

    spec = pl.BlockSpec((tr, c), lambda i: (i, 0))
    return _pcall(
        body, name=name, grid=(r // tr,), in_specs=[spec] * 5, out_specs=[spec] * 4,
        out_shape=[jax.ShapeDtypeStruct((r, c), F32)] * 4, compiler_params=_params(("parallel",)))(w, m, v, ga, gb)


def _adam_small(w, m, v, g8):
    shape = w.shape

    def body(w_ref, m_ref, v_ref, g8_ref, g_ref, d_ref, mo_ref, vo_ref):
        g = g8_ref[0]
        for dev in range(1, 8):
            g = g + g8_ref[dev]
        g_ref[...] = g
        d_ref[...], mo_ref[...], vo_ref[...] = _adam_math(w_ref[...], g, m_ref[...], v_ref[...])

    return _pcall(
        body, name="adam_small", out_shape=[jax.ShapeDtypeStruct(shape, F32)] * 4,
        compiler_params=_params())(w, m, v, g8)


def _place():
    x, y, c = lax.axis_index("x"), lax.axis_index("y"), lax.axis_index("c")
    return x, y, c, [(1 - x, y), (x, 1 - y), (1 - x, 1 - y)]


def _remote(src, dst, send_sems, recv_sems, k, dev):
    return pltpu.make_async_remote_copy(src_ref=src, dst_ref=dst, send_sem=send_sems.at[k], recv_sem=recv_sems.at[k],
                                        device_id=dev, device_id_type=MESH)


HBM = pl.BlockSpec(memory_space=pltpu.HBM)
SEM = pl.BlockSpec(memory_space=pltpu.SEMAPHORE)


def _send3_start(name, src, land_shape, gather):
    def body(src_ref, land_ref, send_sem, recv_sem, src_thru, land_thru, token):
        x, y, c, chips = _place()
        for k, (px, py) in enumerate(chips):
            s_k = src_ref if gather else src_ref.at[2 * px + py]
            d_k = land_ref.at[2 * x + y] if gather else land_ref.at[k]
            pltpu.make_async_remote_copy(src_ref=s_k, dst_ref=d_k, send_sem=send_sem, recv_sem=recv_sem,
                                         device_id=(px, py, c), device_id_type=MESH).start()
        token[...] = jnp.zeros_like(token)

    return pl.pallas_call(
        body, name=name,
        out_shape=(pltpu.SemaphoreType.DMA(()), pltpu.SemaphoreType.DMA(()), pltpu.HBM(src.shape, src.dtype),
                   pltpu.HBM(land_shape, src.dtype), jax.ShapeDtypeStruct((8, LANES), F32)),
        in_specs=(HBM, HBM), out_specs=(SEM, SEM, HBM, HBM, pl.BlockSpec(memory_space=pltpu.VMEM)),
        input_output_aliases={0: 2, 1: 3},
        compiler_params=pltpu.CompilerParams(has_side_effects=pltpu.SideEffectType.DATAFLOW_SIDE_EFFECTING),
    )(pltpu.with_memory_space_constraint(src, pltpu.HBM),
      pltpu.with_memory_space_constraint(lax.empty(land_shape, src.dtype), pltpu.HBM))


def _send3_wait(name, started, after):
    send_sem, recv_sem, src_thru, land_thru, _ = started

    def body(src_ref, land_ref, send_sem, recv_sem, after_ref, src_dead, got_ref):
        x, y, c, chips = _place()
        three = land_ref.at[pl.ds(0, 3)]
        cp = pltpu.make_async_remote_copy(src_ref=three, dst_ref=three, send_sem=send_sem, recv_sem=recv_sem,
                                          device_id=(chips[0][0], chips[0][1], c), device_id_type=MESH)
        cp.wait_send()
        cp.wait_recv()

    return pl.pallas_call(
        body, name=name,
        out_shape=(pltpu.HBM(src_thru.shape, src_thru.dtype), pltpu.HBM(land_thru.shape, land_thru.dtype)),
        in_specs=(HBM, HBM, SEM, SEM, ANY), out_specs=(HBM, HBM), input_output_aliases={0: 0, 1: 1},
        compiler_params=pltpu.CompilerParams(has_side_effects=pltpu.SideEffectType.DATAFLOW_SIDE_EFFECTING),
    )(src_thru, land_thru, send_sem, recv_sem, after)


def _swap_sibling(name, part):
    def body(src, out, send_sems, recv_sems):
        x, y, c, _ = _place()
        cp = _remote(src, out, send_sems, recv_sems, 0, (x, y, 1 - c))
        cp.start()
        cp.wait_recv()
        cp.wait_send()

    return _pcall(
        body, name=name, in_specs=[ANY], out_specs=ANY, out_shape=jax.ShapeDtypeStruct(part.shape, part.dtype),
        scratch_shapes=[pltpu.SemaphoreType.DMA((1,)), pltpu.SemaphoreType.DMA((1,))])(part)


def _gather8(flat):
    def body(src, out, send_sems, recv_sems, local_sem):
        x, y, c, _ = _place()
        own = pltpu.make_async_copy(src, out.at[4 * x + 2 * y + c], local_sem)
        own.start()
        peers = []
        for mask in range(1, 8):
            px = x ^ ((mask >> 2) & 1)
            py = y ^ ((mask >> 1) & 1)
            pc = c ^ (mask & 1)
            peers.append((px, py, pc))
        sends = [_remote(src, out.at[4 * x + 2 * y + c], send_sems, recv_sems, k, dev) for k, dev in enumerate(peers)]
        for cp in sends:
            cp.start()
        for k, (px, py, pc) in enumerate(peers):
            _remote(src, out.at[4 * px + 2 * py + pc], send_sems, recv_sems, k, (px, py, pc)).wait_recv()
        for cp in sends:
            cp.wait_send()
        own.wait()

    return _pcall(
        body, name="gather8_small", in_specs=[ANY], out_specs=ANY,
        out_shape=jax.ShapeDtypeStruct((8,) + flat.shape, flat.dtype),
        scratch_shapes=[pltpu.SemaphoreType.DMA((7,)), pltpu.SemaphoreType.DMA((7,)), pltpu.SemaphoreType.DMA(())],
    )(flat)


def _pad_cols(a, width):
    return jnp.pad(a, ((0, 0), (0, width - a.shape[1])))


def _col_shards(full):
    r, c = full.shape
    return full.reshape(r, 4, c // 4).transpose(1, 0, 2)


def kernel(x, mem, mix_norm_pre, mix_norm_post, w_in, rwkv_shift_mix, rwkv_w0, rwkv_w_up, rwkv_a0, rwkv_a_up, rwkv_g_up, rwkv_k_k, rwkv_k_a, rwkv_r_k, rwkv_gn_w, rwkv_gn_b, gdn_conv_w, gdn_a_log, gdn_dt_bias, gdn_norm_w, w_branch_rwkv, w_branch_gdn, w_mix_out, xa_norm_pre, xa_norm_mem, xa_norm_post, xa_w_q, xa_w_kv, xa_w_o, mlp_norm_pre, mlp_norm_post, mlp_w_up, mlp_w_down, loss_target, m_mix_norm_pre, m_mix_norm_post, m_w_in, m_rwkv_shift_mix, m_rwkv_w0, m_rwkv_w_up, m_rwkv_a0, m_rwkv_a_up, m_rwkv_g_up, m_rwkv_k_k, m_rwkv_k_a, m_rwkv_r_k, m_rwkv_gn_w, m_rwkv_gn_b, m_gdn_conv_w, m_gdn_a_log, m_gdn_dt_bias, m_gdn_norm_w, m_w_branch_rwkv, m_w_branch_gdn, m_w_mix_out, m_xa_norm_pre, m_xa_norm_mem, m_xa_norm_post, m_xa_w_q, m_xa_w_kv, m_xa_w_o, m_mlp_norm_pre, m_mlp_norm_post, m_mlp_w_up, m_mlp_w_down, v_mix_norm_pre, v_mix_norm_post, v_w_in, v_rwkv_shift_mix, v_rwkv_w0, v_rwkv_w_up, v_rwkv_a0, v_rwkv_a_up, v_rwkv_g_up, v_rwkv_k_k, v_rwkv_k_a, v_rwkv_r_k, v_rwkv_gn_w, v_rwkv_gn_b, v_gdn_conv_w, v_gdn_a_log, v_gdn_dt_bias, v_gdn_norm_w, v_w_branch_rwkv, v_w_branch_gdn, v_w_mix_out, v_xa_norm_pre, v_xa_norm_mem, v_xa_norm_post, v_xa_w_q, v_xa_w_kv, v_xa_w_o, v_mlp_norm_pre, v_mlp_norm_post, v_mlp_w_up, v_mlp_w_down):
    given = dict(locals())
    W = {n: given[n] if given[n].ndim == 2 else given[n][0] for n in WEIGHTS}
    Mo = {n: given['m_' + n].reshape(W[n].shape) for n in WEIGHTS}
    Vo = {n: given['v_' + n].reshape(W[n].shape) for n in WEIGHTS}
    xs, mems, tgt = x[0], mem[0], loss_target[0]
    T, D = xs.shape
    RW = W['rwkv_w0'].shape[-1]
    GW = W['gdn_conv_w'].shape[-1] * 4 // 3
    NH = W['gdn_a_log'].shape[-1]
    LW, LA, LG = W['rwkv_w_up'].shape[0], W['rwkv_a_up'].shape[0], W['rwkv_g_up'].shape[0]
    XW = W['xa_w_q'].shape[-1]
    assert LW <= LANES and LA <= LANES and LG % LANES == 0 and 2 * NH <= LANES

    del _PENDING[:]
    me = 2 * lax.axis_index("x") + lax.axis_index("y")
    small_sizes = [W[n].size for n in SMALL_SHARDED]
    small_len = -(-sum(small_sizes) // 1024) * 1024

    def pack_small(vals):
        flat = jnp.concatenate([vv.reshape(-1).astype(F32) for vv in vals])
        return jnp.pad(flat, (0, small_len - flat.shape[0])).reshape(small_len // 1024, 1024)

    def unpack_small(packed):
        flat, out, off = packed.reshape(-1), [], 0
        for n, sz in zip(SMALL_SHARDED, small_sizes):
            out.append(flat[off:off + sz].reshape(W[n].shape))
            off += sz
        return out

    gstarted = {}

    def gstart(n, after=None):
        local = pack_small([W[s] for s in SMALL_SHARDED]) if n == 'small' else W[n].astype(BF16)
        if after is not None:
            local, _ = lax.optimization_barrier((local, after))
        gstarted[n] = _send3_start("gather_start_" + n, local, (4,) + local.shape, True)
        _PENDING.append(gstarted[n][4])

    gstart('w_in')
    gstart('small')

    def gathered(n, after):
        src, land = _send3_wait("gather_wait_" + n, gstarted[n], after)
        return lax.dynamic_update_index_in_dim(land, src, me, 0)

    def gathered_full(n, after):
        g = gathered(n, after)
        if n in COL_SHARDED:
            return jnp.concatenate([g[0], g[1], g[2], g[3]], axis=1)
        return g.reshape((4 * g.shape[1],) + g.shape[2:])

    RC = 3 * RW + LW + LA + LG
    o_z_raw = RC + 3 * GW
    o_b_raw = o_z_raw + GW
    o_grw_raw = o_b_raw + 2 * NH
    O_GQ = 3 * RW
    O_GRW = O_GQ + 3 * GW
    O_GGDN = O_GRW + D
    O_Z = O_GGDN + D
    O_LW = O_Z + GW
    O_LA = O_LW + LANES
    O_LG = O_LA + LANES
    O_BA = O_LG + LG
    END = O_BA + LANES
    NP = -(-END // 1024) * 1024
    XM = 3 * RW + 2 * LANES + LG
    assert O_GRW % D == 0 and O_Z % GW == 0 and O_LG % LG == 0 and (3 * RW + 2 * LANES) % LG == 0

    def pad_layout(raw, lead):
        return jnp.concatenate([
            raw[:, :3 * RW], raw[:, RC:RC + 3 * GW], raw[:, o_grw_raw:o_grw_raw + 2 * D], raw[:, o_z_raw:o_z_raw + GW],
            _pad_cols(raw[:, 3 * RW:3 * RW + LW], LANES), _pad_cols(raw[:, 3 * RW + LW:3 * RW + LW + LA], LANES),
            raw[:, 3 * RW + LW + LA:RC], _pad_cols(raw[:, o_b_raw:o_b_raw + 2 * NH], LANES),
            jnp.zeros((lead, NP - END), raw.dtype)], axis=1)

    def unpad_layout(pd):
        return jnp.concatenate([
            pd[:, :3 * RW], pd[:, O_LW:O_LW + LW], pd[:, O_LA:O_LA + LA], pd[:, O_LG:O_LG + LG],
            pd[:, O_GQ:O_GQ + 3 * GW], pd[:, O_Z:O_Z + GW], pd[:, O_BA:O_BA + 2 * NH], pd[:, O_GRW:O_GRW + 2 * D]], axis=1)

    def xm_layout(raw):
        return jnp.concatenate([raw[:, :3 * RW], _pad_cols(raw[:, 3 * RW:3 * RW + LW], LANES),
                                _pad_cols(raw[:, 3 * RW + LW:3 * RW + LW + LA], LANES), raw[:, 3 * RW + LW + LA:]], axis=1)

    def xm_unlayout(pd):
        return jnp.concatenate([pd[:, :3 * RW], pd[:, 3 * RW:3 * RW + LW], pd[:, 3 * RW + LANES:3 * RW + LANES + LA],
                                pd[:, 3 * RW + 2 * LANES:]], axis=1)

    mix_x = xm_layout(W['rwkv_shift_mix'])
    r_k = W['rwkv_r_k'].reshape(1, RW)
    alog_p = jnp.pad(W['gdn_a_log'], ((0, 0), (NH, LANES - 2 * NH)))
    dt_p = jnp.pad(W['gdn_dt_bias'], ((0, 0), (NH, LANES - 2 * NH)))
    nw_t = jnp.tile(W['gdn_norm_w'], (1, GW // GDN_HEAD))
    n_main, lora_cb = 3 * RW // LANES, O_LW // LANES
    f_gdn_pre = functools.partial(_f_gdn_pre, heads=NH)

    (u,) = _rows_fwd("rms_pre", _f_rms, [(xs, D, 0)], [W['mix_norm_pre']], [(D, BF16)], 256)
    full = {'w_in': gathered_full('w_in', u)}
    w_p = pad_layout(full['w_in'], D)
    small4 = gathered('small', w_p)
    for n in ['w_branch_rwkv', 'w_branch_gdn', 'w_mix_out', 'xa_w_q', 'xa_w_kv', 'xa_w_o', 'mlp_w_up']:
        gstart(n, small4)
    p = _mm("mm_in", u, w_p)
    small_parts = [unpack_small(small4[s]) for s in range(4)]
    for j, n in enumerate(SMALL_SHARDED):
        full[n] = jnp.concatenate([small_parts[s][j] for s in range(4)], axis=1)
    w_up_p = jnp.pad(full['rwkv_w_up'], ((0, LANES - LW), (0, 0)))
    a_up_p = jnp.pad(full['rwkv_a_up'], ((0, LANES - LA), (0, 0)))
    g_up = full['rwkv_g_up']
    conv8 = jnp.pad(full['gdn_conv_w'], ((0, 4), (0, 0)))
    rwkv_par = [W['rwkv_w0'], W['rwkv_a0'], W['rwkv_k_k'], W['rwkv_k_a'], w_up_p, a_up_p, g_up]
    xm = _tokshift_fwd(p, mix_x, n_main, lora_cb)
    pre_rows = [(xm, RW, 0), (xm, RW, 1), (xm, RW, 2), (xm, LANES, n_main), (xm, LANES, n_main + 1),
                (xm, LG, (3 * RW + 2 * LANES) // LG)]
    lwd, k2, av, bv, gate = _rows_fwd("rwkv_pre", _f_rwkv_pre, pre_rows, rwkv_par,
                                      [None, (RW, F32), (RW, F32), None, (RW, F32), (RW, F32), (RW, F32)], 128)
    rec_rw = [(xm, 0), (lwd, 0), (k2, 0), (xm, 2 * RW // LANES), (av, 0), (bv, 0)]
    y_rec, s_rw = _rec_fwd("rwkv_rec_fwd", _rwkv_chunk, rec_rw, RW // LANES)
    post_rows = [(y_rec, RW, 0), (xm, RW, 0), (k2, RW, 0), (xm, RW, 2), (gate, RW, 0)]
    post_par = [r_k, W['rwkv_gn_w'], W['rwkv_gn_b']]
    (y_rw,) = _rows_fwd("rwkv_post", _f_rwkv_post, post_rows, post_par, [(RW, BF16)], 128)

    qkvc = _conv_fwd(p, conv8, O_GQ // LANES)
    gpre_rows = [(qkvc, GW, 0), (qkvc, GW, 1), (qkvc, GW, 2), (p, LANES, O_BA // LANES)]
    qn, kn, beta_x, g_x = _rows_fwd("gdn_pre", f_gdn_pre, gpre_rows, [alog_p, dt_p],
                                    [(GW, F32), (GW, F32), None, (GW, F32), (GW, F32)], 128)
    rec_gd = [(qn, 0), (kn, 0), (qkvc, 2 * GW // LANES), (beta_x, 0), (g_x, 0)]
    o_gd, s_gd = _rec_fwd("gdn_rec_fwd", _gdn_chunk, rec_gd, GW // LANES)
    gpost_rows = [(o_gd, GW, 0), (p, GW, O_Z // GW)]
    (y_gdn,) = _rows_fwd("gdn_post", _f_gdn_post, gpost_rows, [nw_t], [(GW, BF16)], 128)

    full['w_branch_rwkv'] = gathered_full('w_branch_rwkv', y_gdn)
    gstart('mlp_w_down', full['w_branch_rwkv'])
    br_rw = _mm("mm_br_rw", y_rw, full['w_branch_rwkv'])
    full['w_branch_gdn'] = gathered_full('w_branch_gdn', y_gdn)
    br_gdn = _mm("mm_br_gdn", y_gdn, full['w_branch_gdn'])
    merge_rows = [(p, D, O_GRW // D), (p, D, O_GGDN // D), (br_rw, D, 0), (br_gdn, D, 0)]
    (merged,) = _rows_fwd("merge", _f_merge, merge_rows, [], [(D, BF16)], 128)
    full['w_mix_out'] = gathered_full('w_mix_out', merged)
    ymix = _mm("mm_mix_out", merged, full['w_mix_out'])
    r1_par = [W['mix_norm_post'], W['xa_norm_pre']]
    h1, cq = _rows_fwd("resid1", _f_resid, [(xs, D, 0), (ymix, D, 0)], r1_par, [(D, F32), (D, BF16)], 128)
    full['xa_w_q'] = gathered_full('xa_w_q', cq)
    q = _mm("mm_xa_q", cq, full['xa_w_q'])
    (mn,) = _rows_fwd("mem_norm", _f_rms, [(mems, D, 0)], [W['xa_norm_mem']], [(D, BF16)], 128)
    full['xa_w_kv'] = gathered_full('xa_w_kv', mn)
    kv = _mm("mm_xa_kv", mn, full['xa_w_kv'])
    k_x, v_x = kv[:, :XW], kv[:, XW:]
    (o_x,) = _rows_fwd("xattn", _f_xattn, [(q, XW, 0)], [k_x, v_x], [(XW, BF16)], 256)
    full['xa_w_o'] = gathered_full('xa_w_o', o_x)
    xo = _mm("mm_xa_o", o_x, full['xa_w_o'])
    r2_par = [W['xa_norm_post'], W['mlp_norm_pre']]
    h2, fm = _rows_fwd("resid2", _f_resid, [(h1, D, 0), (xo, D, 0)], r2_par, [(D, F32), (D, BF16)], 128)
    full['mlp_w_up'] = gathered_full('mlp_w_up', fm)
    up = _mm("mm_mlp_up", fm, full['mlp_w_up'])
    DFF = up.shape[1]
    (act,) = _rows_fwd("mlp_act", _f_act, [(up, DFF, 0)], [], [(DFF, BF16)], 64)
    full['mlp_w_down'] = gathered_full('mlp_w_down', act)
    down = _mm("mm_mlp_down", act, full['mlp_w_down'])

    G = {}
    xstarted = {}

    def xstart(n, g4):
        own = lax.dynamic_index_in_dim(g4, me, 0, keepdims=False)
        xstarted[n] = (_send3_start("grad_start_" + n, g4, (3,) + g4.shape[1:], False), own)
        _PENDING.append(xstarted[n][0][4])

    (dh2, ddown), (G['mlp_norm_post'],), (loss_acc,) = _rows_bwd(
        "final", _f_final, [(h2, D, 0), (down, D, 0), (tgt, D, 0)], [W['mlp_norm_post']], [[]], 128,
        [(0, 0), (1, 0), None], [D, D], loss=True, out_dtypes=[F32, BF16])
    d_act = _mm("mm_d_act", ddown, full['mlp_w_down'], tb=True)
    dw_down = _mm("mm_dw_down", act, ddown, ta=True)
    xstart('mlp_w_down', dw_down.astype(BF16).reshape(4, DFF // 4, D))
    (d_up,), _, _ = _rows_bwd("mlp_act_bwd", _f_act, [(up, DFF, 0)], [], [[(d_act, DFF, 0)]], 32, [(0, 0)], [DFF],
                              out_dtypes=[BF16])
    d_fm = _mm("mm_d_fm", d_up, full['mlp_w_up'], tb=True)
    dw_up = _mm("mm_dw_up", fm, d_up, ta=True)
    xstart('mlp_w_up', _col_shards(dw_up.astype(BF16)))
    (dh1, dxo), (G['xa_norm_post'], G['mlp_norm_pre']), _ = _rows_bwd(
        "resid2_bwd", _f_resid, [(h1, D, 0), (xo, D, 0)], r2_par, [[(dh2, D, 0)], [(d_fm, D, 0)]], 128,
        [(0, 0), (1, 0)], [D, D], out_dtypes=[F32, BF16])
    d_ox = _mm("mm_d_ox", dxo, full['xa_w_o'], tb=True)
    dw_o = _mm("mm_dw_o", o_x, dxo, ta=True)
    xstart('xa_w_o', _col_shards(dw_o.astype(BF16)))
    (dq,), (dk_x, dv_x), _ = _rows_bwd("xattn_bwd", _f_xattn, [(q, XW, 0)], [k_x, v_x], [[(d_ox, XW, 0)]], 128,
                                       [(0, 0)], [XW], out_dtypes=[BF16])
    d_cq = _mm("mm_d_cq", dq, full['xa_w_q'], tb=True)
    dw_q = _mm("mm_dw_q", cq, dq, ta=True)
    xstart('xa_w_q', dw_q.astype(BF16).reshape(4, D // 4, XW))
    dkv = jnp.concatenate([dk_x, dv_x], axis=1)
    d_mn = _mm("mm_d_mn", dkv, full['xa_w_kv'], tb=True)
    dw_kv = _mm("mm_dw_kv", mn, dkv, ta=True)
    xstart('xa_w_kv', dw_kv.astype(BF16).reshape(4, D // 4, 2 * XW))
    _, (G['xa_norm_mem'],), _ = _rows_bwd("mem_norm_bwd", _f_rms, [(mems, D, 0)], [W['xa_norm_mem']],
                                          [[(d_mn, D, 0)]], 128, [None], [])
    (dx_a, dymix), (G['mix_norm_post'], G['xa_norm_pre']), _ = _rows_bwd(
        "resid1_bwd", _f_resid, [(xs, D, 0), (ymix, D, 0)], r1_par, [[(dh1, D, 0)], [(d_cq, D, 0)]], 128,
        [(0, 0), (1, 0)], [D, D], out_dtypes=[F32, BF16])
    d_merged = _mm("mm_d_merged", dymix, full['w_mix_out'], tb=True)
    dw_out = _mm("mm_dw_out", merged, dymix, ta=True)
    xstart('w_mix_out', dw_out.astype(BF16).reshape(4, D // 4, D))
    (d_grw, d_ggdn, d_brw, d_bgdn), _, _ = _rows_bwd(
        "merge_bwd", _f_merge, merge_rows, [], [[(d_merged, D, 0)]], 128, [(0, 0), (1, 0), (2, 0), (3, 0)], [D, D, D, D],
        out_dtypes=[BF16] * 4)
    d_yrw = _mm("mm_d_yrw", d_brw, full['w_branch_rwkv'], tb=True)
    dw_brw = _mm("mm_dw_brw", y_rw, d_brw, ta=True)
    xstart('w_branch_rwkv', _col_shards(dw_brw.astype(BF16)))
    d_ygdn = _mm("mm_d_ygdn", d_bgdn, full['w_branch_gdn'], tb=True)
    dw_bgdn = _mm("mm_dw_bgdn", y_gdn, d_bgdn, ta=True)
    xstart('w_branch_gdn', _col_shards(dw_bgdn.astype(BF16)))

    (d_y, d_r1, d_k2a, d_v1, d_gate), (d_rk, G['rwkv_gn_w'], G['rwkv_gn_b']), _ = _rows_bwd(
        "rwkv_post_bwd", _f_rwkv_post, post_rows, post_par, [[(d_yrw, RW, 0)]], 64,
        [(0, 0), (1, 0), (2, 0), (3, 0), (4, 0)], [RW] * 5)
    d_r2, d_lwd, d_k2b, d_v2, d_av, d_bv = _rec_bwd("rwkv_rec_bwd", _rwkv_chunk, rec_rw, d_y, s_rw, RW // LANES)
    pre_cts = [[(d_r1, RW, 0), (d_r2, RW, 0)], [(d_lwd, RW, 0)], [(d_k2a, RW, 0), (d_k2b, RW, 0)],
               [(d_v1, RW, 0), (d_v2, RW, 0)], [(d_av, RW, 0)], [(d_bv, RW, 0)], [(d_gate, RW, 0)]]
    pre_out = [(0, 0), (0, RW), (0, 2 * RW), (0, 3 * RW), (0, 3 * RW + LANES), (0, 3 * RW + 2 * LANES)]
    (d_xm,), pre_pg, _ = _rows_bwd("rwkv_pre_bwd", _f_rwkv_pre, pre_rows, rwkv_par, pre_cts, 64, pre_out, [XM])
    G['rwkv_w0'], G['rwkv_a0'], G['rwkv_k_k'], G['rwkv_k_a'], d_wup_p, d_aup_p, d_gup = pre_pg
    d_prw, d_mix_x = _tokshift_bwd(d_xm, p, mix_x, n_main, lora_cb)

    (d_ogd, d_z), (d_nw_t,), _ = _rows_bwd("gdn_post_bwd", _f_gdn_post, gpost_rows, [nw_t], [[(d_ygdn, GW, 0)]], 64,
                                           [(0, 0), (1, 0)], [GW, GW], out_dtypes=[F32, BF16])
    d_qn, d_kn, d_vg, d_betax, d_gx = _rec_bwd("gdn_rec_bwd", _gdn_chunk, rec_gd, d_ogd, s_gd, GW // LANES)
    gpre_cts = [[(d_qn, GW, 0)], [(d_kn, GW, 0)], [(d_vg, GW, 0)], [(d_betax, GW, 0)], [(d_gx, GW, 0)]]
    (d_qkvc, d_ba), (d_alog_p, d_dt_p), _ = _rows_bwd(
        "gdn_pre_bwd", f_gdn_pre, gpre_rows, [alog_p, dt_p], gpre_cts, 64,
        [(0, 0), (0, GW), (0, 2 * GW), (1, 0)], [3 * GW, LANES], out_dtypes=[F32, BF16])
    d_pqkv, d_conv8 = _conv_bwd(d_qkvc, p, conv8, O_GQ // LANES)

    dp = jnp.concatenate([d_prw[:, :3 * RW], d_pqkv, d_grw, d_ggdn, d_z, d_prw[:, 3 * RW:], d_ba,
                          jnp.zeros((T, NP - END), BF16)], axis=1)
    small_g = [_col_shards(d_wup_p[:LW]), _col_shards(d_aup_p[:LA]), _col_shards(d_gup), _col_shards(d_conv8[:4])]
    xstart('small', jnp.stack([pack_small([g[s] for g in small_g]) for s in range(4)]))
    dw_p = _mm("mm_dw_in", u, dp, ta=True)
    xstart('w_in', _col_shards(unpad_layout(dw_p.astype(BF16))))
    d_u = _mm("mm_d_u", dp, w_p, tb=True)
    (grad_x,), (G['mix_norm_pre'],), _ = _rows_bwd(
        "rms_pre_bwd", _f_rms, [(xs, D, 0)], [W['mix_norm_pre']], [[(d_u, D, 0)]], 128, [(0, 0)], [D],
        adds=[(0, (dx_a, D, 0))])

    G['rwkv_shift_mix'] = xm_unlayout(d_mix_x)
    G['rwkv_r_k'] = d_rk
    G['gdn_a_log'] = d_alog_p[:, NH:2 * NH]
    G['gdn_dt_bias'] = d_dt_p[:, NH:2 * NH]
    G['gdn_norm_w'] = jnp.sum(d_nw_t.reshape(GW // GDN_HEAD, GDN_HEAD), axis=0, keepdims=True)

    res = {}
    for n in xstarted:
        started, own = xstarted[n]
        _, land = _send3_wait("grad_wait_" + n, started, grad_x)
        part = _sum4("sum4_" + n, own, land)
        other = _swap_sibling("swap_" + n, part)
        if n == 'small':
            for sn, pa, pb in zip(SMALL_SHARDED, unpack_small(part), unpack_small(other)):
                res[sn] = _adam_big("adam_" + sn, W[sn], Mo[sn], Vo[sn], pa, pb)
        else:
            res[n] = _adam_big("adam_" + n, W[n], Mo[n], Vo[n], part, other)

    sizes = [-(-W[n].size // LANES) * LANES for n in REPLICATED]
    total = -(-sum(sizes) // (8 * LANES)) * (8 * LANES)

    def pack(vals):
        parts = [jnp.pad(vv.reshape(-1), (0, sz - vv.size)) for vv, sz in zip(vals, sizes)]
        flat = jnp.concatenate(parts)
        return jnp.pad(flat, (0, total - flat.shape[0])).reshape(8, total // 8)

    g8 = _gather8(pack([G[n] for n in REPLICATED]))
    small = _adam_small(pack([W[n] for n in REPLICATED]), pack([Mo[n] for n in REPLICATED]),
                        pack([Vo[n] for n in REPLICATED]), g8)
    small = [s.reshape(-1) for s in small]
    off = 0
    for n, sz in zip(REPLICATED, sizes):
        res[n] = [s[off:off + W[n].size].reshape(W[n].shape) for s in small]
        off += sz

    loss = lax.psum(loss_acc[0, 0], ("x", "y", "c"))
    outs = [loss, grad_x[None]]
    for j in range(4):
        outs += [res[n][j].reshape(given[n].shape) for n in WEIGHTS]
    return tuple(outs)
```

```python
import functools
import math

import jax
import jax.numpy as jnp
from jax import lax
from jax.experimental import pallas as pl
from jax.experimental.pallas import tpu as pltpu

F32 = jnp.float32
BF16 = jnp.bfloat16
MESH = pl.DeviceIdType.MESH

LANES = 128
VMEM_LIMIT = 56 * 1024 * 1024
CHUNK = 64
TIME_BLOCK = 256
REC_GROUP = 8
REC_MODE = '1'
INV_MODE = '1'

NORM_EPS = 1e-6
L2_EPS = 1e-6
RWKV_GN_EPS = 64e-5
RWKV_HEAD = 64
GDN_HEAD = 128
XA_HEAD = 128
ADAM_LR, ADAM_B1, ADAM_B2, ADAM_EPS, ADAM_WD, ADAM_STEP = 0.001, 0.9, 0.999, 1e-08, 0.01, 10

WEIGHTS = ['mix_norm_pre', 'mix_norm_post', 'w_in', 'rwkv_shift_mix', 'rwkv_w0', 'rwkv_w_up', 'rwkv_a0', 'rwkv_a_up',
           'rwkv_g_up', 'rwkv_k_k', 'rwkv_k_a', 'rwkv_r_k', 'rwkv_gn_w', 'rwkv_gn_b', 'gdn_conv_w', 'gdn_a_log',
           'gdn_dt_bias', 'gdn_norm_w', 'w_branch_rwkv', 'w_branch_gdn', 'w_mix_out', 'xa_norm_pre', 'xa_norm_mem',
           'xa_norm_post', 'xa_w_q', 'xa_w_kv', 'xa_w_o', 'mlp_norm_pre', 'mlp_norm_post', 'mlp_w_up', 'mlp_w_down']
COL_SHARDED = ['w_in', 'rwkv_w_up', 'rwkv_a_up', 'rwkv_g_up', 'gdn_conv_w', 'w_branch_rwkv', 'w_branch_gdn', 'xa_w_o',
               'mlp_w_up']
ROW_SHARDED = ['w_mix_out', 'xa_w_q', 'xa_w_kv', 'mlp_w_down']
SHARDED = COL_SHARDED + ROW_SHARDED
SMALL_SHARDED = ['rwkv_w_up', 'rwkv_a_up', 'rwkv_g_up', 'gdn_conv_w']
BIG_SHARDED = [n for n in SHARDED if n not in SMALL_SHARDED]
REPLICATED = [n for n in WEIGHTS if n not in SHARDED]


ANY = pl.BlockSpec(memory_space=pl.ANY)
_PENDING = []


def _pcall(body, **kw):
    deps = tuple(_PENDING) if "in_specs" in kw else ()
    if not deps:
        return pl.pallas_call(body, **kw)
    del _PENDING[:]
    n_in = len(kw["in_specs"])
    kw["in_specs"] = list(kw["in_specs"]) + [ANY] * len(deps)

    def with_deps(*refs):
        return body(*refs[:n_in], *refs[n_in + len(deps):])

    call = pl.pallas_call(with_deps, **kw)
    return lambda *args: call(*args, *deps)


def _params(sem=None):
    return pltpu.CompilerParams(dimension_semantics=sem, vmem_limit_bytes=VMEM_LIMIT)


def _pick(n, pref, mult=LANES):
    t = (min(n, pref) // mult) * mult
    while t >= mult:
        if n % t == 0:
            return t
        t -= mult
    return n


def _dg(a, b, ca, cb):
    if a.ndim == 3:
        return lax.dot_general(a, b, (((ca + 1,), (cb + 1,)), ((0,), (0,))), preferred_element_type=F32)
    return lax.dot_general(a, b, (((ca,), (cb,)), ((), ())), preferred_element_type=F32)


def _split(x):
    hi = x.astype(BF16)
    lo = (x - hi.astype(F32)).astype(BF16)
    return hi, lo


def _dot_raw(a, b, ca, cb, mode):
    if mode == '1':
        return _dg(a.astype(BF16), b.astype(BF16), ca, cb)
    if mode == 'cb':
        ah, al = _split(a)
        bb = b.astype(BF16)
        return _dg(ah, bb, ca, cb) + _dg(al, bb, ca, cb)
    if mode == 'ca':
        bh, bl = _split(b)
        ab = a.astype(BF16)
        return _dg(ab, bh, ca, cb) + _dg(ab, bl, ca, cb)
    ah, al = _split(a)
    bh, bl = _split(b)
    return _dg(ah, bh, ca, cb) + (_dg(ah, bl, ca, cb) + _dg(al, bh, ca, cb))


@functools.partial(jax.custom_vjp, nondiff_argnums=(2, 3, 4))
def _dot_diff(a, b, ca, cb, mode):
    return _dot_raw(a, b, ca, cb, mode)


def _dot_diff_fwd(a, b, ca, cb, mode):
    return _dot_raw(a, b, ca, cb, mode), (a, b)


def _dot_diff_bwd(ca, cb, mode, res, ct):
    a, b = res
    if mode == 'ca':
        da = jnp.zeros_like(a)
    else:
        bmode = {'1': '1', '3': '3', 'cb': 'cb'}[mode]
        if ca == 1:
            da = _dot_raw(ct, b, 1, 1 if cb == 0 else 0, bmode)
        else:
            da = _dot_raw(b, ct, 1 if cb == 0 else 0, 1, 'ca' if bmode == 'cb' else bmode)
    if mode == 'cb':
        db = jnp.zeros_like(b)
    else:
        amode = {'1': '1', '3': '3', 'ca': 'ca'}[mode]
        if cb == 0:
            db = _dot_raw(a, ct, 0 if ca == 1 else 1, 0, amode)
        else:
            db = _dot_raw(ct, a, 0, 0 if ca == 1 else 1, 'cb' if amode == 'ca' else amode)
    return da, db


_dot_diff.defvjp(_dot_diff_fwd, _dot_diff_bwd)


class _Ops:
    def __init__(self, diff):
        self.diff = diff

    def dot(self, a, b, ca=1, cb=0, mode='1'):
        if self.diff:
            return _dot_diff(a, b, ca, cb, mode)
        return _dot_raw(a, b, ca, cb, mode)


_RAW = _Ops(False)
_DIFF = _Ops(True)


def _sigmoid(x):
    return 1.0 / (1.0 + jnp.exp(-x))


def _softplus(x):
    return jnp.maximum(x, 0.0) + jnp.log(1.0 + jnp.exp(-jnp.abs(x)))


def _rms(x, g, eps=NORM_EPS):
    return x * lax.rsqrt(jnp.mean(x * x, axis=-1, keepdims=True) + eps) * g


def _head_matrix(width, head):
    ch = lax.broadcasted_iota(jnp.int32, (width, LANES), 0)
    hh = lax.broadcasted_iota(jnp.int32, (width, LANES), 1)
    return (lax.shift_right_logical(ch, int(math.log2(head))) == hh).astype(BF16)


def _headsum(ops, x, head):
    e = _head_matrix(x.shape[-1], head)
    s = ops.dot(x, e, 1, 0, 'cb')
    return ops.dot(s, e, 1, 1, 'cb')


def _f_rms(ops, x, g):
    return (_rms(x, g),)


def _f_resid(ops, h, y, g_post, g_next):
    h1 = h + _rms(y, g_post)
    return h1, _rms(h1, g_next)


def _f_final(ops, h, y, tgt, g_post):
    e = h + _rms(y, g_post) - tgt
    return (0.5 * jnp.mean(e * e, axis=-1, keepdims=True),)


def _f_act(ops, up):
    r = jnp.maximum(up, 0.0)
    return (r * r,)


def _f_merge(ops, g_rw, g_gdn, b_rw, b_gdn):
    return (_sigmoid(g_rw) * b_rw + _sigmoid(g_gdn) * b_gdn,)


def _f_rwkv_pre(ops, xr, xk, xv, xlw, xla, xlg, w0, a0, k_k, k_a, w_up, a_up, g_up):
    z = w0 + ops.dot(jnp.tanh(xlw), w_up)
    lwd = -jnp.exp(-_softplus(-z) - 0.5)
    a = _sigmoid(a0 + ops.dot(xla, a_up))
    gate = ops.dot(_sigmoid(xlg), g_up)
    kk = xk * k_k
    kk = kk * lax.rsqrt(_headsum(ops, kk * kk, RWKV_HEAD) + L2_EPS)
    k2 = xk * (1.0 + (a - 1.0) * k_a)
    return xr, lwd, k2, xv, -kk, kk * a, gate


def _f_rwkv_post(ops, y, r, k2, v, gate, r_k, gn_w, gn_b):
    inv = 1.0 / RWKV_HEAD
    yc = y - _headsum(ops, y, RWKV_HEAD) * inv
    var = _headsum(ops, yc * yc, RWKV_HEAD) * inv
    yn = yc * lax.rsqrt(var + RWKV_GN_EPS) * gn_w + gn_b
    bonus = _headsum(ops, r * k2 * r_k, RWKV_HEAD) * v
    return ((yn + bonus) * gate,)


def _f_gdn_pre(ops, qc, kc, vc, ba, alog_p, dt_p, *, heads):
    lane = lax.broadcasted_iota(jnp.int32, (1, LANES), 1)
    beta = _sigmoid(ba)
    g = -jnp.exp(alog_p) * _softplus(ba + dt_p)
    gb = jnp.where(lane < heads, beta, g)
    width = qc.shape[-1]
    row = lax.broadcasted_iota(jnp.int32, (LANES, width), 0)
    col_head = lax.shift_right_logical(lax.broadcasted_iota(jnp.int32, (LANES, width), 1), int(math.log2(GDN_HEAD)))
    beta_x = ops.dot(gb, (row == col_head).astype(BF16), 1, 0, 'cb')
    g_x = ops.dot(gb, (row == col_head + heads).astype(BF16), 1, 0, 'cb')
    qn = qc * lax.rsqrt(_headsum(ops, qc * qc, GDN_HEAD) + L2_EPS)
    kn = kc * lax.rsqrt(_headsum(ops, kc * kc, GDN_HEAD) + L2_EPS)
    return qn, kn, vc, beta_x, g_x


def _f_gdn_post(ops, o, z, nw):
    ms = _headsum(ops, o * o, GDN_HEAD) * (1.0 / GDN_HEAD)
    return (o * lax.rsqrt(ms + NORM_EPS) * nw * (z * _sigmoid(z)),)


def _f_xattn(ops, q, k, v):
    width = q.shape[-1]
    lane = lax.broadcasted_iota(jnp.int32, (1, width), 1)
    out = jnp.zeros_like(q)
    for h in range(width // XA_HEAD):
        m = ((lane >= h * XA_HEAD) & (lane < (h + 1) * XA_HEAD)).astype(F32)
        s = ops.dot(q * m, k, 1, 1) * (XA_HEAD ** -0.5)
        s = s - jnp.max(s, axis=-1, keepdims=True)
        e = jnp.exp(s)
        pr = e / jnp.sum(e, axis=-1, keepdims=True)
        out = out + ops.dot(pr, v) * m
    return (out,)


def _inv_unit_lower(ops, a):
    n = a.shape[-1]
    ii = lax.broadcasted_iota(jnp.int32, (n, n), 0)
    jj = lax.broadcasted_iota(jnp.int32, (n, n), 1)
    p = (ii == jj).astype(F32) + a
    q = a
    k = 2
    while k < n:
        q = ops.dot(q, q, 1, 0, INV_MODE)
        p = p + ops.dot(p, q, 1, 0, INV_MODE)
        k *= 2
    return p


def _const(mask, batch):
    return jnp.broadcast_to(mask.astype(BF16), (batch,) + mask.shape)


def _rwkv_chunk(ops, s, r, lw, k, v, a, b):
    g, c, _ = r.shape
    ii = lax.broadcasted_iota(jnp.int32, (c, c), 0)
    jj = lax.broadcasted_iota(jnp.int32, (c, c), 1)
    incl = ii >= jj
    strict = ii > jj
    cum = ops.dot(_const(incl, g), lw, 1, 0, 'ca')
    tot = ops.dot(jnp.ones((g, LANES, c), BF16), lw, 1, 0, 'ca')
    w_inv = jnp.exp(-cum)
    at = a * jnp.exp(cum - lw)
    bt = b * w_inv
    kt = k * w_inv
    rt = r * jnp.exp(cum)
    lane = lax.broadcasted_iota(jnp.int32, (2, 1, 1, LANES), 3)
    head = lax.broadcasted_iota(jnp.int32, (2, 1, 1, LANES), 0)
    hm = ((lane >= RWKV_HEAD) == (head == 1)).astype(F32)

    def per_head(x):
        return (x[None] * hm).reshape(2 * g, c, x.shape[-1])

    def both(x):
        return jnp.broadcast_to(x[None], (2,) + x.shape).reshape((2 * g,) + x.shape[1:])

    def merge(x2):
        return jnp.sum(x2.reshape(2, g, c, LANES) * hm, axis=0)

    bt2, kt2, v2 = both(bt), both(kt), both(v)
    atm, rtm = per_head(at), per_head(rt)
    base = ops.dot(at, s, 1, 1, REC_MODE)
    a_ab = jnp.where(strict, ops.dot(atm, bt2, 1, 1, REC_MODE), 0.0)
    a_ak = jnp.where(strict, ops.dot(atm, kt2, 1, 1, REC_MODE), 0.0)
    a_rb = jnp.where(incl, ops.dot(rtm, bt2, 1, 1, REC_MODE), 0.0)
    a_rk = jnp.where(incl, ops.dot(rtm, kt2, 1, 1, REC_MODE), 0.0)
    t = _inv_unit_lower(ops, a_ab)
    u = merge(ops.dot(t, both(base) + ops.dot(a_ak, v2, 1, 0, REC_MODE), 1, 0, REC_MODE))
    y = ops.dot(rt, s, 1, 1, REC_MODE) + merge(ops.dot(a_rb, both(u), 1, 0, REC_MODE) + ops.dot(a_rk, v2, 1, 0, REC_MODE))
    bi = lax.broadcasted_iota(jnp.int32, (LANES, LANES), 0) >= RWKV_HEAD
    bj = lax.broadcasted_iota(jnp.int32, (LANES, LANES), 1) >= RWKV_HEAD
    upd = jnp.where(bi == bj, ops.dot(u, bt, 0, 0, REC_MODE) + ops.dot(v, kt, 0, 0, REC_MODE), 0.0)
    return (s + upd) * jnp.exp(tot), y


def _gdn_chunk(ops, s, q, k, v, beta, g):
    nb, c, _ = q.shape
    ii = lax.broadcasted_iota(jnp.int32, (c, c), 0)
    jj = lax.broadcasted_iota(jnp.int32, (c, c), 1)
    causal = ii >= jj
    strict = ii > jj
    q = q * (GDN_HEAD ** -0.5)
    gc = ops.dot(_const(causal, nb), g, 1, 0, 'ca')
    ones = jnp.ones((nb, c, LANES), BF16)
    gc_col = ops.dot(gc, ones, 1, 1, 'cb') * (1.0 / LANES)
    gc_row = ops.dot(ones, gc, 1, 1, 'ca') * (1.0 / LANES)
    decay = jnp.where(causal, jnp.exp(jnp.where(causal, gc_col - gc_row, 0.0)), 0.0)
    kb = k * beta
    lmat = jnp.where(strict, ops.dot(kb, k, 1, 1, REC_MODE) * decay, 0.0)
    tinv = _inv_unit_lower(ops, -lmat)
    egc = jnp.exp(gc)
    u = ops.dot(tinv, v * beta, 1, 0, REC_MODE)
    w = ops.dot(tinv, kb * egc, 1, 0, REC_MODE)
    a_intra = jnp.where(causal, ops.dot(q, k, 1, 1, REC_MODE) * decay, 0.0)
    g_tot = ops.dot(jnp.ones((nb, c, c), BF16), g, 1, 0, 'ca')
    k_dec = k * jnp.exp(g_tot - gc)
    v_new = u - ops.dot(w, s, 1, 0, REC_MODE)
    o = ops.dot(q * egc, s, 1, 0, REC_MODE) + ops.dot(a_intra, v_new, 1, 0, REC_MODE)
    g_full = ops.dot(jnp.ones((nb, LANES, c), BF16), g, 1, 0, 'ca')
    s_new = s * jnp.exp(g_full) + ops.dot(k_dec, v_new, 0, 0, REC_MODE)
    return s_new, o


def _mm(name, a, b, ta=False, tb=False, out_dtype=F32, tm=1024, tn=1024, tk=2048):
    m = a.shape[1] if ta else a.shape[0]
    kd = a.shape[0] if ta else a.shape[1]
    n = b.shape[0] if tb else b.shape[1]
    assert kd == (b.shape[1] if tb else b.shape[0]), (name, a.shape, b.shape)
    tm, tn, tk = _pick(m, tm), _pick(n, tn), _pick(kd, tk)
    nk = kd // tk

    def body(a_ref, b_ref, o_ref, acc):
        kk = pl.program_id(2)
        part = _dg(a_ref[...].astype(BF16), b_ref[...].astype(BF16), 0 if ta else 1, 1 if tb else 0)
        if nk == 1:
            o_ref[...] = part.astype(o_ref.dtype)
            return

        @pl.when(kk == 0)
        def _():
            acc[...] = part

        @pl.when((kk > 0) & (kk < nk - 1))
        def _():
            acc[...] += part

        @pl.when(kk == nk - 1)
        def _():
            o_ref[...] = (acc[...] + part).astype(o_ref.dtype)

    a_spec = pl.BlockSpec((tk, tm), lambda i, j, k: (k, i)) if ta else pl.BlockSpec((tm, tk), lambda i, j, k: (i, k))
    b_spec = pl.BlockSpec((tn, tk), lambda i, j, k: (j, k)) if tb else pl.BlockSpec((tk, tn), lambda i, j, k: (k, j))
    return _pcall(
        body, name=name, grid=(m // tm, n // tn, nk), in_specs=[a_spec, b_spec],
        out_specs=pl.BlockSpec((tm, tn), lambda i, j, k: (i, j)), out_shape=jax.ShapeDtypeStruct((m, n), out_dtype),
        scratch_shapes=[pltpu.VMEM((tm, tn), F32)],
        compiler_params=_params(("parallel", "parallel", "arbitrary")))(a, b)


def _row_spec(tm, width, cb):
    return pl.BlockSpec((tm, width), lambda i: (i, cb))


def _rows_fwd(name, f, rows, params, outs, tm):
    t = rows[0][0].shape[0]
    tm = min(tm, t)
    n_in = len(rows) + len(params)
    kept = [o for o in outs if o is not None]

    def body(*refs):
        vals = f(_RAW, *[r[...].astype(F32) for r in refs[:n_in]])
        o_refs = refs[n_in:]
        j = 0
        for val, o in zip(vals, outs):
            if o is not None:
                o_refs[j][...] = val.astype(o_refs[j].dtype)
                j += 1

    in_specs = [_row_spec(tm, w, cb) for (_, w, cb) in rows]
    in_specs += [pl.BlockSpec(p.shape, lambda i: (0, 0)) for p in params]
    res = _pcall(
        body, name=name, grid=(t // tm,), in_specs=in_specs,
        out_specs=[_row_spec(tm, w, 0) for (w, _) in kept],
        out_shape=[jax.ShapeDtypeStruct((t, w), dt) for (w, dt) in kept],
        compiler_params=_params(("parallel",)))(*[r[0] for r in rows], *params)
    return res


def _rows_bwd(name, f, rows, params, cts, tm, row_out, out_widths, adds=(), loss=False, out_dtypes=None):
    t = rows[0][0].shape[0]
    tm = min(tm, t)
    n_r, n_p = len(rows), len(params)
    ct_flat = [c for cl in cts for c in cl]
    n_ct, n_add = len(ct_flat), len(adds)
    n_out = len(out_widths)

    def body(*refs):
        i = pl.program_id(0)
        ins = [r[...].astype(F32) for r in refs[:n_r + n_p]]
        ct_refs = refs[n_r + n_p:n_r + n_p + n_ct]
        add_refs = refs[n_r + n_p + n_ct:n_r + n_p + n_ct + n_add]
        o_refs = refs[n_r + n_p + n_ct + n_add:]
        outs, vjp = jax.vjp(lambda *a: f(_DIFF, *a), *ins)
        ctv, pos = [], 0
        for o, cl in zip(outs, cts):
            if loss:
                ctv.append(jnp.ones_like(o))
            elif not cl:
                ctv.append(jnp.zeros_like(o))
            else:
                acc = ct_refs[pos][...].astype(F32)
                for q in range(1, len(cl)):
                    acc = acc + ct_refs[pos + q][...].astype(F32)
                ctv.append(acc)
            pos += len(cl)
        grads = vjp(tuple(ctv))
        for ri, ro in enumerate(row_out):
            if ro is not None:
                oi, off = ro
                val = grads[ri]
                for ai, (aoi, _) in enumerate(adds):
                    if aoi == oi:
                        val = val + add_refs[ai][...].astype(F32)
                o_refs[oi][:, off:off + val.shape[1]] = val.astype(o_refs[oi].dtype)
        p_refs = o_refs[n_out:n_out + n_p]

        @pl.when(i == 0)
        def _():
            for pr in p_refs:
                pr[...] = jnp.zeros_like(pr)
            if loss:
                o_refs[n_out + n_p][...] = jnp.zeros((8, LANES), F32)

        for pi, pr in enumerate(p_refs):
            pr[...] += grads[n_r + pi]
        if loss:
            o_refs[n_out + n_p][...] += jnp.sum(outs[0])

    in_specs = [_row_spec(tm, w, cb) for (_, w, cb) in rows]
    in_specs += [pl.BlockSpec(p.shape, lambda i: (0, 0)) for p in params]
    in_specs += [_row_spec(tm, w, cb) for (_, w, cb) in ct_flat]
    in_specs += [_row_spec(tm, w, cb) for (_, (_, w, cb)) in adds]
    out_specs = [_row_spec(tm, w, 0) for w in out_widths]
    out_specs += [pl.BlockSpec(p.shape, lambda i: (0, 0)) for p in params]
    out_dtypes = out_dtypes or [F32] * n_out
    out_shape = [jax.ShapeDtypeStruct((t, w), dt) for w, dt in zip(out_widths, out_dtypes)]
    out_shape += [jax.ShapeDtypeStruct(p.shape, F32) for p in params]
    if loss:
        out_specs.append(pl.BlockSpec((8, LANES), lambda i: (0, 0)))
        out_shape.append(jax.ShapeDtypeStruct((8, LANES), F32))
    res = _pcall(
        body, name=name, grid=(t // tm,), in_specs=in_specs, out_specs=out_specs, out_shape=out_shape,
        compiler_params=_params(("arbitrary",)))(
            *[r[0] for r in rows], *params, *[c[0] for c in ct_flat], *[a[1][0] for a in adds])
    return res[:n_out], res[n_out:n_out + n_p], res[n_out + n_p:]


def _shift_down(x, k):
    row = lax.broadcasted_iota(jnp.int32, x.shape, 0)
    return jnp.where(row >= k, pltpu.roll(x, k, 0), 0.0)


def _shift_up(x, k):
    t = x.shape[0]
    row = lax.broadcasted_iota(jnp.int32, x.shape, 0)
    return jnp.where(row < t - k, pltpu.roll(x, t - k, 0), 0.0)


def _tokshift_fwd(p, mix, n_main, lora_cb):
    t = p.shape[0]
    nblk = mix.shape[1] // LANES

    def src(j):
        return (0, jnp.where(j < n_main, j, j - n_main + lora_cb))

    def body(p_ref, mix_ref, o_ref):
        xv = p_ref[...]
        o_ref[...] = xv + (_shift_down(xv, 1) - xv) * mix_ref[...]

    return _pcall(
        body, name="tokshift_fwd", grid=(nblk,),
        in_specs=[pl.BlockSpec((t, LANES), src), pl.BlockSpec((1, LANES), lambda j: (0, j))],
        out_specs=pl.BlockSpec((t, LANES), lambda j: (0, j)),
        out_shape=jax.ShapeDtypeStruct((t, mix.shape[1]), F32), compiler_params=_params(("parallel",)))(p, mix)


def _tokshift_bwd(dxm, p, mix, n_main, lora_cb):
    t = p.shape[0]
    nblk = mix.shape[1] // LANES

    def src(j):
        return (0, jnp.where(j < n_main, j, j - n_main + lora_cb))

    def body(d_ref, p_ref, mix_ref, dp_ref, dmix_ref):
        xv, d, mx = p_ref[...], d_ref[...], mix_ref[...]
        dp_ref[...] = (d * (1.0 - mx) + _shift_up(d * mx, 1)).astype(dp_ref.dtype)
        dmix_ref[...] = jnp.sum(d * (_shift_down(xv, 1) - xv), axis=0, keepdims=True)

    return _pcall(
        body, name="tokshift_bwd", grid=(nblk,),
        in_specs=[pl.BlockSpec((t, LANES), lambda j: (0, j)), pl.BlockSpec((t, LANES), src),
                  pl.BlockSpec((1, LANES), lambda j: (0, j))],
        out_specs=[pl.BlockSpec((t, LANES), lambda j: (0, j)), pl.BlockSpec((1, LANES), lambda j: (0, j))],
        out_shape=[jax.ShapeDtypeStruct((t, mix.shape[1]), BF16), jax.ShapeDtypeStruct((1, mix.shape[1]), F32)],
        compiler_params=_params(("parallel",)))(dxm, p, mix)


def _conv_pre(xv, w):
    return (w[3:4, :] * xv + w[2:3, :] * _shift_down(xv, 1) + w[1:2, :] * _shift_down(xv, 2)
            + w[0:1, :] * _shift_down(xv, 3))


def _conv_fwd(p, w8, cb0):
    t = p.shape[0]
    width = w8.shape[1]

    def body(p_ref, w_ref, o_ref):
        c = _conv_pre(p_ref[...], w_ref[...])
        o_ref[...] = c * _sigmoid(c)

    return _pcall(
        body, name="conv_fwd", grid=(width // LANES,),
        in_specs=[pl.BlockSpec((t, LANES), lambda j: (0, j + cb0)), pl.BlockSpec((8, LANES), lambda j: (0, j))],
        out_specs=pl.BlockSpec((t, LANES), lambda j: (0, j)),
        out_shape=jax.ShapeDtypeStruct((t, width), F32), compiler_params=_params(("parallel",)))(p, w8)


def _conv_bwd(dy, p, w8, cb0):
    t = p.shape[0]
    width = w8.shape[1]

    def body(d_ref, p_ref, w_ref, dx_ref, dw_ref):
        xv, w = p_ref[...], w_ref[...]
        c = _conv_pre(xv, w)
        sg = _sigmoid(c)
        dc = d_ref[...] * (sg * (1.0 + c * (1.0 - sg)))
        dx_ref[...] = (w[3:4, :] * dc + w[2:3, :] * _shift_up(dc, 1) + w[1:2, :] * _shift_up(dc, 2)
                       + w[0:1, :] * _shift_up(dc, 3)).astype(dx_ref.dtype)
        row =lax.broadcasted_iota(jnp.int32, (8, LANES), 0)
        dw = jnp.zeros((8, LANES), F32)
        for j in range(4):
            sj = jnp.sum(dc * (xv if j == 3 else _shift_down(xv, 3 - j)), axis=0, keepdims=True)
            dw = dw + jnp.where(row == j, sj, 0.0)
        dw_ref[...] = dw

    return _pcall(
        body, name="conv_bwd", grid=(width // LANES,),
        in_specs=[pl.BlockSpec((t, LANES), lambda j: (0, j)), pl.BlockSpec((t, LANES), lambda j: (0, j + cb0)),
                  pl.BlockSpec((8, LANES), lambda j: (0, j))],
        out_specs=[pl.BlockSpec((t, LANES), lambda j: (0, j)), pl.BlockSpec((8, LANES), lambda j: (0, j))],
        out_shape=[jax.ShapeDtypeStruct((t, width), BF16), jax.ShapeDtypeStruct((8, width), F32)],
        compiler_params=_params(("parallel",)))(dy, p, w8)


def _rec_fwd(name, chunk_fn, ins, nblocks):
    t = ins[0][0].shape[0]
    tb = min(TIME_BLOCK, t)
    nt, ncb = t // tb, tb // CHUNK
    n_in = len(ins)
    grp = REC_GROUP if nblocks % REC_GROUP == 0 and all(c0 % REC_GROUP == 0 for (_, c0) in ins) else 1
    width = grp * LANES

    def body(*refs):
        in_refs, y_ref, sall_ref, s_scr = refs[:n_in], refs[n_in], refs[n_in + 1], refs[n_in + 2]

        @pl.when(pl.program_id(1) == 0)
        def _():
            s_scr[...] = jnp.zeros_like(s_scr)

        def step(ci, s):
            sl = pl.ds(pl.multiple_of(ci * CHUNK, CHUNK), CHUNK)
            lanes = [slice(gi * LANES, (gi + 1) * LANES) for gi in range(grp)]
            s1, y = chunk_fn(_RAW, s, *[jnp.stack([r[sl, ln] for ln in lanes]) for r in in_refs])
            for gi, ln in enumerate(lanes):
                sall_ref[gi, ci] = s[gi]
                y_ref[sl, ln] = y[gi]
            return s1

        s_scr[...] = lax.fori_loop(0, ncb, step, s_scr[...])

    in_specs = [pl.BlockSpec((tb, width), functools.partial(lambda h, tt, cb: (tt, cb + h), cb=c0 // grp))
                for (_, c0) in ins]
    return _pcall(
        body, name=name, grid=(nblocks // grp, nt), in_specs=in_specs,
        out_specs=[pl.BlockSpec((tb, width), lambda h, tt: (tt, h)),
                   pl.BlockSpec((grp, ncb, LANES, LANES), lambda h, tt: (h, tt, 0, 0))],
        out_shape=[jax.ShapeDtypeStruct((t, nblocks * LANES), F32),
                   jax.ShapeDtypeStruct((nblocks, t // CHUNK, LANES, LANES), F32)],
        scratch_shapes=[pltpu.VMEM((grp, LANES, LANES), F32)],
        compiler_params=_params(("parallel", "arbitrary")))(*[a for (a, _) in ins])


def _rec_bwd(name, chunk_fn, ins, dy, sall, nblocks):
    t = ins[0][0].shape[0]
    tb = min(TIME_BLOCK, t)
    nt, ncb = t // tb, tb // CHUNK
    n_in = len(ins)
    grp = REC_GROUP if nblocks % REC_GROUP == 0 and all(c0 % REC_GROUP == 0 for (_, c0) in ins) else 1
    width = grp * LANES

    def body(*refs):
        in_refs, dy_ref, sall_ref = refs[:n_in], refs[n_in], refs[n_in + 1]
        d_refs, ds_scr = refs[n_in + 2:2 * n_in + 2], refs[2 * n_in + 2]

        @pl.when(pl.program_id(1) == 0)
        def _():
            ds_scr[...] = jnp.zeros_like(ds_scr)

        def step(j, ds):
            ci = ncb - 1 - j
            sl = pl.ds(pl.multiple_of(ci * CHUNK, CHUNK), CHUNK)
            lanes = [slice(gi * LANES, (gi + 1) * LANES) for gi in range(grp)]
            s0 = jnp.stack([sall_ref[gi, ci] for gi in range(grp)])
            vals = [jnp.stack([r[sl, ln] for ln in lanes]) for r in in_refs]
            _, vjp = jax.vjp(lambda *a: chunk_fn(_DIFF, *a), s0, *vals)
            grads = vjp((ds, jnp.stack([dy_ref[sl, ln] for ln in lanes])))
            for dr, gval in zip(d_refs, grads[1:]):
                for gi, ln in enumerate(lanes):
                    dr[sl, ln] = gval[gi]
            return grads[0]

        ds_scr[...] = lax.fori_loop(0, ncb, step, ds_scr[...])

    in_specs = [pl.BlockSpec((tb, width), functools.partial(lambda h, tt, cb: (nt - 1 - tt, cb + h), cb=c0 // grp))
                for (_, c0) in ins]
    in_specs.append(pl.BlockSpec((tb, width), lambda h, tt: (nt - 1 - tt, h)))
    in_specs.append(pl.BlockSpec((grp, ncb, LANES, LANES), lambda h, tt: (h, nt - 1 - tt, 0, 0)))
    return _pcall(
        body, name=name, grid=(nblocks // grp, nt), in_specs=in_specs,
        out_specs=[pl.BlockSpec((tb, width), lambda h, tt: (nt - 1 - tt, h)) for _ in ins],
        out_shape=[jax.ShapeDtypeStruct((t, nblocks * LANES), F32) for _ in ins],
        scratch_shapes=[pltpu.VMEM((grp, LANES, LANES), F32)],
        compiler_params=_params(("parallel", "arbitrary")))(*[a for (a, _) in ins], dy, sall)


def _pick_rows(r, c):
    tr = max(8, ((1 << 18) // c) // 8 * 8)
    while tr > 8 and r % tr:
        tr -= 8
    return tr if r % tr == 0 else r


def _sum4(name, own, r3):
    _, r, c = r3.shape
    tr = _pick_rows(r, c)

    def body(own_ref, r_ref, o_ref):
        o_ref[...] = ((own_ref[...].astype(F32) + r_ref[0].astype(F32)) + r_ref[1].astype(F32)) + r_ref[2].astype(F32)

    return _pcall(
        body, name=name, grid=(r // tr,),
        in_specs=[pl.BlockSpec((tr, c), lambda i: (i, 0)), pl.BlockSpec((3, tr, c), lambda i: (0, i, 0))],
        out_specs=pl.BlockSpec((tr, c), lambda i: (i, 0)), out_shape=jax.ShapeDtypeStruct((r, c), F32),
        compiler_params=_params(("parallel",)))(own, r3)


def _adam_math(w, g, m, v):
    m = ADAM_B1 * m + (1.0 - ADAM_B1) * g
    v = ADAM_B2 * v + (1.0 - ADAM_B2) * (g * g)
    m_hat = m / (1.0 - ADAM_B1 ** ADAM_STEP)
    v_hat = v / (1.0 - ADAM_B2 ** ADAM_STEP)
    return -ADAM_LR * (m_hat / (jnp.sqrt(v_hat) + ADAM_EPS) + ADAM_WD * w), m, v


def _adam_big(name, w, m, v, ga, gb):
    r, c = w.shape
    tr = _pick_rows(r, c)

    def body(w_ref, m_ref, v_ref, ga_ref, gb_ref, g_ref, d_ref, mo_ref, vo_ref):
        g = ga_ref[...] + gb_ref[...]
        g_ref[...] = g
        d_ref[...], mo_ref[...], vo_ref[...] = _adam_math(w_ref[...], g, m_ref[...], v_ref[...])

    spec = pl.BlockSpec((tr, c), lambda i: (i, 0))
    return _pcall(
        body, name=name, grid=(r // tr,), in_specs=[spec] * 5, out_specs=[spec] * 4,
        out_shape=[jax.ShapeDtypeStruct((r, c), F32)] * 4, compiler_params=_params(("parallel",)))(w, m, v, ga, gb)


def _adam_small(w, m, v, g8):
    shape = w.shape

    def body(w_ref, m_ref, v_ref, g8_ref, g_ref, d_ref, mo_ref, vo_ref):
        g = g8_ref[0]
        for dev in range(1, 8):
            g = g + g8_ref[dev]
        g_ref[...] = g
        d_ref[...], mo_ref[...], vo_ref[...] = _adam_math(w_ref[...], g, m_ref[...], v_ref[...])

    return _pcall(
        body, name="adam_small", out_shape=[jax.ShapeDtypeStruct(shape, F32)] * 4,
        compiler_params=_params())(w, m, v, g8)


def _place():
    x, y, c = lax.axis_index("x"), lax.axis_index("y"), lax.axis_index("c")
    return x, y, c, [(1 - x, y), (x, 1 - y), (1 - x, 1 - y)]


def _remote(src, dst, send_sems, recv_sems, k, dev):
    return pltpu.make_async_remote_copy(src_ref=src, dst_ref=dst, send_sem=send_sems.at[k], recv_sem=recv_sems.at[k],
                                        device_id=dev, device_id_type=MESH)


HBM = pl.BlockSpec(memory_space=pltpu.HBM)
SEM = pl.BlockSpec(memory_space=pltpu.SEMAPHORE)


def _send3_start(name, src, land, mode):
    land_arr = land if mode == 'forward_half' else lax.empty(land, src.dtype)
    half = land_arr.shape[1] // 2

    def body(src_ref, land_ref, send_sem, recv_sem, src_thru, land_thru, token):
        x, y, c, chips = _place()
        rows = pl.ds(pl.multiple_of(c * half, half), half)
        for k, (px, py) in enumerate(chips):
            dev = (px, py, c)
            if mode == 'gather':
                s_k, d_k = src_ref, land_ref.at[2 * x + y]
            elif mode == 'gather_half':
                s_k, d_k = src_ref.at[rows], land_ref.at[2 * x + y, rows]
            elif mode == 'forward_half':
                s_k = d_k = land_ref.at[2 * px + py, rows]
                dev = (x, y, 1 - c)
            else:
                s_k, d_k = src_ref.at[2 * px + py], land_ref.at[k]
            pltpu.make_async_remote_copy(src_ref=s_k, dst_ref=d_k, send_sem=send_sem, recv_sem=recv_sem,
                                         device_id=dev, device_id_type=MESH).start()
        token[...] = jnp.zeros_like(token)

    return pl.pallas_call(
        body, name=name,
        out_shape=(pltpu.SemaphoreType.DMA(()), pltpu.SemaphoreType.DMA(()), pltpu.HBM(src.shape, src.dtype),
                   pltpu.HBM(land_arr.shape, src.dtype), jax.ShapeDtypeStruct((8, LANES), F32)),
        in_specs=(HBM, HBM), out_specs=(SEM, SEM, HBM, HBM, pl.BlockSpec(memory_space=pltpu.VMEM)),
        input_output_aliases={0: 2, 1: 3},
        compiler_params=pltpu.CompilerParams(has_side_effects=pltpu.SideEffectType.DATAFLOW_SIDE_EFFECTING),
    )(pltpu.with_memory_space_constraint(src, pltpu.HBM), pltpu.with_memory_space_constraint(land_arr, pltpu.HBM))


def _send3_wait(name, started, after, halves=False):
    send_sem, recv_sem, src_thru, land_thru, _ = started
    half = land_thru.shape[1] // 2

    def body(src_ref, land_ref, send_sem, recv_sem, after_ref, src_dead, got_ref):
        x, y, c, chips = _place()
        three = land_ref.at[pl.ds(0, 3), pl.ds(0, half)] if halves else land_ref.at[pl.ds(0, 3)]
        cp = pltpu.make_async_remote_copy(src_ref=three, dst_ref=three, send_sem=send_sem, recv_sem=recv_sem,
                                          device_id=(chips[0][0], chips[0][1], c), device_id_type=MESH)
        cp.wait_send()
        cp.wait_recv()

    return pl.pallas_call(
        body, name=name,
        out_shape=(pltpu.HBM(src_thru.shape, src_thru.dtype), pltpu.HBM(land_thru.shape, land_thru.dtype)),
        in_specs=(HBM, HBM, SEM, SEM, ANY), out_specs=(HBM, HBM), input_output_aliases={0: 0, 1: 1},
        compiler_params=pltpu.CompilerParams(has_side_effects=pltpu.SideEffectType.DATAFLOW_SIDE_EFFECTING),
    )(src_thru, land_thru, send_sem, recv_sem, after)


def _swap_sibling(name, part):
    def body(src, out, send_sems, recv_sems):
        x, y, c, _ = _place()
        cp = _remote(src, out, send_sems, recv_sems, 0, (x, y, 1 - c))
        cp.start()
        cp.wait_recv()
        cp.wait_send()

    return _pcall(
        body, name=name, in_specs=[ANY], out_specs=ANY, out_shape=jax.ShapeDtypeStruct(part.shape, part.dtype),
        scratch_shapes=[pltpu.SemaphoreType.DMA((1,)), pltpu.SemaphoreType.DMA((1,))])(part)


def _gather8(flat):
    def body(src, out, send_sems, recv_sems, local_sem):
        x, y, c, _ = _place()
        own = pltpu.make_async_copy(src, out.at[4 * x + 2 * y + c], local_sem)
        own.start()
        peers = []
        for mask in range(1, 8):
            px = x ^ ((mask >> 2) & 1)
            py = y ^ ((mask >> 1) & 1)
            pc = c ^ (mask & 1)
            peers.append((px, py, pc))
        sends = [_remote(src, out.at[4 * x + 2 * y + c], send_sems, recv_sems, k, dev) for k, dev in enumerate(peers)]
        for cp in sends:
            cp.start()
        for k, (px, py, pc) in enumerate(peers):
            _remote(src, out.at[4 * px + 2 * py + pc], send_sems, recv_sems, k, (px, py, pc)).wait_recv()
        for cp in sends:
            cp.wait_send()
        own.wait()

    return _pcall(
        body, name="gather8_small", in_specs=[ANY], out_specs=ANY,
        out_shape=jax.ShapeDtypeStruct((8,) + flat.shape, flat.dtype),
        scratch_shapes=[pltpu.SemaphoreType.DMA((7,)), pltpu.SemaphoreType.DMA((7,)), pltpu.SemaphoreType.DMA(())],
    )(flat)


def _pad_cols(a, width):
    return jnp.pad(a, ((0, 0), (0, width - a.shape[1])))


def _col_shards(full):
    r, c = full.shape
    return full.reshape(r, 4, c // 4).transpose(1, 0, 2)


def kernel(x, mem, mix_norm_pre, mix_norm_post, w_in, rwkv_shift_mix, rwkv_w0, rwkv_w_up, rwkv_a0, rwkv_a_up, rwkv_g_up, rwkv_k_k, rwkv_k_a, rwkv_r_k, rwkv_gn_w, rwkv_gn_b, gdn_conv_w, gdn_a_log, gdn_dt_bias, gdn_norm_w, w_branch_rwkv, w_branch_gdn, w_mix_out, xa_norm_pre, xa_norm_mem, xa_norm_post, xa_w_q, xa_w_kv, xa_w_o, mlp_norm_pre, mlp_norm_post, mlp_w_up, mlp_w_down, loss_target, m_mix_norm_pre, m_mix_norm_post, m_w_in, m_rwkv_shift_mix, m_rwkv_w0, m_rwkv_w_up, m_rwkv_a0, m_rwkv_a_up, m_rwkv_g_up, m_rwkv_k_k, m_rwkv_k_a, m_rwkv_r_k, m_rwkv_gn_w, m_rwkv_gn_b, m_gdn_conv_w, m_gdn_a_log, m_gdn_dt_bias, m_gdn_norm_w, m_w_branch_rwkv, m_w_branch_gdn, m_w_mix_out, m_xa_norm_pre, m_xa_norm_mem, m_xa_norm_post, m_xa_w_q, m_xa_w_kv, m_xa_w_o, m_mlp_norm_pre, m_mlp_norm_post, m_mlp_w_up, m_mlp_w_down, v_mix_norm_pre, v_mix_norm_post, v_w_in, v_rwkv_shift_mix, v_rwkv_w0, v_rwkv_w_up, v_rwkv_a0, v_rwkv_a_up, v_rwkv_g_up, v_rwkv_k_k, v_rwkv_k_a, v_rwkv_r_k, v_rwkv_gn_w, v_rwkv_gn_b, v_gdn_conv_w, v_gdn_a_log, v_gdn_dt_bias, v_gdn_norm_w, v_w_branch_rwkv, v_w_branch_gdn, v_w_mix_out, v_xa_norm_pre, v_xa_norm_mem, v_xa_norm_post, v_xa_w_q, v_xa_w_kv, v_xa_w_o, v_mlp_norm_pre, v_mlp_norm_post, v_mlp_w_up, v_mlp_w_down):
    given = dict(locals())
    W = {n: given[n] if given[n].ndim == 2 else given[n][0] for n in WEIGHTS}
    Mo = {n: given['m_' + n].reshape(W[n].shape) for n in WEIGHTS}
    Vo = {n: given['v_' + n].reshape(W[n].shape) for n in WEIGHTS}
    xs, mems, tgt = x[0], mem[0], loss_target[0]
    T, D = xs.shape
    RW = W['rwkv_w0'].shape[-1]
    GW = W['gdn_conv_w'].shape[-1] * 4 // 3
    NH = W['gdn_a_log'].shape[-1]
    LW, LA, LG = W['rwkv_w_up'].shape[0], W['rwkv_a_up'].shape[0], W['rwkv_g_up'].shape[0]
    XW = W['xa_w_q'].shape[-1]
    assert LW <= LANES and LA <= LANES and LG % LANES == 0 and 2 * NH <= LANES

    del _PENDING[:]
    me = 2 * lax.axis_index("x") + lax.axis_index("y")
    small_sizes = [W[n].size for n in SMALL_SHARDED]
    small_len = -(-sum(small_sizes) // 1024) * 1024

    def pack_small(vals):
        flat = jnp.concatenate([vv.reshape(-1).astype(F32) for vv in vals])
        return jnp.pad(flat, (0, small_len - flat.shape[0])).reshape(small_len // 1024, 1024)

    def unpack_small(packed):
        flat, out, off = packed.reshape(-1), [], 0
        for n, sz in zip(SMALL_SHARDED, small_sizes):
            out.append(flat[off:off + sz].reshape(W[n].shape))
            off += sz
        return out

    gstarted = {}

    def gstart(n, after=None):
        local = pack_small([W[s] for s in SMALL_SHARDED]) if n == 'small' else W[n].astype(BF16)
        if after is not None:
            local, _ = lax.optimization_barrier((local, after))
        mode = 'gather_half' if n in HALVED else 'gather'
        gstarted[n] = _send3_start("gather_start_" + n, local, (4,) + local.shape, mode)
        _PENDING.append(gstarted[n][4])

    HALVED = ('w_in',)
    gstart('w_in')
    gstart('small')

    def gathered(n, after):
        src, land = _send3_wait("gather_wait_" + n, gstarted[n], after, halves=n in HALVED)
        if n in HALVED:
            fwd = _send3_start("gather_forward_" + n, src, land, 'forward_half')
            src, land = _send3_wait("gather_forward_wait_" + n, fwd, fwd[4], halves=True)
        return lax.dynamic_update_index_in_dim(land, src, me, 0)

    def gathered_full(n, after):
        g = gathered(n, after)
        if n in COL_SHARDED:
            return jnp.concatenate([g[0], g[1], g[2], g[3]], axis=1)
        return g.reshape((4 * g.shape[1],) + g.shape[2:])

    RC = 3 * RW + LW + LA + LG
    o_z_raw = RC + 3 * GW
    o_b_raw = o_z_raw + GW
    o_grw_raw = o_b_raw + 2 * NH
    O_GQ = 3 * RW
    O_GRW = O_GQ + 3 * GW
    O_GGDN = O_GRW + D
    O_Z = O_GGDN + D
    O_LW = O_Z + GW
    O_LA = O_LW + LANES
    O_LG = O_LA + LANES
    O_BA = O_LG + LG
    END = O_BA + LANES
    NP = -(-END // 1024) * 1024
    XM = 3 * RW + 2 * LANES + LG
    assert O_GRW % D == 0 and O_Z % GW == 0 and O_LG % LG == 0 and (3 * RW + 2 * LANES) % LG == 0

    def pad_layout(raw, lead):
        return jnp.concatenate([
            raw[:, :3 * RW], raw[:, RC:RC + 3 * GW], raw[:, o_grw_raw:o_grw_raw + 2 * D], raw[:, o_z_raw:o_z_raw + GW],
            _pad_cols(raw[:, 3 * RW:3 * RW + LW], LANES), _pad_cols(raw[:, 3 * RW + LW:3 * RW + LW + LA], LANES),
            raw[:, 3 * RW + LW + LA:RC], _pad_cols(raw[:, o_b_raw:o_b_raw + 2 * NH], LANES),
            jnp.zeros((lead, NP - END), raw.dtype)], axis=1)

    def unpad_layout(pd):
        return jnp.concatenate([
            pd[:, :3 * RW], pd[:, O_LW:O_LW + LW], pd[:, O_LA:O_LA + LA], pd[:, O_LG:O_LG + LG],
            pd[:, O_GQ:O_GQ + 3 * GW], pd[:, O_Z:O_Z + GW], pd[:, O_BA:O_BA + 2 * NH], pd[:, O_GRW:O_GRW + 2 * D]], axis=1)

    def xm_layout(raw):
        return jnp.concatenate([raw[:, :3 * RW], _pad_cols(raw[:, 3 * RW:3 * RW + LW], LANES),
                                _pad_cols(raw[:, 3 * RW + LW:3 * RW + LW + LA], LANES), raw[:, 3 * RW + LW + LA:]], axis=1)

    def xm_unlayout(pd):
        return jnp.concatenate([pd[:, :3 * RW], pd[:, 3 * RW:3 * RW + LW], pd[:, 3 * RW + LANES:3 * RW + LANES + LA],
                                pd[:, 3 * RW + 2 * LANES:]], axis=1)

    mix_x = xm_layout(W['rwkv_shift_mix'])
    r_k = W['rwkv_r_k'].reshape(1, RW)
    alog_p = jnp.pad(W['gdn_a_log'], ((0, 0), (NH, LANES - 2 * NH)))
    dt_p = jnp.pad(W['gdn_dt_bias'], ((0, 0), (NH, LANES - 2 * NH)))
    nw_t = jnp.tile(W['gdn_norm_w'], (1, GW // GDN_HEAD))
    n_main, lora_cb = 3 * RW // LANES, O_LW // LANES
    f_gdn_pre = functools.partial(_f_gdn_pre, heads=NH)

    (u,) = _rows_fwd("rms_pre", _f_rms, [(xs, D, 0)], [W['mix_norm_pre']], [(D, BF16)], 256)
    full = {'w_in': gathered_full('w_in', u)}
    w_p = pad_layout(full['w_in'], D)
    small4 = gathered('small', w_p)
    for n in ['w_branch_rwkv', 'w_branch_gdn', 'w_mix_out', 'xa_w_q', 'xa_w_kv', 'xa_w_o', 'mlp_w_up']:
        gstart(n, small4)
    p = _mm("mm_in", u, w_p)
    small_parts = [unpack_small(small4[s]) for s in range(4)]
    for j, n in enumerate(SMALL_SHARDED):
        full[n] = jnp.concatenate([small_parts[s][j] for s in range(4)], axis=1)
    w_up_p = jnp.pad(full['rwkv_w_up'], ((0, LANES - LW), (0, 0)))
    a_up_p = jnp.pad(full['rwkv_a_up'], ((0, LANES - LA), (0, 0)))
    g_up = full['rwkv_g_up']
    conv8 = jnp.pad(full['gdn_conv_w'], ((0, 4), (0, 0)))
    rwkv_par = [W['rwkv_w0'], W['rwkv_a0'], W['rwkv_k_k'], W['rwkv_k_a'], w_up_p, a_up_p, g_up]
    xm = _tokshift_fwd(p, mix_x, n_main, lora_cb)
    pre_rows = [(xm, RW, 0), (xm, RW, 1), (xm, RW, 2), (xm, LANES, n_main), (xm, LANES, n_main + 1),
                (xm, LG, (3 * RW + 2 * LANES) // LG)]
    lwd, k2, av, bv, gate = _rows_fwd("rwkv_pre", _f_rwkv_pre, pre_rows, rwkv_par,
                                      [None, (RW, F32), (RW, F32), None, (RW, F32), (RW, F32), (RW, F32)], 128)
    rec_rw = [(xm, 0), (lwd, 0), (k2, 0), (xm, 2 * RW // LANES), (av, 0), (bv, 0)]
    y_rec, s_rw = _rec_fwd("rwkv_rec_fwd", _rwkv_chunk, rec_rw, RW // LANES)
    post_rows = [(y_rec, RW, 0), (xm, RW, 0), (k2, RW, 0), (xm, RW, 2), (gate, RW, 0)]
    post_par = [r_k, W['rwkv_gn_w'], W['rwkv_gn_b']]
    (y_rw,) = _rows_fwd("rwkv_post", _f_rwkv_post, post_rows, post_par, [(RW, BF16)], 128)

    qkvc = _conv_fwd(p, conv8, O_GQ // LANES)
    gpre_rows = [(qkvc, GW, 0), (qkvc, GW, 1), (qkvc, GW, 2), (p, LANES, O_BA // LANES)]
    qn, kn, beta_x, g_x = _rows_fwd("gdn_pre", f_gdn_pre, gpre_rows, [alog_p, dt_p],
                                    [(GW, F32), (GW, F32), None, (GW, F32), (GW, F32)], 128)
    rec_gd = [(qn, 0), (kn, 0), (qkvc, 2 * GW // LANES), (beta_x, 0), (g_x, 0)]
    o_gd, s_gd = _rec_fwd("gdn_rec_fwd", _gdn_chunk, rec_gd, GW // LANES)
    gpost_rows = [(o_gd, GW, 0), (p, GW, O_Z // GW)]
    (y_gdn,) = _rows_fwd("gdn_post", _f_gdn_post, gpost_rows, [nw_t], [(GW, BF16)], 128)

    full['w_branch_rwkv'] = gathered_full('w_branch_rwkv', y_gdn)
    gstart('mlp_w_down', full['w_branch_rwkv'])
    br_rw = _mm("mm_br_rw", y_rw, full['w_branch_rwkv'])
    full['w_branch_gdn'] = gathered_full('w_branch_gdn', y_gdn)
    br_gdn = _mm("mm_br_gdn", y_gdn, full['w_branch_gdn'])
    merge_rows = [(p, D, O_GRW // D), (p, D, O_GGDN // D), (br_rw, D, 0), (br_gdn, D, 0)]
    (merged,) = _rows_fwd("merge", _f_merge, merge_rows, [], [(D, BF16)], 128)
    full['w_mix_out'] = gathered_full('w_mix_out', merged)
    ymix = _mm("mm_mix_out", merged, full['w_mix_out'])
    r1_par = [W['mix_norm_post'], W['xa_norm_pre']]
    h1, cq = _rows_fwd("resid1", _f_resid, [(xs, D, 0), (ymix, D, 0)], r1_par, [(D, F32), (D, BF16)], 128)
    full['xa_w_q'] = gathered_full('xa_w_q', cq)
    q = _mm("mm_xa_q", cq, full['xa_w_q'])
    (mn,) = _rows_fwd("mem_norm", _f_rms, [(mems, D, 0)], [W['xa_norm_mem']], [(D, BF16)], 128)
    full['xa_w_kv'] = gathered_full('xa_w_kv', mn)
    kv = _mm("mm_xa_kv", mn, full['xa_w_kv'])
    k_x, v_x = kv[:, :XW], kv[:, XW:]
    (o_x,) = _rows_fwd("xattn", _f_xattn, [(q, XW, 0)], [k_x, v_x], [(XW, BF16)], 256)
    full['xa_w_o'] = gathered_full('xa_w_o', o_x)
    xo = _mm("mm_xa_o", o_x, full['xa_w_o'])
    r2_par = [W['xa_norm_post'], W['mlp_norm_pre']]
    h2, fm = _rows_fwd("resid2", _f_resid, [(h1, D, 0), (xo, D, 0)], r2_par, [(D, F32), (D, BF16)], 128)
    full['mlp_w_up'] = gathered_full('mlp_w_up', fm)
    up = _mm("mm_mlp_up", fm, full['mlp_w_up'])
    DFF = up.shape[1]
    (act,) = _rows_fwd("mlp_act", _f_act, [(up, DFF, 0)], [], [(DFF, BF16)], 64)
    full['mlp_w_down'] = gathered_full('mlp_w_down', act)
    down = _mm("mm_mlp_down", act, full['mlp_w_down'])

    G = {}
    xstarted = {}

    def xstart(n, g4):
        own = lax.dynamic_index_in_dim(g4, me, 0, keepdims=False)
        xstarted[n] = (_send3_start("grad_start_" + n, g4, (3,) + g4.shape[1:], 'exchange'), own)
        _PENDING.append(xstarted[n][0][4])

    (dh2, ddown), (G['mlp_norm_post'],), (loss_acc,) = _rows_bwd(
        "final", _f_final, [(h2, D, 0), (down, D, 0), (tgt, D, 0)], [W['mlp_norm_post']], [[]], 128,
        [(0, 0), (1, 0), None], [D, D], loss=True, out_dtypes=[F32, BF16])
    d_act = _mm("mm_d_act", ddown, full['mlp_w_down'], tb=True)
    dw_down = _mm("mm_dw_down", act, ddown, ta=True, out_dtype=BF16)
    xstart('mlp_w_down', dw_down.astype(BF16).reshape(4, DFF // 4, D))
    (d_up,), _, _ = _rows_bwd("mlp_act_bwd", _f_act, [(up, DFF, 0)], [], [[(d_act, DFF, 0)]], 32, [(0, 0)], [DFF],
                              out_dtypes=[BF16])
    d_fm = _mm("mm_d_fm", d_up, full['mlp_w_up'], tb=True)
    dw_up = _mm("mm_dw_up", fm, d_up, ta=True, out_dtype=BF16)
    xstart('mlp_w_up', _col_shards(dw_up.astype(BF16)))
    (dh1, dxo), (G['xa_norm_post'], G['mlp_norm_pre']), _ = _rows_bwd(
        "resid2_bwd", _f_resid, [(h1, D, 0), (xo, D, 0)], r2_par, [[(dh2, D, 0)], [(d_fm, D, 0)]], 128,
        [(0, 0), (1, 0)], [D, D], out_dtypes=[F32, BF16])
    d_ox = _mm("mm_d_ox", dxo, full['xa_w_o'], tb=True)
    dw_o = _mm("mm_dw_o", o_x, dxo, ta=True, out_dtype=BF16)
    xstart('xa_w_o', _col_shards(dw_o.astype(BF16)))
    (dq,), (dk_x, dv_x), _ = _rows_bwd("xattn_bwd", _f_xattn, [(q, XW, 0)], [k_x, v_x], [[(d_ox, XW, 0)]], 128,
                                       [(0, 0)], [XW], out_dtypes=[BF16])
    d_cq = _mm("mm_d_cq", dq, full['xa_w_q'], tb=True)
    dw_q = _mm("mm_dw_q", cq, dq, ta=True, out_dtype=BF16)
    xstart('xa_w_q', dw_q.astype(BF16).reshape(4, D // 4, XW))
    dkv = jnp.concatenate([dk_x, dv_x], axis=1)
    d_mn = _mm("mm_d_mn", dkv, full['xa_w_kv'], tb=True)
    dw_kv = _mm("mm_dw_kv", mn, dkv, ta=True, out_dtype=BF16)
    xstart('xa_w_kv', dw_kv.astype(BF16).reshape(4, D // 4, 2 * XW))
    _, (G['xa_norm_mem'],), _ = _rows_bwd("mem_norm_bwd", _f_rms, [(mems, D, 0)], [W['xa_norm_mem']],
                                          [[(d_mn, D, 0)]], 128, [None], [])
    (dx_a, dymix), (G['mix_norm_post'], G['xa_norm_pre']), _ = _rows_bwd(
        "resid1_bwd", _f_resid, [(xs, D, 0), (ymix, D, 0)], r1_par, [[(dh1, D, 0)], [(d_cq, D, 0)]], 128,
        [(0, 0), (1, 0)], [D, D], out_dtypes=[F32, BF16])
    d_merged = _mm("mm_d_merged", dymix, full['w_mix_out'], tb=True)
    dw_out = _mm("mm_dw_out", merged, dymix, ta=True, out_dtype=BF16)
    xstart('w_mix_out', dw_out.astype(BF16).reshape(4, D // 4, D))
    (d_grw, d_ggdn, d_brw, d_bgdn), _, _ = _rows_bwd(
        "merge_bwd", _f_merge, merge_rows, [], [[(d_merged, D, 0)]], 128, [(0, 0), (1, 0), (2, 0), (3, 0)], [D, D, D, D],
        out_dtypes=[BF16] * 4)
    d_yrw = _mm("mm_d_yrw", d_brw, full['w_branch_rwkv'], tb=True)
    dw_brw = _mm("mm_dw_brw", y_rw, d_brw, ta=True, out_dtype=BF16)
    xstart('w_branch_rwkv', _col_shards(dw_brw.astype(BF16)))
    d_ygdn = _mm("mm_d_ygdn", d_bgdn, full['w_branch_gdn'], tb=True)
    dw_bgdn = _mm("mm_dw_bgdn", y_gdn, d_bgdn, ta=True, out_dtype=BF16)
    xstart('w_branch_gdn', _col_shards(dw_bgdn.astype(BF16)))

    (d_y, d_r1, d_k2a, d_v1, d_gate), (d_rk, G['rwkv_gn_w'], G['rwkv_gn_b']), _ = _rows_bwd(
        "rwkv_post_bwd", _f_rwkv_post, post_rows, post_par, [[(d_yrw, RW, 0)]], 64,
        [(0, 0), (1, 0), (2, 0), (3, 0), (4, 0)], [RW] * 5)
    d_r2, d_lwd, d_k2b, d_v2, d_av, d_bv = _rec_bwd("rwkv_rec_bwd", _rwkv_chunk, rec_rw, d_y, s_rw, RW // LANES)
    pre_cts = [[(d_r1, RW, 0), (d_r2, RW, 0)], [(d_lwd, RW, 0)], [(d_k2a, RW, 0), (d_k2b, RW, 0)],
               [(d_v1, RW, 0), (d_v2, RW, 0)], [(d_av, RW, 0)], [(d_bv, RW, 0)], [(d_gate, RW, 0)]]
    pre_out = [(0, 0), (0, RW), (0, 2 * RW), (0, 3 * RW), (0, 3 * RW + LANES), (0, 3 * RW + 2 * LANES)]
    (d_xm,), pre_pg, _ = _rows_bwd("rwkv_pre_bwd", _f_rwkv_pre, pre_rows, rwkv_par, pre_cts, 64, pre_out, [XM])
    G['rwkv_w0'], G['rwkv_a0'], G['rwkv_k_k'], G['rwkv_k_a'], d_wup_p, d_aup_p, d_gup = pre_pg
    d_prw, d_mix_x = _tokshift_bwd(d_xm, p, mix_x, n_main, lora_cb)

    (d_ogd, d_z), (d_nw_t,), _ = _rows_bwd("gdn_post_bwd", _f_gdn_post, gpost_rows, [nw_t], [[(d_ygdn, GW, 0)]], 64,
                                           [(0, 0), (1, 0)], [GW, GW], out_dtypes=[F32, BF16])
    d_qn, d_kn, d_vg, d_betax, d_gx = _rec_bwd("gdn_rec_bwd", _gdn_chunk, rec_gd, d_ogd, s_gd, GW // LANES)
    gpre_cts = [[(d_qn, GW, 0)], [(d_kn, GW, 0)], [(d_vg, GW, 0)], [(d_betax, GW, 0)], [(d_gx, GW, 0)]]
    (d_qkvc, d_ba), (d_alog_p, d_dt_p), _ = _rows_bwd(
        "gdn_pre_bwd", f_gdn_pre, gpre_rows, [alog_p, dt_p], gpre_cts, 64,
        [(0, 0), (0, GW), (0, 2 * GW), (1, 0)], [3 * GW, LANES], out_dtypes=[F32, BF16])
    d_pqkv, d_conv8 = _conv_bwd(d_qkvc, p, conv8, O_GQ // LANES)

    dp = jnp.concatenate([d_prw[:, :3 * RW], d_pqkv, d_grw, d_ggdn, d_z, d_prw[:, 3 * RW:], d_ba,
                          jnp.zeros((T, NP - END), BF16)], axis=1)
    small_g = [_col_shards(d_wup_p[:LW]), _col_shards(d_aup_p[:LA]), _col_shards(d_gup), _col_shards(d_conv8[:4])]
    xstart('small', jnp.stack([pack_small([g[s] for g in small_g]) for s in range(4)]))
    dw_p = _mm("mm_dw_in", u, dp, ta=True, out_dtype=BF16)
    xstart('w_in', _col_shards(unpad_layout(dw_p.astype(BF16))))
    d_u = _mm("mm_d_u", dp, w_p, tb=True)
    (grad_x,), (G['mix_norm_pre'],), _ = _rows_bwd(
        "rms_pre_bwd", _f_rms, [(xs, D, 0)], [W['mix_norm_pre']], [[(d_u, D, 0)]], 128, [(0, 0)], [D],
        adds=[(0, (dx_a, D, 0))])

    G['rwkv_shift_mix'] = xm_unlayout(d_mix_x)
    G['rwkv_r_k'] = d_rk
    G['gdn_a_log'] = d_alog_p[:, NH:2 * NH]
    G['gdn_dt_bias'] = d_dt_p[:, NH:2 * NH]
    G['gdn_norm_w'] = jnp.sum(d_nw_t.reshape(GW // GDN_HEAD, GDN_HEAD), axis=0, keepdims=True)

    res = {}
    for n in xstarted:
        started, own = xstarted[n]
        _, land = _send3_wait("grad_wait_" + n, started, grad_x)
        part = _sum4("sum4_" + n, own, land)
        other = _swap_sibling("swap_" + n, part)
        if n == 'small':
            for sn, pa, pb in zip(SMALL_SHARDED, unpack_small(part), unpack_small(other)):
                res[sn] = _adam_big("adam_" + sn, W[sn], Mo[sn], Vo[sn], pa, pb)
        else:
            res[n] = _adam_big("adam_" + n, W[n], Mo[n], Vo[n], part, other)

    sizes = [-(-W[n].size // LANES) * LANES for n in REPLICATED]
    total = -(-sum(sizes) // (8 * LANES)) * (8 * LANES)

    def pack(vals):
        parts = [jnp.pad(vv.reshape(-1), (0, sz - vv.size)) for vv, sz in zip(vals, sizes)]
        flat = jnp.concatenate(parts)
        return jnp.pad(flat, (0, total - flat.shape[0])).reshape(8, total // 8)

    g8 = _gather8(pack([G[n] for n in REPLICATED]))
    small = _adam_small(pack([W[n] for n in REPLICATED]), pack([Mo[n] for n in REPLICATED]),
                        pack([Vo[n] for n in REPLICATED]), g8)
    small = [s.reshape(-1) for s in small]
    off = 0
    for n, sz in zip(REPLICATED, sizes):
        res[n] = [s[off:off + W[n].size].reshape(W[n].shape) for s in small]
        off += sz

    loss = lax.psum(loss_acc[0, 0], ("x", "y", "c"))
    outs = [loss, grad_x[None]]
    for j in range(4):
        outs += [res[n][j].reshape(given[n].shape) for n in WEIGHTS]
    return tuple(outs)
```

```python
import functools
import math

import jax
import jax.numpy as jnp
from jax import lax
from jax.experimental import pallas as pl
from jax.experimental.pallas import tpu as pltpu

F32 = jnp.float32
BF16 = jnp.bfloat16
MESH = pl.DeviceIdType.MESH

LANES = 128
VMEM_LIMIT = 56 * 1024 * 1024
CHUNK = 64
TIME_BLOCK = 256
REC_GROUP = 8
REC_MODE = '1'
INV_MODE = '1'

NORM_EPS = 1e-6
L2_EPS = 1e-6
RWKV_GN_EPS = 64e-5
RWKV_HEAD = 64
GDN_HEAD = 128
XA_HEAD = 128
ADAM_LR, ADAM_B1, ADAM_B2, ADAM_EPS, ADAM_WD, ADAM_STEP = 0.001, 0.9, 0.999, 1e-08, 0.01, 10

WEIGHTS = ['mix_norm_pre', 'mix_norm_post', 'w_in', 'rwkv_shift_mix', 'rwkv_w0', 'rwkv_w_up', 'rwkv_a0', 'rwkv_a_up',
           'rwkv_g_up', 'rwkv_k_k', 'rwkv_k_a', 'rwkv_r_k', 'rwkv_gn_w', 'rwkv_gn_b', 'gdn_conv_w', 'gdn_a_log',
           'gdn_dt_bias', 'gdn_norm_w', 'w_branch_rwkv', 'w_branch_gdn', 'w_mix_out', 'xa_norm_pre', 'xa_norm_mem',
           'xa_norm_post', 'xa_w_q', 'xa_w_kv', 'xa_w_o', 'mlp_norm_pre', 'mlp_norm_post', 'mlp_w_up', 'mlp_w_down']
COL_SHARDED = ['w_in', 'rwkv_w_up', 'rwkv_a_up', 'rwkv_g_up', 'gdn_conv_w', 'w_branch_rwkv', 'w_branch_gdn', 'xa_w_o',
               'mlp_w_up']
ROW_SHARDED = ['w_mix_out', 'xa_w_q', 'xa_w_kv', 'mlp_w_down']
SHARDED = COL_SHARDED + ROW_SHARDED
SMALL_SHARDED = ['rwkv_w_up', 'rwkv_a_up', 'rwkv_g_up', 'gdn_conv_w']
BIG_SHARDED = [n for n in SHARDED if n not in SMALL_SHARDED]
REPLICATED = [n for n in WEIGHTS if n not in SHARDED]


ANY = pl.BlockSpec(memory_space=pl.ANY)
_PENDING = []


def _pcall(body, **kw):
    deps = tuple(_PENDING) if "in_specs" in kw else ()
    if not deps:
        return pl.pallas_call(body, **kw)
    del _PENDING[:]
    n_in = len(kw["in_specs"])
    kw["in_specs"] = list(kw["in_specs"]) + [ANY] * len(deps)

    def with_deps(*refs):
        return body(*refs[:n_in], *refs[n_in + len(deps):])

    call = pl.pallas_call(with_deps, **kw)
    return lambda *args: call(*args, *deps)


def _params(sem=None):
    return pltpu.CompilerParams(dimension_semantics=sem, vmem_limit_bytes=VMEM_LIMIT)


def _pick(n, pref, mult=LANES):
    t = (min(n, pref) // mult) * mult
    while t >= mult:
        if n % t == 0:
            return t
        t -= mult
    return n


def _dg(a, b, ca, cb):
    if a.ndim == 3:
        return lax.dot_general(a, b, (((ca + 1,), (cb + 1,)), ((0,), (0,))), preferred_element_type=F32)
    return lax.dot_general(a, b, (((ca,), (cb,)), ((), ())), preferred_element_type=F32)


def _split(x):
    hi = x.astype(BF16)
    lo = (x - hi.astype(F32)).astype(BF16)
    return hi, lo


def _dot_raw(a, b, ca, cb, mode):
    if mode == '1':
        return _dg(a.astype(BF16), b.astype(BF16), ca, cb)
    if mode == 'cb':
        ah, al = _split(a)
        bb = b.astype(BF16)
        return _dg(ah, bb, ca, cb) + _dg(al, bb, ca, cb)
    if mode == 'ca':
        bh, bl = _split(b)
        ab = a.astype(BF16)
        return _dg(ab, bh, ca, cb) + _dg(ab, bl, ca, cb)
    ah, al = _split(a)
    bh, bl = _split(b)
    return _dg(ah, bh, ca, cb) + (_dg(ah, bl, ca, cb) + _dg(al, bh, ca, cb))


@functools.partial(jax.custom_vjp, nondiff_argnums=(2, 3, 4))
def _dot_diff(a, b, ca, cb, mode):
    return _dot_raw(a, b, ca, cb, mode)


def _dot_diff_fwd(a, b, ca, cb, mode):
    return _dot_raw(a, b, ca, cb, mode), (a, b)


def _dot_diff_bwd(ca, cb, mode, res, ct):
    a, b = res
    if mode == 'ca':
        da = jnp.zeros_like(a)
    else:
        bmode = {'1': '1', '3': '3', 'cb': 'cb'}[mode]
        if ca == 1:
            da = _dot_raw(ct, b, 1, 1 if cb == 0 else 0, bmode)
        else:
            da = _dot_raw(b, ct, 1 if cb == 0 else 0, 1, 'ca' if bmode == 'cb' else bmode)
    if mode == 'cb':
        db = jnp.zeros_like(b)
    else:
        amode = {'1': '1', '3': '3', 'ca': 'ca'}[mode]
        if cb == 0:
            db = _dot_raw(a, ct, 0 if ca == 1 else 1, 0, amode)
        else:
            db = _dot_raw(ct, a, 0, 0 if ca == 1 else 1, 'cb' if amode == 'ca' else amode)
    return da, db


_dot_diff.defvjp(_dot_diff_fwd, _dot_diff_bwd)


class _Ops:
    def __init__(self, diff):
        self.diff = diff

    def dot(self, a, b, ca=1, cb=0, mode='1'):
        if self.diff:
            return _dot_diff(a, b, ca, cb, mode)
        return _dot_raw(a, b, ca, cb, mode)


_RAW = _Ops(False)
_DIFF = _Ops(True)


def _sigmoid(x):
    return 1.0 / (1.0 + jnp.exp(-x))


def _softplus(x):
    return jnp.maximum(x, 0.0) + jnp.log(1.0 + jnp.exp(-jnp.abs(x)))


def _rms(x, g, eps=NORM_EPS):
    return x * lax.rsqrt(jnp.mean(x * x, axis=-1, keepdims=True) + eps) * g


def _head_matrix(width, head):
    ch = lax.broadcasted_iota(jnp.int32, (width, LANES), 0)
    hh = lax.broadcasted_iota(jnp.int32, (width, LANES), 1)
    return (lax.shift_right_logical(ch, int(math.log2(head))) == hh).astype(BF16)


def _headsum(ops, x, head):
    e = _head_matrix(x.shape[-1], head)
    s = ops.dot(x, e, 1, 0, 'cb')
    return ops.dot(s, e, 1, 1, 'cb')


def _f_rms(ops, x, g):
    return (_rms(x, g),)


def _f_resid(ops, h, y, g_post, g_next):
    h1 = h + _rms(y, g_post)
    return h1, _rms(h1, g_next)


def _f_final(ops, h, y, tgt, g_post):
    e = h + _rms(y, g_post) - tgt
    return (0.5 * jnp.mean(e * e, axis=-1, keepdims=True),)


def _f_act(ops, up):
    r = jnp.maximum(up, 0.0)
    return (r * r,)


def _f_merge(ops, g_rw, g_gdn, b_rw, b_gdn):
    return (_sigmoid(g_rw) * b_rw + _sigmoid(g_gdn) * b_gdn,)


def _f_rwkv_pre(ops, xr, xk, xv, xlw, xla, xlg, w0, a0, k_k, k_a, w_up, a_up, g_up):
    z = w0 + ops.dot(jnp.tanh(xlw), w_up)
    lwd = -jnp.exp(-_softplus(-z) - 0.5)
    a = _sigmoid(a0 + ops.dot(xla, a_up))
    gate = ops.dot(_sigmoid(xlg), g_up)
    kk = xk * k_k
    kk = kk * lax.rsqrt(_headsum(ops, kk * kk, RWKV_HEAD) + L2_EPS)
    k2 = xk * (1.0 + (a - 1.0) * k_a)
    return xr, lwd, k2, xv, -kk, kk * a, gate


def _f_rwkv_post(ops, y, r, k2, v, gate, r_k, gn_w, gn_b):
    inv = 1.0 / RWKV_HEAD
    yc = y - _headsum(ops, y, RWKV_HEAD) * inv
    var = _headsum(ops, yc * yc, RWKV_HEAD) * inv
    yn = yc * lax.rsqrt(var + RWKV_GN_EPS) * gn_w + gn_b
    bonus = _headsum(ops, r * k2 * r_k, RWKV_HEAD) * v
    return ((yn + bonus) * gate,)


def _f_gdn_pre(ops, qc, kc, vc, ba, alog_p, dt_p, *, heads):
    lane = lax.broadcasted_iota(jnp.int32, (1, LANES), 1)
    beta = _sigmoid(ba)
    g = -jnp.exp(alog_p) * _softplus(ba + dt_p)
    gb = jnp.where(lane < heads, beta, g)
    width = qc.shape[-1]
    row = lax.broadcasted_iota(jnp.int32, (LANES, width), 0)
    col_head = lax.shift_right_logical(lax.broadcasted_iota(jnp.int32, (LANES, width), 1), int(math.log2(GDN_HEAD)))
    beta_x = ops.dot(gb, (row == col_head).astype(BF16), 1, 0, 'cb')
    g_x = ops.dot(gb, (row == col_head + heads).astype(BF16), 1, 0, 'cb')
    qn = qc * lax.rsqrt(_headsum(ops, qc * qc, GDN_HEAD) + L2_EPS)
    kn = kc * lax.rsqrt(_headsum(ops, kc * kc, GDN_HEAD) + L2_EPS)
    return qn, kn, vc, beta_x, g_x


def _f_gdn_post(ops, o, z, nw):
    ms = _headsum(ops, o * o, GDN_HEAD) * (1.0 / GDN_HEAD)
    return (o * lax.rsqrt(ms + NORM_EPS) * nw * (z * _sigmoid(z)),)


def _f_xattn(ops, q, k, v):
    width = q.shape[-1]
    lane = lax.broadcasted_iota(jnp.int32, (1, width), 1)
    out = jnp.zeros_like(q)
    for h in range(width // XA_HEAD):
        m = ((lane >= h * XA_HEAD) & (lane < (h + 1) * XA_HEAD)).astype(F32)
        s = ops.dot(q * m, k, 1, 1) * (XA_HEAD ** -0.5)
        s = s - jnp.max(s, axis=-1, keepdims=True)
        e = jnp.exp(s)
        pr = e / jnp.sum(e, axis=-1, keepdims=True)
        out = out + ops.dot(pr, v) * m
    return (out,)


def _inv_unit_lower(ops, a):
    n = a.shape[-1]
    ii = lax.broadcasted_iota(jnp.int32, (n, n), 0)
    jj = lax.broadcasted_iota(jnp.int32, (n, n), 1)
    p = (ii == jj).astype(F32) + a
    q = a
    k = 2
    while k < n:
        q = ops.dot(q, q, 1, 0, INV_MODE)
        p = p + ops.dot(p, q, 1, 0, INV_MODE)
        k *= 2
    return p


def _const(mask, batch):
    return jnp.broadcast_to(mask.astype(BF16), (batch,) + mask.shape)


def _rwkv_chunk(ops, s, r, lw, k, v, a, b):
    g, c, _ = r.shape
    ii = lax.broadcasted_iota(jnp.int32, (c, c), 0)
    jj = lax.broadcasted_iota(jnp.int32, (c, c), 1)
    incl = ii >= jj
    strict = ii > jj
    cum = ops.dot(_const(incl, g), lw, 1, 0, 'ca')
    tot = ops.dot(jnp.ones((g, LANES, c), BF16), lw, 1, 0, 'ca')
    w_inv = jnp.exp(-cum)
    at = a * jnp.exp(cum - lw)
    bt = b * w_inv
    kt = k * w_inv
    rt = r * jnp.exp(cum)
    lane = lax.broadcasted_iota(jnp.int32, (2, 1, 1, LANES), 3)
    head = lax.broadcasted_iota(jnp.int32, (2, 1, 1, LANES), 0)
    hm = ((lane >= RWKV_HEAD) == (head == 1)).astype(F32)

    def per_head(x):
        return (x[None] * hm).reshape(2 * g, c, x.shape[-1])

    def both(x):
        return jnp.broadcast_to(x[None], (2,) + x.shape).reshape((2 * g,) + x.shape[1:])

    def merge(x2):
        return jnp.sum(x2.reshape(2, g, c, LANES) * hm, axis=0)

    bt2, kt2, v2 = both(bt), both(kt), both(v)
    atm, rtm = per_head(at), per_head(rt)
    base = ops.dot(at, s, 1, 1, REC_MODE)
    a_ab = jnp.where(strict, ops.dot(atm, bt2, 1, 1, REC_MODE), 0.0)
    a_ak = jnp.where(strict, ops.dot(atm, kt2, 1, 1, REC_MODE), 0.0)
    a_rb = jnp.where(incl, ops.dot(rtm, bt2, 1, 1, REC_MODE), 0.0)
    a_rk = jnp.where(incl, ops.dot(rtm, kt2, 1, 1, REC_MODE), 0.0)
    t = _inv_unit_lower(ops, a_ab)
    u = merge(ops.dot(t, both(base) + ops.dot(a_ak, v2, 1, 0, REC_MODE), 1, 0, REC_MODE))
    y = ops.dot(rt, s, 1, 1, REC_MODE) + merge(ops.dot(a_rb, both(u), 1, 0, REC_MODE) + ops.dot(a_rk, v2, 1, 0, REC_MODE))
    bi = lax.broadcasted_iota(jnp.int32, (LANES, LANES), 0) >= RWKV_HEAD
    bj = lax.broadcasted_iota(jnp.int32, (LANES, LANES), 1) >= RWKV_HEAD
    upd = jnp.where(bi == bj, ops.dot(u, bt, 0, 0, REC_MODE) + ops.dot(v, kt, 0, 0, REC_MODE), 0.0)
    return (s + upd) * jnp.exp(tot), y


def _gdn_chunk(ops, s, q, k, v, beta, g):
    nb, c, _ = q.shape
    ii = lax.broadcasted_iota(jnp.int32, (c, c), 0)
    jj = lax.broadcasted_iota(jnp.int32, (c, c), 1)
    causal = ii >= jj
    strict = ii > jj
    q = q * (GDN_HEAD ** -0.5)
    gc = ops.dot(_const(causal, nb), g, 1, 0, 'ca')
    ones = jnp.ones((nb, c, LANES), BF16)
    gc_col = ops.dot(gc, ones, 1, 1, 'cb') * (1.0 / LANES)
    gc_row = ops.dot(ones, gc, 1, 1, 'ca') * (1.0 / LANES)
    decay = jnp.where(causal, jnp.exp(jnp.where(causal, gc_col - gc_row, 0.0)), 0.0)
    kb = k * beta
    lmat = jnp.where(strict, ops.dot(kb, k, 1, 1, REC_MODE) * decay, 0.0)
    tinv = _inv_unit_lower(ops, -lmat)
    egc = jnp.exp(gc)
    u = ops.dot(tinv, v * beta, 1, 0, REC_MODE)
    w = ops.dot(tinv, kb * egc, 1, 0, REC_MODE)
    a_intra = jnp.where(causal, ops.dot(q, k, 1, 1, REC_MODE) * decay, 0.0)
    g_tot = ops.dot(jnp.ones((nb, c, c), BF16), g, 1, 0, 'ca')
    k_dec = k * jnp.exp(g_tot - gc)
    v_new = u - ops.dot(w, s, 1, 0, REC_MODE)
    o = ops.dot(q * egc, s, 1, 0, REC_MODE) + ops.dot(a_intra, v_new, 1, 0, REC_MODE)
    g_full = ops.dot(jnp.ones((nb, LANES, c), BF16), g, 1, 0, 'ca')
    s_new = s * jnp.exp(g_full) + ops.dot(k_dec, v_new, 0, 0, REC_MODE)
    return s_new, o


def _mm(name, a, b, ta=False, tb=False, out_dtype=F32, tm=1024, tn=1024, tk=2048, epilogue=None, extra=()):
    m = a.shape[1] if ta else a.shape[0]
    kd = a.shape[0] if ta else a.shape[1]
    n = b.shape[0] if tb else b.shape[1]
    assert kd == (b.shape[1] if tb else b.shape[0]), (name, a.shape, b.shape)
    tm, tn, tk = _pick(m, tm), _pick(n, tn), _pick(kd, tk)
    nk = kd // tk
    n_x = len(extra)
    out_dtypes = out_dtype if epilogue is not None else (out_dtype,)

    def body(a_ref, b_ref, *rest):
        x_refs, o_refs, acc = rest[:n_x], rest[n_x:-1], rest[-1]
        kk = pl.program_id(2)
        part = _dg(a_ref[...].astype(BF16), b_ref[...].astype(BF16), 0 if ta else 1, 1 if tb else 0)

        def finish(z):
            vals = (z,) if epilogue is None else epilogue(z, *[xr[...] for xr in x_refs])
            for o_ref, val in zip(o_refs, vals):
                o_ref[...] = val.astype(o_ref.dtype)

        if nk == 1:
            finish(part)
            return

        @pl.when(kk == 0)
        def _():
            acc[...] = part

        @pl.when((kk > 0) & (kk < nk - 1))
        def _():
            acc[...] += part

        @pl.when(kk == nk - 1)
        def _():
            finish(acc[...] + part)

    a_spec = pl.BlockSpec((tk, tm), lambda i, j, k: (k, i)) if ta else pl.BlockSpec((tm, tk), lambda i, j, k: (i, k))
    b_spec = pl.BlockSpec((tn, tk), lambda i, j, k: (j, k)) if tb else pl.BlockSpec((tk, tn), lambda i, j, k: (k, j))
    o_spec = pl.BlockSpec((tm, tn), lambda i, j, k: (i, j))
    res = _pcall(
        body, name=name, grid=(m // tm, n // tn, nk), in_specs=[a_spec, b_spec] + [o_spec] * n_x,
        out_specs=[o_spec] * len(out_dtypes), out_shape=[jax.ShapeDtypeStruct((m, n), dt) for dt in out_dtypes],
        scratch_shapes=[pltpu.VMEM((tm, tn), F32)],
        compiler_params=_params(("parallel", "parallel", "arbitrary")))(a, b, *extra)
    return res if epilogue is not None else res[0]


def _row_spec(tm, width, cb):
    return pl.BlockSpec((tm, width), lambda i: (i, cb))


def _rows_fwd(name, f, rows, params, outs, tm):
    t = rows[0][0].shape[0]
    tm = min(tm, t)
    n_in = len(rows) + len(params)
    kept = [o for o in outs if o is not None]

    def body(*refs):
        vals = f(_RAW, *[r[...].astype(F32) for r in refs[:n_in]])
        o_refs = refs[n_in:]
        j = 0
        for val, o in zip(vals, outs):
            if o is not None:
                o_refs[j][...] = val.astype(o_refs[j].dtype)
                j += 1

    in_specs = [_row_spec(tm, w, cb) for (_, w, cb) in rows]
    in_specs += [pl.BlockSpec(p.shape, lambda i: (0, 0)) for p in params]
    res = _pcall(
        body, name=name, grid=(t // tm,), in_specs=in_specs,
        out_specs=[_row_spec(tm, w, 0) for (w, _) in kept],
        out_shape=[jax.ShapeDtypeStruct((t, w), dt) for (w, dt) in kept],
        compiler_params=_params(("parallel",)))(*[r[0] for r in rows], *params)
    return res


def _rows_bwd(name, f, rows, params, cts, tm, row_out, out_widths, adds=(), loss=False, out_dtypes=None, alias=None):
    t = rows[0][0].shape[0]
    tm = min(tm, t)
    n_r, n_p = len(rows), len(params)
    ct_flat = [c for cl in cts for c in cl]
    n_ct, n_add = len(ct_flat), len(adds)
    n_out = len(out_widths)

    def body(*refs):
        i = pl.program_id(0)
        ins = [r[...].astype(F32) for r in refs[:n_r + n_p]]
        ct_refs = refs[n_r + n_p:n_r + n_p + n_ct]
        add_refs = refs[n_r + n_p + n_ct:n_r + n_p + n_ct + n_add]
        o_refs = refs[n_r + n_p + n_ct + n_add + (alias is not None):]
        outs, vjp = jax.vjp(lambda *a: f(_DIFF, *a), *ins)
        ctv, pos = [], 0
        for o, cl in zip(outs, cts):
            if loss:
                ctv.append(jnp.ones_like(o))
            elif not cl:
                ctv.append(jnp.zeros_like(o))
            else:
                acc = ct_refs[pos][...].astype(F32)
                for q in range(1, len(cl)):
                    acc = acc + ct_refs[pos + q][...].astype(F32)
                ctv.append(acc)
            pos += len(cl)
        grads = vjp(tuple(ctv))
        for ri, ro in enumerate(row_out):
            if ro is not None:
                oi, off = ro
                val = grads[ri]
                for ai, (aoi, _) in enumerate(adds):
                    if aoi == oi:
                        val = val + add_refs[ai][...].astype(F32)
                o_refs[oi][:, off:off + val.shape[1]] = val.astype(o_refs[oi].dtype)
        p_refs = o_refs[n_out:n_out + n_p]

        @pl.when(i == 0)
        def _():
            for pr in p_refs:
                pr[...] = jnp.zeros_like(pr)
            if loss:
                o_refs[n_out + n_p][...] = jnp.zeros((8, LANES), F32)

        for pi, pr in enumerate(p_refs):
            pr[...] += grads[n_r + pi]
        if loss:
            o_refs[n_out + n_p][...] += jnp.sum(outs[0])

    in_specs = [_row_spec(tm, w, cb) for (_, w, cb) in rows]
    in_specs += [pl.BlockSpec(p.shape, lambda i: (0, 0)) for p in params]
    in_specs += [_row_spec(tm, w, cb) for (_, w, cb) in ct_flat]
    in_specs += [_row_spec(tm, w, cb) for (_, (_, w, cb)) in adds]
    out_specs = [_row_spec(tm, w, 0) for w in out_widths]
    out_specs += [pl.BlockSpec(p.shape, lambda i: (0, 0)) for p in params]
    out_dtypes = out_dtypes or [F32] * n_out
    out_shape = [jax.ShapeDtypeStruct((t, w), dt) for w, dt in zip(out_widths, out_dtypes)]
    out_shape += [jax.ShapeDtypeStruct(p.shape, F32) for p in params]
    if loss:
        out_specs.append(pl.BlockSpec((8, LANES), lambda i: (0, 0)))
        out_shape.append(jax.ShapeDtypeStruct((8, LANES), F32))
    extra, aliases = [], {}
    if alias is not None:
        oi, buf, cb = alias
        in_specs.append(ANY)
        out_specs[oi] = _row_spec(tm, out_widths[oi], cb)
        out_shape[oi] = jax.ShapeDtypeStruct(buf.shape, buf.dtype)
        extra, aliases = [buf], {n_r + n_p + n_ct + n_add: oi}
    res = _pcall(
        body, name=name, grid=(t // tm,), in_specs=in_specs, out_specs=out_specs, out_shape=out_shape,
        input_output_aliases=aliases, compiler_params=_params(("arbitrary",)))(
            *[r[0] for r in rows], *params, *[c[0] for c in ct_flat], *[a[1][0] for a in adds], *extra)
    return res[:n_out], res[n_out:n_out + n_p], res[n_out + n_p:]


def _shift_down(x, k):
    row = lax.broadcasted_iota(jnp.int32, x.shape, 0)
    return jnp.where(row >= k, pltpu.roll(x, k, 0), 0.0)


def _shift_up(x, k):
    t = x.shape[0]
    row = lax.broadcasted_iota(jnp.int32, x.shape, 0)
    return jnp.where(row < t - k, pltpu.roll(x, t - k, 0), 0.0)


def _tokshift_fwd(p, mix, n_main, main_cb, lora_cb):
    t = p.shape[0]
    nblk = mix.shape[1] // LANES

    def src(j):
        return (0, jnp.where(j < n_main, j + main_cb, j - n_main + lora_cb))

    def body(p_ref, mix_ref, o_ref):
        xv = p_ref[...]
        o_ref[...] = xv + (_shift_down(xv, 1) - xv) * mix_ref[...]

    return _pcall(
        body, name="tokshift_fwd", grid=(nblk,),
        in_specs=[pl.BlockSpec((t, LANES), src), pl.BlockSpec((1, LANES), lambda j: (0, j))],
        out_specs=pl.BlockSpec((t, LANES), lambda j: (0, j)),
        out_shape=jax.ShapeDtypeStruct((t, mix.shape[1]), F32), compiler_params=_params(("parallel",)))(p, mix)


def _tokshift_bwd(dxm, p, mix, dp_buf, n_main, main_cb, lora_cb):
    t = p.shape[0]
    nblk = mix.shape[1] // LANES

    def src(j):
        return (0, jnp.where(j < n_main, j + main_cb, j - n_main + lora_cb))

    def body(d_ref, p_ref, mix_ref, buf_ref, dp_ref, dmix_ref):
        xv, d, mx = p_ref[...], d_ref[...], mix_ref[...]
        dp_ref[...] = (d * (1.0 - mx) + _shift_up(d * mx, 1)).astype(dp_ref.dtype)
        dmix_ref[...] = jnp.sum(d * (_shift_down(xv, 1) - xv), axis=0, keepdims=True)

    return _pcall(
        body, name="tokshift_bwd", grid=(nblk,),
        in_specs=[pl.BlockSpec((t, LANES), lambda j: (0, j)), pl.BlockSpec((t, LANES), src),
                  pl.BlockSpec((1, LANES), lambda j: (0, j)), ANY],
        out_specs=[pl.BlockSpec((t, LANES), src), pl.BlockSpec((1, LANES), lambda j: (0, j))],
        out_shape=[jax.ShapeDtypeStruct(dp_buf.shape, dp_buf.dtype), jax.ShapeDtypeStruct((1, mix.shape[1]), F32)],
        input_output_aliases={3: 0},
        compiler_params=_params(("parallel",)))(dxm, p, mix, dp_buf)


def _conv_pre(xv, w):
    return (w[3:4, :] * xv + w[2:3, :] * _shift_down(xv, 1) + w[1:2, :] * _shift_down(xv, 2)
            + w[0:1, :] * _shift_down(xv, 3))


def _conv_fwd(p, w8, cb0):
    t = p.shape[0]
    width = w8.shape[1]

    def body(p_ref, w_ref, o_ref):
        c = _conv_pre(p_ref[...], w_ref[...])
        o_ref[...] = c * _sigmoid(c)

    return _pcall(
        body, name="conv_fwd", grid=(width // LANES,),
        in_specs=[pl.BlockSpec((t, LANES), lambda j: (0, j + cb0)), pl.BlockSpec((8, LANES), lambda j: (0, j))],
        out_specs=pl.BlockSpec((t, LANES), lambda j: (0, j)),
        out_shape=jax.ShapeDtypeStruct((t, width), F32), compiler_params=_params(("parallel",)))(p, w8)


def _conv_bwd(dy, p, w8, dp_buf, cb0):
    t = p.shape[0]
    width = w8.shape[1]

    def body(d_ref, p_ref, w_ref, buf_ref, dx_ref, dw_ref):
        xv, w = p_ref[...], w_ref[...]
        c = _conv_pre(xv, w)
        sg = _sigmoid(c)
        dc = d_ref[...] * (sg * (1.0 + c * (1.0 - sg)))
        dx_ref[...] = (w[3:4, :] * dc + w[2:3, :] * _shift_up(dc, 1) + w[1:2, :] * _shift_up(dc, 2)
                       + w[0:1, :] * _shift_up(dc, 3)).astype(dx_ref.dtype)
        row =lax.broadcasted_iota(jnp.int32, (8, LANES), 0)
        dw = jnp.zeros((8, LANES), F32)
        for j in range(4):
            sj = jnp.sum(dc * (xv if j == 3 else _shift_down(xv, 3 - j)), axis=0, keepdims=True)
            dw = dw + jnp.where(row == j, sj, 0.0)
        dw_ref[...] = dw

    return _pcall(
        body, name="conv_bwd", grid=(width // LANES,),
        in_specs=[pl.BlockSpec((t, LANES), lambda j: (0, j)), pl.BlockSpec((t, LANES), lambda j: (0, j + cb0)),
                  pl.BlockSpec((8, LANES), lambda j: (0, j)), ANY],
        out_specs=[pl.BlockSpec((t, LANES), lambda j: (0, j + cb0)), pl.BlockSpec((8, LANES), lambda j: (0, j))],
        out_shape=[jax.ShapeDtypeStruct(dp_buf.shape, dp_buf.dtype), jax.ShapeDtypeStruct((8, width), F32)],
        input_output_aliases={3: 0},
        compiler_params=_params(("parallel",)))(dy, p, w8, dp_buf)


def _rec_fwd(name, chunk_fn, ins, nblocks):
    t = ins[0][0].shape[0]
    tb = min(TIME_BLOCK, t)
    nt, ncb = t // tb, tb // CHUNK
    n_in = len(ins)
    grp = REC_GROUP if nblocks % REC_GROUP == 0 and all(c0 % REC_GROUP == 0 for (_, c0) in ins) else 1
    width = grp * LANES

    def body(*refs):
        in_refs, y_ref, sall_ref, s_scr = refs[:n_in], refs[n_in], refs[n_in + 1], refs[n_in + 2]

        @pl.when(pl.program_id(1) == 0)
        def _():
            s_scr[...] = jnp.zeros_like(s_scr)

        def step(ci, s):
            sl = pl.ds(pl.multiple_of(ci * CHUNK, CHUNK), CHUNK)
            lanes = [slice(gi * LANES, (gi + 1) * LANES) for gi in range(grp)]
            s1, y = chunk_fn(_RAW, s, *[jnp.stack([r[sl, ln] for ln in lanes]) for r in in_refs])
            for gi, ln in enumerate(lanes):
                sall_ref[gi, ci] = s[gi]
                y_ref[sl, ln] = y[gi]
            return s1

        s_scr[...] = lax.fori_loop(0, ncb, step, s_scr[...])

    in_specs = [pl.BlockSpec((tb, width), functools.partial(lambda h, tt, cb: (tt, cb + h), cb=c0 // grp))
                for (_, c0) in ins]
    return _pcall(
        body, name=name, grid=(nblocks // grp, nt), in_specs=in_specs,
        out_specs=[pl.BlockSpec((tb, width), lambda h, tt: (tt, h)),
                   pl.BlockSpec((grp, ncb, LANES, LANES), lambda h, tt: (h, tt, 0, 0))],
        out_shape=[jax.ShapeDtypeStruct((t, nblocks * LANES), F32),
                   jax.ShapeDtypeStruct((nblocks, t // CHUNK, LANES, LANES), F32)],
        scratch_shapes=[pltpu.VMEM((grp, LANES, LANES), F32)],
        compiler_params=_params(("parallel", "arbitrary")))(*[a for (a, _) in ins])


def _rec_bwd(name, chunk_fn, ins, dy, sall, nblocks):
    t = ins[0][0].shape[0]
    tb = min(TIME_BLOCK, t)
    nt, ncb = t // tb, tb // CHUNK
    n_in = len(ins)
    grp = REC_GROUP if nblocks % REC_GROUP == 0 and all(c0 % REC_GROUP == 0 for (_, c0) in ins) else 1
    width = grp * LANES

    def body(*refs):
        in_refs, dy_ref, sall_ref = refs[:n_in], refs[n_in], refs[n_in + 1]
        d_refs, ds_scr = refs[n_in + 2:2 * n_in + 2], refs[2 * n_in + 2]

        @pl.when(pl.program_id(1) == 0)
        def _():
            ds_scr[...] = jnp.zeros_like(ds_scr)

        def step(j, ds):
            ci = ncb - 1 - j
            sl = pl.ds(pl.multiple_of(ci * CHUNK, CHUNK), CHUNK)
            lanes = [slice(gi * LANES, (gi + 1) * LANES) for gi in range(grp)]
            s0 = jnp.stack([sall_ref[gi, ci] for gi in range(grp)])
            vals = [jnp.stack([r[sl, ln] for ln in lanes]) for r in in_refs]
            _, vjp = jax.vjp(lambda *a: chunk_fn(_DIFF, *a), s0, *vals)
            grads = vjp((ds, jnp.stack([dy_ref[sl, ln] for ln in lanes])))
            for dr, gval in zip(d_refs, grads[1:]):
                for gi, ln in enumerate(lanes):
                    dr[sl, ln] = gval[gi]
            return grads[0]

        ds_scr[...] = lax.fori_loop(0, ncb, step, ds_scr[...])

    in_specs = [pl.BlockSpec((tb, width), functools.partial(lambda h, tt, cb: (nt - 1 - tt, cb + h), cb=c0 // grp))
                for (_, c0) in ins]
    in_specs.append(pl.BlockSpec((tb, width), lambda h, tt: (nt - 1 - tt, h)))
    in_specs.append(pl.BlockSpec((grp, ncb, LANES, LANES), lambda h, tt: (h, nt - 1 - tt, 0, 0)))
    return _pcall(
        body, name=name, grid=(nblocks // grp, nt), in_specs=in_specs,
        out_specs=[pl.BlockSpec((tb, width), lambda h, tt: (nt - 1 - tt, h)) for _ in ins],
        out_shape=[jax.ShapeDtypeStruct((t, nblocks * LANES), F32) for _ in ins],
        scratch_shapes=[pltpu.VMEM((grp, LANES, LANES), F32)],
        compiler_params=_params(("parallel", "arbitrary")))(*[a for (a, _) in ins], dy, sall)


def _pick_rows(r, c):
    tr = max(8, ((1 << 18) // c) // 8 * 8)
    while tr > 8 and r % tr:
        tr -= 8
    return tr if r % tr == 0 else r


def _sum4(name, own, r3):
    _, r, c = r3.shape
    tr = _pick_rows(r, c)

    def body(own_ref, r_ref, o_ref):
        o_ref[...] = ((own_ref[...].astype(F32) + r_ref[0].astype(F32)) + r_ref[1].astype(F32)) + r_ref[2].astype(F32)

    return _pcall(
        body, name=name, grid=(r // tr,),
        in_specs=[pl.BlockSpec((tr, c), lambda i: (i, 0)), pl.BlockSpec((3, tr, c), lambda i: (0, i, 0))],
        out_specs=pl.BlockSpec((tr, c), lambda i: (i, 0)), out_shape=jax.ShapeDtypeStruct((r, c), F32),
        compiler_params=_params(("parallel",)))(own, r3)


def _adam_math(w, g, m, v):
    m = ADAM_B1 * m + (1.0 - ADAM_B1) * g
    v = ADAM_B2 * v + (1.0 - ADAM_B2) * (g * g)
    m_hat = m / (1.0 - ADAM_B1 ** ADAM_STEP)
    v_hat = v / (1.0 - ADAM_B2 ** ADAM_STEP)
    return -ADAM_LR * (m_hat / (jnp.sqrt(v_hat) + ADAM_EPS) + ADAM_WD * w), m, v


def _adam_big(name, w, m, v, ga, gb):
    r, c = w.shape
    tr = _pick_rows(r, c)

    def body(w_ref, m_ref, v_ref, ga_ref, gb_ref, g_ref, d_ref, mo_ref, vo_ref):
        g = ga_ref[...] + gb_ref[...]
        g_ref[...] = g
        d_ref[...], mo_ref[...], vo_ref[...] = _adam_math(w_ref[...], g, m_ref[...], v_ref[...])

    spec = pl.BlockSpec((tr, c), lambda i: (i, 0))
    return _pcall(
        body, name=name, grid=(r // tr,), in_specs=[spec] * 5, out_specs=[spec] * 4,
        out_shape=[jax.ShapeDtypeStruct((r, c), F32)] * 4, compiler_params=_params(("parallel",)))(w, m, v, ga, gb)


def _adam_small(w, m, v, g8):
    shape = w.shape

    def body(w_ref, m_ref, v_ref, g8_ref, g_ref, d_ref, mo_ref, vo_ref):
        g = g8_ref[0]
        for dev in range(1, 8):
            g = g + g8_ref[dev]
        g_ref[...] = g
        d_ref[...], mo_ref[...], vo_ref[...] = _adam_math(w_ref[...], g, m_ref[...], v_ref[...])

    return _pcall(
        body, name="adam_small", out_shape=[jax.ShapeDtypeStruct(shape, F32)] * 4,
        compiler_params=_params())(w, m, v, g8)


def _place():
    x, y, c = lax.axis_index("x"), lax.axis_index("y"), lax.axis_index("c")
    return x, y, c, [(1 - x, y), (x, 1 - y), (1 - x, 1 - y)]


def _remote(src, dst, send_sems, recv_sems, k, dev):
    return pltpu.make_async_remote_copy(src_ref=src, dst_ref=dst, send_sem=send_sems.at[k], recv_sem=recv_sems.at[k],
                                        device_id=dev, device_id_type=MESH)


HBM = pl.BlockSpec(memory_space=pltpu.HBM)
SEM = pl.BlockSpec(memory_space=pltpu.SEMAPHORE)


def _send3_start(name, src, land, mode):
    land_arr = land if mode == 'forward_half' else lax.empty(land, src.dtype)
    half = land_arr.shape[1] // 2

    def body(src_ref, land_ref, send_sem, recv_sem, src_thru, land_thru, token):
        x, y, c, chips = _place()
        rows = pl.ds(pl.multiple_of(c * half, half), half)
        for k, (px, py) in enumerate(chips):
            dev = (px, py, c)
            if mode == 'gather':
                s_k, d_k = src_ref, land_ref.at[2 * x + y]
            elif mode == 'gather_half':
                s_k, d_k = src_ref.at[rows], land_ref.at[2 * x + y, rows]
            elif mode == 'forward_half':
                s_k = d_k = land_ref.at[2 * px + py, rows]
                dev = (x, y, 1 - c)
            else:
                s_k, d_k = src_ref.at[2 * px + py], land_ref.at[k]
            pltpu.make_async_remote_copy(src_ref=s_k, dst_ref=d_k, send_sem=send_sem, recv_sem=recv_sem,
                                         device_id=dev, device_id_type=MESH).start()
        token[...] = jnp.zeros_like(token)

    return pl.pallas_call(
        body, name=name,
        out_shape=(pltpu.SemaphoreType.DMA(()), pltpu.SemaphoreType.DMA(()), pltpu.HBM(src.shape, src.dtype),
                   pltpu.HBM(land_arr.shape, src.dtype), jax.ShapeDtypeStruct((8, LANES), F32)),
        in_specs=(HBM, HBM), out_specs=(SEM, SEM, HBM, HBM, pl.BlockSpec(memory_space=pltpu.VMEM)),
        input_output_aliases={0: 2, 1: 3},
        compiler_params=pltpu.CompilerParams(has_side_effects=pltpu.SideEffectType.DATAFLOW_SIDE_EFFECTING),
    )(pltpu.with_memory_space_constraint(src, pltpu.HBM), pltpu.with_memory_space_constraint(land_arr, pltpu.HBM))


def _send3_wait(name, started, after, halves=False):
    send_sem, recv_sem, src_thru, land_thru, _ = started
    half = land_thru.shape[1] // 2

    def body(src_ref, land_ref, send_sem, recv_sem, after_ref, src_dead, got_ref):
        x, y, c, chips = _place()
        three = land_ref.at[pl.ds(0, 3), pl.ds(0, half)] if halves else land_ref.at[pl.ds(0, 3)]
        cp = pltpu.make_async_remote_copy(src_ref=three, dst_ref=three, send_sem=send_sem, recv_sem=recv_sem,
                                          device_id=(chips[0][0], chips[0][1], c), device_id_type=MESH)
        cp.wait_send()
        cp.wait_recv()

    return pl.pallas_call(
        body, name=name,
        out_shape=(pltpu.HBM(src_thru.shape, src_thru.dtype), pltpu.HBM(land_thru.shape, land_thru.dtype)),
        in_specs=(HBM, HBM, SEM, SEM, ANY), out_specs=(HBM, HBM), input_output_aliases={0: 0, 1: 1},
        compiler_params=pltpu.CompilerParams(has_side_effects=pltpu.SideEffectType.DATAFLOW_SIDE_EFFECTING),
    )(src_thru, land_thru, send_sem, recv_sem, after)


def _swap_sibling(name, part):
    def body(src, out, send_sems, recv_sems):
        x, y, c, _ = _place()
        cp = _remote(src, out, send_sems, recv_sems, 0, (x, y, 1 - c))
        cp.start()
        cp.wait_recv()
        cp.wait_send()

    return _pcall(
        body, name=name, in_specs=[ANY], out_specs=ANY, out_shape=jax.ShapeDtypeStruct(part.shape, part.dtype),
        scratch_shapes=[pltpu.SemaphoreType.DMA((1,)), pltpu.SemaphoreType.DMA((1,))])(part)


def _gather8(flat):
    def body(src, out, send_sems, recv_sems, local_sem):
        x, y, c, _ = _place()
        own = pltpu.make_async_copy(src, out.at[4 * x + 2 * y + c], local_sem)
        own.start()
        peers = []
        for mask in range(1, 8):
            px = x ^ ((mask >> 2) & 1)
            py = y ^ ((mask >> 1) & 1)
            pc = c ^ (mask & 1)
            peers.append((px, py, pc))
        sends = [_remote(src, out.at[4 * x + 2 * y + c], send_sems, recv_sems, k, dev) for k, dev in enumerate(peers)]
        for cp in sends:
            cp.start()
        for k, (px, py, pc) in enumerate(peers):
            _remote(src, out.at[4 * px + 2 * py + pc], send_sems, recv_sems, k, (px, py, pc)).wait_recv()
        for cp in sends:
            cp.wait_send()
        own.wait()

    return _pcall(
        body, name="gather8_small", in_specs=[ANY], out_specs=ANY,
        out_shape=jax.ShapeDtypeStruct((8,) + flat.shape, flat.dtype),
        scratch_shapes=[pltpu.SemaphoreType.DMA((7,)), pltpu.SemaphoreType.DMA((7,)), pltpu.SemaphoreType.DMA(())],
    )(flat)


def _pad_cols(a, width):
    return jnp.pad(a, ((0, 0), (0, width - a.shape[1])))


def _col_shards(full):
    r, c = full.shape
    return full.reshape(r, 4, c // 4).transpose(1, 0, 2)


def kernel(x, mem, mix_norm_pre, mix_norm_post, w_in, rwkv_shift_mix, rwkv_w0, rwkv_w_up, rwkv_a0, rwkv_a_up, rwkv_g_up, rwkv_k_k, rwkv_k_a, rwkv_r_k, rwkv_gn_w, rwkv_gn_b, gdn_conv_w, gdn_a_log, gdn_dt_bias, gdn_norm_w, w_branch_rwkv, w_branch_gdn, w_mix_out, xa_norm_pre, xa_norm_mem, xa_norm_post, xa_w_q, xa_w_kv, xa_w_o, mlp_norm_pre, mlp_norm_post, mlp_w_up, mlp_w_down, loss_target, m_mix_norm_pre, m_mix_norm_post, m_w_in, m_rwkv_shift_mix, m_rwkv_w0, m_rwkv_w_up, m_rwkv_a0, m_rwkv_a_up, m_rwkv_g_up, m_rwkv_k_k, m_rwkv_k_a, m_rwkv_r_k, m_rwkv_gn_w, m_rwkv_gn_b, m_gdn_conv_w, m_gdn_a_log, m_gdn_dt_bias, m_gdn_norm_w, m_w_branch_rwkv, m_w_branch_gdn, m_w_mix_out, m_xa_norm_pre, m_xa_norm_mem, m_xa_norm_post, m_xa_w_q, m_xa_w_kv, m_xa_w_o, m_mlp_norm_pre, m_mlp_norm_post, m_mlp_w_up, m_mlp_w_down, v_mix_norm_pre, v_mix_norm_post, v_w_in, v_rwkv_shift_mix, v_rwkv_w0, v_rwkv_w_up, v_rwkv_a0, v_rwkv_a_up, v_rwkv_g_up, v_rwkv_k_k, v_rwkv_k_a, v_rwkv_r_k, v_rwkv_gn_w, v_rwkv_gn_b, v_gdn_conv_w, v_gdn_a_log, v_gdn_dt_bias, v_gdn_norm_w, v_w_branch_rwkv, v_w_branch_gdn, v_w_mix_out, v_xa_norm_pre, v_xa_norm_mem, v_xa_norm_post, v_xa_w_q, v_xa_w_kv, v_xa_w_o, v_mlp_norm_pre, v_mlp_norm_post, v_mlp_w_up, v_mlp_w_down):
    given = dict(locals())
    W = {n: given[n] if given[n].ndim == 2 else given[n][0] for n in WEIGHTS}
    Mo = {n: given['m_' + n].reshape(W[n].shape) for n in WEIGHTS}
    Vo = {n: given['v_' + n].reshape(W[n].shape) for n in WEIGHTS}
    xs, mems, tgt = x[0], mem[0], loss_target[0]
    T, D = xs.shape
    RW = W['rwkv_w0'].shape[-1]
    GW = W['gdn_conv_w'].shape[-1] * 4 // 3
    NH = W['gdn_a_log'].shape[-1]
    LW, LA, LG = W['rwkv_w_up'].shape[0], W['rwkv_a_up'].shape[0], W['rwkv_g_up'].shape[0]
    XW = W['xa_w_q'].shape[-1]
    assert LW <= LANES and LA <= LANES and LG % LANES == 0 and 2 * NH <= LANES

    del _PENDING[:]
    me = 2 * lax.axis_index("x") + lax.axis_index("y")
    small_sizes = [W[n].size for n in SMALL_SHARDED]
    small_len = -(-sum(small_sizes) // 1024) * 1024

    def pack_small(vals):
        flat = jnp.concatenate([vv.reshape(-1).astype(F32) for vv in vals])
        return jnp.pad(flat, (0, small_len - flat.shape[0])).reshape(small_len // 1024, 1024)

    def unpack_small(packed):
        flat, out, off = packed.reshape(-1), [], 0
        for n, sz in zip(SMALL_SHARDED, small_sizes):
            out.append(flat[off:off + sz].reshape(W[n].shape))
            off += sz
        return out

    gstarted = {}

    def gstart(n, after=None):
        local = pack_small([W[s] for s in SMALL_SHARDED]) if n == 'small' else W[n].astype(BF16)
        if after is not None:
            local, _ = lax.optimization_barrier((local, after))
        mode = 'gather_half' if n in HALVED else 'gather'
        gstarted[n] = _send3_start("gather_start_" + n, local, (4,) + local.shape, mode)
        _PENDING.append(gstarted[n][4])

    HALVED = ('w_in', 'mlp_w_up', 'mlp_w_down')
    gstart('w_in')
    gstart('small')

    def gathered(n, after):
        src, land = _send3_wait("gather_wait_" + n, gstarted[n], after, halves=n in HALVED)
        if n in HALVED:
            fwd = _send3_start("gather_forward_" + n, src, land, 'forward_half')
            src, land = _send3_wait("gather_forward_wait_" + n, fwd, fwd[4], halves=True)
        return lax.dynamic_update_index_in_dim(land, src, me, 0)

    def gathered_full(n, after):
        g = gathered(n, after)
        if n in COL_SHARDED:
            return jnp.concatenate([g[0], g[1], g[2], g[3]], axis=1)
        return g.reshape((4 * g.shape[1],) + g.shape[2:])

    RC = 3 * RW + LW + LA + LG
    o_z_raw = RC + 3 * GW
    o_b_raw = o_z_raw + GW
    o_grw_raw = o_b_raw + 2 * NH
    O_GRW = 0
    O_GGDN = D
    O_R = 2 * D
    O_GQ = O_R + 3 * RW
    O_Z = O_GQ + 3 * GW
    O_LW = O_Z + GW
    O_LA = O_LW + LANES
    O_LG = O_LA + LANES
    O_BA = O_LG + LG
    END = O_BA + LANES
    NP = -(-END // 1024) * 1024
    XM = 3 * RW + 2 * LANES + LG
    assert O_R % RW == 0 and O_GQ % GW == 0 and O_Z % GW == 0 and O_LG % LG == 0 and (3 * RW + 2 * LANES) % LG == 0

    def pad_layout(raw, lead):
        return jnp.concatenate([
            raw[:, o_grw_raw:o_grw_raw + 2 * D], raw[:, :3 * RW], raw[:, RC:RC + 3 * GW], raw[:, o_z_raw:o_z_raw + GW],
            _pad_cols(raw[:, 3 * RW:3 * RW + LW], LANES), _pad_cols(raw[:, 3 * RW + LW:3 * RW + LW + LA], LANES),
            raw[:, 3 * RW + LW + LA:RC], _pad_cols(raw[:, o_b_raw:o_b_raw + 2 * NH], LANES),
            jnp.zeros((lead, NP - END), raw.dtype)], axis=1)

    def unpad_layout(pd):
        return jnp.concatenate([
            pd[:, O_R:O_R + 3 * RW], pd[:, O_LW:O_LW + LW], pd[:, O_LA:O_LA + LA], pd[:, O_LG:O_LG + LG],
            pd[:, O_GQ:O_GQ + 3 * GW], pd[:, O_Z:O_Z + GW], pd[:, O_BA:O_BA + 2 * NH], pd[:, O_GRW:O_GRW + 2 * D]], axis=1)

    def xm_layout(raw):
        return jnp.concatenate([raw[:, :3 * RW], _pad_cols(raw[:, 3 * RW:3 * RW + LW], LANES),
                                _pad_cols(raw[:, 3 * RW + LW:3 * RW + LW + LA], LANES), raw[:, 3 * RW + LW + LA:]], axis=1)

    def xm_unlayout(pd):
        return jnp.concatenate([pd[:, :3 * RW], pd[:, 3 * RW:3 * RW + LW], pd[:, 3 * RW + LANES:3 * RW + LANES + LA],
                                pd[:, 3 * RW + 2 * LANES:]], axis=1)

    mix_x = xm_layout(W['rwkv_shift_mix'])
    r_k = W['rwkv_r_k'].reshape(1, RW)
    alog_p = jnp.pad(W['gdn_a_log'], ((0, 0), (NH, LANES - 2 * NH)))
    dt_p = jnp.pad(W['gdn_dt_bias'], ((0, 0), (NH, LANES - 2 * NH)))
    nw_t = jnp.tile(W['gdn_norm_w'], (1, GW // GDN_HEAD))
    n_main, main_cb, lora_cb = 3 * RW // LANES, O_R // LANES, O_LW // LANES
    f_gdn_pre = functools.partial(_f_gdn_pre, heads=NH)

    (u,) = _rows_fwd("rms_pre", _f_rms, [(xs, D, 0)], [W['mix_norm_pre']], [(D, BF16)], 256)
    full = {'w_in': gathered_full('w_in', u)}
    w_p = pad_layout(full['w_in'], D)
    small4 = gathered('small', w_p)
    for n in ['w_branch_rwkv', 'w_branch_gdn', 'w_mix_out', 'xa_w_q', 'xa_w_kv', 'xa_w_o', 'mlp_w_up']:
        gstart(n, small4)
    p = _mm("mm_in", u, w_p)
    small_parts = [unpack_small(small4[s]) for s in range(4)]
    for j, n in enumerate(SMALL_SHARDED):
        full[n] = jnp.concatenate([small_parts[s][j] for s in range(4)], axis=1)
    w_up_p = jnp.pad(full['rwkv_w_up'], ((0, LANES - LW), (0, 0)))
    a_up_p = jnp.pad(full['rwkv_a_up'], ((0, LANES - LA), (0, 0)))
    g_up = full['rwkv_g_up']
    conv8 = jnp.pad(full['gdn_conv_w'], ((0, 4), (0, 0)))
    rwkv_par = [W['rwkv_w0'], W['rwkv_a0'], W['rwkv_k_k'], W['rwkv_k_a'], w_up_p, a_up_p, g_up]
    xm = _tokshift_fwd(p, mix_x, n_main, main_cb, lora_cb)
    pre_rows = [(xm, RW, 0), (xm, RW, 1), (xm, RW, 2), (xm, LANES, n_main), (xm, LANES, n_main + 1),
                (xm, LG, (3 * RW + 2 * LANES) // LG)]
    lwd, k2, av, bv, gate = _rows_fwd("rwkv_pre", _f_rwkv_pre, pre_rows, rwkv_par,
                                      [None, (RW, F32), (RW, F32), None, (RW, F32), (RW, F32), (RW, F32)], 128)
    rec_rw = [(xm, 0), (lwd, 0), (k2, 0), (xm, 2 * RW // LANES), (av, 0), (bv, 0)]
    y_rec, s_rw = _rec_fwd("rwkv_rec_fwd", _rwkv_chunk, rec_rw, RW // LANES)
    post_rows = [(y_rec, RW, 0), (xm, RW, 0), (k2, RW, 0), (xm, RW, 2), (gate, RW, 0)]
    post_par = [r_k, W['rwkv_gn_w'], W['rwkv_gn_b']]
    (y_rw,) = _rows_fwd("rwkv_post", _f_rwkv_post, post_rows, post_par, [(RW, BF16)], 128)

    qkvc = _conv_fwd(p, conv8, O_GQ // LANES)
    gpre_rows = [(qkvc, GW, 0), (qkvc, GW, 1), (qkvc, GW, 2), (p, LANES, O_BA // LANES)]
    qn, kn, beta_x, g_x = _rows_fwd("gdn_pre", f_gdn_pre, gpre_rows, [alog_p, dt_p],
                                    [(GW, F32), (GW, F32), None, (GW, F32), (GW, F32)], 128)
    rec_gd = [(qn, 0), (kn, 0), (qkvc, 2 * GW // LANES), (beta_x, 0), (g_x, 0)]
    o_gd, s_gd = _rec_fwd("gdn_rec_fwd", _gdn_chunk, rec_gd, GW // LANES)
    gpost_rows = [(o_gd, GW, 0), (p, GW, O_Z // GW)]
    (y_gdn,) = _rows_fwd("gdn_post", _f_gdn_post, gpost_rows, [nw_t], [(GW, BF16)], 128)

    full['w_branch_rwkv'] = gathered_full('w_branch_rwkv', y_gdn)
    gstart('mlp_w_down', full['w_branch_rwkv'])
    br_rw = _mm("mm_br_rw", y_rw, full['w_branch_rwkv'])
    full['w_branch_gdn'] = gathered_full('w_branch_gdn', y_gdn)
    br_gdn = _mm("mm_br_gdn", y_gdn, full['w_branch_gdn'])
    merge_rows = [(p, D, O_GRW // D), (p, D, O_GGDN // D), (br_rw, D, 0), (br_gdn, D, 0)]
    (merged,) = _rows_fwd("merge", _f_merge, merge_rows, [], [(D, BF16)], 128)
    full['w_mix_out'] = gathered_full('w_mix_out', merged)
    ymix = _mm("mm_mix_out", merged, full['w_mix_out'])
    r1_par = [W['mix_norm_post'], W['xa_norm_pre']]
    h1, cq = _rows_fwd("resid1", _f_resid, [(xs, D, 0), (ymix, D, 0)], r1_par, [(D, F32), (D, BF16)], 128)
    full['xa_w_q'] = gathered_full('xa_w_q', cq)
    q = _mm("mm_xa_q", cq, full['xa_w_q'])
    (mn,) = _rows_fwd("mem_norm", _f_rms, [(mems, D, 0)], [W['xa_norm_mem']], [(D, BF16)], 128)
    full['xa_w_kv'] = gathered_full('xa_w_kv', mn)
    kv = _mm("mm_xa_kv", mn, full['xa_w_kv'])
    k_x, v_x = kv[:, :XW], kv[:, XW:]
    (o_x,) = _rows_fwd("xattn", _f_xattn, [(q, XW, 0)], [k_x, v_x], [(XW, BF16)], 256)
    full['xa_w_o'] = gathered_full('xa_w_o', o_x)
    xo = _mm("mm_xa_o", o_x, full['xa_w_o'])
    r2_par = [W['xa_norm_post'], W['mlp_norm_pre']]
    h2, fm = _rows_fwd("resid2", _f_resid, [(h1, D, 0), (xo, D, 0)], r2_par, [(D, F32), (D, BF16)], 128)
    full['mlp_w_up'] = gathered_full('mlp_w_up', fm)
    up, act = _mm("mm_mlp_up", fm, full['mlp_w_up'], out_dtype=(F32, BF16),
                  epilogue=lambda z: (z, jnp.square(jnp.maximum(z, 0.0))))
    DFF = up.shape[1]
    full['mlp_w_down'] = gathered_full('mlp_w_down', act)
    down = _mm("mm_mlp_down", act, full['mlp_w_down'])

    G = {}
    xstarted = {}

    def xstart(n, g4):
        own = lax.dynamic_index_in_dim(g4, me, 0, keepdims=False)
        xstarted[n] = (_send3_start("grad_start_" + n, g4, (3,) + g4.shape[1:], 'exchange'), own)
        _PENDING.append(xstarted[n][0][4])

    (dh2, ddown), (G['mlp_norm_post'],), (loss_acc,) = _rows_bwd(
        "final", _f_final, [(h2, D, 0), (down, D, 0), (tgt, D, 0)], [W['mlp_norm_post']], [[]], 128,
        [(0, 0), (1, 0), None], [D, D], loss=True, out_dtypes=[F32, BF16])
    (d_up,) = _mm("mm_d_act", ddown, full['mlp_w_down'], tb=True, out_dtype=(BF16,), extra=(up,),
                  epilogue=lambda z, upt: (z * (2.0 * jnp.maximum(upt, 0.0)),))
    dw_down = _mm("mm_dw_down", act, ddown, ta=True, out_dtype=BF16)
    xstart('mlp_w_down', dw_down.astype(BF16).reshape(4, DFF // 4, D))
    d_fm = _mm("mm_d_fm", d_up, full['mlp_w_up'], tb=True)
    dw_up = _mm("mm_dw_up", fm, d_up, ta=True, out_dtype=BF16)
    xstart('mlp_w_up', _col_shards(dw_up.astype(BF16)))
    (dh1, dxo), (G['xa_norm_post'], G['mlp_norm_pre']), _ = _rows_bwd(
        "resid2_bwd", _f_resid, [(h1, D, 0), (xo, D, 0)], r2_par, [[(dh2, D, 0)], [(d_fm, D, 0)]], 128,
        [(0, 0), (1, 0)], [D, D], out_dtypes=[F32, BF16])
    d_ox = _mm("mm_d_ox", dxo, full['xa_w_o'], tb=True)
    dw_o = _mm("mm_dw_o", o_x, dxo, ta=True, out_dtype=BF16)
    xstart('xa_w_o', _col_shards(dw_o.astype(BF16)))
    (dq,), (dk_x, dv_x), _ = _rows_bwd("xattn_bwd", _f_xattn, [(q, XW, 0)], [k_x, v_x], [[(d_ox, XW, 0)]], 128,
                                       [(0, 0)], [XW], out_dtypes=[BF16])
    d_cq = _mm("mm_d_cq", dq, full['xa_w_q'], tb=True)
    dw_q = _mm("mm_dw_q", cq, dq, ta=True, out_dtype=BF16)
    xstart('xa_w_q', dw_q.astype(BF16).reshape(4, D // 4, XW))
    dkv = jnp.concatenate([dk_x, dv_x], axis=1)
    d_mn = _mm("mm_d_mn", dkv, full['xa_w_kv'], tb=True)
    dw_kv = _mm("mm_dw_kv", mn, dkv, ta=True, out_dtype=BF16)
    xstart('xa_w_kv', dw_kv.astype(BF16).reshape(4, D // 4, 2 * XW))
    _, (G['xa_norm_mem'],), _ = _rows_bwd("mem_norm_bwd", _f_rms, [(mems, D, 0)], [W['xa_norm_mem']],
                                          [[(d_mn, D, 0)]], 128, [None], [])
    (dx_a, dymix), (G['mix_norm_post'], G['xa_norm_pre']), _ = _rows_bwd(
        "resid1_bwd", _f_resid, [(xs, D, 0), (ymix, D, 0)], r1_par, [[(dh1, D, 0)], [(d_cq, D, 0)]], 128,
        [(0, 0), (1, 0)], [D, D], out_dtypes=[F32, BF16])
    d_merged = _mm("mm_d_merged", dymix, full['w_mix_out'], tb=True)
    dw_out = _mm("mm_dw_out", merged, dymix, ta=True, out_dtype=BF16)
    xstart('w_mix_out', dw_out.astype(BF16).reshape(4, D // 4, D))
    dp = jnp.zeros((T, NP), BF16)
    (dp, d_brw, d_bgdn), _, _ = _rows_bwd(
        "merge_bwd", _f_merge, merge_rows, [], [[(d_merged, D, 0)]], 128, [(0, 0), (0, D), (1, 0), (2, 0)], [2 * D, D, D],
        out_dtypes=[BF16] * 3, alias=(0, dp, O_GRW // (2 * D)))
    d_yrw = _mm("mm_d_yrw", d_brw, full['w_branch_rwkv'], tb=True)
    dw_brw = _mm("mm_dw_brw", y_rw, d_brw, ta=True, out_dtype=BF16)
    xstart('w_branch_rwkv', _col_shards(dw_brw.astype(BF16)))
    d_ygdn = _mm("mm_d_ygdn", d_bgdn, full['w_branch_gdn'], tb=True)
    dw_bgdn = _mm("mm_dw_bgdn", y_gdn, d_bgdn, ta=True, out_dtype=BF16)
    xstart('w_branch_gdn', _col_shards(dw_bgdn.astype(BF16)))

    (d_y, d_r1, d_k2a, d_v1, d_gate), (d_rk, G['rwkv_gn_w'], G['rwkv_gn_b']), _ = _rows_bwd(
        "rwkv_post_bwd", _f_rwkv_post, post_rows, post_par, [[(d_yrw, RW, 0)]], 64,
        [(0, 0), (1, 0), (2, 0), (3, 0), (4, 0)], [RW] * 5)
    d_r2, d_lwd, d_k2b, d_v2, d_av, d_bv = _rec_bwd("rwkv_rec_bwd", _rwkv_chunk, rec_rw, d_y, s_rw, RW // LANES)
    pre_cts = [[(d_r1, RW, 0), (d_r2, RW, 0)], [(d_lwd, RW, 0)], [(d_k2a, RW, 0), (d_k2b, RW, 0)],
               [(d_v1, RW, 0), (d_v2, RW, 0)], [(d_av, RW, 0)], [(d_bv, RW, 0)], [(d_gate, RW, 0)]]
    pre_out = [(0, 0), (0, RW), (0, 2 * RW), (0, 3 * RW), (0, 3 * RW + LANES), (0, 3 * RW + 2 * LANES)]
    (d_xm,), pre_pg, _ = _rows_bwd("rwkv_pre_bwd", _f_rwkv_pre, pre_rows, rwkv_par, pre_cts, 64, pre_out, [XM])
    G['rwkv_w0'], G['rwkv_a0'], G['rwkv_k_k'], G['rwkv_k_a'], d_wup_p, d_aup_p, d_gup = pre_pg
    dp, d_mix_x = _tokshift_bwd(d_xm, p, mix_x, dp, n_main, main_cb, lora_cb)

    (d_ogd, dp), (d_nw_t,), _ = _rows_bwd("gdn_post_bwd", _f_gdn_post, gpost_rows, [nw_t], [[(d_ygdn, GW, 0)]], 64,
                                          [(0, 0), (1, 0)], [GW, GW], out_dtypes=[F32, BF16], alias=(1, dp, O_Z // GW))
    d_qn, d_kn, d_vg, d_betax, d_gx = _rec_bwd("gdn_rec_bwd", _gdn_chunk, rec_gd, d_ogd, s_gd, GW // LANES)
    gpre_cts = [[(d_qn, GW, 0)], [(d_kn, GW, 0)], [(d_vg, GW, 0)], [(d_betax, GW, 0)], [(d_gx, GW, 0)]]
    (d_qkvc, dp), (d_alog_p, d_dt_p), _ = _rows_bwd(
        "gdn_pre_bwd", f_gdn_pre, gpre_rows, [alog_p, dt_p], gpre_cts, 64,
        [(0, 0), (0, GW), (0, 2 * GW), (1, 0)], [3 * GW, LANES], out_dtypes=[F32, BF16], alias=(1, dp, O_BA // LANES))
    dp, d_conv8 = _conv_bwd(d_qkvc, p, conv8, dp, O_GQ // LANES)
    small_g = [_col_shards(d_wup_p[:LW]), _col_shards(d_aup_p[:LA]), _col_shards(d_gup), _col_shards(d_conv8[:4])]
    xstart('small', jnp.stack([pack_small([g[s] for g in small_g]) for s in range(4)]))
    dw_p = _mm("mm_dw_in", u, dp, ta=True, out_dtype=BF16)
    xstart('w_in', _col_shards(unpad_layout(dw_p.astype(BF16))))
    d_u = _mm("mm_d_u", dp, w_p, tb=True)
    (grad_x,), (G['mix_norm_pre'],), _ = _rows_bwd(
        "rms_pre_bwd", _f_rms, [(xs, D, 0)], [W['mix_norm_pre']], [[(d_u, D, 0)]], 128, [(0, 0)], [D],
        adds=[(0, (dx_a, D, 0))])

    G['rwkv_shift_mix'] = xm_unlayout(d_mix_x)
    G['rwkv_r_k'] = d_rk
    G['gdn_a_log'] = d_alog_p[:, NH:2 * NH]
    G['gdn_dt_bias'] = d_dt_p[:, NH:2 * NH]
    G['gdn_norm_w'] = jnp.sum(d_nw_t.reshape(GW // GDN_HEAD, GDN_HEAD), axis=0, keepdims=True)

    res = {}
    for n in xstarted:
        started, own = xstarted[n]
        _, land = _send3_wait("grad_wait_" + n, started, grad_x)
        part = _sum4("sum4_" + n, own, land)
        other = _swap_sibling("swap_" + n, part)
        if n == 'small':
            for sn, pa, pb in zip(SMALL_SHARDED, unpack_small(part), unpack_small(other)):
                res[sn] = _adam_big("adam_" + sn, W[sn], Mo[sn], Vo[sn], pa, pb)
        else:
            res[n] = _adam_big("adam_" + n, W[n], Mo[n], Vo[n], part, other)

    sizes = [-(-W[n].size // LANES) * LANES for n in REPLICATED]
    total = -(-sum(sizes) // (8 * LANES)) * (8 * LANES)

    def pack(vals):
        parts = [jnp.pad(vv.reshape(-1), (0, sz - vv.size)) for vv, sz in zip(vals, sizes)]
        flat = jnp.concatenate(parts)
        return jnp.pad(flat, (0, total - flat.shape[0])).reshape(8, total // 8)

    g8 = _gather8(pack([G[n] for n in REPLICATED]))
    small = _adam_small(pack([W[n] for n in REPLICATED]), pack([Mo[n] for n in REPLICATED]),
                        pack([Vo[n] for n in REPLICATED]), g8)
    small = [s.reshape(-1) for s in small]
    off = 0
    for n, sz in zip(REPLICATED, sizes):
        res[n] = [s[off:off + W[n].size].reshape(W[n].shape) for s in small]
        off += sz

    loss = lax.psum(loss_acc[0, 0], ("x", "y", "c"))
    outs = [loss, grad_x[None]]
    for j in range(4):
        outs += [res[n][j].reshape(given[n].shape) for n in WEIGHTS]
    return tuple(outs)
```

```python
import functools
import math

import jax
import jax.numpy as jnp
from jax import lax
from jax.experimental import pallas as pl
from jax.experimental.pallas import tpu as pltpu

F32 = jnp.float32
BF16 = jnp.bfloat16
MESH = pl.DeviceIdType.MESH

LANES = 128
VMEM_LIMIT = 56 * 1024 * 1024
CHUNK = 64
TIME_BLOCK = 256
REC_GROUP = 8
REC_MODE = '1'
INV_MODE = '1'

NORM_EPS = 1e-6
L2_EPS = 1e-6
RWKV_GN_EPS = 64e-5
RWKV_HEAD = 64
GDN_HEAD = 128
XA_HEAD = 128
ADAM_LR, ADAM_B1, ADAM_B2, ADAM_EPS, ADAM_WD, ADAM_STEP = 0.001, 0.9, 0.999, 1e-08, 0.01, 10

WEIGHTS = ['mix_norm_pre', 'mix_norm_post', 'w_in', 'rwkv_shift_mix', 'rwkv_w0', 'rwkv_w_up', 'rwkv_a0', 'rwkv_a_up',
           'rwkv_g_up', 'rwkv_k_k', 'rwkv_k_a', 'rwkv_r_k', 'rwkv_gn_w', 'rwkv_gn_b', 'gdn_conv_w', 'gdn_a_log',
           'gdn_dt_bias', 'gdn_norm_w', 'w_branch_rwkv', 'w_branch_gdn', 'w_mix_out', 'xa_norm_pre', 'xa_norm_mem',
           'xa_norm_post', 'xa_w_q', 'xa_w_kv', 'xa_w_o', 'mlp_norm_pre', 'mlp_norm_post', 'mlp_w_up', 'mlp_w_down']
COL_SHARDED = ['w_in', 'rwkv_w_up', 'rwkv_a_up', 'rwkv_g_up', 'gdn_conv_w', 'w_branch_rwkv', 'w_branch_gdn', 'xa_w_o',
               'mlp_w_up']
ROW_SHARDED = ['w_mix_out', 'xa_w_q', 'xa_w_kv', 'mlp_w_down']
SHARDED = COL_SHARDED + ROW_SHARDED
SMALL_SHARDED = ['rwkv_w_up', 'rwkv_a_up', 'rwkv_g_up', 'gdn_conv_w']
BIG_SHARDED = [n for n in SHARDED if n not in SMALL_SHARDED]
REPLICATED = [n for n in WEIGHTS if n not in SHARDED]


ANY = pl.BlockSpec(memory_space=pl.ANY)
_PENDING = []


def _pcall(body, **kw):
    deps = tuple(_PENDING) if "in_specs" in kw else ()
    if not deps:
        return pl.pallas_call(body, **kw)
    del _PENDING[:]
    n_in = len(kw["in_specs"])
    kw["in_specs"] = list(kw["in_specs"]) + [ANY] * len(deps)

    def with_deps(*refs):
        return body(*refs[:n_in], *refs[n_in + len(deps):])

    call = pl.pallas_call(with_deps, **kw)
    return lambda *args: call(*args, *deps)


def _params(sem=None):
    return pltpu.CompilerParams(dimension_semantics=sem, vmem_limit_bytes=VMEM_LIMIT)


def _pick(n, pref, mult=LANES):
    t = (min(n, pref) // mult) * mult
    while t >= mult:
        if n % t == 0:
            return t
        t -= mult
    return n


def _dg(a, b, ca, cb):
    if a.ndim == 3:
        return lax.dot_general(a, b, (((ca + 1,), (cb + 1,)), ((0,), (0,))), preferred_element_type=F32)
    return lax.dot_general(a, b, (((ca,), (cb,)), ((), ())), preferred_element_type=F32)


def _split(x):
    hi = x.astype(BF16)
    lo = (x - hi.astype(F32)).astype(BF16)
    return hi, lo


def _dot_raw(a, b, ca, cb, mode):
    if mode == '1':
        return _dg(a.astype(BF16), b.astype(BF16), ca, cb)
    if mode == 'cb':
        ah, al = _split(a)
        bb = b.astype(BF16)
        return _dg(ah, bb, ca, cb) + _dg(al, bb, ca, cb)
    if mode == 'ca':
        bh, bl = _split(b)
        ab = a.astype(BF16)
        return _dg(ab, bh, ca, cb) + _dg(ab, bl, ca, cb)
    ah, al = _split(a)
    bh, bl = _split(b)
    return _dg(ah, bh, ca, cb) + (_dg(ah, bl, ca, cb) + _dg(al, bh, ca, cb))


@functools.partial(jax.custom_vjp, nondiff_argnums=(2, 3, 4))
def _dot_diff(a, b, ca, cb, mode):
    return _dot_raw(a, b, ca, cb, mode)


def _dot_diff_fwd(a, b, ca, cb, mode):
    return _dot_raw(a, b, ca, cb, mode), (a, b)


def _dot_diff_bwd(ca, cb, mode, res, ct):
    a, b = res
    if mode == 'ca':
        da = jnp.zeros_like(a)
    else:
        bmode = {'1': '1', '3': '3', 'cb': 'cb'}[mode]
        if ca == 1:
            da = _dot_raw(ct, b, 1, 1 if cb == 0 else 0, bmode)
        else:
            da = _dot_raw(b, ct, 1 if cb == 0 else 0, 1, 'ca' if bmode == 'cb' else bmode)
    if mode == 'cb':
        db = jnp.zeros_like(b)
    else:
        amode = {'1': '1', '3': '3', 'ca': 'ca'}[mode]
        if cb == 0:
            db = _dot_raw(a, ct, 0 if ca == 1 else 1, 0, amode)
        else:
            db = _dot_raw(ct, a, 0, 0 if ca == 1 else 1, 'cb' if amode == 'ca' else amode)
    return da, db


_dot_diff.defvjp(_dot_diff_fwd, _dot_diff_bwd)


class _Ops:
    def __init__(self, diff):
        self.diff = diff

    def dot(self, a, b, ca=1, cb=0, mode='1'):
        if self.diff:
            return _dot_diff(a, b, ca, cb, mode)
        return _dot_raw(a, b, ca, cb, mode)


_RAW = _Ops(False)
_DIFF = _Ops(True)


def _sigmoid(x):
    return 1.0 / (1.0 + jnp.exp(-x))


def _softplus(x):
    return jnp.maximum(x, 0.0) + jnp.log(1.0 + jnp.exp(-jnp.abs(x)))


def _rms(x, g, eps=NORM_EPS):
    return x * lax.rsqrt(jnp.mean(x * x, axis=-1, keepdims=True) + eps) * g


def _head_matrix(width, head):
    ch = lax.broadcasted_iota(jnp.int32, (width, LANES), 0)
    hh = lax.broadcasted_iota(jnp.int32, (width, LANES), 1)
    return (lax.shift_right_logical(ch, int(math.log2(head))) == hh).astype(BF16)


def _headsum(ops, x, head):
    e = _head_matrix(x.shape[-1], head)
    s = ops.dot(x, e, 1, 0, 'cb')
    return ops.dot(s, e, 1, 1, 'cb')


def _f_rms(ops, x, g):
    return (_rms(x, g),)


def _f_resid(ops, h, y, g_post, g_next):
    h1 = h + _rms(y, g_post)
    return h1, _rms(h1, g_next)


def _f_final(ops, h, y, tgt, g_post):
    e = h + _rms(y, g_post) - tgt
    return (0.5 * jnp.mean(e * e, axis=-1, keepdims=True),)


def _f_act(ops, up):
    r = jnp.maximum(up, 0.0)
    return (r * r,)


def _f_merge(ops, g_rw, g_gdn, b_rw, b_gdn):
    return (_sigmoid(g_rw) * b_rw + _sigmoid(g_gdn) * b_gdn,)


def _f_rwkv_pre(ops, xr, xk, xv, xlw, xla, xlg, w0, a0, k_k, k_a, w_up, a_up, g_up):
    z = w0 + ops.dot(jnp.tanh(xlw), w_up)
    lwd = -jnp.exp(-_softplus(-z) - 0.5)
    a = _sigmoid(a0 + ops.dot(xla, a_up))
    gate = ops.dot(_sigmoid(xlg), g_up)
    kk = xk * k_k
    kk = kk * lax.rsqrt(_headsum(ops, kk * kk, RWKV_HEAD) + L2_EPS)
    k2 = xk * (1.0 + (a - 1.0) * k_a)
    return xr, lwd, k2, xv, -kk, kk * a, gate


def _f_rwkv_post(ops, y, r, k2, v, gate, r_k, gn_w, gn_b):
    inv = 1.0 / RWKV_HEAD
    yc = y - _headsum(ops, y, RWKV_HEAD) * inv
    var = _headsum(ops, yc * yc, RWKV_HEAD) * inv
    yn = yc * lax.rsqrt(var + RWKV_GN_EPS) * gn_w + gn_b
    bonus = _headsum(ops, r * k2 * r_k, RWKV_HEAD) * v
    return ((yn + bonus) * gate,)


def _f_gdn_pre(ops, qc, kc, vc, ba, alog_p, dt_p, *, heads):
    lane = lax.broadcasted_iota(jnp.int32, (1, LANES), 1)
    beta = _sigmoid(ba)
    g = -jnp.exp(alog_p) * _softplus(ba + dt_p)
    gb = jnp.where(lane < heads, beta, g)
    width = qc.shape[-1]
    row = lax.broadcasted_iota(jnp.int32, (LANES, width), 0)
    col_head = lax.shift_right_logical(lax.broadcasted_iota(jnp.int32, (LANES, width), 1), int(math.log2(GDN_HEAD)))
    beta_x = ops.dot(gb, (row == col_head).astype(BF16), 1, 0, 'cb')
    g_x = ops.dot(gb, (row == col_head + heads).astype(BF16), 1, 0, 'cb')
    qn = qc * lax.rsqrt(_headsum(ops, qc * qc, GDN_HEAD) + L2_EPS)
    kn = kc * lax.rsqrt(_headsum(ops, kc * kc, GDN_HEAD) + L2_EPS)
    return qn, kn, vc, beta_x, g_x


def _f_gdn_post(ops, o, z, nw):
    ms = _headsum(ops, o * o, GDN_HEAD) * (1.0 / GDN_HEAD)
    return (o * lax.rsqrt(ms + NORM_EPS) * nw * (z * _sigmoid(z)),)


def _f_xattn(ops, q, k, v):
    width = q.shape[-1]
    lane = lax.broadcasted_iota(jnp.int32, (1, width), 1)
    out = jnp.zeros_like(q)
    for h in range(width // XA_HEAD):
        m = ((lane >= h * XA_HEAD) & (lane < (h + 1) * XA_HEAD)).astype(F32)
        s = ops.dot(q * m, k, 1, 1) * (XA_HEAD ** -0.5)
        s = s - jnp.max(s, axis=-1, keepdims=True)
        e = jnp.exp(s)
        pr = e / jnp.sum(e, axis=-1, keepdims=True)
        out = out + ops.dot(pr, v) * m
    return (out,)


def _inv_unit_lower(ops, a):
    n = a.shape[-1]
    ii = lax.broadcasted_iota(jnp.int32, (n, n), 0)
    jj = lax.broadcasted_iota(jnp.int32, (n, n), 1)
    p = (ii == jj).astype(F32) + a
    q = a
    k = 2
    while k < n:
        q = ops.dot(q, q, 1, 0, INV_MODE)
        p = p + ops.dot(p, q, 1, 0, INV_MODE)
        k *= 2
    return p


def _const(mask, batch):
    return jnp.broadcast_to(mask.astype(BF16), (batch,) + mask.shape)


def _rwkv_chunk(ops, s, r, lw, k, v, a, b):
    g, c, _ = r.shape
    ii = lax.broadcasted_iota(jnp.int32, (c, c), 0)
    jj = lax.broadcasted_iota(jnp.int32, (c, c), 1)
    incl = ii >= jj
    strict = ii > jj
    cum = ops.dot(_const(incl, g), lw, 1, 0, 'ca')
    tot = ops.dot(jnp.ones((g, LANES, c), BF16), lw, 1, 0, 'ca')
    w_inv = jnp.exp(-cum)
    at = a * jnp.exp(cum - lw)
    bt = b * w_inv
    kt = k * w_inv
    rt = r * jnp.exp(cum)
    lane = lax.broadcasted_iota(jnp.int32, (2, 1, 1, LANES), 3)
    head = lax.broadcasted_iota(jnp.int32, (2, 1, 1, LANES), 0)
    hm = ((lane >= RWKV_HEAD) == (head == 1)).astype(F32)

    def per_head(x):
        return (x[None] * hm).reshape(2 * g, c, x.shape[-1])

    def both(x):
        return jnp.broadcast_to(x[None], (2,) + x.shape).reshape((2 * g,) + x.shape[1:])

    def merge(x2):
        return jnp.sum(x2.reshape(2, g, c, LANES) * hm, axis=0)

    bt2, kt2, v2 = both(bt), both(kt), both(v)
    atm, rtm = per_head(at), per_head(rt)
    base = ops.dot(at, s, 1, 1, REC_MODE)
    a_ab = jnp.where(strict, ops.dot(atm, bt2, 1, 1, REC_MODE), 0.0)
    a_ak = jnp.where(strict, ops.dot(atm, kt2, 1, 1, REC_MODE), 0.0)
    a_rb = jnp.where(incl, ops.dot(rtm, bt2, 1, 1, REC_MODE), 0.0)
    a_rk = jnp.where(incl, ops.dot(rtm, kt2, 1, 1, REC_MODE), 0.0)
    t = _inv_unit_lower(ops, a_ab)
    u = merge(ops.dot(t, both(base) + ops.dot(a_ak, v2, 1, 0, REC_MODE), 1, 0, REC_MODE))
    y = ops.dot(rt, s, 1, 1, REC_MODE) + merge(ops.dot(a_rb, both(u), 1, 0, REC_MODE) + ops.dot(a_rk, v2, 1, 0, REC_MODE))
    bi = lax.broadcasted_iota(jnp.int32, (LANES, LANES), 0) >= RWKV_HEAD
    bj = lax.broadcasted_iota(jnp.int32, (LANES, LANES), 1) >= RWKV_HEAD
    upd = jnp.where(bi == bj, ops.dot(u, bt, 0, 0, REC_MODE) + ops.dot(v, kt, 0, 0, REC_MODE), 0.0)
    return (s + upd) * jnp.exp(tot), y


def _gdn_chunk(ops, s, q, k, v, beta, g):
    nb, c, _ = q.shape
    ii = lax.broadcasted_iota(jnp.int32, (c, c), 0)
    jj = lax.broadcasted_iota(jnp.int32, (c, c), 1)
    causal = ii >= jj
    strict = ii > jj
    q = q * (GDN_HEAD ** -0.5)
    gc = ops.dot(_const(causal, nb), g, 1, 0, 'ca')
    ones = jnp.ones((nb, c, LANES), BF16)
    gc_col = ops.dot(gc, ones, 1, 1, 'cb') * (1.0 / LANES)
    gc_row = ops.dot(ones, gc, 1, 1, 'ca') * (1.0 / LANES)
    decay = jnp.where(causal, jnp.exp(jnp.where(causal, gc_col - gc_row, 0.0)), 0.0)
    kb = k * beta
    lmat = jnp.where(strict, ops.dot(kb, k, 1, 1, REC_MODE) * decay, 0.0)
    tinv = _inv_unit_lower(ops, -lmat)
    egc = jnp.exp(gc)
    u = ops.dot(tinv, v * beta, 1, 0, REC_MODE)
    w = ops.dot(tinv, kb * egc, 1, 0, REC_MODE)
    a_intra = jnp.where(causal, ops.dot(q, k, 1, 1, REC_MODE) * decay, 0.0)
    g_tot = ops.dot(jnp.ones((nb, c, c), BF16), g, 1, 0, 'ca')
    k_dec = k * jnp.exp(g_tot - gc)
    v_new = u - ops.dot(w, s, 1, 0, REC_MODE)
    o = ops.dot(q * egc, s, 1, 0, REC_MODE) + ops.dot(a_intra, v_new, 1, 0, REC_MODE)
    g_full = ops.dot(jnp.ones((nb, LANES, c), BF16), g, 1, 0, 'ca')
    s_new = s * jnp.exp(g_full) + ops.dot(k_dec, v_new, 0, 0, REC_MODE)
    return s_new, o


def _mm(name, a, b, ta=False, tb=False, out_dtype=F32, tm=1024, tn=1024, tk=2048, epilogue=None, extra=(),
        b_shards=False, out_shards=False):
    m = a.shape[1] if ta else a.shape[0]
    kd = a.shape[0] if ta else a.shape[1]
    ns = b.shape[2] if b_shards else None
    b_rows, b_cols = (b.shape[1], 4 * ns) if b_shards else b.shape
    n = b_rows if tb else b_cols
    assert kd == (b_cols if tb else b_rows), (name, a.shape, b.shape)
    n_unit = ns if (b_shards and not tb) else (n // 4 if out_shards else n)
    k_unit = ns if (b_shards and tb) else kd
    tm, tn, tk = _pick(m, tm), _pick(n_unit, tn), _pick(k_unit, tk)
    nk = kd // tk
    n_x = len(extra)
    out_dtypes = out_dtype if epilogue is not None else (out_dtype,)

    def body(a_ref, b_ref, *rest):
        x_refs, o_refs, acc = rest[:n_x], rest[n_x:-1], rest[-1]
        kk = pl.program_id(2)
        part = _dg(a_ref[...].astype(BF16), b_ref[...].astype(BF16), 0 if ta else 1, 1 if tb else 0)

        def finish(z):
            vals = (z,) if epilogue is None else epilogue(z, *[xr[...] for xr in x_refs])
            for o_ref, val in zip(o_refs, vals):
                o_ref[...] = val.astype(o_ref.dtype)

        if nk == 1:
            finish(part)
            return

        @pl.when(kk == 0)
        def _():
            acc[...] = part

        @pl.when((kk > 0) & (kk < nk - 1))
        def _():
            acc[...] += part

        @pl.when(kk == nk - 1)
        def _():
            finish(acc[...] + part)

    a_spec = pl.BlockSpec((tk, tm), lambda i, j, k: (k, i)) if ta else pl.BlockSpec((tm, tk), lambda i, j, k: (i, k))
    if not b_shards:
        b_spec = pl.BlockSpec((tn, tk), lambda i, j, k: (j, k)) if tb else pl.BlockSpec((tk, tn), lambda i, j, k: (k, j))
    elif tb:
        per_k = ns // tk
        b_spec = pl.BlockSpec((None, tn, tk), lambda i, j, k: (k // per_k, j, k % per_k))
    else:
        per_n = ns // tn
        b_spec = pl.BlockSpec((None, tk, tn), lambda i, j, k: (j // per_n, k, j % per_n))
    o_spec = pl.BlockSpec((tm, tn), lambda i, j, k: (i, j))
    if out_shards:
        per_o = (n // 4) // tn
        res_spec = [pl.BlockSpec((None, tm, tn), lambda i, j, k: (j // per_o, i, j % per_o))]
        res_shape = [jax.ShapeDtypeStruct((4, m, n // 4), out_dtypes[0])]
    else:
        res_spec = [o_spec] * len(out_dtypes)
        res_shape = [jax.ShapeDtypeStruct((m, n), dt) for dt in out_dtypes]
    res = _pcall(
        body, name=name, grid=(m // tm, n // tn, nk), in_specs=[a_spec, b_spec] + [o_spec] * n_x,
        out_specs=res_spec, out_shape=res_shape,
        scratch_shapes=[pltpu.VMEM((tm, tn), F32)],
        compiler_params=_params(("parallel", "parallel", "arbitrary")))(a, b, *extra)
    return res if epilogue is not None else res[0]


def _row_spec(tm, width, cb):
    return pl.BlockSpec((tm, width), lambda i: (i, cb))


def _rows_fwd(name, f, rows, params, outs, tm):
    t = rows[0][0].shape[0]
    tm = min(tm, t)
    n_in = len(rows) + len(params)
    kept = [o for o in outs if o is not None]

    def body(*refs):
        vals = f(_RAW, *[r[...].astype(F32) for r in refs[:n_in]])
        o_refs = refs[n_in:]
        j = 0
        for val, o in zip(vals, outs):
            if o is not None:
                o_refs[j][...] = val.astype(o_refs[j].dtype)
                j += 1

    in_specs = [_row_spec(tm, w, cb) for (_, w, cb) in rows]
    in_specs += [pl.BlockSpec(p.shape, lambda i: (0, 0)) for p in params]
    res = _pcall(
        body, name=name, grid=(t // tm,), in_specs=in_specs,
        out_specs=[_row_spec(tm, w, 0) for (w, _) in kept],
        out_shape=[jax.ShapeDtypeStruct((t, w), dt) for (w, dt) in kept],
        compiler_params=_params(("parallel",)))(*[r[0] for r in rows], *params)
    return res


def _rows_bwd(name, f, rows, params, cts, tm, row_out, out_widths, adds=(), loss=False, out_dtypes=None, alias=None):
    t = rows[0][0].shape[0]
    tm = min(tm, t)
    n_r, n_p = len(rows), len(params)
    ct_flat = [c for cl in cts for c in cl]
    n_ct, n_add = len(ct_flat), len(adds)
    n_out = len(out_widths)

    def body(*refs):
        i = pl.program_id(0)
        ins = [r[...].astype(F32) for r in refs[:n_r + n_p]]
        ct_refs = refs[n_r + n_p:n_r + n_p + n_ct]
        add_refs = refs[n_r + n_p + n_ct:n_r + n_p + n_ct + n_add]
        o_refs = refs[n_r + n_p + n_ct + n_add + (alias is not None):]
        outs, vjp = jax.vjp(lambda *a: f(_DIFF, *a), *ins)
        ctv, pos = [], 0
        for o, cl in zip(outs, cts):
            if loss:
                ctv.append(jnp.ones_like(o))
            elif not cl:
                ctv.append(jnp.zeros_like(o))
            else:
                acc = ct_refs[pos][...].astype(F32)
                for q in range(1, len(cl)):
                    acc = acc + ct_refs[pos + q][...].astype(F32)
                ctv.append(acc)
            pos += len(cl)
        grads = vjp(tuple(ctv))
        for ri, ro in enumerate(row_out):
            if ro is not None:
                oi, off = ro
                val = grads[ri]
                for ai, (aoi, _) in enumerate(adds):
                    if aoi == oi:
                        val = val + add_refs[ai][...].astype(F32)
                o_refs[oi][:, off:off + val.shape[1]] = val.astype(o_refs[oi].dtype)
        p_refs = o_refs[n_out:n_out + n_p]

        @pl.when(i == 0)
        def _():
            for pr in p_refs:
                pr[...] = jnp.zeros_like(pr)
            if loss:
                o_refs[n_out + n_p][...] = jnp.zeros((8, LANES), F32)

        for pi, pr in enumerate(p_refs):
            pr[...] += grads[n_r + pi]
        if loss:
            o_refs[n_out + n_p][...] += jnp.sum(outs[0])

    in_specs = [_row_spec(tm, w, cb) for (_, w, cb) in rows]
    in_specs += [pl.BlockSpec(p.shape, lambda i: (0, 0)) for p in params]
    in_specs += [_row_spec(tm, w, cb) for (_, w, cb) in ct_flat]
    in_specs += [_row_spec(tm, w, cb) for (_, (_, w, cb)) in adds]
    out_specs = [_row_spec(tm, w, 0) for w in out_widths]
    out_specs += [pl.BlockSpec(p.shape, lambda i: (0, 0)) for p in params]
    out_dtypes = out_dtypes or [F32] * n_out
    out_shape = [jax.ShapeDtypeStruct((t, w), dt) for w, dt in zip(out_widths, out_dtypes)]
    out_shape += [jax.ShapeDtypeStruct(p.shape, F32) for p in params]
    if loss:
        out_specs.append(pl.BlockSpec((8, LANES), lambda i: (0, 0)))
        out_shape.append(jax.ShapeDtypeStruct((8, LANES), F32))
    extra, aliases = [], {}
    if alias is not None:
        oi, buf, cb = alias
        in_specs.append(ANY)
        out_specs[oi] = _row_spec(tm, out_widths[oi], cb)
        out_shape[oi] = jax.ShapeDtypeStruct(buf.shape, buf.dtype)
        extra, aliases = [buf], {n_r + n_p + n_ct + n_add: oi}
    res = _pcall(
        body, name=name, grid=(t // tm,), in_specs=in_specs, out_specs=out_specs, out_shape=out_shape,
        input_output_aliases=aliases, compiler_params=_params(("arbitrary",)))(
            *[r[0] for r in rows], *params, *[c[0] for c in ct_flat], *[a[1][0] for a in adds], *extra)
    return res[:n_out], res[n_out:n_out + n_p], res[n_out + n_p:]


def _shift_down(x, k):
    row = lax.broadcasted_iota(jnp.int32, x.shape, 0)
    return jnp.where(row >= k, pltpu.roll(x, k, 0), 0.0)


def _shift_up(x, k):
    t = x.shape[0]
    row = lax.broadcasted_iota(jnp.int32, x.shape, 0)
    return jnp.where(row < t - k, pltpu.roll(x, t - k, 0), 0.0)


def _tokshift_fwd(p, mix, n_main, main_cb, lora_cb):
    t = p.shape[0]
    nblk = mix.shape[1] // LANES

    def src(j):
        return (0, jnp.where(j < n_main, j + main_cb, j - n_main + lora_cb))

    def body(p_ref, mix_ref, o_ref):
        xv = p_ref[...]
        o_ref[...] = xv + (_shift_down(xv, 1) - xv) * mix_ref[...]

    return _pcall(
        body, name="tokshift_fwd", grid=(nblk,),
        in_specs=[pl.BlockSpec((t, LANES), src), pl.BlockSpec((1, LANES), lambda j: (0, j))],
        out_specs=pl.BlockSpec((t, LANES), lambda j: (0, j)),
        out_shape=jax.ShapeDtypeStruct((t, mix.shape[1]), F32), compiler_params=_params(("parallel",)))(p, mix)


def _tokshift_bwd(dxm, p, mix, dp_buf, n_main, main_cb, lora_cb):
    t = p.shape[0]
    nblk = mix.shape[1] // LANES

    def src(j):
        return (0, jnp.where(j < n_main, j + main_cb, j - n_main + lora_cb))

    def body(d_ref, p_ref, mix_ref, buf_ref, dp_ref, dmix_ref):
        xv, d, mx = p_ref[...], d_ref[...], mix_ref[...]
        dp_ref[...] = (d * (1.0 - mx) + _shift_up(d * mx, 1)).astype(dp_ref.dtype)
        dmix_ref[...] = jnp.sum(d * (_shift_down(xv, 1) - xv), axis=0, keepdims=True)

    return _pcall(
        body, name="tokshift_bwd", grid=(nblk,),
        in_specs=[pl.BlockSpec((t, LANES), lambda j: (0, j)), pl.BlockSpec((t, LANES), src),
                  pl.BlockSpec((1, LANES), lambda j: (0, j)), ANY],
        out_specs=[pl.BlockSpec((t, LANES), src), pl.BlockSpec((1, LANES), lambda j: (0, j))],
        out_shape=[jax.ShapeDtypeStruct(dp_buf.shape, dp_buf.dtype), jax.ShapeDtypeStruct((1, mix.shape[1]), F32)],
        input_output_aliases={3: 0},
        compiler_params=_params(("parallel",)))(dxm, p, mix, dp_buf)


def _conv_pre(xv, w):
    return (w[3:4, :] * xv + w[2:3, :] * _shift_down(xv, 1) + w[1:2, :] * _shift_down(xv, 2)
            + w[0:1, :] * _shift_down(xv, 3))


def _conv_fwd(p, w8, cb0):
    t = p.shape[0]
    width = w8.shape[1]

    def body(p_ref, w_ref, o_ref):
        c = _conv_pre(p_ref[...], w_ref[...])
        o_ref[...] = c * _sigmoid(c)

    return _pcall(
        body, name="conv_fwd", grid=(width // LANES,),
        in_specs=[pl.BlockSpec((t, LANES), lambda j: (0, j + cb0)), pl.BlockSpec((8, LANES), lambda j: (0, j))],
        out_specs=pl.BlockSpec((t, LANES), lambda j: (0, j)),
        out_shape=jax.ShapeDtypeStruct((t, width), F32), compiler_params=_params(("parallel",)))(p, w8)


def _conv_bwd(dy, p, w8, dp_buf, cb0):
    t = p.shape[0]
    width = w8.shape[1]

    def body(d_ref, p_ref, w_ref, buf_ref, dx_ref, dw_ref):
        xv, w = p_ref[...], w_ref[...]
        c = _conv_pre(xv, w)
        sg = _sigmoid(c)
        dc = d_ref[...] * (sg * (1.0 + c * (1.0 - sg)))
        dx_ref[...] = (w[3:4, :] * dc + w[2:3, :] * _shift_up(dc, 1) + w[1:2, :] * _shift_up(dc, 2)
                       + w[0:1, :] * _shift_up(dc, 3)).astype(dx_ref.dtype)
        row =lax.broadcasted_iota(jnp.int32, (8, LANES), 0)
        dw = jnp.zeros((8, LANES), F32)
        for j in range(4):
            sj = jnp.sum(dc * (xv if j == 3 else _shift_down(xv, 3 - j)), axis=0, keepdims=True)
            dw = dw + jnp.where(row == j, sj, 0.0)
        dw_ref[...] = dw

    return _pcall(
        body, name="conv_bwd", grid=(width // LANES,),
        in_specs=[pl.BlockSpec((t, LANES), lambda j: (0, j)), pl.BlockSpec((t, LANES), lambda j: (0, j + cb0)),
                  pl.BlockSpec((8, LANES), lambda j: (0, j)), ANY],
        out_specs=[pl.BlockSpec((t, LANES), lambda j: (0, j + cb0)), pl.BlockSpec((8, LANES), lambda j: (0, j))],
        out_shape=[jax.ShapeDtypeStruct(dp_buf.shape, dp_buf.dtype), jax.ShapeDtypeStruct((8, width), F32)],
        input_output_aliases={3: 0},
        compiler_params=_params(("parallel",)))(dy, p, w8, dp_buf)


def _rec_fwd(name, chunk_fn, ins, nblocks):
    t = ins[0][0].shape[0]
    tb = min(TIME_BLOCK, t)
    nt, ncb = t // tb, tb // CHUNK
    n_in = len(ins)
    grp = REC_GROUP if nblocks % REC_GROUP == 0 and all(c0 % REC_GROUP == 0 for (_, c0) in ins) else 1
    width = grp * LANES

    def body(*refs):
        in_refs, y_ref, sall_ref, s_scr = refs[:n_in], refs[n_in], refs[n_in + 1], refs[n_in + 2]

        @pl.when(pl.program_id(1) == 0)
        def _():
            s_scr[...] = jnp.zeros_like(s_scr)

        def step(ci, s):
            sl = pl.ds(pl.multiple_of(ci * CHUNK, CHUNK), CHUNK)
            lanes = [slice(gi * LANES, (gi + 1) * LANES) for gi in range(grp)]
            s1, y = chunk_fn(_RAW, s, *[jnp.stack([r[sl, ln] for ln in lanes]) for r in in_refs])
            for gi, ln in enumerate(lanes):
                sall_ref[gi, ci] = s[gi]
                y_ref[sl, ln] = y[gi]
            return s1

        s_scr[...] = lax.fori_loop(0, ncb, step, s_scr[...])

    in_specs = [pl.BlockSpec((tb, width), functools.partial(lambda h, tt, cb: (tt, cb + h), cb=c0 // grp))
                for (_, c0) in ins]
    return _pcall(
        body, name=name, grid=(nblocks // grp, nt), in_specs=in_specs,
        out_specs=[pl.BlockSpec((tb, width), lambda h, tt: (tt, h)),
                   pl.BlockSpec((grp, ncb, LANES, LANES), lambda h, tt: (h, tt, 0, 0))],
        out_shape=[jax.ShapeDtypeStruct((t, nblocks * LANES), F32),
                   jax.ShapeDtypeStruct((nblocks, t // CHUNK, LANES, LANES), F32)],
        scratch_shapes=[pltpu.VMEM((grp, LANES, LANES), F32)],
        compiler_params=_params(("parallel", "arbitrary")))(*[a for (a, _) in ins])


def _rec_bwd(name, chunk_fn, ins, dy, sall, nblocks):
    t = ins[0][0].shape[0]
    tb = min(TIME_BLOCK, t)
    nt, ncb = t // tb, tb // CHUNK
    n_in = len(ins)
    grp = REC_GROUP if nblocks % REC_GROUP == 0 and all(c0 % REC_GROUP == 0 for (_, c0) in ins) else 1
    width = grp * LANES

    def body(*refs):
        in_refs, dy_ref, sall_ref = refs[:n_in], refs[n_in], refs[n_in + 1]
        d_refs, ds_scr = refs[n_in + 2:2 * n_in + 2], refs[2 * n_in + 2]

        @pl.when(pl.program_id(1) == 0)
        def _():
            ds_scr[...] = jnp.zeros_like(ds_scr)

        def step(j, ds):
            ci = ncb - 1 - j
            sl = pl.ds(pl.multiple_of(ci * CHUNK, CHUNK), CHUNK)
            lanes = [slice(gi * LANES, (gi + 1) * LANES) for gi in range(grp)]
            s0 = jnp.stack([sall_ref[gi, ci] for gi in range(grp)])
            vals = [jnp.stack([r[sl, ln] for ln in lanes]) for r in in_refs]
            _, vjp = jax.vjp(lambda *a: chunk_fn(_DIFF, *a), s0, *vals)
            grads = vjp((ds, jnp.stack([dy_ref[sl, ln] for ln in lanes])))
            for dr, gval in zip(d_refs, grads[1:]):
                for gi, ln in enumerate(lanes):
                    dr[sl, ln] = gval[gi]
            return grads[0]

        ds_scr[...] = lax.fori_loop(0, ncb, step, ds_scr[...])

    in_specs = [pl.BlockSpec((tb, width), functools.partial(lambda h, tt, cb: (nt - 1 - tt, cb + h), cb=c0 // grp))
                for (_, c0) in ins]
    in_specs.append(pl.BlockSpec((tb, width), lambda h, tt: (nt - 1 - tt, h)))
    in_specs.append(pl.BlockSpec((grp, ncb, LANES, LANES), lambda h, tt: (h, nt - 1 - tt, 0, 0)))
    return _pcall(
        body, name=name, grid=(nblocks // grp, nt), in_specs=in_specs,
        out_specs=[pl.BlockSpec((tb, width), lambda h, tt: (nt - 1 - tt, h)) for _ in ins],
        out_shape=[jax.ShapeDtypeStruct((t, nblocks * LANES), F32) for _ in ins],
        scratch_shapes=[pltpu.VMEM((grp, LANES, LANES), F32)],
        compiler_params=_params(("parallel", "arbitrary")))(*[a for (a, _) in ins], dy, sall)


def _pick_rows(r, c):
    tr = max(16, ((1 << 18) // c) // 16 * 16)
    while tr > 16 and r % tr:
        tr -= 16
    return tr if r % tr == 0 else r


def _sum4(name, own, r3):
    _, r, c = r3.shape
    tr = _pick_rows(r, c)

    def body(own_ref, r_ref, o_ref):
        total = ((own_ref[...].astype(F32) + r_ref[0].astype(F32)) + r_ref[1].astype(F32)) + r_ref[2].astype(F32)
        o_ref[...] = total.astype(o_ref.dtype)

    return _pcall(
        body, name=name, grid=(r // tr,),
        in_specs=[pl.BlockSpec((tr, c), lambda i: (i, 0)), pl.BlockSpec((3, tr, c), lambda i: (0, i, 0))],
        out_specs=pl.BlockSpec((tr, c), lambda i: (i, 0)), out_shape=jax.ShapeDtypeStruct((r, c), BF16),
        compiler_params=_params(("parallel",)))(own, r3)


def _adam_math(w, g, m, v):
    m = ADAM_B1 * m + (1.0 - ADAM_B1) * g
    v = ADAM_B2 * v + (1.0 - ADAM_B2) * (g * g)
    m_hat = m / (1.0 - ADAM_B1 ** ADAM_STEP)
    v_hat = v / (1.0 - ADAM_B2 ** ADAM_STEP)
    return -ADAM_LR * (m_hat / (jnp.sqrt(v_hat) + ADAM_EPS) + ADAM_WD * w), m, v


def _adam_big(name, w, m, v, ga, gb):
    r, c = w.shape
    tr = _pick_rows(r, c)

    def body(w_ref, m_ref, v_ref, ga_ref, gb_ref, g_ref, d_ref, mo_ref, vo_ref):
        g = ga_ref[...].astype(F32) + gb_ref[...].astype(F32)
        g_ref[...] = g
        d_ref[...], mo_ref[...], vo_ref[...] = _adam_math(w_ref[...], g, m_ref[...], v_ref[...])

    spec = pl.BlockSpec((tr, c), lambda i: (i, 0))
    return _pcall(
        body, name=name, grid=(r // tr,), in_specs=[spec] * 5, out_specs=[spec] * 4,
        out_shape=[jax.ShapeDtypeStruct((r, c), F32)] * 4, compiler_params=_params(("parallel",)))(w, m, v, ga, gb)


def _adam_small(w, m, v, g8):
    shape = w.shape

    def body(w_ref, m_ref, v_ref, g8_ref, g_ref, d_ref, mo_ref, vo_ref):
        g = g8_ref[0]
        for dev in range(1, 8):
            g = g + g8_ref[dev]
        g_ref[...] = g
        d_ref[...], mo_ref[...], vo_ref[...] = _adam_math(w_ref[...], g, m_ref[...], v_ref[...])

    return _pcall(
        body, name="adam_small", out_shape=[jax.ShapeDtypeStruct(shape, F32)] * 4,
        compiler_params=_params())(w, m, v, g8)


def _place():
    x, y, c = lax.axis_index("x"), lax.axis_index("y"), lax.axis_index("c")
    return x, y, c, [(1 - x, y), (x, 1 - y), (1 - x, 1 - y)]


def _remote(src, dst, send_sems, recv_sems, k, dev):
    return pltpu.make_async_remote_copy(src_ref=src, dst_ref=dst, send_sem=send_sems.at[k], recv_sem=recv_sems.at[k],
                                        device_id=dev, device_id_type=MESH)


HBM = pl.BlockSpec(memory_space=pltpu.HBM)
SEM = pl.BlockSpec(memory_space=pltpu.SEMAPHORE)


def _send3_start(name, src, land, mode):
    land_arr = land if mode == 'forward_half' else lax.empty(land, src.dtype)
    half = land_arr.shape[1] // 2

    def body(src_ref, land_ref, send_sem, recv_sem, src_thru, land_thru, token):
        x, y, c, chips = _place()
        rows = pl.ds(pl.multiple_of(c * half, half), half)
        for k, (px, py) in enumerate(chips):
            dev = (px, py, c)
            if mode == 'gather':
                s_k, d_k = src_ref, land_ref.at[2 * x + y]
            elif mode == 'gather_half':
                s_k, d_k = src_ref.at[rows], land_ref.at[2 * x + y, rows]
            elif mode == 'forward_half':
                s_k = d_k = land_ref.at[2 * px + py, rows]
                dev = (x, y, 1 - c)
            else:
                s_k, d_k = src_ref.at[2 * px + py], land_ref.at[k]
            pltpu.make_async_remote_copy(src_ref=s_k, dst_ref=d_k, send_sem=send_sem, recv_sem=recv_sem,
                                         device_id=dev, device_id_type=MESH).start()
        token[...] = jnp.zeros_like(token)

    return pl.pallas_call(
        body, name=name,
        out_shape=(pltpu.SemaphoreType.DMA(()), pltpu.SemaphoreType.DMA(()), pltpu.HBM(src.shape, src.dtype),
                   pltpu.HBM(land_arr.shape, src.dtype), jax.ShapeDtypeStruct((8, LANES), F32)),
        in_specs=(HBM, HBM), out_specs=(SEM, SEM, HBM, HBM, pl.BlockSpec(memory_space=pltpu.VMEM)),
        input_output_aliases={0: 2, 1: 3},
        compiler_params=pltpu.CompilerParams(has_side_effects=pltpu.SideEffectType.DATAFLOW_SIDE_EFFECTING),
    )(pltpu.with_memory_space_constraint(src, pltpu.HBM), pltpu.with_memory_space_constraint(land_arr, pltpu.HBM))


def _send3_wait(name, started, after, halves=False):
    send_sem, recv_sem, src_thru, land_thru, _ = started
    half = land_thru.shape[1] // 2

    def body(src_ref, land_ref, send_sem, recv_sem, after_ref, src_dead, got_ref):
        x, y, c, chips = _place()
        three = land_ref.at[pl.ds(0, 3), pl.ds(0, half)] if halves else land_ref.at[pl.ds(0, 3)]
        cp = pltpu.make_async_remote_copy(src_ref=three, dst_ref=three, send_sem=send_sem, recv_sem=recv_sem,
                                          device_id=(chips[0][0], chips[0][1], c), device_id_type=MESH)
        cp.wait_send()
        cp.wait_recv()

    return pl.pallas_call(
        body, name=name,
        out_shape=(pltpu.HBM(src_thru.shape, src_thru.dtype), pltpu.HBM(land_thru.shape, land_thru.dtype)),
        in_specs=(HBM, HBM, SEM, SEM, ANY), out_specs=(HBM, HBM), input_output_aliases={0: 0, 1: 1},
        compiler_params=pltpu.CompilerParams(has_side_effects=pltpu.SideEffectType.DATAFLOW_SIDE_EFFECTING),
    )(src_thru, land_thru, send_sem, recv_sem, after)


def _swap_sibling(name, part):
    def body(src, out, send_sems, recv_sems):
        x, y, c, _ = _place()
        cp = _remote(src, out, send_sems, recv_sems, 0, (x, y, 1 - c))
        cp.start()
        cp.wait_recv()
        cp.wait_send()

    return _pcall(
        body, name=name, in_specs=[ANY], out_specs=ANY, out_shape=jax.ShapeDtypeStruct(part.shape, part.dtype),
        scratch_shapes=[pltpu.SemaphoreType.DMA((1,)), pltpu.SemaphoreType.DMA((1,))])(part)


def _gather8(flat):
    def body(src, out, send_sems, recv_sems, local_sem):
        x, y, c, _ = _place()
        own = pltpu.make_async_copy(src, out.at[4 * x + 2 * y + c], local_sem)
        own.start()
        peers = []
        for mask in range(1, 8):
            px = x ^ ((mask >> 2) & 1)
            py = y ^ ((mask >> 1) & 1)
            pc = c ^ (mask & 1)
            peers.append((px, py, pc))
        sends = [_remote(src, out.at[4 * x + 2 * y + c], send_sems, recv_sems, k, dev) for k, dev in enumerate(peers)]
        for cp in sends:
            cp.start()
        for k, (px, py, pc) in enumerate(peers):
            _remote(src, out.at[4 * px + 2 * py + pc], send_sems, recv_sems, k, (px, py, pc)).wait_recv()
        for cp in sends:
            cp.wait_send()
        own.wait()

    return _pcall(
        body, name="gather8_small", in_specs=[ANY], out_specs=ANY,
        out_shape=jax.ShapeDtypeStruct((8,) + flat.shape, flat.dtype),
        scratch_shapes=[pltpu.SemaphoreType.DMA((7,)), pltpu.SemaphoreType.DMA((7,)), pltpu.SemaphoreType.DMA(())],
    )(flat)


def _pad_cols(a, width):
    return jnp.pad(a, ((0, 0), (0, width - a.shape[1])))


def _col_shards(full):
    r, c = full.shape
    return full.reshape(r, 4, c // 4).transpose(1, 0, 2)


def kernel(x, mem, mix_norm_pre, mix_norm_post, w_in, rwkv_shift_mix, rwkv_w0, rwkv_w_up, rwkv_a0, rwkv_a_up, rwkv_g_up, rwkv_k_k, rwkv_k_a, rwkv_r_k, rwkv_gn_w, rwkv_gn_b, gdn_conv_w, gdn_a_log, gdn_dt_bias, gdn_norm_w, w_branch_rwkv, w_branch_gdn, w_mix_out, xa_norm_pre, xa_norm_mem, xa_norm_post, xa_w_q, xa_w_kv, xa_w_o, mlp_norm_pre, mlp_norm_post, mlp_w_up, mlp_w_down, loss_target, m_mix_norm_pre, m_mix_norm_post, m_w_in, m_rwkv_shift_mix, m_rwkv_w0, m_rwkv_w_up, m_rwkv_a0, m_rwkv_a_up, m_rwkv_g_up, m_rwkv_k_k, m_rwkv_k_a, m_rwkv_r_k, m_rwkv_gn_w, m_rwkv_gn_b, m_gdn_conv_w, m_gdn_a_log, m_gdn_dt_bias, m_gdn_norm_w, m_w_branch_rwkv, m_w_branch_gdn, m_w_mix_out, m_xa_norm_pre, m_xa_norm_mem, m_xa_norm_post, m_xa_w_q, m_xa_w_kv, m_xa_w_o, m_mlp_norm_pre, m_mlp_norm_post, m_mlp_w_up, m_mlp_w_down, v_mix_norm_pre, v_mix_norm_post, v_w_in, v_rwkv_shift_mix, v_rwkv_w0, v_rwkv_w_up, v_rwkv_a0, v_rwkv_a_up, v_rwkv_g_up, v_rwkv_k_k, v_rwkv_k_a, v_rwkv_r_k, v_rwkv_gn_w, v_rwkv_gn_b, v_gdn_conv_w, v_gdn_a_log, v_gdn_dt_bias, v_gdn_norm_w, v_w_branch_rwkv, v_w_branch_gdn, v_w_mix_out, v_xa_norm_pre, v_xa_norm_mem, v_xa_norm_post, v_xa_w_q, v_xa_w_kv, v_xa_w_o, v_mlp_norm_pre, v_mlp_norm_post, v_mlp_w_up, v_mlp_w_down):
    given = dict(locals())
    W = {n: given[n] if given[n].ndim == 2 else given[n][0] for n in WEIGHTS}
    Mo = {n: given['m_' + n].reshape(W[n].shape) for n in WEIGHTS}
    Vo = {n: given['v_' + n].reshape(W[n].shape) for n in WEIGHTS}
    xs, mems, tgt = x[0], mem[0], loss_target[0]
    T, D = xs.shape
    RW = W['rwkv_w0'].shape[-1]
    GW = W['gdn_conv_w'].shape[-1] * 4 // 3
    NH = W['gdn_a_log'].shape[-1]
    LW, LA, LG = W['rwkv_w_up'].shape[0], W['rwkv_a_up'].shape[0], W['rwkv_g_up'].shape[0]
    XW = W['xa_w_q'].shape[-1]
    assert LW <= LANES and LA <= LANES and LG % LANES == 0 and 2 * NH <= LANES

    del _PENDING[:]
    me = 2 * lax.axis_index("x") + lax.axis_index("y")
    small_sizes = [W[n].size for n in SMALL_SHARDED]
    small_len = -(-sum(small_sizes) // 1024) * 1024

    def pack_small(vals):
        flat = jnp.concatenate([vv.reshape(-1).astype(F32) for vv in vals])
        return jnp.pad(flat, (0, small_len - flat.shape[0])).reshape(small_len // 1024, 1024)

    def unpack_small(packed):
        flat, out, off = packed.reshape(-1), [], 0
        for n, sz in zip(SMALL_SHARDED, small_sizes):
            out.append(flat[off:off + sz].reshape(W[n].shape))
            off += sz
        return out

    gstarted = {}

    def gstart(n, after=None):
        local = pack_small([W[s] for s in SMALL_SHARDED]) if n == 'small' else W[n].astype(BF16)
        if after is not None:
            local, _ = lax.optimization_barrier((local, after))
        mode = 'gather_half' if n in HALVED else 'gather'
        gstarted[n] = _send3_start("gather_start_" + n, local, (4,) + local.shape, mode)
        _PENDING.append(gstarted[n][4])

    HALVED = ('w_in', 'mlp_w_up', 'mlp_w_down')
    gstart('w_in')
    gstart('small')

    def gathered(n, after):
        src, land = _send3_wait("gather_wait_" + n, gstarted[n], after, halves=n in HALVED)
        if n in HALVED:
            fwd = _send3_start("gather_forward_" + n, src, land, 'forward_half')
            src, land = _send3_wait("gather_forward_wait_" + n, fwd, fwd[4], halves=True)
        return lax.dynamic_update_index_in_dim(land, src, me, 0)

    def gathered_full(n, after):
        g = gathered(n, after)
        if n == 'w_in':
            return jnp.concatenate([g[0], g[1], g[2], g[3]], axis=1)
        if n in COL_SHARDED:
            return g
        return g.reshape((4 * g.shape[1],) + g.shape[2:])

    RC = 3 * RW + LW + LA + LG
    o_z_raw = RC + 3 * GW
    o_b_raw = o_z_raw + GW
    o_grw_raw = o_b_raw + 2 * NH
    O_GRW = 0
    O_GGDN = D
    O_R = 2 * D
    O_GQ = O_R + 3 * RW
    O_Z = O_GQ + 3 * GW
    O_LW = O_Z + GW
    O_LA = O_LW + LANES
    O_LG = O_LA + LANES
    O_BA = O_LG + LG
    END = O_BA + LANES
    NP = -(-END // 1024) * 1024
    XM = 3 * RW + 2 * LANES + LG
    assert O_R % RW == 0 and O_GQ % GW == 0 and O_Z % GW == 0 and O_LG % LG == 0 and (3 * RW + 2 * LANES) % LG == 0

    def pad_layout(raw, lead):
        return jnp.concatenate([
            raw[:, o_grw_raw:o_grw_raw + 2 * D], raw[:, :3 * RW], raw[:, RC:RC + 3 * GW], raw[:, o_z_raw:o_z_raw + GW],
            _pad_cols(raw[:, 3 * RW:3 * RW + LW], LANES), _pad_cols(raw[:, 3 * RW + LW:3 * RW + LW + LA], LANES),
            raw[:, 3 * RW + LW + LA:RC], _pad_cols(raw[:, o_b_raw:o_b_raw + 2 * NH], LANES),
            jnp.zeros((lead, NP - END), raw.dtype)], axis=1)

    def unpad_layout(pd):
        return jnp.concatenate([
            pd[:, O_R:O_R + 3 * RW], pd[:, O_LW:O_LW + LW], pd[:, O_LA:O_LA + LA], pd[:, O_LG:O_LG + LG],
            pd[:, O_GQ:O_GQ + 3 * GW], pd[:, O_Z:O_Z + GW], pd[:, O_BA:O_BA + 2 * NH], pd[:, O_GRW:O_GRW + 2 * D]], axis=1)

    def xm_layout(raw):
        return jnp.concatenate([raw[:, :3 * RW], _pad_cols(raw[:, 3 * RW:3 * RW + LW], LANES),
                                _pad_cols(raw[:, 3 * RW + LW:3 * RW + LW + LA], LANES), raw[:, 3 * RW + LW + LA:]], axis=1)

    def xm_unlayout(pd):
        return jnp.concatenate([pd[:, :3 * RW], pd[:, 3 * RW:3 * RW + LW], pd[:, 3 * RW + LANES:3 * RW + LANES + LA],
                                pd[:, 3 * RW + 2 * LANES:]], axis=1)

    mix_x = xm_layout(W['rwkv_shift_mix'])
    r_k = W['rwkv_r_k'].reshape(1, RW)
    alog_p = jnp.pad(W['gdn_a_log'], ((0, 0), (NH, LANES - 2 * NH)))
    dt_p = jnp.pad(W['gdn_dt_bias'], ((0, 0), (NH, LANES - 2 * NH)))
    nw_t = jnp.tile(W['gdn_norm_w'], (1, GW // GDN_HEAD))
    n_main, main_cb, lora_cb = 3 * RW // LANES, O_R // LANES, O_LW // LANES
    f_gdn_pre = functools.partial(_f_gdn_pre, heads=NH)

    (u,) = _rows_fwd("rms_pre", _f_rms, [(xs, D, 0)], [W['mix_norm_pre']], [(D, BF16)], 256)
    full = {'w_in': gathered_full('w_in', u)}
    w_p = pad_layout(full['w_in'], D)
    small4 = gathered('small', w_p)
    for n in ['w_branch_rwkv', 'w_branch_gdn', 'w_mix_out', 'xa_w_q', 'xa_w_kv', 'xa_w_o', 'mlp_w_up']:
        gstart(n, small4)
    p = _mm("mm_in", u, w_p)
    small_parts = [unpack_small(small4[s]) for s in range(4)]
    for j, n in enumerate(SMALL_SHARDED):
        full[n] = jnp.concatenate([small_parts[s][j] for s in range(4)], axis=1)
    w_up_p = jnp.pad(full['rwkv_w_up'], ((0, LANES - LW), (0, 0)))
    a_up_p = jnp.pad(full['rwkv_a_up'], ((0, LANES - LA), (0, 0)))
    g_up = full['rwkv_g_up']
    conv8 = jnp.pad(full['gdn_conv_w'], ((0, 4), (0, 0)))
    rwkv_par = [W['rwkv_w0'], W['rwkv_a0'], W['rwkv_k_k'], W['rwkv_k_a'], w_up_p, a_up_p, g_up]
    xm = _tokshift_fwd(p, mix_x, n_main, main_cb, lora_cb)
    pre_rows = [(xm, RW, 0), (xm, RW, 1), (xm, RW, 2), (xm, LANES, n_main), (xm, LANES, n_main + 1),
                (xm, LG, (3 * RW + 2 * LANES) // LG)]
    lwd, k2, av, bv, gate = _rows_fwd("rwkv_pre", _f_rwkv_pre, pre_rows, rwkv_par,
                                      [None, (RW, F32), (RW, F32), None, (RW, F32), (RW, F32), (RW, F32)], 128)
    rec_rw = [(xm, 0), (lwd, 0), (k2, 0), (xm, 2 * RW // LANES), (av, 0), (bv, 0)]
    y_rec, s_rw = _rec_fwd("rwkv_rec_fwd", _rwkv_chunk, rec_rw, RW // LANES)
    post_rows = [(y_rec, RW, 0), (xm, RW, 0), (k2, RW, 0), (xm, RW, 2), (gate, RW, 0)]
    post_par = [r_k, W['rwkv_gn_w'], W['rwkv_gn_b']]
    (y_rw,) = _rows_fwd("rwkv_post", _f_rwkv_post, post_rows, post_par, [(RW, BF16)], 128)

    qkvc = _conv_fwd(p, conv8, O_GQ // LANES)
    gpre_rows = [(qkvc, GW, 0), (qkvc, GW, 1), (qkvc, GW, 2), (p, LANES, O_BA // LANES)]
    qn, kn, beta_x, g_x = _rows_fwd("gdn_pre", f_gdn_pre, gpre_rows, [alog_p, dt_p],
                                    [(GW, F32), (GW, F32), None, (GW, F32), (GW, F32)], 128)
    rec_gd = [(qn, 0), (kn, 0), (qkvc, 2 * GW // LANES), (beta_x, 0), (g_x, 0)]
    o_gd, s_gd = _rec_fwd("gdn_rec_fwd", _gdn_chunk, rec_gd, GW // LANES)
    gpost_rows = [(o_gd, GW, 0), (p, GW, O_Z // GW)]
    (y_gdn,) = _rows_fwd("gdn_post", _f_gdn_post, gpost_rows, [nw_t], [(GW, BF16)], 128)

    full['w_branch_rwkv'] = gathered_full('w_branch_rwkv', y_gdn)
    gstart('mlp_w_down', full['w_branch_rwkv'])
    br_rw = _mm("mm_br_rw", y_rw, full['w_branch_rwkv'], b_shards=True)
    full['w_branch_gdn'] = gathered_full('w_branch_gdn', y_gdn)
    br_gdn = _mm("mm_br_gdn", y_gdn, full['w_branch_gdn'], b_shards=True)
    merge_rows = [(p, D, O_GRW // D), (p, D, O_GGDN // D), (br_rw, D, 0), (br_gdn, D, 0)]
    (merged,) = _rows_fwd("merge", _f_merge, merge_rows, [], [(D, BF16)], 128)
    full['w_mix_out'] = gathered_full('w_mix_out', merged)
    ymix = _mm("mm_mix_out", merged, full['w_mix_out'])
    r1_par = [W['mix_norm_post'], W['xa_norm_pre']]
    h1, cq = _rows_fwd("resid1", _f_resid, [(xs, D, 0), (ymix, D, 0)], r1_par, [(D, F32), (D, BF16)], 128)
    full['xa_w_q'] = gathered_full('xa_w_q', cq)
    q = _mm("mm_xa_q", cq, full['xa_w_q'])
    (mn,) = _rows_fwd("mem_norm", _f_rms, [(mems, D, 0)], [W['xa_norm_mem']], [(D, BF16)], 128)
    full['xa_w_kv'] = gathered_full('xa_w_kv', mn)
    kv = _mm("mm_xa_kv", mn, full['xa_w_kv'])
    k_x, v_x = kv[:, :XW], kv[:, XW:]
    (o_x,) = _rows_fwd("xattn", _f_xattn, [(q, XW, 0)], [k_x, v_x], [(XW, BF16)], 256)
    full['xa_w_o'] = gathered_full('xa_w_o', o_x)
    xo = _mm("mm_xa_o", o_x, full['xa_w_o'], b_shards=True)
    r2_par = [W['xa_norm_post'], W['mlp_norm_pre']]
    h2, fm = _rows_fwd("resid2", _f_resid, [(h1, D, 0), (xo, D, 0)], r2_par, [(D, F32), (D, BF16)], 128)
    full['mlp_w_up'] = gathered_full('mlp_w_up', fm)
    up, act = _mm("mm_mlp_up", fm, full['mlp_w_up'], out_dtype=(F32, BF16), b_shards=True,
                  epilogue=lambda z: (z, jnp.square(jnp.maximum(z, 0.0))))
    DFF = up.shape[1]
    full['mlp_w_down'] = gathered_full('mlp_w_down', act)
    down = _mm("mm_mlp_down", act, full['mlp_w_down'])

    G = {}
    xstarted = {}

    def xstart(n, g4):
        own = lax.dynamic_index_in_dim(g4, me, 0, keepdims=False)
        xstarted[n] = (_send3_start("grad_start_" + n, g4, (3,) + g4.shape[1:], 'exchange'), own)
        _PENDING.append(xstarted[n][0][4])

    (dh2, ddown), (G['mlp_norm_post'],), (loss_acc,) = _rows_bwd(
        "final", _f_final, [(h2, D, 0), (down, D, 0), (tgt, D, 0)], [W['mlp_norm_post']], [[]], 128,
        [(0, 0), (1, 0), None], [D, D], loss=True, out_dtypes=[F32, BF16])
    (d_up,) = _mm("mm_d_act", ddown, full['mlp_w_down'], tb=True, out_dtype=(BF16,), extra=(up,),
                  epilogue=lambda z, upt: (z * (2.0 * jnp.maximum(upt, 0.0)),))
    dw_down = _mm("mm_dw_down", act, ddown, ta=True, out_dtype=BF16)
    xstart('mlp_w_down', dw_down.astype(BF16).reshape(4, DFF // 4, D))
    d_fm = _mm("mm_d_fm", d_up, full['mlp_w_up'], tb=True, b_shards=True)
    xstart('mlp_w_up', _mm("mm_dw_up", fm, d_up, ta=True, out_dtype=BF16, out_shards=True))
    (dh1, dxo), (G['xa_norm_post'], G['mlp_norm_pre']), _ = _rows_bwd(
        "resid2_bwd", _f_resid, [(h1, D, 0), (xo, D, 0)], r2_par, [[(dh2, D, 0)], [(d_fm, D, 0)]], 128,
        [(0, 0), (1, 0)], [D, D], out_dtypes=[F32, BF16])
    d_ox = _mm("mm_d_ox", dxo, full['xa_w_o'], tb=True, b_shards=True)
    xstart('xa_w_o', _mm("mm_dw_o", o_x, dxo, ta=True, out_dtype=BF16, out_shards=True))
    (dq,), (dk_x, dv_x), _ = _rows_bwd("xattn_bwd", _f_xattn, [(q, XW, 0)], [k_x, v_x], [[(d_ox, XW, 0)]], 128,
                                       [(0, 0)], [XW], out_dtypes=[BF16])
    d_cq = _mm("mm_d_cq", dq, full['xa_w_q'], tb=True)
    dw_q = _mm("mm_dw_q", cq, dq, ta=True, out_dtype=BF16)
    xstart('xa_w_q', dw_q.astype(BF16).reshape(4, D // 4, XW))
    dkv = jnp.concatenate([dk_x, dv_x], axis=1)
    d_mn = _mm("mm_d_mn", dkv, full['xa_w_kv'], tb=True)
    dw_kv = _mm("mm_dw_kv", mn, dkv, ta=True, out_dtype=BF16)
    xstart('xa_w_kv', dw_kv.astype(BF16).reshape(4, D // 4, 2 * XW))
    _, (G['xa_norm_mem'],), _ = _rows_bwd("mem_norm_bwd", _f_rms, [(mems, D, 0)], [W['xa_norm_mem']],
                                          [[(d_mn, D, 0)]], 128, [None], [])
    (dx_a, dymix), (G['mix_norm_post'], G['xa_norm_pre']), _ = _rows_bwd(
        "resid1_bwd", _f_resid, [(xs, D, 0), (ymix, D, 0)], r1_par, [[(dh1, D, 0)], [(d_cq, D, 0)]], 128,
        [(0, 0), (1, 0)], [D, D], out_dtypes=[F32, BF16])
    d_merged = _mm("mm_d_merged", dymix, full['w_mix_out'], tb=True)
    dw_out = _mm("mm_dw_out", merged, dymix, ta=True, out_dtype=BF16)
    xstart('w_mix_out', dw_out.astype(BF16).reshape(4, D // 4, D))
    dp = jnp.zeros((T, NP), BF16)
    (dp, d_brw, d_bgdn), _, _ = _rows_bwd(
        "merge_bwd", _f_merge, merge_rows, [], [[(d_merged, D, 0)]], 128, [(0, 0), (0, D), (1, 0), (2, 0)], [2 * D, D, D],
        out_dtypes=[BF16] * 3, alias=(0, dp, O_GRW // (2 * D)))
    d_yrw = _mm("mm_d_yrw", d_brw, full['w_branch_rwkv'], tb=True, b_shards=True)
    xstart('w_branch_rwkv', _mm("mm_dw_brw", y_rw, d_brw, ta=True, out_dtype=BF16, out_shards=True))
    d_ygdn = _mm("mm_d_ygdn", d_bgdn, full['w_branch_gdn'], tb=True, b_shards=True)
    xstart('w_branch_gdn', _mm("mm_dw_bgdn", y_gdn, d_bgdn, ta=True, out_dtype=BF16, out_shards=True))

    (d_y, d_r1, d_k2a, d_v1, d_gate), (d_rk, G['rwkv_gn_w'], G['rwkv_gn_b']), _ = _rows_bwd(
        "rwkv_post_bwd", _f_rwkv_post, post_rows, post_par, [[(d_yrw, RW, 0)]], 64,
        [(0, 0), (1, 0), (2, 0), (3, 0), (4, 0)], [RW] * 5)
    d_r2, d_lwd, d_k2b, d_v2, d_av, d_bv = _rec_bwd("rwkv_rec_bwd", _rwkv_chunk, rec_rw, d_y, s_rw, RW // LANES)
    pre_cts = [[(d_r1, RW, 0), (d_r2, RW, 0)], [(d_lwd, RW, 0)], [(d_k2a, RW, 0), (d_k2b, RW, 0)],
               [(d_v1, RW, 0), (d_v2, RW, 0)], [(d_av, RW, 0)], [(d_bv, RW, 0)], [(d_gate, RW, 0)]]
    pre_out = [(0, 0), (0, RW), (0, 2 * RW), (0, 3 * RW), (0, 3 * RW + LANES), (0, 3 * RW + 2 * LANES)]
    (d_xm,), pre_pg, _ = _rows_bwd("rwkv_pre_bwd", _f_rwkv_pre, pre_rows, rwkv_par, pre_cts, 64, pre_out, [XM])
    G['rwkv_w0'], G['rwkv_a0'], G['rwkv_k_k'], G['rwkv_k_a'], d_wup_p, d_aup_p, d_gup = pre_pg
    dp, d_mix_x = _tokshift_bwd(d_xm, p, mix_x, dp, n_main, main_cb, lora_cb)

    (d_ogd, dp), (d_nw_t,), _ = _rows_bwd("gdn_post_bwd", _f_gdn_post, gpost_rows, [nw_t], [[(d_ygdn, GW, 0)]], 64,
                                          [(0, 0), (1, 0)], [GW, GW], out_dtypes=[F32, BF16], alias=(1, dp, O_Z // GW))
    d_qn, d_kn, d_vg, d_betax, d_gx = _rec_bwd("gdn_rec_bwd", _gdn_chunk, rec_gd, d_ogd, s_gd, GW // LANES)
    gpre_cts = [[(d_qn, GW, 0)], [(d_kn, GW, 0)], [(d_vg, GW, 0)], [(d_betax, GW, 0)], [(d_gx, GW, 0)]]
    (d_qkvc, dp), (d_alog_p, d_dt_p), _ = _rows_bwd(
        "gdn_pre_bwd", f_gdn_pre, gpre_rows, [alog_p, dt_p], gpre_cts, 64,
        [(0, 0), (0, GW), (0, 2 * GW), (1, 0)], [3 * GW, LANES], out_dtypes=[F32, BF16], alias=(1, dp, O_BA // LANES))
    dp, d_conv8 = _conv_bwd(d_qkvc, p, conv8, dp, O_GQ // LANES)
    small_g = [_col_shards(d_wup_p[:LW]), _col_shards(d_aup_p[:LA]), _col_shards(d_gup), _col_shards(d_conv8[:4])]
    xstart('small', jnp.stack([pack_small([g[s] for g in small_g]) for s in range(4)]))
    dw_p = _mm("mm_dw_in", u, dp, ta=True, out_dtype=BF16)
    xstart('w_in', _col_shards(unpad_layout(dw_p.astype(BF16))))
    d_u = _mm("mm_d_u", dp, w_p, tb=True)
    (grad_x,), (G['mix_norm_pre'],), _ = _rows_bwd(
        "rms_pre_bwd", _f_rms, [(xs, D, 0)], [W['mix_norm_pre']], [[(d_u, D, 0)]], 128, [(0, 0)], [D],
        adds=[(0, (dx_a, D, 0))])

    G['rwkv_shift_mix'] = xm_unlayout(d_mix_x)
    G['rwkv_r_k'] = d_rk
    G['gdn_a_log'] = d_alog_p[:, NH:2 * NH]
    G['gdn_dt_bias'] = d_dt_p[:, NH:2 * NH]
    G['gdn_norm_w'] = jnp.sum(d_nw_t.reshape(GW // GDN_HEAD, GDN_HEAD), axis=0, keepdims=True)

    res = {}
    for n in xstarted:
        started, own = xstarted[n]
        _, land = _send3_wait("grad_wait_" + n, started, grad_x)
        part = _sum4("sum4_" + n, own, land)
        other = _swap_sibling("swap_" + n, part)
        if n == 'small':
            for sn, pa, pb in zip(SMALL_SHARDED, unpack_small(part), unpack_small(other)):
                res[sn] = _adam_big("adam_" + sn, W[sn], Mo[sn], Vo[sn], pa, pb)
        else:
            res[n] = _adam_big("adam_" + n, W[n], Mo[n], Vo[n], part, other)

    sizes = [-(-W[n].size // LANES) * LANES for n in REPLICATED]
    total = -(-sum(sizes) // (8 * LANES)) * (8 * LANES)

    def pack(vals):
        parts = [jnp.pad(vv.reshape(-1), (0, sz - vv.size)) for vv, sz in zip(vals, sizes)]
        flat = jnp.concatenate(parts)
        return jnp.pad(flat, (0, total - flat.shape[0])).reshape(8, total // 8)

    g8 = _gather8(pack([G[n] for n in REPLICATED]))
    small = _adam_small(pack([W[n] for n in REPLICATED]), pack([Mo[n] for n in REPLICATED]),
                        pack([Vo[n] for n in REPLICATED]), g8)
    small = [s.reshape(-1) for s in small]
    off = 0
    for n, sz in zip(REPLICATED, sizes):
        res[n] = [s[off:off + W[n].size].reshape(W[n].shape) for s in small]
        off += sz

    loss = lax.psum(loss_acc[0, 0], ("x", "y", "c"))
    outs = [loss, grad_x[None]]
    for j in range(4):
        outs += [res[n][j].reshape(given[n].shape) for n in WEIGHTS]
    return tuple(outs)
```

```python
import functools
import math

import jax
import jax.numpy as jnp
from jax import lax
from jax.experimental import pallas as pl
from jax.experimental.pallas import tpu as pltpu

F32 = jnp.float32
BF16 = jnp.bfloat16
MESH = pl.DeviceIdType.MESH

LANES = 128
VMEM_LIMIT = 56 * 1024 * 1024
CHUNK = 64
TIME_BLOCK = 128
REC_GROUP = 16
REC_MODE = '1'
INV_MODE = '1'

NORM_EPS = 1e-6
L2_EPS = 1e-6
RWKV_GN_EPS = 64e-5
RWKV_HEAD = 64
GDN_HEAD = 128
XA_HEAD = 128
ADAM_LR, ADAM_B1, ADAM_B2, ADAM_EPS, ADAM_WD, ADAM_STEP = 0.001, 0.9, 0.999, 1e-08, 0.01, 10

WEIGHTS = ['mix_norm_pre', 'mix_norm_post', 'w_in', 'rwkv_shift_mix', 'rwkv_w0', 'rwkv_w_up', 'rwkv_a0', 'rwkv_a_up',
           'rwkv_g_up', 'rwkv_k_k', 'rwkv_k_a', 'rwkv_r_k', 'rwkv_gn_w', 'rwkv_gn_b', 'gdn_conv_w', 'gdn_a_log',
           'gdn_dt_bias', 'gdn_norm_w', 'w_branch_rwkv', 'w_branch_gdn', 'w_mix_out', 'xa_norm_pre', 'xa_norm_mem',
           'xa_norm_post', 'xa_w_q', 'xa_w_kv', 'xa_w_o', 'mlp_norm_pre', 'mlp_norm_post', 'mlp_w_up', 'mlp_w_down']
COL_SHARDED = ['w_in', 'rwkv_w_up', 'rwkv_a_up', 'rwkv_g_up', 'gdn_conv_w', 'w_branch_rwkv', 'w_branch_gdn', 'xa_w_o',
               'mlp_w_up']
ROW_SHARDED = ['w_mix_out', 'xa_w_q', 'xa_w_kv', 'mlp_w_down']
SHARDED = COL_SHARDED + ROW_SHARDED
SMALL_SHARDED = ['rwkv_w_up', 'rwkv_a_up', 'rwkv_g_up', 'gdn_conv_w']
BIG_SHARDED = [n for n in SHARDED if n not in SMALL_SHARDED]
REPLICATED = [n for n in WEIGHTS if n not in SHARDED]


ANY = pl.BlockSpec(memory_space=pl.ANY)
_PENDING = []


def _pcall(body, **kw):
    deps = tuple(_PENDING) if "in_specs" in kw else ()
    if not deps:
        return pl.pallas_call(body, **kw)
    del _PENDING[:]
    n_in = len(kw["in_specs"])
    kw["in_specs"] = list(kw["in_specs"]) + [ANY] * len(deps)

    def with_deps(*refs):
        return body(*refs[:n_in], *refs[n_in + len(deps):])

    call = pl.pallas_call(with_deps, **kw)
    return lambda *args: call(*args, *deps)


def _params(sem=None):
    return pltpu.CompilerParams(dimension_semantics=sem, vmem_limit_bytes=VMEM_LIMIT)


def _pick(n, pref, mult=LANES):
    t = (min(n, pref) // mult) * mult
    while t >= mult:
        if n % t == 0:
            return t
        t -= mult
    return n


def _dg(a, b, ca, cb):
    if a.ndim == 3:
        return lax.dot_general(a, b, (((ca + 1,), (cb + 1,)), ((0,), (0,))), preferred_element_type=F32)
    return lax.dot_general(a, b, (((ca,), (cb,)), ((), ())), preferred_element_type=F32)


def _split(x):
    hi = x.astype(BF16)
    lo = (x - hi.astype(F32)).astype(BF16)
    return hi, lo


def _dot_raw(a, b, ca, cb, mode):
    if mode == '1':
        return _dg(a.astype(BF16), b.astype(BF16), ca, cb)
    if mode == 'cb':
        ah, al = _split(a)
        bb = b.astype(BF16)
        return _dg(ah, bb, ca, cb) + _dg(al, bb, ca, cb)
    if mode == 'ca':
        bh, bl = _split(b)
        ab = a.astype(BF16)
        return _dg(ab, bh, ca, cb) + _dg(ab, bl, ca, cb)
    ah, al = _split(a)
    bh, bl = _split(b)
    return _dg(ah, bh, ca, cb) + (_dg(ah, bl, ca, cb) + _dg(al, bh, ca, cb))


@functools.partial(jax.custom_vjp, nondiff_argnums=(2, 3, 4))
def _dot_diff(a, b, ca, cb, mode):
    return _dot_raw(a, b, ca, cb, mode)


def _dot_diff_fwd(a, b, ca, cb, mode):
    return _dot_raw(a, b, ca, cb, mode), (a, b)


def _dot_diff_bwd(ca, cb, mode, res, ct):
    a, b = res
    if mode == 'ca':
        da = jnp.zeros_like(a)
    else:
        bmode = {'1': '1', '3': '3', 'cb': 'cb'}[mode]
        if ca == 1:
            da = _dot_raw(ct, b, 1, 1 if cb == 0 else 0, bmode)
        else:
            da = _dot_raw(b, ct, 1 if cb == 0 else 0, 1, 'ca' if bmode == 'cb' else bmode)
    if mode == 'cb':
        db = jnp.zeros_like(b)
    else:
        amode = {'1': '1', '3': '3', 'ca': 'ca'}[mode]
        if cb == 0:
            db = _dot_raw(a, ct, 0 if ca == 1 else 1, 0, amode)
        else:
            db = _dot_raw(ct, a, 0, 0 if ca == 1 else 1, 'cb' if amode == 'ca' else amode)
    return da, db


_dot_diff.defvjp(_dot_diff_fwd, _dot_diff_bwd)


class _Ops:
    def __init__(self, diff):
        self.diff = diff

    def dot(self, a, b, ca=1, cb=0, mode='1'):
        if self.diff:
            return _dot_diff(a, b, ca, cb, mode)
        return _dot_raw(a, b, ca, cb, mode)


_RAW = _Ops(False)
_DIFF = _Ops(True)


def _sigmoid(x):
    return 1.0 / (1.0 + jnp.exp(-x))


def _softplus(x):
    return jnp.maximum(x, 0.0) + jnp.log(1.0 + jnp.exp(-jnp.abs(x)))


def _rms(x, g, eps=NORM_EPS):
    return x * lax.rsqrt(jnp.mean(x * x, axis=-1, keepdims=True) + eps) * g


def _head_matrix(width, head):
    ch = lax.broadcasted_iota(jnp.int32, (width, LANES), 0)
    hh = lax.broadcasted_iota(jnp.int32, (width, LANES), 1)
    return (lax.shift_right_logical(ch, int(math.log2(head))) == hh).astype(BF16)


def _headsum(ops, x, head):
    e = _head_matrix(x.shape[-1], head)
    s = ops.dot(x, e, 1, 0, 'cb')
    return ops.dot(s, e, 1, 1, 'cb')


def _f_rms(ops, x, g):
    return (_rms(x, g),)


def _f_resid(ops, h, y, g_post, g_next):
    h1 = h + _rms(y, g_post)
    return h1, _rms(h1, g_next)


def _f_final(ops, h, y, tgt, g_post):
    e = h + _rms(y, g_post) - tgt
    return (0.5 * jnp.mean(e * e, axis=-1, keepdims=True),)


def _f_merge(ops, g_rw, g_gdn, b_rw, b_gdn):
    return (_sigmoid(g_rw) * b_rw + _sigmoid(g_gdn) * b_gdn,)


def _f_rwkv_pre(ops, xr, xk, xv, xlw, xla, xlg, w0, a0, k_k, k_a, w_up, a_up, g_up):
    z = w0 + ops.dot(jnp.tanh(xlw), w_up)
    lwd = -jnp.exp(-_softplus(-z) - 0.5)
    a = _sigmoid(a0 + ops.dot(xla, a_up))
    gate = ops.dot(_sigmoid(xlg), g_up)
    kk = xk * k_k
    kk = kk * lax.rsqrt(_headsum(ops, kk * kk, RWKV_HEAD) + L2_EPS)
    k2 = xk * (1.0 + (a - 1.0) * k_a)
    return xr, lwd, k2, xv, -kk, kk * a, gate


def _f_rwkv_post(ops, y, r, k2, v, gate, r_k, gn_w, gn_b):
    inv = 1.0 / RWKV_HEAD
    yc = y - _headsum(ops, y, RWKV_HEAD) * inv
    var = _headsum(ops, yc * yc, RWKV_HEAD) * inv
    yn = yc * lax.rsqrt(var + RWKV_GN_EPS) * gn_w + gn_b
    bonus = _headsum(ops, r * k2 * r_k, RWKV_HEAD) * v
    return ((yn + bonus) * gate,)


def _f_gdn_pre(ops, qc, kc, vc, ba, alog_p, dt_p, *, heads):
    lane = lax.broadcasted_iota(jnp.int32, (1, LANES), 1)
    beta = _sigmoid(ba)
    g = -jnp.exp(alog_p) * _softplus(ba + dt_p)
    gb = jnp.where(lane < heads, beta, g)
    width = qc.shape[-1]
    row = lax.broadcasted_iota(jnp.int32, (LANES, width), 0)
    col_head = lax.shift_right_logical(lax.broadcasted_iota(jnp.int32, (LANES, width), 1), int(math.log2(GDN_HEAD)))
    beta_x = ops.dot(gb, (row == col_head).astype(BF16), 1, 0, 'cb')
    g_x = ops.dot(gb, (row == col_head + heads).astype(BF16), 1, 0, 'cb')
    qn = qc * lax.rsqrt(_headsum(ops, qc * qc, GDN_HEAD) + L2_EPS)
    kn = kc * lax.rsqrt(_headsum(ops, kc * kc, GDN_HEAD) + L2_EPS)
    return qn, kn, vc, beta_x, g_x


def _f_gdn_post(ops, o, z, nw):
    ms = _headsum(ops, o * o, GDN_HEAD) * (1.0 / GDN_HEAD)
    return (o * lax.rsqrt(ms + NORM_EPS) * nw * (z * _sigmoid(z)),)


def _f_xattn(ops, q, k, v):
    width = q.shape[-1]
    lane = lax.broadcasted_iota(jnp.int32, (1, width), 1)
    out = jnp.zeros_like(q)
    for h in range(width // XA_HEAD):
        m = ((lane >= h * XA_HEAD) & (lane < (h + 1) * XA_HEAD)).astype(F32)
        s = ops.dot(q * m, k, 1, 1) * (XA_HEAD ** -0.5)
        s = s - jnp.max(s, axis=-1, keepdims=True)
        e = jnp.exp(s)
        pr = e / jnp.sum(e, axis=-1, keepdims=True)
        out = out + ops.dot(pr, v) * m
    return (out,)


def _inv_unit_lower(ops, a):
    n = a.shape[-1]
    ii = lax.broadcasted_iota(jnp.int32, (n, n), 0)
    jj = lax.broadcasted_iota(jnp.int32, (n, n), 1)
    p = (ii == jj).astype(F32) + a
    q = a
    k = 2
    while k < n:
        q = ops.dot(q, q, 1, 0, INV_MODE)
        p = p + ops.dot(p, q, 1, 0, INV_MODE)
        k *= 2
    return p


def _const(mask, batch):
    return jnp.broadcast_to(mask.astype(BF16), (batch,) + mask.shape)


def _rwkv_chunk(ops, s, r, lw, k, v, a, b):
    g, c, _ = r.shape
    ii = lax.broadcasted_iota(jnp.int32, (c, c), 0)
    jj = lax.broadcasted_iota(jnp.int32, (c, c), 1)
    incl = ii >= jj
    strict = ii > jj
    cum = ops.dot(_const(incl, g), lw, 1, 0, 'ca')
    tot = ops.dot(jnp.ones((g, LANES, c), BF16), lw, 1, 0, 'ca')
    w_inv = jnp.exp(-cum)
    at = a * jnp.exp(cum - lw)
    bt = b * w_inv
    kt = k * w_inv
    rt = r * jnp.exp(cum)
    lane = lax.broadcasted_iota(jnp.int32, (2, 1, 1, LANES), 3)
    head = lax.broadcasted_iota(jnp.int32, (2, 1, 1, LANES), 0)
    hm = ((lane >= RWKV_HEAD) == (head == 1)).astype(F32)

    def per_head(x):
        return (x[None] * hm).reshape(2 * g, c, x.shape[-1])

    def both(x):
        return jnp.broadcast_to(x[None], (2,) + x.shape).reshape((2 * g,) + x.shape[1:])

    def merge(x2):
        return jnp.sum(x2.reshape(2, g, c, LANES) * hm, axis=0)

    bt2, kt2, v2 = both(bt), both(kt), both(v)
    atm, rtm = per_head(at), per_head(rt)
    base = ops.dot(at, s, 1, 1, REC_MODE)
    a_ab = jnp.where(strict, ops.dot(atm, bt2, 1, 1, REC_MODE), 0.0)
    a_ak = jnp.where(strict, ops.dot(atm, kt2, 1, 1, REC_MODE), 0.0)
    a_rb = jnp.where(incl, ops.dot(rtm, bt2, 1, 1, REC_MODE), 0.0)
    a_rk = jnp.where(incl, ops.dot(rtm, kt2, 1, 1, REC_MODE), 0.0)
    t = _inv_unit_lower(ops, a_ab)
    u = merge(ops.dot(t, both(base) + ops.dot(a_ak, v2, 1, 0, REC_MODE), 1, 0, REC_MODE))
    y = ops.dot(rt, s, 1, 1, REC_MODE) + merge(ops.dot(a_rb, both(u), 1, 0, REC_MODE) + ops.dot(a_rk, v2, 1, 0, REC_MODE))
    bi = lax.broadcasted_iota(jnp.int32, (LANES, LANES), 0) >= RWKV_HEAD
    bj = lax.broadcasted_iota(jnp.int32, (LANES, LANES), 1) >= RWKV_HEAD
    upd = jnp.where(bi == bj, ops.dot(u, bt, 0, 0, REC_MODE) + ops.dot(v, kt, 0, 0, REC_MODE), 0.0)
    return (s + upd) * jnp.exp(tot), y


def _gdn_chunk(ops, s, q, k, v, beta, g):
    nb, c, _ = q.shape
    ii = lax.broadcasted_iota(jnp.int32, (c, c), 0)
    jj = lax.broadcasted_iota(jnp.int32, (c, c), 1)
    causal = ii >= jj
    strict = ii > jj
    q = q * (GDN_HEAD ** -0.5)
    gc = ops.dot(_const(causal, nb), g, 1, 0, 'ca')
    ones = jnp.ones((nb, c, LANES), BF16)
    gc_col = ops.dot(gc, ones, 1, 1, 'cb') * (1.0 / LANES)
    gc_row = ops.dot(ones, gc, 1, 1, 'ca') * (1.0 / LANES)
    decay = jnp.where(causal, jnp.exp(jnp.where(causal, gc_col - gc_row, 0.0)), 0.0)
    kb = k * beta
    lmat = jnp.where(strict, ops.dot(kb, k, 1, 1, REC_MODE) * decay, 0.0)
    tinv = _inv_unit_lower(ops, -lmat)
    egc = jnp.exp(gc)
    u = ops.dot(tinv, v * beta, 1, 0, REC_MODE)
    w = ops.dot(tinv, kb * egc, 1, 0, REC_MODE)
    a_intra = jnp.where(causal, ops.dot(q, k, 1, 1, REC_MODE) * decay, 0.0)
    g_tot = ops.dot(jnp.ones((nb, c, c), BF16), g, 1, 0, 'ca')
    k_dec = k * jnp.exp(g_tot - gc)
    v_new = u - ops.dot(w, s, 1, 0, REC_MODE)
    o = ops.dot(q * egc, s, 1, 0, REC_MODE) + ops.dot(a_intra, v_new, 1, 0, REC_MODE)
    g_full = ops.dot(jnp.ones((nb, LANES, c), BF16), g, 1, 0, 'ca')
    s_new = s * jnp.exp(g_full) + ops.dot(k_dec, v_new, 0, 0, REC_MODE)
    return s_new, o


def _mm(name, a, b, ta=False, tb=False, out_dtype=F32, tm=1024, tn=1024, tk=2048, epilogue=None, extra=(),
        b_shards=False, out_shards=False):
    m = a.shape[1] if ta else a.shape[0]
    kd = a.shape[0] if ta else a.shape[1]
    ns = b.shape[2] if b_shards else None
    b_rows, b_cols = (b.shape[1], 4 * ns) if b_shards else b.shape
    n = b_rows if tb else b_cols
    assert kd == (b_cols if tb else b_rows), (name, a.shape, b.shape)
    n_unit = ns if (b_shards and not tb) else (n // 4 if out_shards else n)
    k_unit = ns if (b_shards and tb) else kd
    tm, tn, tk = _pick(m, tm), _pick(n_unit, tn), _pick(k_unit, tk)
    nk = kd // tk
    n_x = len(extra)
    out_dtypes = out_dtype if epilogue is not None else (out_dtype,)

    def body(a_ref, b_ref, *rest):
        x_refs, o_refs, acc = rest[:n_x], rest[n_x:-1], rest[-1]
        kk = pl.program_id(2)
        part = _dg(a_ref[...].astype(BF16), b_ref[...].astype(BF16), 0 if ta else 1, 1 if tb else 0)

        def finish(z):
            vals = (z,) if epilogue is None else epilogue(z, *[xr[...] for xr in x_refs])
            for o_ref, val in zip(o_refs, vals):
                o_ref[...] = val.astype(o_ref.dtype)

        if nk == 1:
            finish(part)
            return

        @pl.when(kk == 0)
        def _():
            acc[...] = part

        @pl.when((kk > 0) & (kk < nk - 1))
        def _():
            acc[...] += part

        @pl.when(kk == nk - 1)
        def _():
            finish(acc[...] + part)

    a_spec = pl.BlockSpec((tk, tm), lambda i, j, k: (k, i)) if ta else pl.BlockSpec((tm, tk), lambda i, j, k: (i, k))
    if not b_shards:
        b_spec = pl.BlockSpec((tn, tk), lambda i, j, k: (j, k)) if tb else pl.BlockSpec((tk, tn), lambda i, j, k: (k, j))
    elif tb:
        per_k = ns // tk
        b_spec = pl.BlockSpec((None, tn, tk), lambda i, j, k: (k // per_k, j, k % per_k))
    else:
        per_n = ns // tn
        b_spec = pl.BlockSpec((None, tk, tn), lambda i, j, k: (j // per_n, k, j % per_n))
    o_spec = pl.BlockSpec((tm, tn), lambda i, j, k: (i, j))
    if out_shards:
        per_o = (n // 4) // tn
        res_spec = [pl.BlockSpec((None, tm, tn), lambda i, j, k: (j // per_o, i, j % per_o))]
        res_shape = [jax.ShapeDtypeStruct((4, m, n // 4), out_dtypes[0])]
    else:
        res_spec = [o_spec] * len(out_dtypes)
        res_shape = [jax.ShapeDtypeStruct((m, n), dt) for dt in out_dtypes]
    res = _pcall(
        body, name=name, grid=(m // tm, n // tn, nk), in_specs=[a_spec, b_spec] + [o_spec] * n_x,
        out_specs=res_spec, out_shape=res_shape,
        scratch_shapes=[pltpu.VMEM((tm, tn), F32)],
        compiler_params=_params(("parallel", "parallel", "arbitrary")))(a, b, *extra)
    return res if epilogue is not None else res[0]


def _row_spec(tm, width, cb):
    return pl.BlockSpec((tm, width), lambda i: (i, cb))


def _rows_fwd(name, f, rows, params, outs, tm):
    t = rows[0][0].shape[0]
    tm = min(tm, t)
    n_in = len(rows) + len(params)
    kept = [o for o in outs if o is not None]

    def body(*refs):
        vals = f(_RAW, *[r[...].astype(F32) for r in refs[:n_in]])
        o_refs = refs[n_in:]
        j = 0
        for val, o in zip(vals, outs):
            if o is not None:
                o_refs[j][...] = val.astype(o_refs[j].dtype)
                j += 1

    in_specs = [_row_spec(tm, w, cb) for (_, w, cb) in rows]
    in_specs += [pl.BlockSpec(p.shape, lambda i: (0, 0)) for p in params]
    res = _pcall(
        body, name=name, grid=(t // tm,), in_specs=in_specs,
        out_specs=[_row_spec(tm, w, 0) for (w, _) in kept],
        out_shape=[jax.ShapeDtypeStruct((t, w), dt) for (w, dt) in kept],
        compiler_params=_params(("parallel",)))(*[r[0] for r in rows], *params)
    return res


def _rows_bwd(name, f, rows, params, cts, tm, row_out, out_widths, adds=(), loss=False, out_dtypes=None, alias=None):
    t = rows[0][0].shape[0]
    tm = min(tm, t)
    n_r, n_p = len(rows), len(params)
    ct_flat = [c for cl in cts for c in cl]
    n_ct, n_add = len(ct_flat), len(adds)
    n_out = len(out_widths)

    def body(*refs):
        i = pl.program_id(0)
        ins = [r[...].astype(F32) for r in refs[:n_r + n_p]]
        ct_refs = refs[n_r + n_p:n_r + n_p + n_ct]
        add_refs = refs[n_r + n_p + n_ct:n_r + n_p + n_ct + n_add]
        o_refs = refs[n_r + n_p + n_ct + n_add + (alias is not None):]
        outs, vjp = jax.vjp(lambda *a: f(_DIFF, *a), *ins)
        ctv, pos = [], 0
        for o, cl in zip(outs, cts):
            if loss:
                ctv.append(jnp.ones_like(o))
            elif not cl:
                ctv.append(jnp.zeros_like(o))
            else:
                acc = ct_refs[pos][...].astype(F32)
                for q in range(1, len(cl)):
                    acc = acc + ct_refs[pos + q][...].astype(F32)
                ctv.append(acc)
            pos += len(cl)
        grads = vjp(tuple(ctv))
        for ri, ro in enumerate(row_out):
            if ro is not None:
                oi, off = ro
                val = grads[ri]
                for ai, (aoi, _) in enumerate(adds):
                    if aoi == oi:
                        val = val + add_refs[ai][...].astype(F32)
                o_refs[oi][:, off:off + val.shape[1]] = val.astype(o_refs[oi].dtype)
        p_refs = o_refs[n_out:n_out + n_p]

        @pl.when(i == 0)
        def _():
            for pr in p_refs:
                pr[...] = jnp.zeros_like(pr)
            if loss:
                o_refs[n_out + n_p][...] = jnp.zeros((8, LANES), F32)

        for pi, pr in enumerate(p_refs):
            pr[...] += grads[n_r + pi]
        if loss:
            o_refs[n_out + n_p][...] += jnp.sum(outs[0])

    in_specs = [_row_spec(tm, w, cb) for (_, w, cb) in rows]
    in_specs += [pl.BlockSpec(p.shape, lambda i: (0, 0)) for p in params]
    in_specs += [_row_spec(tm, w, cb) for (_, w, cb) in ct_flat]
    in_specs += [_row_spec(tm, w, cb) for (_, (_, w, cb)) in adds]
    out_specs = [_row_spec(tm, w, 0) for w in out_widths]
    out_specs += [pl.BlockSpec(p.shape, lambda i: (0, 0)) for p in params]
    out_dtypes = out_dtypes or [F32] * n_out
    out_shape = [jax.ShapeDtypeStruct((t, w), dt) for w, dt in zip(out_widths, out_dtypes)]
    out_shape += [jax.ShapeDtypeStruct(p.shape, F32) for p in params]
    if loss:
        out_specs.append(pl.BlockSpec((8, LANES), lambda i: (0, 0)))
        out_shape.append(jax.ShapeDtypeStruct((8, LANES), F32))
    extra, aliases = [], {}
    if alias is not None:
        oi, buf, cb = alias
        in_specs.append(ANY)
        out_specs[oi] = _row_spec(tm, out_widths[oi], cb)
        out_shape[oi] = jax.ShapeDtypeStruct(buf.shape, buf.dtype)
        extra, aliases = [buf], {n_r + n_p + n_ct + n_add: oi}
    res = _pcall(
        body, name=name, grid=(t // tm,), in_specs=in_specs, out_specs=out_specs, out_shape=out_shape,
        input_output_aliases=aliases, compiler_params=_params(("arbitrary",)))(
            *[r[0] for r in rows], *params, *[c[0] for c in ct_flat], *[a[1][0] for a in adds], *extra)
    return res[:n_out], res[n_out:n_out + n_p], res[n_out + n_p:]


def _shift_down(x, k):
    row = lax.broadcasted_iota(jnp.int32, x.shape, 0)
    return jnp.where(row >= k, pltpu.roll(x, k, 0), 0.0)


def _shift_up(x, k):
    t = x.shape[0]
    row = lax.broadcasted_iota(jnp.int32, x.shape, 0)
    return jnp.where(row < t - k, pltpu.roll(x, t - k, 0), 0.0)


def _tokshift_fwd(p, mix, n_main, main_cb, lora_cb):
    t = p.shape[0]
    nblk = mix.shape[1] // LANES

    def src(j):
        return (0, jnp.where(j < n_main, j + main_cb, j - n_main + lora_cb))

    def body(p_ref, mix_ref, o_ref):
        xv = p_ref[...]
        o_ref[...] = xv + (_shift_down(xv, 1) - xv) * mix_ref[...]

    return _pcall(
        body, name="tokshift_fwd", grid=(nblk,),
        in_specs=[pl.BlockSpec((t, LANES), src), pl.BlockSpec((1, LANES), lambda j: (0, j))],
        out_specs=pl.BlockSpec((t, LANES), lambda j: (0, j)),
        out_shape=jax.ShapeDtypeStruct((t, mix.shape[1]), F32), compiler_params=_params(("parallel",)))(p, mix)


def _tokshift_bwd(dxm, p, mix, dp_buf, n_main, main_cb, lora_cb):
    t = p.shape[0]
    nblk = mix.shape[1] // LANES

    def src(j):
        return (0, jnp.where(j < n_main, j + main_cb, j - n_main + lora_cb))

    def body(d_ref, p_ref, mix_ref, buf_ref, dp_ref, dmix_ref):
        xv, d, mx = p_ref[...], d_ref[...], mix_ref[...]
        dp_ref[...] = (d * (1.0 - mx) + _shift_up(d * mx, 1)).astype(dp_ref.dtype)
        dmix_ref[...] = jnp.sum(d * (_shift_down(xv, 1) - xv), axis=0, keepdims=True)

    return _pcall(
        body, name="tokshift_bwd", grid=(nblk,),
        in_specs=[pl.BlockSpec((t, LANES), lambda j: (0, j)), pl.BlockSpec((t, LANES), src),
                  pl.BlockSpec((1, LANES), lambda j: (0, j)), ANY],
        out_specs=[pl.BlockSpec((t, LANES), src), pl.BlockSpec((1, LANES), lambda j: (0, j))],
        out_shape=[jax.ShapeDtypeStruct(dp_buf.shape, dp_buf.dtype), jax.ShapeDtypeStruct((1, mix.shape[1]), F32)],
        input_output_aliases={3: 0},
        compiler_params=_params(("parallel",)))(dxm, p, mix, dp_buf)


def _conv_pre(xv, w):
    return (w[3:4, :] * xv + w[2:3, :] * _shift_down(xv, 1) + w[1:2, :] * _shift_down(xv, 2)
            + w[0:1, :] * _shift_down(xv, 3))


def _conv_fwd(p, w8, cb0):
    t = p.shape[0]
    width = w8.shape[1]

    def body(p_ref, w_ref, o_ref):
        c = _conv_pre(p_ref[...], w_ref[...])
        o_ref[...] = c * _sigmoid(c)

    return _pcall(
        body, name="conv_fwd", grid=(width // LANES,),
        in_specs=[pl.BlockSpec((t, LANES), lambda j: (0, j + cb0)), pl.BlockSpec((8, LANES), lambda j: (0, j))],
        out_specs=pl.BlockSpec((t, LANES), lambda j: (0, j)),
        out_shape=jax.ShapeDtypeStruct((t, width), F32), compiler_params=_params(("parallel",)))(p, w8)


def _conv_bwd(dy, p, w8, dp_buf, cb0):
    t = p.shape[0]
    width = w8.shape[1]

    def body(d_ref, p_ref, w_ref, buf_ref, dx_ref, dw_ref):
        xv, w = p_ref[...], w_ref[...]
        c = _conv_pre(xv, w)
        sg = _sigmoid(c)
        dc = d_ref[...] * (sg * (1.0 + c * (1.0 - sg)))
        dx_ref[...] = (w[3:4, :] * dc + w[2:3, :] * _shift_up(dc, 1) + w[1:2, :] * _shift_up(dc, 2)
                       + w[0:1, :] * _shift_up(dc, 3)).astype(dx_ref.dtype)
        row =lax.broadcasted_iota(jnp.int32, (8, LANES), 0)
        dw = jnp.zeros((8, LANES), F32)
        for j in range(4):
            sj = jnp.sum(dc * (xv if j == 3 else _shift_down(xv, 3 - j)), axis=0, keepdims=True)
            dw = dw + jnp.where(row == j, sj, 0.0)
        dw_ref[...] = dw

    return _pcall(
        body, name="conv_bwd", grid=(width // LANES,),
        in_specs=[pl.BlockSpec((t, LANES), lambda j: (0, j)), pl.BlockSpec((t, LANES), lambda j: (0, j + cb0)),
                  pl.BlockSpec((8, LANES), lambda j: (0, j)), ANY],
        out_specs=[pl.BlockSpec((t, LANES), lambda j: (0, j + cb0)), pl.BlockSpec((8, LANES), lambda j: (0, j))],
        out_shape=[jax.ShapeDtypeStruct(dp_buf.shape, dp_buf.dtype), jax.ShapeDtypeStruct((8, width), F32)],
        input_output_aliases={3: 0},
        compiler_params=_params(("parallel",)))(dy, p, w8, dp_buf)


def _rec_fwd(name, chunk_fn, ins, nblocks):
    t = ins[0][0].shape[0]
    tb = min(TIME_BLOCK, t)
    nt, ncb = t // tb, tb // CHUNK
    n_in = len(ins)
    grp = REC_GROUP if nblocks % REC_GROUP == 0 and all(c0 % REC_GROUP == 0 for (_, c0) in ins) else 1
    width = grp * LANES

    def body(*refs):
        in_refs, y_ref, sall_ref, s_scr = refs[:n_in], refs[n_in], refs[n_in + 1], refs[n_in + 2]

        @pl.when(pl.program_id(1) == 0)
        def _():
            s_scr[...] = jnp.zeros_like(s_scr)

        def step(ci, s):
            sl = pl.ds(pl.multiple_of(ci * CHUNK, CHUNK), CHUNK)
            lanes = [slice(gi * LANES, (gi + 1) * LANES) for gi in range(grp)]
            s1, y = chunk_fn(_RAW, s, *[jnp.stack([r[sl, ln] for ln in lanes]) for r in in_refs])
            for gi, ln in enumerate(lanes):
                sall_ref[gi, ci] = s[gi]
                y_ref[sl, ln] = y[gi]
            return s1

        s_scr[...] = lax.fori_loop(0, ncb, step, s_scr[...])

    in_specs = [pl.BlockSpec((tb, width), functools.partial(lambda h, tt, cb: (tt, cb + h), cb=c0 // grp))
                for (_, c0) in ins]
    return _pcall(
        body, name=name, grid=(nblocks // grp, nt), in_specs=in_specs,
        out_specs=[pl.BlockSpec((tb, width), lambda h, tt: (tt, h)),
                   pl.BlockSpec((grp, ncb, LANES, LANES), lambda h, tt: (h, tt, 0, 0))],
        out_shape=[jax.ShapeDtypeStruct((t, nblocks * LANES), F32),
                   jax.ShapeDtypeStruct((nblocks, t // CHUNK, LANES, LANES), F32)],
        scratch_shapes=[pltpu.VMEM((grp, LANES, LANES), F32)],
        compiler_params=_params(("parallel", "arbitrary")))(*[a for (a, _) in ins])


def _rec_bwd(name, chunk_fn, ins, dy, sall, nblocks):
    t = ins[0][0].shape[0]
    tb = min(TIME_BLOCK, t)
    nt, ncb = t // tb, tb // CHUNK
    n_in = len(ins)
    grp = REC_GROUP if nblocks % REC_GROUP == 0 and all(c0 % REC_GROUP == 0 for (_, c0) in ins) else 1
    width = grp * LANES

    def body(*refs):
        in_refs, dy_ref, sall_ref = refs[:n_in], refs[n_in], refs[n_in + 1]
        d_refs, ds_scr = refs[n_in + 2:2 * n_in + 2], refs[2 * n_in + 2]

        @pl.when(pl.program_id(1) == 0)
        def _():
            ds_scr[...] = jnp.zeros_like(ds_scr)

        def step(j, ds):
            ci = ncb - 1 - j
            sl = pl.ds(pl.multiple_of(ci * CHUNK, CHUNK), CHUNK)
            lanes = [slice(gi * LANES, (gi + 1) * LANES) for gi in range(grp)]
            s0 = jnp.stack([sall_ref[gi, ci] for gi in range(grp)])
            vals = [jnp.stack([r[sl, ln] for ln in lanes]) for r in in_refs]
            _, vjp = jax.vjp(lambda *a: chunk_fn(_DIFF, *a), s0, *vals)
            grads = vjp((ds, jnp.stack([dy_ref[sl, ln] for ln in lanes])))
            for dr, gval in zip(d_refs, grads[1:]):
                for gi, ln in enumerate(lanes):
                    dr[sl, ln] = gval[gi]
            return grads[0]

        ds_scr[...] = lax.fori_loop(0, ncb, step, ds_scr[...])

    in_specs = [pl.BlockSpec((tb, width), functools.partial(lambda h, tt, cb: (nt - 1 - tt, cb + h), cb=c0 // grp))
                for (_, c0) in ins]
    in_specs.append(pl.BlockSpec((tb, width), lambda h, tt: (nt - 1 - tt, h)))
    in_specs.append(pl.BlockSpec((grp, ncb, LANES, LANES), lambda h, tt: (h, nt - 1 - tt, 0, 0)))
    return _pcall(
        body, name=name, grid=(nblocks // grp, nt), in_specs=in_specs,
        out_specs=[pl.BlockSpec((tb, width), lambda h, tt: (nt - 1 - tt, h)) for _ in ins],
        out_shape=[jax.ShapeDtypeStruct((t, nblocks * LANES), F32) for _ in ins],
        scratch_shapes=[pltpu.VMEM((grp, LANES, LANES), F32)],
        compiler_params=_params(("parallel", "arbitrary")))(*[a for (a, _) in ins], dy, sall)


def _pick_rows(r, c):
    tr = max(16, ((1 << 18) // c) // 16 * 16)
    while tr > 16 and r % tr:
        tr -= 16
    return tr if r % tr == 0 else r


def _sum4(name, own, r3):
    _, r, c = r3.shape
    tr = _pick_rows(r, c)

    def body(own_ref, r_ref, o_ref):
        total = ((own_ref[...].astype(F32) + r_ref[0].astype(F32)) + r_ref[1].astype(F32)) + r_ref[2].astype(F32)
        o_ref[...] = total.astype(o_ref.dtype)

    return _pcall(
        body, name=name, grid=(r // tr,),
        in_specs=[pl.BlockSpec((tr, c), lambda i: (i, 0)), pl.BlockSpec((3, tr, c), lambda i: (0, i, 0))],
        out_specs=pl.BlockSpec((tr, c), lambda i: (i, 0)), out_shape=jax.ShapeDtypeStruct((r, c), BF16),
        compiler_params=_params(("parallel",)))(own, r3)


def _adam_math(w, g, m, v):
    m = ADAM_B1 * m + (1.0 - ADAM_B1) * g
    v = ADAM_B2 * v + (1.0 - ADAM_B2) * (g * g)
    m_hat = m / (1.0 - ADAM_B1 ** ADAM_STEP)
    v_hat = v / (1.0 - ADAM_B2 ** ADAM_STEP)
    return -ADAM_LR * (m_hat / (jnp.sqrt(v_hat) + ADAM_EPS) + ADAM_WD * w), m, v


def _adam_big(name, w, m, v, ga, gb):
    r, c = w.shape
    tr = _pick_rows(r, c)

    def body(w_ref, m_ref, v_ref, ga_ref, gb_ref, g_ref, d_ref, mo_ref, vo_ref):
        g = ga_ref[...].astype(F32) + gb_ref[...].astype(F32)
        g_ref[...] = g
        d_ref[...], mo_ref[...], vo_ref[...] = _adam_math(w_ref[...], g, m_ref[...], v_ref[...])

    spec = pl.BlockSpec((tr, c), lambda i: (i, 0))
    return _pcall(
        body, name=name, grid=(r // tr,), in_specs=[spec] * 5, out_specs=[spec] * 4,
        out_shape=[jax.ShapeDtypeStruct((r, c), F32)] * 4, compiler_params=_params(("parallel",)))(w, m, v, ga, gb)


def _adam_small(w, m, v, g8):
    shape = w.shape

    def body(w_ref, m_ref, v_ref, g8_ref, g_ref, d_ref, mo_ref, vo_ref):
        g = g8_ref[0]
        for dev in range(1, 8):
            g = g + g8_ref[dev]
        g_ref[...] = g
        d_ref[...], mo_ref[...], vo_ref[...] = _adam_math(w_ref[...], g, m_ref[...], v_ref[...])

    return _pcall(
        body, name="adam_small", out_shape=[jax.ShapeDtypeStruct(shape, F32)] * 4,
        compiler_params=_params())(w, m, v, g8)


def _place():
    x, y, c = lax.axis_index("x"), lax.axis_index("y"), lax.axis_index("c")
    return x, y, c, [(1 - x, y), (x, 1 - y), (1 - x, 1 - y)]


def _remote(src, dst, send_sems, recv_sems, k, dev):
    return pltpu.make_async_remote_copy(src_ref=src, dst_ref=dst, send_sem=send_sems.at[k], recv_sem=recv_sems.at[k],
                                        device_id=dev, device_id_type=MESH)


HBM = pl.BlockSpec(memory_space=pltpu.HBM)
SEM = pl.BlockSpec(memory_space=pltpu.SEMAPHORE)


def _send3_start(name, src, land, mode):
    land_arr = land if mode == 'forward_half' else lax.empty(land, src.dtype)
    half = land_arr.shape[1] // 2

    def body(src_ref, land_ref, send_sem, recv_sem, src_thru, land_thru, token):
        x, y, c, chips = _place()
        rows = pl.ds(pl.multiple_of(c * half, half), half)
        for k, (px, py) in enumerate(chips):
            dev = (px, py, c)
            if mode == 'gather':
                s_k, d_k = src_ref, land_ref.at[2 * x + y]
            elif mode == 'gather_half':
                s_k, d_k = src_ref.at[rows], land_ref.at[2 * x + y, rows]
            elif mode == 'forward_half':
                s_k = d_k = land_ref.at[2 * px + py, rows]
                dev = (x, y, 1 - c)
            else:
                s_k, d_k = src_ref.at[2 * px + py], land_ref.at[k]
            pltpu.make_async_remote_copy(src_ref=s_k, dst_ref=d_k, send_sem=send_sem, recv_sem=recv_sem,
                                         device_id=dev, device_id_type=MESH).start()
        token[...] = jnp.zeros_like(token)

    return pl.pallas_call(
        body, name=name,
        out_shape=(pltpu.SemaphoreType.DMA(()), pltpu.SemaphoreType.DMA(()), pltpu.HBM(src.shape, src.dtype),
                   pltpu.HBM(land_arr.shape, src.dtype), jax.ShapeDtypeStruct((8, LANES), F32)),
        in_specs=(HBM, HBM), out_specs=(SEM, SEM, HBM, HBM, pl.BlockSpec(memory_space=pltpu.VMEM)),
        input_output_aliases={0: 2, 1: 3},
        compiler_params=pltpu.CompilerParams(has_side_effects=pltpu.SideEffectType.DATAFLOW_SIDE_EFFECTING),
    )(pltpu.with_memory_space_constraint(src, pltpu.HBM), pltpu.with_memory_space_constraint(land_arr, pltpu.HBM))


def _send3_wait(name, started, after, halves=False):
    send_sem, recv_sem, src_thru, land_thru, _ = started
    half = land_thru.shape[1] // 2

    def body(src_ref, land_ref, send_sem, recv_sem, after_ref, src_dead, got_ref):
        x, y, c, chips = _place()
        three = land_ref.at[pl.ds(0, 3), pl.ds(0, half)] if halves else land_ref.at[pl.ds(0, 3)]
        cp = pltpu.make_async_remote_copy(src_ref=three, dst_ref=three, send_sem=send_sem, recv_sem=recv_sem,
                                          device_id=(chips[0][0], chips[0][1], c), device_id_type=MESH)
        cp.wait_send()
        cp.wait_recv()

    return pl.pallas_call(
        body, name=name,
        out_shape=(pltpu.HBM(src_thru.shape, src_thru.dtype), pltpu.HBM(land_thru.shape, land_thru.dtype)),
        in_specs=(HBM, HBM, SEM, SEM, ANY), out_specs=(HBM, HBM), input_output_aliases={0: 0, 1: 1},
        compiler_params=pltpu.CompilerParams(has_side_effects=pltpu.SideEffectType.DATAFLOW_SIDE_EFFECTING),
    )(src_thru, land_thru, send_sem, recv_sem, after)


def _swap_sibling(name, part):
    def body(src, out, send_sems, recv_sems):
        x, y, c, _ = _place()
        cp = _remote(src, out, send_sems, recv_sems, 0, (x, y, 1 - c))
        cp.start()
        cp.wait_recv()
        cp.wait_send()

    return _pcall(
        body, name=name, in_specs=[ANY], out_specs=ANY, out_shape=jax.ShapeDtypeStruct(part.shape, part.dtype),
        scratch_shapes=[pltpu.SemaphoreType.DMA((1,)), pltpu.SemaphoreType.DMA((1,))])(part)


def _gather8(flat):
    def body(src, out, send_sems, recv_sems, local_sem):
        x, y, c, _ = _place()
        own = pltpu.make_async_copy(src, out.at[4 * x + 2 * y + c], local_sem)
        own.start()
        peers = []
        for mask in range(1, 8):
            px = x ^ ((mask >> 2) & 1)
            py = y ^ ((mask >> 1) & 1)
            pc = c ^ (mask & 1)
            peers.append((px, py, pc))
        sends = [_remote(src, out.at[4 * x + 2 * y + c], send_sems, recv_sems, k, dev) for k, dev in enumerate(peers)]
        for cp in sends:
            cp.start()
        for k, (px, py, pc) in enumerate(peers):
            _remote(src, out.at[4 * px + 2 * py + pc], send_sems, recv_sems, k, (px, py, pc)).wait_recv()
        for cp in sends:
            cp.wait_send()
        own.wait()

    return _pcall(
        body, name="gather8_small", in_specs=[ANY], out_specs=ANY,
        out_shape=jax.ShapeDtypeStruct((8,) + flat.shape, flat.dtype),
        scratch_shapes=[pltpu.SemaphoreType.DMA((7,)), pltpu.SemaphoreType.DMA((7,)), pltpu.SemaphoreType.DMA(())],
    )(flat)


def _pad_cols(a, width):
    return jnp.pad(a, ((0, 0), (0, width - a.shape[1])))


def _col_shards(full):
    r, c = full.shape
    return full.reshape(r, 4, c // 4).transpose(1, 0, 2)


def kernel(x, mem, mix_norm_pre, mix_norm_post, w_in, rwkv_shift_mix, rwkv_w0, rwkv_w_up, rwkv_a0, rwkv_a_up, rwkv_g_up, rwkv_k_k, rwkv_k_a, rwkv_r_k, rwkv_gn_w, rwkv_gn_b, gdn_conv_w, gdn_a_log, gdn_dt_bias, gdn_norm_w, w_branch_rwkv, w_branch_gdn, w_mix_out, xa_norm_pre, xa_norm_mem, xa_norm_post, xa_w_q, xa_w_kv, xa_w_o, mlp_norm_pre, mlp_norm_post, mlp_w_up, mlp_w_down, loss_target, m_mix_norm_pre, m_mix_norm_post, m_w_in, m_rwkv_shift_mix, m_rwkv_w0, m_rwkv_w_up, m_rwkv_a0, m_rwkv_a_up, m_rwkv_g_up, m_rwkv_k_k, m_rwkv_k_a, m_rwkv_r_k, m_rwkv_gn_w, m_rwkv_gn_b, m_gdn_conv_w, m_gdn_a_log, m_gdn_dt_bias, m_gdn_norm_w, m_w_branch_rwkv, m_w_branch_gdn, m_w_mix_out, m_xa_norm_pre, m_xa_norm_mem, m_xa_norm_post, m_xa_w_q, m_xa_w_kv, m_xa_w_o, m_mlp_norm_pre, m_mlp_norm_post, m_mlp_w_up, m_mlp_w_down, v_mix_norm_pre, v_mix_norm_post, v_w_in, v_rwkv_shift_mix, v_rwkv_w0, v_rwkv_w_up, v_rwkv_a0, v_rwkv_a_up, v_rwkv_g_up, v_rwkv_k_k, v_rwkv_k_a, v_rwkv_r_k, v_rwkv_gn_w, v_rwkv_gn_b, v_gdn_conv_w, v_gdn_a_log, v_gdn_dt_bias, v_gdn_norm_w, v_w_branch_rwkv, v_w_branch_gdn, v_w_mix_out, v_xa_norm_pre, v_xa_norm_mem, v_xa_norm_post, v_xa_w_q, v_xa_w_kv, v_xa_w_o, v_mlp_norm_pre, v_mlp_norm_post, v_mlp_w_up, v_mlp_w_down):
    given = dict(locals())
    W = {n: given[n] if given[n].ndim == 2 else given[n][0] for n in WEIGHTS}
    Mo = {n: given['m_' + n].reshape(W[n].shape) for n in WEIGHTS}
    Vo = {n: given['v_' + n].reshape(W[n].shape) for n in WEIGHTS}
    xs, mems, tgt = x[0], mem[0], loss_target[0]
    T, D = xs.shape
    RW = W['rwkv_w0'].shape[-1]
    GW = W['gdn_conv_w'].shape[-1] * 4 // 3
    NH = W['gdn_a_log'].shape[-1]
    LW, LA, LG = W['rwkv_w_up'].shape[0], W['rwkv_a_up'].shape[0], W['rwkv_g_up'].shape[0]
    XW = W['xa_w_q'].shape[-1]
    assert LW <= LANES and LA <= LANES and LG % LANES == 0 and 2 * NH <= LANES

    del _PENDING[:]
    me = 2 * lax.axis_index("x") + lax.axis_index("y")
    small_sizes = [W[n].size for n in SMALL_SHARDED]
    small_len = -(-sum(small_sizes) // 1024) * 1024

    def pack_small(vals):
        flat = jnp.concatenate([vv.reshape(-1).astype(F32) for vv in vals])
        return jnp.pad(flat, (0, small_len - flat.shape[0])).reshape(small_len // 1024, 1024)

    def unpack_small(packed):
        flat, out, off = packed.reshape(-1), [], 0
        for n, sz in zip(SMALL_SHARDED, small_sizes):
            out.append(flat[off:off + sz].reshape(W[n].shape))
            off += sz
        return out

    gstarted = {}

    def gstart(n, after=None):
        local = pack_small([W[s] for s in SMALL_SHARDED]) if n == 'small' else W[n].astype(BF16)
        if after is not None:
            local, _ = lax.optimization_barrier((local, after))
        mode = 'gather_half' if n in HALVED else 'gather'
        gstarted[n] = _send3_start("gather_start_" + n, local, (4,) + local.shape, mode)
        _PENDING.append(gstarted[n][4])

    HALVED = ('w_in', 'mlp_w_up', 'mlp_w_down')
    gstart('w_in')
    gstart('small')

    def gathered(n, after):
        src, land = _send3_wait("gather_wait_" + n, gstarted[n], after, halves=n in HALVED)
        if n in HALVED:
            fwd = _send3_start("gather_forward_" + n, src, land, 'forward_half')
            src, land = _send3_wait("gather_forward_wait_" + n, fwd, fwd[4], halves=True)
        return lax.dynamic_update_index_in_dim(land, src, me, 0)

    def gathered_full(n, after):
        g = gathered(n, after)
        if n == 'w_in':
            return jnp.concatenate([g[0], g[1], g[2], g[3]], axis=1)
        if n in COL_SHARDED:
            return g
        return g.reshape((4 * g.shape[1],) + g.shape[2:])

    RC = 3 * RW + LW + LA + LG
    o_z_raw = RC + 3 * GW
    o_b_raw = o_z_raw + GW
    o_grw_raw = o_b_raw + 2 * NH
    O_GRW = 0
    O_GGDN = D
    O_R = 2 * D
    O_GQ = O_R + 3 * RW
    O_Z = O_GQ + 3 * GW
    O_LW = O_Z + GW
    O_LA = O_LW + LANES
    O_LG = O_LA + LANES
    O_BA = O_LG + LG
    END = O_BA + LANES
    NP = -(-END // 1024) * 1024
    XM = 3 * RW + 2 * LANES + LG
    assert O_R % RW == 0 and O_GQ % GW == 0 and O_Z % GW == 0 and O_LG % LG == 0 and (3 * RW + 2 * LANES) % LG == 0

    def pad_layout(raw, lead):
        return jnp.concatenate([
            raw[:, o_grw_raw:o_grw_raw + 2 * D], raw[:, :3 * RW], raw[:, RC:RC + 3 * GW], raw[:, o_z_raw:o_z_raw + GW],
            _pad_cols(raw[:, 3 * RW:3 * RW + LW], LANES), _pad_cols(raw[:, 3 * RW + LW:3 * RW + LW + LA], LANES),
            raw[:, 3 * RW + LW + LA:RC], _pad_cols(raw[:, o_b_raw:o_b_raw + 2 * NH], LANES),
            jnp.zeros((lead, NP - END), raw.dtype)], axis=1)

    def unpad_layout(pd):
        return jnp.concatenate([
            pd[:, O_R:O_R + 3 * RW], pd[:, O_LW:O_LW + LW], pd[:, O_LA:O_LA + LA], pd[:, O_LG:O_LG + LG],
            pd[:, O_GQ:O_GQ + 3 * GW], pd[:, O_Z:O_Z + GW], pd[:, O_BA:O_BA + 2 * NH], pd[:, O_GRW:O_GRW + 2 * D]], axis=1)

    def xm_layout(raw):
        return jnp.concatenate([raw[:, :3 * RW], _pad_cols(raw[:, 3 * RW:3 * RW + LW], LANES),
                                _pad_cols(raw[:, 3 * RW + LW:3 * RW + LW + LA], LANES), raw[:, 3 * RW + LW + LA:]], axis=1)

    def xm_unlayout(pd):
        return jnp.concatenate([pd[:, :3 * RW], pd[:, 3 * RW:3 * RW + LW], pd[:, 3 * RW + LANES:3 * RW + LANES + LA],
                                pd[:, 3 * RW + 2 * LANES:]], axis=1)

    mix_x = xm_layout(W['rwkv_shift_mix'])
    r_k = W['rwkv_r_k'].reshape(1, RW)
    alog_p = jnp.pad(W['gdn_a_log'], ((0, 0), (NH, LANES - 2 * NH)))
    dt_p = jnp.pad(W['gdn_dt_bias'], ((0, 0), (NH, LANES - 2 * NH)))
    nw_t = jnp.tile(W['gdn_norm_w'], (1, GW // GDN_HEAD))
    n_main, main_cb, lora_cb = 3 * RW // LANES, O_R // LANES, O_LW // LANES
    f_gdn_pre = functools.partial(_f_gdn_pre, heads=NH)

    (u,) = _rows_fwd("rms_pre", _f_rms, [(xs, D, 0)], [W['mix_norm_pre']], [(D, BF16)], 256)
    full = {'w_in': gathered_full('w_in', u)}
    w_p = pad_layout(full['w_in'], D)
    small4 = gathered('small', w_p)
    for n in ['w_branch_rwkv', 'w_branch_gdn', 'w_mix_out', 'xa_w_q', 'xa_w_kv', 'xa_w_o', 'mlp_w_up']:
        gstart(n, small4)
    p = _mm("mm_in", u, w_p)
    small_parts = [unpack_small(small4[s]) for s in range(4)]
    for j, n in enumerate(SMALL_SHARDED):
        full[n] = jnp.concatenate([small_parts[s][j] for s in range(4)], axis=1)
    w_up_p = jnp.pad(full['rwkv_w_up'], ((0, LANES - LW), (0, 0)))
    a_up_p = jnp.pad(full['rwkv_a_up'], ((0, LANES - LA), (0, 0)))
    g_up = full['rwkv_g_up']
    conv8 = jnp.pad(full['gdn_conv_w'], ((0, 4), (0, 0)))
    rwkv_par = [W['rwkv_w0'], W['rwkv_a0'], W['rwkv_k_k'], W['rwkv_k_a'], w_up_p, a_up_p, g_up]
    xm = _tokshift_fwd(p, mix_x, n_main, main_cb, lora_cb)
    pre_rows = [(xm, RW, 0), (xm, RW, 1), (xm, RW, 2), (xm, LANES, n_main), (xm, LANES, n_main + 1),
                (xm, LG, (3 * RW + 2 * LANES) // LG)]
    lwd, k2, av, bv, gate = _rows_fwd("rwkv_pre", _f_rwkv_pre, pre_rows, rwkv_par,
                                      [None, (RW, F32), (RW, F32), None, (RW, F32), (RW, F32), (RW, F32)], 128)
    rec_rw = [(xm, 0), (lwd, 0), (k2, 0), (xm, 2 * RW // LANES), (av, 0), (bv, 0)]
    y_rec, s_rw = _rec_fwd("rwkv_rec_fwd", _rwkv_chunk, rec_rw, RW // LANES)
    post_rows = [(y_rec, RW, 0), (xm, RW, 0), (k2, RW, 0), (xm, RW, 2), (gate, RW, 0)]
    post_par = [r_k, W['rwkv_gn_w'], W['rwkv_gn_b']]
    (y_rw,) = _rows_fwd("rwkv_post", _f_rwkv_post, post_rows, post_par, [(RW, BF16)], 128)

    qkvc = _conv_fwd(p, conv8, O_GQ // LANES)
    gpre_rows = [(qkvc, GW, 0), (qkvc, GW, 1), (qkvc, GW, 2), (p, LANES, O_BA // LANES)]
    qn, kn, beta_x, g_x = _rows_fwd("gdn_pre", f_gdn_pre, gpre_rows, [alog_p, dt_p],
                                    [(GW, F32), (GW, F32), None, (GW, F32), (GW, F32)], 128)
    rec_gd = [(qn, 0), (kn, 0), (qkvc, 2 * GW // LANES), (beta_x, 0), (g_x, 0)]
    o_gd, s_gd = _rec_fwd("gdn_rec_fwd", _gdn_chunk, rec_gd, GW // LANES)
    gpost_rows = [(o_gd, GW, 0), (p, GW, O_Z // GW)]
    (y_gdn,) = _rows_fwd("gdn_post", _f_gdn_post, gpost_rows, [nw_t], [(GW, BF16)], 128)

    full['w_branch_rwkv'] = gathered_full('w_branch_rwkv', y_gdn)
    gstart('mlp_w_down', full['w_branch_rwkv'])
    br_rw = _mm("mm_br_rw", y_rw, full['w_branch_rwkv'], b_shards=True)
    full['w_branch_gdn'] = gathered_full('w_branch_gdn', y_gdn)
    br_gdn = _mm("mm_br_gdn", y_gdn, full['w_branch_gdn'], b_shards=True)
    merge_rows = [(p, D, O_GRW // D), (p, D, O_GGDN // D), (br_rw, D, 0), (br_gdn, D, 0)]
    (merged,) = _rows_fwd("merge", _f_merge, merge_rows, [], [(D, BF16)], 128)
    full['w_mix_out'] = gathered_full('w_mix_out', merged)
    ymix = _mm("mm_mix_out", merged, full['w_mix_out'])
    r1_par = [W['mix_norm_post'], W['xa_norm_pre']]
    h1, cq = _rows_fwd("resid1", _f_resid, [(xs, D, 0), (ymix, D, 0)], r1_par, [(D, F32), (D, BF16)], 128)
    full['xa_w_q'] = gathered_full('xa_w_q', cq)
    q = _mm("mm_xa_q", cq, full['xa_w_q'])
    (mn,) = _rows_fwd("mem_norm", _f_rms, [(mems, D, 0)], [W['xa_norm_mem']], [(D, BF16)], 128)
    full['xa_w_kv'] = gathered_full('xa_w_kv', mn)
    kv = _mm("mm_xa_kv", mn, full['xa_w_kv'])
    k_x, v_x = kv[:, :XW], kv[:, XW:]
    (o_x,) = _rows_fwd("xattn", _f_xattn, [(q, XW, 0)], [k_x, v_x], [(XW, BF16)], 256)
    full['xa_w_o'] = gathered_full('xa_w_o', o_x)
    xo = _mm("mm_xa_o", o_x, full['xa_w_o'], b_shards=True)
    r2_par = [W['xa_norm_post'], W['mlp_norm_pre']]
    h2, fm = _rows_fwd("resid2", _f_resid, [(h1, D, 0), (xo, D, 0)], r2_par, [(D, F32), (D, BF16)], 128)
    full['mlp_w_up'] = gathered_full('mlp_w_up', fm)
    up, act = _mm("mm_mlp_up", fm, full['mlp_w_up'], out_dtype=(F32, BF16), b_shards=True,
                  epilogue=lambda z: (z, jnp.square(jnp.maximum(z, 0.0))))
    DFF = up.shape[1]
    full['mlp_w_down'] = gathered_full('mlp_w_down', act)
    down = _mm("mm_mlp_down", act, full['mlp_w_down'])

    G = {}
    xstarted = {}

    def xstart(n, g4):
        own = lax.dynamic_index_in_dim(g4, me, 0, keepdims=False)
        xstarted[n] = (_send3_start("grad_start_" + n, g4, (3,) + g4.shape[1:], 'exchange'), own)
        _PENDING.append(xstarted[n][0][4])

    (dh2, ddown), (G['mlp_norm_post'],), (loss_acc,) = _rows_bwd(
        "final", _f_final, [(h2, D, 0), (down, D, 0), (tgt, D, 0)], [W['mlp_norm_post']], [[]], 128,
        [(0, 0), (1, 0), None], [D, D], loss=True, out_dtypes=[F32, BF16])
    (d_up,) = _mm("mm_d_act", ddown, full['mlp_w_down'], tb=True, out_dtype=(BF16,), extra=(up,),
                  epilogue=lambda z, upt: (z * (2.0 * jnp.maximum(upt, 0.0)),))
    dw_down = _mm("mm_dw_down", act, ddown, ta=True, out_dtype=BF16)
    xstart('mlp_w_down', dw_down.astype(BF16).reshape(4, DFF // 4, D))
    d_fm = _mm("mm_d_fm", d_up, full['mlp_w_up'], tb=True, b_shards=True)
    xstart('mlp_w_up', _mm("mm_dw_up", fm, d_up, ta=True, out_dtype=BF16, out_shards=True))
    (dh1, dxo), (G['xa_norm_post'], G['mlp_norm_pre']), _ = _rows_bwd(
        "resid2_bwd", _f_resid, [(h1, D, 0), (xo, D, 0)], r2_par, [[(dh2, D, 0)], [(d_fm, D, 0)]], 128,
        [(0, 0), (1, 0)], [D, D], out_dtypes=[F32, BF16])
    d_ox = _mm("mm_d_ox", dxo, full['xa_w_o'], tb=True, b_shards=True)
    xstart('xa_w_o', _mm("mm_dw_o", o_x, dxo, ta=True, out_dtype=BF16, out_shards=True))
    (dq,), (dk_x, dv_x), _ = _rows_bwd("xattn_bwd", _f_xattn, [(q, XW, 0)], [k_x, v_x], [[(d_ox, XW, 0)]], 128,
                                       [(0, 0)], [XW], out_dtypes=[BF16])
    d_cq = _mm("mm_d_cq", dq, full['xa_w_q'], tb=True)
    dw_q = _mm("mm_dw_q", cq, dq, ta=True, out_dtype=BF16)
    xstart('xa_w_q', dw_q.astype(BF16).reshape(4, D // 4, XW))
    dkv = jnp.concatenate([dk_x, dv_x], axis=1)
    d_mn = _mm("mm_d_mn", dkv, full['xa_w_kv'], tb=True)
    dw_kv = _mm("mm_dw_kv", mn, dkv, ta=True, out_dtype=BF16)
    xstart('xa_w_kv', dw_kv.astype(BF16).reshape(4, D // 4, 2 * XW))
    _, (G['xa_norm_mem'],), _ = _rows_bwd("mem_norm_bwd", _f_rms, [(mems, D, 0)], [W['xa_norm_mem']],
                                          [[(d_mn, D, 0)]], 128, [None], [])
    (dx_a, dymix), (G['mix_norm_post'], G['xa_norm_pre']), _ = _rows_bwd(
        "resid1_bwd", _f_resid, [(xs, D, 0), (ymix, D, 0)], r1_par, [[(dh1, D, 0)], [(d_cq, D, 0)]], 128,
        [(0, 0), (1, 0)], [D, D], out_dtypes=[F32, BF16])
    d_merged = _mm("mm_d_merged", dymix, full['w_mix_out'], tb=True)
    dw_out = _mm("mm_dw_out", merged, dymix, ta=True, out_dtype=BF16)
    xstart('w_mix_out', dw_out.astype(BF16).reshape(4, D // 4, D))
    dp = jnp.zeros((T, NP), BF16)
    (dp, d_brw, d_bgdn), _, _ = _rows_bwd(
        "merge_bwd", _f_merge, merge_rows, [], [[(d_merged, D, 0)]], 128, [(0, 0), (0, D), (1, 0), (2, 0)], [2 * D, D, D],
        out_dtypes=[BF16] * 3, alias=(0, dp, O_GRW // (2 * D)))
    d_yrw = _mm("mm_d_yrw", d_brw, full['w_branch_rwkv'], tb=True, b_shards=True)
    xstart('w_branch_rwkv', _mm("mm_dw_brw", y_rw, d_brw, ta=True, out_dtype=BF16, out_shards=True))
    d_ygdn = _mm("mm_d_ygdn", d_bgdn, full['w_branch_gdn'], tb=True, b_shards=True)
    xstart('w_branch_gdn', _mm("mm_dw_bgdn", y_gdn, d_bgdn, ta=True, out_dtype=BF16, out_shards=True))

    (d_y, d_r1, d_k2a, d_v1, d_gate), (d_rk, G['rwkv_gn_w'], G['rwkv_gn_b']), _ = _rows_bwd(
        "rwkv_post_bwd", _f_rwkv_post, post_rows, post_par, [[(d_yrw, RW, 0)]], 64,
        [(0, 0), (1, 0), (2, 0), (3, 0), (4, 0)], [RW] * 5)
    d_r2, d_lwd, d_k2b, d_v2, d_av, d_bv = _rec_bwd("rwkv_rec_bwd", _rwkv_chunk, rec_rw, d_y, s_rw, RW // LANES)
    pre_cts = [[(d_r1, RW, 0), (d_r2, RW, 0)], [(d_lwd, RW, 0)], [(d_k2a, RW, 0), (d_k2b, RW, 0)],
               [(d_v1, RW, 0), (d_v2, RW, 0)], [(d_av, RW, 0)], [(d_bv, RW, 0)], [(d_gate, RW, 0)]]
    pre_out = [(0, 0), (0, RW), (0, 2 * RW), (0, 3 * RW), (0, 3 * RW + LANES), (0, 3 * RW + 2 * LANES)]
    (d_xm,), pre_pg, _ = _rows_bwd("rwkv_pre_bwd", _f_rwkv_pre, pre_rows, rwkv_par, pre_cts, 64, pre_out, [XM])
    G['rwkv_w0'], G['rwkv_a0'], G['rwkv_k_k'], G['rwkv_k_a'], d_wup_p, d_aup_p, d_gup = pre_pg
    dp, d_mix_x = _tokshift_bwd(d_xm, p, mix_x, dp, n_main, main_cb, lora_cb)

    (d_ogd, dp), (d_nw_t,), _ = _rows_bwd("gdn_post_bwd", _f_gdn_post, gpost_rows, [nw_t], [[(d_ygdn, GW, 0)]], 64,
                                          [(0, 0), (1, 0)], [GW, GW], out_dtypes=[F32, BF16], alias=(1, dp, O_Z // GW))
    d_qn, d_kn, d_vg, d_betax, d_gx = _rec_bwd("gdn_rec_bwd", _gdn_chunk, rec_gd, d_ogd, s_gd, GW // LANES)
    gpre_cts = [[(d_qn, GW, 0)], [(d_kn, GW, 0)], [(d_vg, GW, 0)], [(d_betax, GW, 0)], [(d_gx, GW, 0)]]
    (d_qkvc, dp), (d_alog_p, d_dt_p), _ = _rows_bwd(
        "gdn_pre_bwd", f_gdn_pre, gpre_rows, [alog_p, dt_p], gpre_cts, 64,
        [(0, 0), (0, GW), (0, 2 * GW), (1, 0)], [3 * GW, LANES], out_dtypes=[F32, BF16], alias=(1, dp, O_BA // LANES))
    dp, d_conv8 = _conv_bwd(d_qkvc, p, conv8, dp, O_GQ // LANES)
    small_g = [_col_shards(d_wup_p[:LW]), _col_shards(d_aup_p[:LA]), _col_shards(d_gup), _col_shards(d_conv8[:4])]
    xstart('small', jnp.stack([pack_small([g[s] for g in small_g]) for s in range(4)]))
    dw_p = _mm("mm_dw_in", u, dp, ta=True, out_dtype=BF16)
    xstart('w_in', _col_shards(unpad_layout(dw_p.astype(BF16))))
    d_u = _mm("mm_d_u", dp, w_p, tb=True)
    (grad_x,), (G['mix_norm_pre'],), _ = _rows_bwd(
        "rms_pre_bwd", _f_rms, [(xs, D, 0)], [W['mix_norm_pre']], [[(d_u, D, 0)]], 128, [(0, 0)], [D],
        adds=[(0, (dx_a, D, 0))])

    G['rwkv_shift_mix'] = xm_unlayout(d_mix_x)
    G['rwkv_r_k'] = d_rk
    G['gdn_a_log'] = d_alog_p[:, NH:2 * NH]
    G['gdn_dt_bias'] = d_dt_p[:, NH:2 * NH]
    G['gdn_norm_w'] = jnp.sum(d_nw_t.reshape(GW // GDN_HEAD, GDN_HEAD), axis=0, keepdims=True)

    res = {}
    for n in xstarted:
        started, own = xstarted[n]
        _, land = _send3_wait("grad_wait_" + n, started, grad_x)
        part = _sum4("sum4_" + n, own, land)
        other = _swap_sibling("swap_" + n, part)
        if n == 'small':
            for sn, pa, pb in zip(SMALL_SHARDED, unpack_small(part), unpack_small(other)):
                res[sn] = _adam_big("adam_" + sn, W[sn], Mo[sn], Vo[sn], pa, pb)
        else:
            res[n] = _adam_big("adam_" + n, W[n], Mo[n], Vo[n], part, other)

    sizes = [-(-W[n].size // LANES) * LANES for n in REPLICATED]
    total = -(-sum(sizes) // (8 * LANES)) * (8 * LANES)

    def pack(vals):
        parts = [jnp.pad(vv.reshape(-1), (0, sz - vv.size)) for vv, sz in zip(vals, sizes)]
        flat = jnp.concatenate(parts)
        return jnp.pad(flat, (0, total - flat.shape[0])).reshape(8, total // 8)

    g8 = _gather8(pack([G[n] for n in REPLICATED]))
    small = _adam_small(pack([W[n] for n in REPLICATED]), pack([Mo[n] for n in REPLICATED]),
                        pack([Vo[n] for n in REPLICATED]), g8)
    small = [s.reshape(-1) for s in small]
    off = 0
    for n, sz in zip(REPLICATED, sizes):
        res[n] = [s[off:off + W[n].size].reshape(W[n].shape) for s in small]
        off += sz

    loss = lax.psum(loss_acc[0, 0], ("x", "y", "c"))
    outs = [loss, grad_x[None]]
    for j in range(4):
        outs += [res[n][j].reshape(given[n].shape) for n in WEIGHTS]
    return tuple(outs)
```

```python
import functools
import math

import jax
import jax.numpy as jnp
from jax import lax
from jax.experimental import pallas as pl
from jax.experimental.pallas import tpu as pltpu

F32 = jnp.float32
BF16 = jnp.bfloat16
MESH = pl.DeviceIdType.MESH

LANES = 128
VMEM_LIMIT = 56 * 1024 * 1024
CHUNK = 64
TIME_BLOCK = 128
REC_GROUP = 16
REC_MODE = '1'
INV_MODE = '1'

NORM_EPS = 1e-6
L2_EPS = 1e-6
RWKV_GN_EPS = 64e-5
RWKV_HEAD = 64
GDN_HEAD = 128
XA_HEAD = 128
ADAM_LR, ADAM_B1, ADAM_B2, ADAM_EPS, ADAM_WD, ADAM_STEP = 0.001, 0.9, 0.999, 1e-08, 0.01, 10

WEIGHTS = ['mix_norm_pre', 'mix_norm_post', 'w_in', 'rwkv_shift_mix', 'rwkv_w0', 'rwkv_w_up', 'rwkv_a0', 'rwkv_a_up',
           'rwkv_g_up', 'rwkv_k_k', 'rwkv_k_a', 'rwkv_r_k', 'rwkv_gn_w', 'rwkv_gn_b', 'gdn_conv_w', 'gdn_a_log',
           'gdn_dt_bias', 'gdn_norm_w', 'w_branch_rwkv', 'w_branch_gdn', 'w_mix_out', 'xa_norm_pre', 'xa_norm_mem',
           'xa_norm_post', 'xa_w_q', 'xa_w_kv', 'xa_w_o', 'mlp_norm_pre', 'mlp_norm_post', 'mlp_w_up', 'mlp_w_down']
COL_SHARDED = ['w_in', 'rwkv_w_up', 'rwkv_a_up', 'rwkv_g_up', 'gdn_conv_w', 'w_branch_rwkv', 'w_branch_gdn', 'xa_w_o',
               'mlp_w_up']
ROW_SHARDED = ['w_mix_out', 'xa_w_q', 'xa_w_kv', 'mlp_w_down']
SHARDED = COL_SHARDED + ROW_SHARDED
SMALL_SHARDED = ['rwkv_w_up', 'rwkv_a_up', 'rwkv_g_up', 'gdn_conv_w']
BIG_SHARDED = [n for n in SHARDED if n not in SMALL_SHARDED]
REPLICATED = [n for n in WEIGHTS if n not in SHARDED]


ANY = pl.BlockSpec(memory_space=pl.ANY)
_PENDING = []


def _pcall(body, **kw):
    deps = tuple(_PENDING) if "in_specs" in kw else ()
    if not deps:
        return pl.pallas_call(body, **kw)
    del _PENDING[:]
    n_in = len(kw["in_specs"])
    kw["in_specs"] = list(kw["in_specs"]) + [ANY] * len(deps)

    def with_deps(*refs):
        return body(*refs[:n_in], *refs[n_in + len(deps):])

    call = pl.pallas_call(with_deps, **kw)
    return lambda *args: call(*args, *deps)


def _params(sem=None):
    return pltpu.CompilerParams(dimension_semantics=sem, vmem_limit_bytes=VMEM_LIMIT)


def _pick(n, pref, mult=LANES):
    t = (min(n, pref) // mult) * mult
    while t >= mult:
        if n % t == 0:
            return t
        t -= mult
    return n


def _dg(a, b, ca, cb):
    if a.ndim == 3:
        return lax.dot_general(a, b, (((ca + 1,), (cb + 1,)), ((0,), (0,))), preferred_element_type=F32)
    return lax.dot_general(a, b, (((ca,), (cb,)), ((), ())), preferred_element_type=F32)


def _split(x):
    hi = x.astype(BF16)
    lo = (x - hi.astype(F32)).astype(BF16)
    return hi, lo


def _dot_raw(a, b, ca, cb, mode):
    if mode == '1':
        return _dg(a.astype(BF16), b.astype(BF16), ca, cb)
    if mode == 'cb':
        ah, al = _split(a)
        bb = b.astype(BF16)
        return _dg(ah, bb, ca, cb) + _dg(al, bb, ca, cb)
    if mode == 'ca':
        bh, bl = _split(b)
        ab = a.astype(BF16)
        return _dg(ab, bh, ca, cb) + _dg(ab, bl, ca, cb)
    ah, al = _split(a)
    bh, bl = _split(b)
    return _dg(ah, bh, ca, cb) + (_dg(ah, bl, ca, cb) + _dg(al, bh, ca, cb))


@functools.partial(jax.custom_vjp, nondiff_argnums=(2, 3, 4))
def _dot_diff(a, b, ca, cb, mode):
    return _dot_raw(a, b, ca, cb, mode)


def _dot_diff_fwd(a, b, ca, cb, mode):
    return _dot_raw(a, b, ca, cb, mode), (a, b)


def _dot_diff_bwd(ca, cb, mode, res, ct):
    a, b = res
    if mode == 'ca':
        da = jnp.zeros_like(a)
    else:
        bmode = {'1': '1', '3': '3', 'cb': 'cb'}[mode]
        if ca == 1:
            da = _dot_raw(ct, b, 1, 1 if cb == 0 else 0, bmode)
        else:
            da = _dot_raw(b, ct, 1 if cb == 0 else 0, 1, 'ca' if bmode == 'cb' else bmode)
    if mode == 'cb':
        db = jnp.zeros_like(b)
    else:
        amode = {'1': '1', '3': '3', 'ca': 'ca'}[mode]
        if cb == 0:
            db = _dot_raw(a, ct, 0 if ca == 1 else 1, 0, amode)
        else:
            db = _dot_raw(ct, a, 0, 0 if ca == 1 else 1, 'cb' if amode == 'ca' else amode)
    return da, db


_dot_diff.defvjp(_dot_diff_fwd, _dot_diff_bwd)


class _Ops:
    def __init__(self, diff):
        self.diff = diff

    def dot(self, a, b, ca=1, cb=0, mode='1'):
        if self.diff:
            return _dot_diff(a, b, ca, cb, mode)
        return _dot_raw(a, b, ca, cb, mode)


_RAW = _Ops(False)
_DIFF = _Ops(True)


def _sigmoid(x):
    return 1.0 / (1.0 + jnp.exp(-x))


def _softplus(x):
    return jnp.maximum(x, 0.0) + jnp.log(1.0 + jnp.exp(-jnp.abs(x)))


def _rms(x, g, eps=NORM_EPS):
    return x * lax.rsqrt(jnp.mean(x * x, axis=-1, keepdims=True) + eps) * g


def _head_matrix(width, head):
    ch = lax.broadcasted_iota(jnp.int32, (width, LANES), 0)
    hh = lax.broadcasted_iota(jnp.int32, (width, LANES), 1)
    return (lax.shift_right_logical(ch, int(math.log2(head))) == hh).astype(BF16)


def _headsum(ops, x, head):
    e = _head_matrix(x.shape[-1], head)
    s = ops.dot(x, e, 1, 0, 'cb')
    return ops.dot(s, e, 1, 1, 'cb')


def _f_rms(ops, x, g):
    return (_rms(x, g),)


def _f_resid(ops, h, y, g_post, g_next):
    h1 = h + _rms(y, g_post)
    return h1, _rms(h1, g_next)


def _f_final(ops, h, y, tgt, g_post):
    e = h + _rms(y, g_post) - tgt
    return (0.5 * jnp.mean(e * e, axis=-1, keepdims=True),)


def _f_merge(ops, g_rw, g_gdn, b_rw, b_gdn):
    return (_sigmoid(g_rw) * b_rw + _sigmoid(g_gdn) * b_gdn,)


def _f_rwkv_pre(ops, xr, xk, xv, xlw, xla, xlg, w0, a0, k_k, k_a, w_up, a_up, g_up):
    z = w0 + ops.dot(jnp.tanh(xlw), w_up)
    lwd = -jnp.exp(-_softplus(-z) - 0.5)
    a = _sigmoid(a0 + ops.dot(xla, a_up))
    gate = ops.dot(_sigmoid(xlg), g_up)
    kk = xk * k_k
    kk = kk * lax.rsqrt(_headsum(ops, kk * kk, RWKV_HEAD) + L2_EPS)
    k2 = xk * (1.0 + (a - 1.0) * k_a)
    return xr, lwd, k2, xv, -kk, kk * a, gate


def _f_rwkv_post(ops, y, r, k2, v, gate, r_k, gn_w, gn_b):
    inv = 1.0 / RWKV_HEAD
    yc = y - _headsum(ops, y, RWKV_HEAD) * inv
    var = _headsum(ops, yc * yc, RWKV_HEAD) * inv
    yn = yc * lax.rsqrt(var + RWKV_GN_EPS) * gn_w + gn_b
    bonus = _headsum(ops, r * k2 * r_k, RWKV_HEAD) * v
    return ((yn + bonus) * gate,)


def _f_gdn_pre(ops, qc, kc, vc, ba, alog_p, dt_p, *, heads):
    lane = lax.broadcasted_iota(jnp.int32, (1, LANES), 1)
    beta = _sigmoid(ba)
    g = -jnp.exp(alog_p) * _softplus(ba + dt_p)
    gb = jnp.where(lane < heads, beta, g)
    width = qc.shape[-1]
    row = lax.broadcasted_iota(jnp.int32, (LANES, width), 0)
    col_head = lax.shift_right_logical(lax.broadcasted_iota(jnp.int32, (LANES, width), 1), int(math.log2(GDN_HEAD)))
    beta_x = ops.dot(gb, (row == col_head).astype(BF16), 1, 0, 'cb')
    g_x = ops.dot(gb, (row == col_head + heads).astype(BF16), 1, 0, 'cb')
    qn = qc * lax.rsqrt(_headsum(ops, qc * qc, GDN_HEAD) + L2_EPS)
    kn = kc * lax.rsqrt(_headsum(ops, kc * kc, GDN_HEAD) + L2_EPS)
    return qn, kn, vc, beta_x, g_x


def _f_gdn_post(ops, o, z, nw):
    ms = _headsum(ops, o * o, GDN_HEAD) * (1.0 / GDN_HEAD)
    return (o * lax.rsqrt(ms + NORM_EPS) * nw * (z * _sigmoid(z)),)


def _f_xattn(ops, q, k, v):
    width = q.shape[-1]
    lane = lax.broadcasted_iota(jnp.int32, (1, width), 1)
    out = jnp.zeros_like(q)
    for h in range(width // XA_HEAD):
        m = ((lane >= h * XA_HEAD) & (lane < (h + 1) * XA_HEAD)).astype(F32)
        s = ops.dot(q * m, k, 1, 1) * (XA_HEAD ** -0.5)
        s = s - jnp.max(s, axis=-1, keepdims=True)
        e = jnp.exp(s)
        pr = e / jnp.sum(e, axis=-1, keepdims=True)
        out = out + ops.dot(pr, v) * m
    return (out,)


def _inv_unit_lower(ops, a):
    n = a.shape[-1]
    ii = lax.broadcasted_iota(jnp.int32, (n, n), 0)
    jj = lax.broadcasted_iota(jnp.int32, (n, n), 1)
    p = (ii == jj).astype(F32) + a
    q = a
    k = 2
    while k < n:
        q = ops.dot(q, q, 1, 0, INV_MODE)
        p = p + ops.dot(p, q, 1, 0, INV_MODE)
        k *= 2
    return p


def _const(mask, batch):
    return jnp.broadcast_to(mask.astype(BF16), (batch,) + mask.shape)


def _rwkv_chunk(ops, s, r, lw, k, v, a, b):
    g, c, _ = r.shape
    ii = lax.broadcasted_iota(jnp.int32, (c, c), 0)
    jj = lax.broadcasted_iota(jnp.int32, (c, c), 1)
    incl = ii >= jj
    strict = ii > jj
    cum = ops.dot(_const(incl, g), lw, 1, 0, 'ca')
    tot = ops.dot(jnp.ones((g, LANES, c), BF16), lw, 1, 0, 'ca')
    w_inv = jnp.exp(-cum)
    at = a * jnp.exp(cum - lw)
    bt = b * w_inv
    kt = k * w_inv
    rt = r * jnp.exp(cum)
    lane = lax.broadcasted_iota(jnp.int32, (2, 1, 1, LANES), 3)
    head = lax.broadcasted_iota(jnp.int32, (2, 1, 1, LANES), 0)
    hm = ((lane >= RWKV_HEAD) == (head == 1)).astype(F32)

    def per_head(x):
        return (x[None] * hm).reshape(2 * g, c, x.shape[-1])

    def both(x):
        return jnp.broadcast_to(x[None], (2,) + x.shape).reshape((2 * g,) + x.shape[1:])

    def merge(x2):
        return jnp.sum(x2.reshape(2, g, c, LANES) * hm, axis=0)

    bt2, kt2, v2 = both(bt), both(kt), both(v)
    atm, rtm = per_head(at), per_head(rt)
    base = ops.dot(at, s, 1, 1, REC_MODE)
    a_ab = jnp.where(strict, ops.dot(atm, bt2, 1, 1, REC_MODE), 0.0)
    a_ak = jnp.where(strict, ops.dot(atm, kt2, 1, 1, REC_MODE), 0.0)
    a_rb = jnp.where(incl, ops.dot(rtm, bt2, 1, 1, REC_MODE), 0.0)
    a_rk = jnp.where(incl, ops.dot(rtm, kt2, 1, 1, REC_MODE), 0.0)
    t = _inv_unit_lower(ops, a_ab)
    u = merge(ops.dot(t, both(base) + ops.dot(a_ak, v2, 1, 0, REC_MODE), 1, 0, REC_MODE))
    y = ops.dot(rt, s, 1, 1, REC_MODE) + merge(ops.dot(a_rb, both(u), 1, 0, REC_MODE) + ops.dot(a_rk, v2, 1, 0, REC_MODE))
    bi = lax.broadcasted_iota(jnp.int32, (LANES, LANES), 0) >= RWKV_HEAD
    bj = lax.broadcasted_iota(jnp.int32, (LANES, LANES), 1) >= RWKV_HEAD
    upd = jnp.where(bi == bj, ops.dot(u, bt, 0, 0, REC_MODE) + ops.dot(v, kt, 0, 0, REC_MODE), 0.0)
    return (s + upd) * jnp.exp(tot), y


def _gdn_chunk(ops, s, q, k, v, beta, g):
    nb, c, _ = q.shape
    ii = lax.broadcasted_iota(jnp.int32, (c, c), 0)
    jj = lax.broadcasted_iota(jnp.int32, (c, c), 1)
    causal = ii >= jj
    strict = ii > jj
    q = q * (GDN_HEAD ** -0.5)
    gc = ops.dot(_const(causal, nb), g, 1, 0, 'ca')
    ones = jnp.ones((nb, c, LANES), BF16)
    gc_col = ops.dot(gc, ones, 1, 1, 'cb') * (1.0 / LANES)
    gc_row = ops.dot(ones, gc, 1, 1, 'ca') * (1.0 / LANES)
    decay = jnp.where(causal, jnp.exp(jnp.where(causal, gc_col - gc_row, 0.0)), 0.0)
    kb = k * beta
    lmat = jnp.where(strict, ops.dot(kb, k, 1, 1, REC_MODE) * decay, 0.0)
    tinv = _inv_unit_lower(ops, -lmat)
    egc = jnp.exp(gc)
    u = ops.dot(tinv, v * beta, 1, 0, REC_MODE)
    w = ops.dot(tinv, kb * egc, 1, 0, REC_MODE)
    a_intra = jnp.where(causal, ops.dot(q, k, 1, 1, REC_MODE) * decay, 0.0)
    g_tot = ops.dot(jnp.ones((nb, c, c), BF16), g, 1, 0, 'ca')
    k_dec = k * jnp.exp(g_tot - gc)
    v_new = u - ops.dot(w, s, 1, 0, REC_MODE)
    o = ops.dot(q * egc, s, 1, 0, REC_MODE) + ops.dot(a_intra, v_new, 1, 0, REC_MODE)
    g_full = ops.dot(jnp.ones((nb, LANES, c), BF16), g, 1, 0, 'ca')
    s_new = s * jnp.exp(g_full) + ops.dot(k_dec, v_new, 0, 0, REC_MODE)
    return s_new, o


def _mm(name, a, b, ta=False, tb=False, out_dtype=F32, tm=1024, tn=1024, tk=4096, epilogue=None, extra=(),
        b_shards=False, out_shards=False):
    m = a.shape[1] if ta else a.shape[0]
    kd = a.shape[0] if ta else a.shape[1]
    ns = b.shape[2] if b_shards else None
    b_rows, b_cols = (b.shape[1], 4 * ns) if b_shards else b.shape
    n = b_rows if tb else b_cols
    assert kd == (b_cols if tb else b_rows), (name, a.shape, b.shape)
    n_unit = ns if (b_shards and not tb) else (n // 4 if out_shards else n)
    k_unit = ns if (b_shards and tb) else kd
    tm, tn, tk = _pick(m, tm), _pick(n_unit, tn), _pick(k_unit, tk)
    nk = kd // tk
    n_x = len(extra)
    out_dtypes = out_dtype if epilogue is not None else (out_dtype,)

    def body(a_ref, b_ref, *rest):
        x_refs, o_refs, acc = rest[:n_x], rest[n_x:-1], rest[-1]
        kk = pl.program_id(2)
        part = _dg(a_ref[...].astype(BF16), b_ref[...].astype(BF16), 0 if ta else 1, 1 if tb else 0)

        def finish(z):
            vals = (z,) if epilogue is None else epilogue(z, *[xr[...] for xr in x_refs])
            for o_ref, val in zip(o_refs, vals):
                o_ref[...] = val.astype(o_ref.dtype)

        if nk == 1:
            finish(part)
            return

        @pl.when(kk == 0)
        def _():
            acc[...] = part

        @pl.when((kk > 0) & (kk < nk - 1))
        def _():
            acc[...] += part

        @pl.when(kk == nk - 1)
        def _():
            finish(acc[...] + part)

    a_spec = pl.BlockSpec((tk, tm), lambda i, j, k: (k, i)) if ta else pl.BlockSpec((tm, tk), lambda i, j, k: (i, k))
    if not b_shards:
        b_spec = pl.BlockSpec((tn, tk), lambda i, j, k: (j, k)) if tb else pl.BlockSpec((tk, tn), lambda i, j, k: (k, j))
    elif tb:
        per_k = ns // tk
        b_spec = pl.BlockSpec((None, tn, tk), lambda i, j, k: (k // per_k, j, k % per_k))
    else:
        per_n = ns // tn
        b_spec = pl.BlockSpec((None, tk, tn), lambda i, j, k: (j // per_n, k, j % per_n))
    o_spec = pl.BlockSpec((tm, tn), lambda i, j, k: (i, j))
    if out_shards:
        per_o = (n // 4) // tn
        res_spec = [pl.BlockSpec((None, tm, tn), lambda i, j, k: (j // per_o, i, j % per_o))]
        res_shape = [jax.ShapeDtypeStruct((4, m, n // 4), out_dtypes[0])]
    else:
        res_spec = [o_spec] * len(out_dtypes)
        res_shape = [jax.ShapeDtypeStruct((m, n), dt) for dt in out_dtypes]
    res = _pcall(
        body, name=name, grid=(m // tm, n // tn, nk), in_specs=[a_spec, b_spec] + [o_spec] * n_x,
        out_specs=res_spec, out_shape=res_shape,
        scratch_shapes=[pltpu.VMEM((tm, tn), F32)],
        compiler_params=_params(("parallel", "parallel", "arbitrary")))(a, b, *extra)
    return res if epilogue is not None else res[0]


def _row_spec(tm, width, cb):
    return pl.BlockSpec((tm, width), lambda i: (i, cb))


def _rows_fwd(name, f, rows, params, outs, tm):
    t = rows[0][0].shape[0]
    tm = min(tm, t)
    n_in = len(rows) + len(params)
    kept = [o for o in outs if o is not None]

    def body(*refs):
        vals = f(_RAW, *[r[...].astype(F32) for r in refs[:n_in]])
        o_refs = refs[n_in:]
        j = 0
        for val, o in zip(vals, outs):
            if o is not None:
                o_refs[j][...] = val.astype(o_refs[j].dtype)
                j += 1

    in_specs = [_row_spec(tm, w, cb) for (_, w, cb) in rows]
    in_specs += [pl.BlockSpec(p.shape, lambda i: (0, 0)) for p in params]
    res = _pcall(
        body, name=name, grid=(t // tm,), in_specs=in_specs,
        out_specs=[_row_spec(tm, w, 0) for (w, _) in kept],
        out_shape=[jax.ShapeDtypeStruct((t, w), dt) for (w, dt) in kept],
        compiler_params=_params(("parallel",)))(*[r[0] for r in rows], *params)
    return res


def _rows_bwd(name, f, rows, params, cts, tm, row_out, out_widths, adds=(), loss=False, out_dtypes=None, alias=None):
    t = rows[0][0].shape[0]
    tm = min(tm, t)
    n_r, n_p = len(rows), len(params)
    ct_flat = [c for cl in cts for c in cl]
    n_ct, n_add = len(ct_flat), len(adds)
    n_out = len(out_widths)

    def body(*refs):
        i = pl.program_id(0)
        ins = [r[...].astype(F32) for r in refs[:n_r + n_p]]
        ct_refs = refs[n_r + n_p:n_r + n_p + n_ct]
        add_refs = refs[n_r + n_p + n_ct:n_r + n_p + n_ct + n_add]
        o_refs = refs[n_r + n_p + n_ct + n_add + (alias is not None):]
        outs, vjp = jax.vjp(lambda *a: f(_DIFF, *a), *ins)
        ctv, pos = [], 0
        for o, cl in zip(outs, cts):
            if loss:
                ctv.append(jnp.ones_like(o))
            elif not cl:
                ctv.append(jnp.zeros_like(o))
            else:
                acc = ct_refs[pos][...].astype(F32)
                for q in range(1, len(cl)):
                    acc = acc + ct_refs[pos + q][...].astype(F32)
                ctv.append(acc)
            pos += len(cl)
        grads = vjp(tuple(ctv))
        for ri, ro in enumerate(row_out):
            if ro is not None:
                oi, off = ro
                val = grads[ri]
                for ai, (aoi, _) in enumerate(adds):
                    if aoi == oi:
                        val = val + add_refs[ai][...].astype(F32)
                o_refs[oi][:, off:off + val.shape[1]] = val.astype(o_refs[oi].dtype)
        p_refs = o_refs[n_out:n_out + n_p]

        @pl.when(i == 0)
        def _():
            for pr in p_refs:
                pr[...] = jnp.zeros_like(pr)
            if loss:
                o_refs[n_out + n_p][...] = jnp.zeros((8, LANES), F32)

        for pi, pr in enumerate(p_refs):
            pr[...] += grads[n_r + pi]
        if loss:
            o_refs[n_out + n_p][...] += jnp.sum(outs[0])

    in_specs = [_row_spec(tm, w, cb) for (_, w, cb) in rows]
    in_specs += [pl.BlockSpec(p.shape, lambda i: (0, 0)) for p in params]
    in_specs += [_row_spec(tm, w, cb) for (_, w, cb) in ct_flat]
    in_specs += [_row_spec(tm, w, cb) for (_, (_, w, cb)) in adds]
    out_specs = [_row_spec(tm, w, 0) for w in out_widths]
    out_specs += [pl.BlockSpec(p.shape, lambda i: (0, 0)) for p in params]
    out_dtypes = out_dtypes or [F32] * n_out
    out_shape = [jax.ShapeDtypeStruct((t, w), dt) for w, dt in zip(out_widths, out_dtypes)]
    out_shape += [jax.ShapeDtypeStruct(p.shape, F32) for p in params]
    if loss:
        out_specs.append(pl.BlockSpec((8, LANES), lambda i: (0, 0)))
        out_shape.append(jax.ShapeDtypeStruct((8, LANES), F32))
    extra, aliases = [], {}
    if alias is not None:
        oi, buf, cb = alias
        in_specs.append(ANY)
        out_specs[oi] = _row_spec(tm, out_widths[oi], cb)
        out_shape[oi] = jax.ShapeDtypeStruct(buf.shape, buf.dtype)
        extra, aliases = [buf], {n_r + n_p + n_ct + n_add: oi}
    res = _pcall(
        body, name=name, grid=(t // tm,), in_specs=in_specs, out_specs=out_specs, out_shape=out_shape,
        input_output_aliases=aliases, compiler_params=_params(("arbitrary",)))(
            *[r[0] for r in rows], *params, *[c[0] for c in ct_flat], *[a[1][0] for a in adds], *extra)
    return res[:n_out], res[n_out:n_out + n_p], res[n_out + n_p:]


def _shift_down(x, k):
    row = lax.broadcasted_iota(jnp.int32, x.shape, 0)
    return jnp.where(row >= k, pltpu.roll(x, k, 0), 0.0)


def _shift_up(x, k):
    t = x.shape[0]
    row = lax.broadcasted_iota(jnp.int32, x.shape, 0)
    return jnp.where(row < t - k, pltpu.roll(x, t - k, 0), 0.0)


def _tokshift_fwd(p, mix, n_main, main_cb, lora_cb):
    t = p.shape[0]
    nblk = mix.shape[1] // LANES

    def src(j):
        return (0, jnp.where(j < n_main, j + main_cb, j - n_main + lora_cb))

    def body(p_ref, mix_ref, o_ref):
        xv = p_ref[...]
        o_ref[...] = xv + (_shift_down(xv, 1) - xv) * mix_ref[...]

    return _pcall(
        body, name="tokshift_fwd", grid=(nblk,),
        in_specs=[pl.BlockSpec((t, LANES), src), pl.BlockSpec((1, LANES), lambda j: (0, j))],
        out_specs=pl.BlockSpec((t, LANES), lambda j: (0, j)),
        out_shape=jax.ShapeDtypeStruct((t, mix.shape[1]), F32), compiler_params=_params(("parallel",)))(p, mix)


def _tokshift_bwd(dxm, p, mix, dp_buf, n_main, main_cb, lora_cb):
    t = p.shape[0]
    nblk = mix.shape[1] // LANES

    def src(j):
        return (0, jnp.where(j < n_main, j + main_cb, j - n_main + lora_cb))

    def body(d_ref, p_ref, mix_ref, buf_ref, dp_ref, dmix_ref):
        xv, d, mx = p_ref[...], d_ref[...], mix_ref[...]
        dp_ref[...] = (d * (1.0 - mx) + _shift_up(d * mx, 1)).astype(dp_ref.dtype)
        dmix_ref[...] = jnp.sum(d * (_shift_down(xv, 1) - xv), axis=0, keepdims=True)

    return _pcall(
        body, name="tokshift_bwd", grid=(nblk,),
        in_specs=[pl.BlockSpec((t, LANES), lambda j: (0, j)), pl.BlockSpec((t, LANES), src),
                  pl.BlockSpec((1, LANES), lambda j: (0, j)), ANY],
        out_specs=[pl.BlockSpec((t, LANES), src), pl.BlockSpec((1, LANES), lambda j: (0, j))],
        out_shape=[jax.ShapeDtypeStruct(dp_buf.shape, dp_buf.dtype), jax.ShapeDtypeStruct((1, mix.shape[1]), F32)],
        input_output_aliases={3: 0},
        compiler_params=_params(("parallel",)))(dxm, p, mix, dp_buf)


def _conv_pre(xv, w):
    return (w[3:4, :] * xv + w[2:3, :] * _shift_down(xv, 1) + w[1:2, :] * _shift_down(xv, 2)
            + w[0:1, :] * _shift_down(xv, 3))


def _conv_fwd(p, w8, cb0):
    t = p.shape[0]
    width = w8.shape[1]

    def body(p_ref, w_ref, o_ref):
        c = _conv_pre(p_ref[...], w_ref[...])
        o_ref[...] = c * _sigmoid(c)

    return _pcall(
        body, name="conv_fwd", grid=(width // LANES,),
        in_specs=[pl.BlockSpec((t, LANES), lambda j: (0, j + cb0)), pl.BlockSpec((8, LANES), lambda j: (0, j))],
        out_specs=pl.BlockSpec((t, LANES), lambda j: (0, j)),
        out_shape=jax.ShapeDtypeStruct((t, width), F32), compiler_params=_params(("parallel",)))(p, w8)


def _conv_bwd(dy, p, w8, dp_buf, cb0):
    t = p.shape[0]
    width = w8.shape[1]

    def body(d_ref, p_ref, w_ref, buf_ref, dx_ref, dw_ref):
        xv, w = p_ref[...], w_ref[...]
        c = _conv_pre(xv, w)
        sg = _sigmoid(c)
        dc = d_ref[...] * (sg * (1.0 + c * (1.0 - sg)))
        dx_ref[...] = (w[3:4, :] * dc + w[2:3, :] * _shift_up(dc, 1) + w[1:2, :] * _shift_up(dc, 2)
                       + w[0:1, :] * _shift_up(dc, 3)).astype(dx_ref.dtype)
        row =lax.broadcasted_iota(jnp.int32, (8, LANES), 0)
        dw = jnp.zeros((8, LANES), F32)
        for j in range(4):
            sj = jnp.sum(dc * (xv if j == 3 else _shift_down(xv, 3 - j)), axis=0, keepdims=True)
            dw = dw + jnp.where(row == j, sj, 0.0)
        dw_ref[...] = dw

    return _pcall(
        body, name="conv_bwd", grid=(width // LANES,),
        in_specs=[pl.BlockSpec((t, LANES), lambda j: (0, j)), pl.BlockSpec((t, LANES), lambda j: (0, j + cb0)),
                  pl.BlockSpec((8, LANES), lambda j: (0, j)), ANY],
        out_specs=[pl.BlockSpec((t, LANES), lambda j: (0, j + cb0)), pl.BlockSpec((8, LANES), lambda j: (0, j))],
        out_shape=[jax.ShapeDtypeStruct(dp_buf.shape, dp_buf.dtype), jax.ShapeDtypeStruct((8, width), F32)],
        input_output_aliases={3: 0},
        compiler_params=_params(("parallel",)))(dy, p, w8, dp_buf)


def _rec_fwd(name, chunk_fn, ins, nblocks):
    t = ins[0][0].shape[0]
    tb = min(TIME_BLOCK, t)
    nt, ncb = t // tb, tb // CHUNK
    n_in = len(ins)
    grp = REC_GROUP if nblocks % REC_GROUP == 0 and all(c0 % REC_GROUP == 0 for (_, c0) in ins) else 1
    width = grp * LANES

    def body(*refs):
        in_refs, y_ref, sall_ref, s_scr = refs[:n_in], refs[n_in], refs[n_in + 1], refs[n_in + 2]

        @pl.when(pl.program_id(1) == 0)
        def _():
            s_scr[...] = jnp.zeros_like(s_scr)

        def step(ci, s):
            sl = pl.ds(pl.multiple_of(ci * CHUNK, CHUNK), CHUNK)
            lanes = [slice(gi * LANES, (gi + 1) * LANES) for gi in range(grp)]
            s1, y = chunk_fn(_RAW, s, *[jnp.stack([r[sl, ln] for ln in lanes]) for r in in_refs])
            for gi, ln in enumerate(lanes):
                sall_ref[gi, ci] = s[gi]
                y_ref[sl, ln] = y[gi]
            return s1

        s_scr[...] = lax.fori_loop(0, ncb, step, s_scr[...])

    in_specs = [pl.BlockSpec((tb, width), functools.partial(lambda h, tt, cb: (tt, cb + h), cb=c0 // grp))
                for (_, c0) in ins]
    return _pcall(
        body, name=name, grid=(nblocks // grp, nt), in_specs=in_specs,
        out_specs=[pl.BlockSpec((tb, width), lambda h, tt: (tt, h)),
                   pl.BlockSpec((grp, ncb, LANES, LANES), lambda h, tt: (h, tt, 0, 0))],
        out_shape=[jax.ShapeDtypeStruct((t, nblocks * LANES), F32),
                   jax.ShapeDtypeStruct((nblocks, t // CHUNK, LANES, LANES), F32)],
        scratch_shapes=[pltpu.VMEM((grp, LANES, LANES), F32)],
        compiler_params=_params(("parallel", "arbitrary")))(*[a for (a, _) in ins])


def _rec_bwd(name, chunk_fn, ins, dy, sall, nblocks):
    t = ins[0][0].shape[0]
    tb = min(TIME_BLOCK, t)
    nt, ncb = t // tb, tb // CHUNK
    n_in = len(ins)
    grp = REC_GROUP if nblocks % REC_GROUP == 0 and all(c0 % REC_GROUP == 0 for (_, c0) in ins) else 1
    width = grp * LANES

    def body(*refs):
        in_refs, dy_ref, sall_ref = refs[:n_in], refs[n_in], refs[n_in + 1]
        d_refs, ds_scr = refs[n_in + 2:2 * n_in + 2], refs[2 * n_in + 2]

        @pl.when(pl.program_id(1) == 0)
        def _():
            ds_scr[...] = jnp.zeros_like(ds_scr)

        def step(j, ds):
            ci = ncb - 1 - j
            sl = pl.ds(pl.multiple_of(ci * CHUNK, CHUNK), CHUNK)
            lanes = [slice(gi * LANES, (gi + 1) * LANES) for gi in range(grp)]
            s0 = jnp.stack([sall_ref[gi, ci] for gi in range(grp)])
            vals = [jnp.stack([r[sl, ln] for ln in lanes]) for r in in_refs]
            _, vjp = jax.vjp(lambda *a: chunk_fn(_DIFF, *a), s0, *vals)
            grads = vjp((ds, jnp.stack([dy_ref[sl, ln] for ln in lanes])))
            for dr, gval in zip(d_refs, grads[1:]):
                for gi, ln in enumerate(lanes):
                    dr[sl, ln] = gval[gi]
            return grads[0]

        ds_scr[...] = lax.fori_loop(0, ncb, step, ds_scr[...])

    in_specs = [pl.BlockSpec((tb, width), functools.partial(lambda h, tt, cb: (nt - 1 - tt, cb + h), cb=c0 // grp))
                for (_, c0) in ins]
    in_specs.append(pl.BlockSpec((tb, width), lambda h, tt: (nt - 1 - tt, h)))
    in_specs.append(pl.BlockSpec((grp, ncb, LANES, LANES), lambda h, tt: (h, nt - 1 - tt, 0, 0)))
    return _pcall(
        body, name=name, grid=(nblocks // grp, nt), in_specs=in_specs,
        out_specs=[pl.BlockSpec((tb, width), lambda h, tt: (nt - 1 - tt, h)) for _ in ins],
        out_shape=[jax.ShapeDtypeStruct((t, nblocks * LANES), F32) for _ in ins],
        scratch_shapes=[pltpu.VMEM((grp, LANES, LANES), F32)],
        compiler_params=_params(("parallel", "arbitrary")))(*[a for (a, _) in ins], dy, sall)


def _pick_rows(r, c):
    tr = max(16, ((1 << 18) // c) // 16 * 16)
    while tr > 16 and r % tr:
        tr -= 16
    return tr if r % tr == 0 else r


def _sum4(name, own, r3):
    _, r, c = r3.shape
    tr = _pick_rows(r, c)

    def body(own_ref, r_ref, o_ref):
        total = ((own_ref[...].astype(F32) + r_ref[0].astype(F32)) + r_ref[1].astype(F32)) + r_ref[2].astype(F32)
        o_ref[...] = total.astype(o_ref.dtype)

    return _pcall(
        body, name=name, grid=(r // tr,),
        in_specs=[pl.BlockSpec((tr, c), lambda i: (i, 0)), pl.BlockSpec((3, tr, c), lambda i: (0, i, 0))],
        out_specs=pl.BlockSpec((tr, c), lambda i: (i, 0)), out_shape=jax.ShapeDtypeStruct((r, c), BF16),
        compiler_params=_params(("parallel",)))(own, r3)


def _adam_math(w, g, m, v):
    m = ADAM_B1 * m + (1.0 - ADAM_B1) * g
    v = ADAM_B2 * v + (1.0 - ADAM_B2) * (g * g)
    m_hat = m / (1.0 - ADAM_B1 ** ADAM_STEP)
    v_hat = v / (1.0 - ADAM_B2 ** ADAM_STEP)
    return -ADAM_LR * (m_hat / (jnp.sqrt(v_hat) + ADAM_EPS) + ADAM_WD * w), m, v


def _adam_big(name, w, m, v, ga, gb):
    r, c = w.shape
    tr = _pick_rows(r, c)

    def body(w_ref, m_ref, v_ref, ga_ref, gb_ref, g_ref, d_ref, mo_ref, vo_ref):
        g = ga_ref[...].astype(F32) + gb_ref[...].astype(F32)
        g_ref[...] = g
        d_ref[...], mo_ref[...], vo_ref[...] = _adam_math(w_ref[...], g, m_ref[...], v_ref[...])

    spec = pl.BlockSpec((tr, c), lambda i: (i, 0))
    return _pcall(
        body, name=name, grid=(r // tr,), in_specs=[spec] * 5, out_specs=[spec] * 4,
        out_shape=[jax.ShapeDtypeStruct((r, c), F32)] * 4, compiler_params=_params(("parallel",)))(w, m, v, ga, gb)


def _adam_small(w, m, v, g8):
    shape = w.shape

    def body(w_ref, m_ref, v_ref, g8_ref, g_ref, d_ref, mo_ref, vo_ref):
        g = g8_ref[0]
        for dev in range(1, 8):
            g = g + g8_ref[dev]
        g_ref[...] = g
        d_ref[...], mo_ref[...], vo_ref[...] = _adam_math(w_ref[...], g, m_ref[...], v_ref[...])

    return _pcall(
        body, name="adam_small", out_shape=[jax.ShapeDtypeStruct(shape, F32)] * 4,
        compiler_params=_params())(w, m, v, g8)


def _place():
    x, y, c = lax.axis_index("x"), lax.axis_index("y"), lax.axis_index("c")
    return x, y, c, [(1 - x, y), (x, 1 - y), (1 - x, 1 - y)]


def _remote(src, dst, send_sems, recv_sems, k, dev):
    return pltpu.make_async_remote_copy(src_ref=src, dst_ref=dst, send_sem=send_sems.at[k], recv_sem=recv_sems.at[k],
                                        device_id=dev, device_id_type=MESH)


HBM = pl.BlockSpec(memory_space=pltpu.HBM)
SEM = pl.BlockSpec(memory_space=pltpu.SEMAPHORE)


def _send3_start(name, src, land, mode):
    land_arr = land if mode == 'forward_half' else lax.empty(land, src.dtype)
    half = land_arr.shape[1] // 2

    def body(src_ref, land_ref, send_sem, recv_sem, src_thru, land_thru, token):
        x, y, c, chips = _place()
        rows = pl.ds(pl.multiple_of(c * half, half), half)
        for k, (px, py) in enumerate(chips):
            dev = (px, py, c)
            if mode == 'gather':
                s_k, d_k = src_ref, land_ref.at[2 * x + y]
            elif mode == 'gather_half':
                s_k, d_k = src_ref.at[rows], land_ref.at[2 * x + y, rows]
            elif mode == 'forward_half':
                s_k = d_k = land_ref.at[2 * px + py, rows]
                dev = (x, y, 1 - c)
            else:
                s_k, d_k = src_ref.at[2 * px + py], land_ref.at[k]
            pltpu.make_async_remote_copy(src_ref=s_k, dst_ref=d_k, send_sem=send_sem, recv_sem=recv_sem,
                                         device_id=dev, device_id_type=MESH).start()
        token[...] = jnp.zeros_like(token)

    return pl.pallas_call(
        body, name=name,
        out_shape=(pltpu.SemaphoreType.DMA(()), pltpu.SemaphoreType.DMA(()), pltpu.HBM(src.shape, src.dtype),
                   pltpu.HBM(land_arr.shape, src.dtype), jax.ShapeDtypeStruct((8, LANES), F32)),
        in_specs=(HBM, HBM), out_specs=(SEM, SEM, HBM, HBM, pl.BlockSpec(memory_space=pltpu.VMEM)),
        input_output_aliases={0: 2, 1: 3},
        compiler_params=pltpu.CompilerParams(has_side_effects=pltpu.SideEffectType.DATAFLOW_SIDE_EFFECTING),
    )(pltpu.with_memory_space_constraint(src, pltpu.HBM), pltpu.with_memory_space_constraint(land_arr, pltpu.HBM))


def _send3_wait(name, started, after, halves=False):
    send_sem, recv_sem, src_thru, land_thru, _ = started
    half = land_thru.shape[1] // 2

    def body(src_ref, land_ref, send_sem, recv_sem, after_ref, src_dead, got_ref):
        x, y, c, chips = _place()
        three = land_ref.at[pl.ds(0, 3), pl.ds(0, half)] if halves else land_ref.at[pl.ds(0, 3)]
        cp = pltpu.make_async_remote_copy(src_ref=three, dst_ref=three, send_sem=send_sem, recv_sem=recv_sem,
                                          device_id=(chips[0][0], chips[0][1], c), device_id_type=MESH)
        cp.wait_send()
        cp.wait_recv()

    return pl.pallas_call(
        body, name=name,
        out_shape=(pltpu.HBM(src_thru.shape, src_thru.dtype), pltpu.HBM(land_thru.shape, land_thru.dtype)),
        in_specs=(HBM, HBM, SEM, SEM, ANY), out_specs=(HBM, HBM), input_output_aliases={0: 0, 1: 1},
        compiler_params=pltpu.CompilerParams(has_side_effects=pltpu.SideEffectType.DATAFLOW_SIDE_EFFECTING),
    )(src_thru, land_thru, send_sem, recv_sem, after)


def _swap_sibling(name, part):
    def body(src, out, send_sems, recv_sems):
        x, y, c, _ = _place()
        cp = _remote(src, out, send_sems, recv_sems, 0, (x, y, 1 - c))
        cp.start()
        cp.wait_recv()
        cp.wait_send()

    return _pcall(
        body, name=name, in_specs=[ANY], out_specs=ANY, out_shape=jax.ShapeDtypeStruct(part.shape, part.dtype),
        scratch_shapes=[pltpu.SemaphoreType.DMA((1,)), pltpu.SemaphoreType.DMA((1,))])(part)


def _gather8(flat):
    def body(src, out, send_sems, recv_sems, local_sem):
        x, y, c, _ = _place()
        own = pltpu.make_async_copy(src, out.at[4 * x + 2 * y + c], local_sem)
        own.start()
        peers = []
        for mask in range(1, 8):
            px = x ^ ((mask >> 2) & 1)
            py = y ^ ((mask >> 1) & 1)
            pc = c ^ (mask & 1)
            peers.append((px, py, pc))
        sends = [_remote(src, out.at[4 * x + 2 * y + c], send_sems, recv_sems, k, dev) for k, dev in enumerate(peers)]
        for cp in sends:
            cp.start()
        for k, (px, py, pc) in enumerate(peers):
            _remote(src, out.at[4 * px + 2 * py + pc], send_sems, recv_sems, k, (px, py, pc)).wait_recv()
        for cp in sends:
            cp.wait_send()
        own.wait()

    return _pcall(
        body, name="gather8_small", in_specs=[ANY], out_specs=ANY,
        out_shape=jax.ShapeDtypeStruct((8,) + flat.shape, flat.dtype),
        scratch_shapes=[pltpu.SemaphoreType.DMA((7,)), pltpu.SemaphoreType.DMA((7,)), pltpu.SemaphoreType.DMA(())],
    )(flat)


def _pad_cols(a, width):
    return jnp.pad(a, ((0, 0), (0, width - a.shape[1])))


def _col_shards(full):
    r, c = full.shape
    return full.reshape(r, 4, c // 4).transpose(1, 0, 2)


def kernel(x, mem, mix_norm_pre, mix_norm_post, w_in, rwkv_shift_mix, rwkv_w0, rwkv_w_up, rwkv_a0, rwkv_a_up, rwkv_g_up, rwkv_k_k, rwkv_k_a, rwkv_r_k, rwkv_gn_w, rwkv_gn_b, gdn_conv_w, gdn_a_log, gdn_dt_bias, gdn_norm_w, w_branch_rwkv, w_branch_gdn, w_mix_out, xa_norm_pre, xa_norm_mem, xa_norm_post, xa_w_q, xa_w_kv, xa_w_o, mlp_norm_pre, mlp_norm_post, mlp_w_up, mlp_w_down, loss_target, m_mix_norm_pre, m_mix_norm_post, m_w_in, m_rwkv_shift_mix, m_rwkv_w0, m_rwkv_w_up, m_rwkv_a0, m_rwkv_a_up, m_rwkv_g_up, m_rwkv_k_k, m_rwkv_k_a, m_rwkv_r_k, m_rwkv_gn_w, m_rwkv_gn_b, m_gdn_conv_w, m_gdn_a_log, m_gdn_dt_bias, m_gdn_norm_w, m_w_branch_rwkv, m_w_branch_gdn, m_w_mix_out, m_xa_norm_pre, m_xa_norm_mem, m_xa_norm_post, m_xa_w_q, m_xa_w_kv, m_xa_w_o, m_mlp_norm_pre, m_mlp_norm_post, m_mlp_w_up, m_mlp_w_down, v_mix_norm_pre, v_mix_norm_post, v_w_in, v_rwkv_shift_mix, v_rwkv_w0, v_rwkv_w_up, v_rwkv_a0, v_rwkv_a_up, v_rwkv_g_up, v_rwkv_k_k, v_rwkv_k_a, v_rwkv_r_k, v_rwkv_gn_w, v_rwkv_gn_b, v_gdn_conv_w, v_gdn_a_log, v_gdn_dt_bias, v_gdn_norm_w, v_w_branch_rwkv, v_w_branch_gdn, v_w_mix_out, v_xa_norm_pre, v_xa_norm_mem, v_xa_norm_post, v_xa_w_q, v_xa_w_kv, v_xa_w_o, v_mlp_norm_pre, v_mlp_norm_post, v_mlp_w_up, v_mlp_w_down):
    given = dict(locals())
    W = {n: given[n] if given[n].ndim == 2 else given[n][0] for n in WEIGHTS}
    Mo = {n: given['m_' + n].reshape(W[n].shape) for n in WEIGHTS}
    Vo = {n: given['v_' + n].reshape(W[n].shape) for n in WEIGHTS}
    xs, mems, tgt = x[0], mem[0], loss_target[0]
    T, D = xs.shape
    RW = W['rwkv_w0'].shape[-1]
    GW = W['gdn_conv_w'].shape[-1] * 4 // 3
    NH = W['gdn_a_log'].shape[-1]
    LW, LA, LG = W['rwkv_w_up'].shape[0], W['rwkv_a_up'].shape[0], W['rwkv_g_up'].shape[0]
    XW = W['xa_w_q'].shape[-1]
    assert LW <= LANES and LA <= LANES and LG % LANES == 0 and 2 * NH <= LANES

    del _PENDING[:]
    me = 2 * lax.axis_index("x") + lax.axis_index("y")
    small_sizes = [W[n].size for n in SMALL_SHARDED]
    small_len = -(-sum(small_sizes) // 1024) * 1024

    def pack_small(vals):
        flat = jnp.concatenate([vv.reshape(-1).astype(F32) for vv in vals])
        return jnp.pad(flat, (0, small_len - flat.shape[0])).reshape(small_len // 1024, 1024)

    def unpack_small(packed):
        flat, out, off = packed.reshape(-1), [], 0
        for n, sz in zip(SMALL_SHARDED, small_sizes):
            out.append(flat[off:off + sz].reshape(W[n].shape))
            off += sz
        return out

    gstarted = {}

    def gstart(n, after=None):
        local = pack_small([W[s] for s in SMALL_SHARDED]) if n == 'small' else W[n].astype(BF16)
        if after is not None:
            local, _ = lax.optimization_barrier((local, after))
        mode = 'gather_half' if n in HALVED else 'gather'
        gstarted[n] = _send3_start("gather_start_" + n, local, (4,) + local.shape, mode)
        _PENDING.append(gstarted[n][4])

    HALVED = ('w_in', 'mlp_w_up', 'mlp_w_down')
    gstart('w_in')
    gstart('small')

    def gathered(n, after):
        src, land = _send3_wait("gather_wait_" + n, gstarted[n], after, halves=n in HALVED)
        if n in HALVED:
            fwd = _send3_start("gather_forward_" + n, src, land, 'forward_half')
            src, land = _send3_wait("gather_forward_wait_" + n, fwd, fwd[4], halves=True)
        return lax.dynamic_update_index_in_dim(land, src, me, 0)

    def gathered_full(n, after):
        g = gathered(n, after)
        if n == 'w_in':
            return jnp.concatenate([g[0], g[1], g[2], g[3]], axis=1)
        if n in COL_SHARDED:
            return g
        return g.reshape((4 * g.shape[1],) + g.shape[2:])

    RC = 3 * RW + LW + LA + LG
    o_z_raw = RC + 3 * GW
    o_b_raw = o_z_raw + GW
    o_grw_raw = o_b_raw + 2 * NH
    O_GRW = 0
    O_GGDN = D
    O_R = 2 * D
    O_GQ = O_R + 3 * RW
    O_Z = O_GQ + 3 * GW
    O_LW = O_Z + GW
    O_LA = O_LW + LANES
    O_LG = O_LA + LANES
    O_BA = O_LG + LG
    END = O_BA + LANES
    NP = -(-END // 1024) * 1024
    XM = 3 * RW + 2 * LANES + LG
    assert O_R % RW == 0 and O_GQ % GW == 0 and O_Z % GW == 0 and O_LG % LG == 0 and (3 * RW + 2 * LANES) % LG == 0

    def pad_layout(raw, lead):
        return jnp.concatenate([
            raw[:, o_grw_raw:o_grw_raw + 2 * D], raw[:, :3 * RW], raw[:, RC:RC + 3 * GW], raw[:, o_z_raw:o_z_raw + GW],
            _pad_cols(raw[:, 3 * RW:3 * RW + LW], LANES), _pad_cols(raw[:, 3 * RW + LW:3 * RW + LW + LA], LANES),
            raw[:, 3 * RW + LW + LA:RC], _pad_cols(raw[:, o_b_raw:o_b_raw + 2 * NH], LANES),
            jnp.zeros((lead, NP - END), raw.dtype)], axis=1)

    def unpad_layout(pd):
        return jnp.concatenate([
            pd[:, O_R:O_R + 3 * RW], pd[:, O_LW:O_LW + LW], pd[:, O_LA:O_LA + LA], pd[:, O_LG:O_LG + LG],
            pd[:, O_GQ:O_GQ + 3 * GW], pd[:, O_Z:O_Z + GW], pd[:, O_BA:O_BA + 2 * NH], pd[:, O_GRW:O_GRW + 2 * D]], axis=1)

    def xm_layout(raw):
        return jnp.concatenate([raw[:, :3 * RW], _pad_cols(raw[:, 3 * RW:3 * RW + LW], LANES),
                                _pad_cols(raw[:, 3 * RW + LW:3 * RW + LW + LA], LANES), raw[:, 3 * RW + LW + LA:]], axis=1)

    def xm_unlayout(pd):
        return jnp.concatenate([pd[:, :3 * RW], pd[:, 3 * RW:3 * RW + LW], pd[:, 3 * RW + LANES:3 * RW + LANES + LA],
                                pd[:, 3 * RW + 2 * LANES:]], axis=1)

    mix_x = xm_layout(W['rwkv_shift_mix'])
    r_k = W['rwkv_r_k'].reshape(1, RW)
    alog_p = jnp.pad(W['gdn_a_log'], ((0, 0), (NH, LANES - 2 * NH)))
    dt_p = jnp.pad(W['gdn_dt_bias'], ((0, 0), (NH, LANES - 2 * NH)))
    nw_t = jnp.tile(W['gdn_norm_w'], (1, GW // GDN_HEAD))
    n_main, main_cb, lora_cb = 3 * RW // LANES, O_R // LANES, O_LW // LANES
    f_gdn_pre = functools.partial(_f_gdn_pre, heads=NH)

    (u,) = _rows_fwd("rms_pre", _f_rms, [(xs, D, 0)], [W['mix_norm_pre']], [(D, BF16)], 256)
    full = {'w_in': gathered_full('w_in', u)}
    w_p = pad_layout(full['w_in'], D)
    small4 = gathered('small', w_p)
    for n in ['w_branch_rwkv', 'w_branch_gdn', 'w_mix_out', 'xa_w_q', 'xa_w_kv', 'xa_w_o', 'mlp_w_up']:
        gstart(n, small4)
    p = _mm("mm_in", u, w_p)
    small_parts = [unpack_small(small4[s]) for s in range(4)]
    for j, n in enumerate(SMALL_SHARDED):
        full[n] = jnp.concatenate([small_parts[s][j] for s in range(4)], axis=1)
    w_up_p = jnp.pad(full['rwkv_w_up'], ((0, LANES - LW), (0, 0)))
    a_up_p = jnp.pad(full['rwkv_a_up'], ((0, LANES - LA), (0, 0)))
    g_up = full['rwkv_g_up']
    conv8 = jnp.pad(full['gdn_conv_w'], ((0, 4), (0, 0)))
    rwkv_par = [W['rwkv_w0'], W['rwkv_a0'], W['rwkv_k_k'], W['rwkv_k_a'], w_up_p, a_up_p, g_up]
    xm = _tokshift_fwd(p, mix_x, n_main, main_cb, lora_cb)
    pre_rows = [(xm, RW, 0), (xm, RW, 1), (xm, RW, 2), (xm, LANES, n_main), (xm, LANES, n_main + 1),
                (xm, LG, (3 * RW + 2 * LANES) // LG)]
    lwd, k2, av, bv, gate = _rows_fwd("rwkv_pre", _f_rwkv_pre, pre_rows, rwkv_par,
                                      [None, (RW, F32), (RW, F32), None, (RW, F32), (RW, F32), (RW, F32)], 128)
    rec_rw = [(xm, 0), (lwd, 0), (k2, 0), (xm, 2 * RW // LANES), (av, 0), (bv, 0)]
    y_rec, s_rw = _rec_fwd("rwkv_rec_fwd", _rwkv_chunk, rec_rw, RW // LANES)
    post_rows = [(y_rec, RW, 0), (xm, RW, 0), (k2, RW, 0), (xm, RW, 2), (gate, RW, 0)]
    post_par = [r_k, W['rwkv_gn_w'], W['rwkv_gn_b']]
    (y_rw,) = _rows_fwd("rwkv_post", _f_rwkv_post, post_rows, post_par, [(RW, BF16)], 128)

    qkvc = _conv_fwd(p, conv8, O_GQ // LANES)
    gpre_rows = [(qkvc, GW, 0), (qkvc, GW, 1), (qkvc, GW, 2), (p, LANES, O_BA // LANES)]
    qn, kn, beta_x, g_x = _rows_fwd("gdn_pre", f_gdn_pre, gpre_rows, [alog_p, dt_p],
                                    [(GW, F32), (GW, F32), None, (GW, F32), (GW, F32)], 128)
    rec_gd = [(qn, 0), (kn, 0), (qkvc, 2 * GW // LANES), (beta_x, 0), (g_x, 0)]
    o_gd, s_gd = _rec_fwd("gdn_rec_fwd", _gdn_chunk, rec_gd, GW // LANES)
    gpost_rows = [(o_gd, GW, 0), (p, GW, O_Z // GW)]
    (y_gdn,) = _rows_fwd("gdn_post", _f_gdn_post, gpost_rows, [nw_t], [(GW, BF16)], 128)

    full['w_branch_rwkv'] = gathered_full('w_branch_rwkv', y_gdn)
    gstart('mlp_w_down', full['w_branch_rwkv'])
    br_rw = _mm("mm_br_rw", y_rw, full['w_branch_rwkv'], b_shards=True)
    full['w_branch_gdn'] = gathered_full('w_branch_gdn', y_gdn)
    br_gdn = _mm("mm_br_gdn", y_gdn, full['w_branch_gdn'], b_shards=True)
    merge_rows = [(p, D, O_GRW // D), (p, D, O_GGDN // D), (br_rw, D, 0), (br_gdn, D, 0)]
    (merged,) = _rows_fwd("merge", _f_merge, merge_rows, [], [(D, BF16)], 128)
    full['w_mix_out'] = gathered_full('w_mix_out', merged)
    ymix = _mm("mm_mix_out", merged, full['w_mix_out'])
    r1_par = [W['mix_norm_post'], W['xa_norm_pre']]
    h1, cq = _rows_fwd("resid1", _f_resid, [(xs, D, 0), (ymix, D, 0)], r1_par, [(D, F32), (D, BF16)], 128)
    full['xa_w_q'] = gathered_full('xa_w_q', cq)
    q = _mm("mm_xa_q", cq, full['xa_w_q'])
    (mn,) = _rows_fwd("mem_norm", _f_rms, [(mems, D, 0)], [W['xa_norm_mem']], [(D, BF16)], 128)
    full['xa_w_kv'] = gathered_full('xa_w_kv', mn)
    kv = _mm("mm_xa_kv", mn, full['xa_w_kv'])
    k_x, v_x = kv[:, :XW], kv[:, XW:]
    (o_x,) = _rows_fwd("xattn", _f_xattn, [(q, XW, 0)], [k_x, v_x], [(XW, BF16)], 256)
    full['xa_w_o'] = gathered_full('xa_w_o', o_x)
    xo = _mm("mm_xa_o", o_x, full['xa_w_o'], b_shards=True)
    r2_par = [W['xa_norm_post'], W['mlp_norm_pre']]
    h2, fm = _rows_fwd("resid2", _f_resid, [(h1, D, 0), (xo, D, 0)], r2_par, [(D, F32), (D, BF16)], 128)
    full['mlp_w_up'] = gathered_full('mlp_w_up', fm)
    up, act = _mm("mm_mlp_up", fm, full['mlp_w_up'], out_dtype=(F32, BF16), b_shards=True,
                  epilogue=lambda z: (z, jnp.square(jnp.maximum(z, 0.0))))
    DFF = up.shape[1]
    full['mlp_w_down'] = gathered_full('mlp_w_down', act)
    down = _mm("mm_mlp_down", act, full['mlp_w_down'])

    G = {}
    xstarted = {}

    def xstart(n, g4):
        own = lax.dynamic_index_in_dim(g4, me, 0, keepdims=False)
        xstarted[n] = (_send3_start("grad_start_" + n, g4, (3,) + g4.shape[1:], 'exchange'), own)
        _PENDING.append(xstarted[n][0][4])

    (dh2, ddown), (G['mlp_norm_post'],), (loss_acc,) = _rows_bwd(
        "final", _f_final, [(h2, D, 0), (down, D, 0), (tgt, D, 0)], [W['mlp_norm_post']], [[]], 128,
        [(0, 0), (1, 0), None], [D, D], loss=True, out_dtypes=[F32, BF16])
    (d_up,) = _mm("mm_d_act", ddown, full['mlp_w_down'], tb=True, out_dtype=(BF16,), extra=(up,),
                  epilogue=lambda z, upt: (z * (2.0 * jnp.maximum(upt, 0.0)),))
    dw_down = _mm("mm_dw_down", act, ddown, ta=True, out_dtype=BF16)
    xstart('mlp_w_down', dw_down.astype(BF16).reshape(4, DFF // 4, D))
    d_fm = _mm("mm_d_fm", d_up, full['mlp_w_up'], tb=True, b_shards=True)
    xstart('mlp_w_up', _mm("mm_dw_up", fm, d_up, ta=True, out_dtype=BF16, out_shards=True))
    (dh1, dxo), (G['xa_norm_post'], G['mlp_norm_pre']), _ = _rows_bwd(
        "resid2_bwd", _f_resid, [(h1, D, 0), (xo, D, 0)], r2_par, [[(dh2, D, 0)], [(d_fm, D, 0)]], 128,
        [(0, 0), (1, 0)], [D, D], out_dtypes=[F32, BF16])
    d_ox = _mm("mm_d_ox", dxo, full['xa_w_o'], tb=True, b_shards=True)
    xstart('xa_w_o', _mm("mm_dw_o", o_x, dxo, ta=True, out_dtype=BF16, out_shards=True))
    (dq,), (dk_x, dv_x), _ = _rows_bwd("xattn_bwd", _f_xattn, [(q, XW, 0)], [k_x, v_x], [[(d_ox, XW, 0)]], 128,
                                       [(0, 0)], [XW], out_dtypes=[BF16])
    d_cq = _mm("mm_d_cq", dq, full['xa_w_q'], tb=True)
    dw_q = _mm("mm_dw_q", cq, dq, ta=True, out_dtype=BF16)
    xstart('xa_w_q', dw_q.astype(BF16).reshape(4, D // 4, XW))
    dkv = jnp.concatenate([dk_x, dv_x], axis=1)
    d_mn = _mm("mm_d_mn", dkv, full['xa_w_kv'], tb=True)
    dw_kv = _mm("mm_dw_kv", mn, dkv, ta=True, out_dtype=BF16)
    xstart('xa_w_kv', dw_kv.astype(BF16).reshape(4, D // 4, 2 * XW))
    _, (G['xa_norm_mem'],), _ = _rows_bwd("mem_norm_bwd", _f_rms, [(mems, D, 0)], [W['xa_norm_mem']],
                                          [[(d_mn, D, 0)]], 128, [None], [])
    (dx_a, dymix), (G['mix_norm_post'], G['xa_norm_pre']), _ = _rows_bwd(
        "resid1_bwd", _f_resid, [(xs, D, 0), (ymix, D, 0)], r1_par, [[(dh1, D, 0)], [(d_cq, D, 0)]], 128,
        [(0, 0), (1, 0)], [D, D], out_dtypes=[F32, BF16])
    d_merged = _mm("mm_d_merged", dymix, full['w_mix_out'], tb=True)
    dw_out = _mm("mm_dw_out", merged, dymix, ta=True, out_dtype=BF16)
    xstart('w_mix_out', dw_out.astype(BF16).reshape(4, D // 4, D))
    dp = jnp.zeros((T, NP), BF16)
    (dp, d_brw, d_bgdn), _, _ = _rows_bwd(
        "merge_bwd", _f_merge, merge_rows, [], [[(d_merged, D, 0)]], 128, [(0, 0), (0, D), (1, 0), (2, 0)], [2 * D, D, D],
        out_dtypes=[BF16] * 3, alias=(0, dp, O_GRW // (2 * D)))
    d_yrw = _mm("mm_d_yrw", d_brw, full['w_branch_rwkv'], tb=True, b_shards=True)
    xstart('w_branch_rwkv', _mm("mm_dw_brw", y_rw, d_brw, ta=True, out_dtype=BF16, out_shards=True))
    d_ygdn = _mm("mm_d_ygdn", d_bgdn, full['w_branch_gdn'], tb=True, b_shards=True)
    xstart('w_branch_gdn', _mm("mm_dw_bgdn", y_gdn, d_bgdn, ta=True, out_dtype=BF16, out_shards=True))

    (d_y, d_r1, d_k2a, d_v1, d_gate), (d_rk, G['rwkv_gn_w'], G['rwkv_gn_b']), _ = _rows_bwd(
        "rwkv_post_bwd", _f_rwkv_post, post_rows, post_par, [[(d_yrw, RW, 0)]], 64,
        [(0, 0), (1, 0), (2, 0), (3, 0), (4, 0)], [RW] * 5)
    d_r2, d_lwd, d_k2b, d_v2, d_av, d_bv = _rec_bwd("rwkv_rec_bwd", _rwkv_chunk, rec_rw, d_y, s_rw, RW // LANES)
    pre_cts = [[(d_r1, RW, 0), (d_r2, RW, 0)], [(d_lwd, RW, 0)], [(d_k2a, RW, 0), (d_k2b, RW, 0)],
               [(d_v1, RW, 0), (d_v2, RW, 0)], [(d_av, RW, 0)], [(d_bv, RW, 0)], [(d_gate, RW, 0)]]
    pre_out = [(0, 0), (0, RW), (0, 2 * RW), (0, 3 * RW), (0, 3 * RW + LANES), (0, 3 * RW + 2 * LANES)]
    (d_xm,), pre_pg, _ = _rows_bwd("rwkv_pre_bwd", _f_rwkv_pre, pre_rows, rwkv_par, pre_cts, 64, pre_out, [XM])
    G['rwkv_w0'], G['rwkv_a0'], G['rwkv_k_k'], G['rwkv_k_a'], d_wup_p, d_aup_p, d_gup = pre_pg
    dp, d_mix_x = _tokshift_bwd(d_xm, p, mix_x, dp, n_main, main_cb, lora_cb)

    (d_ogd, dp), (d_nw_t,), _ = _rows_bwd("gdn_post_bwd", _f_gdn_post, gpost_rows, [nw_t], [[(d_ygdn, GW, 0)]], 64,
                                          [(0, 0), (1, 0)], [GW, GW], out_dtypes=[F32, BF16], alias=(1, dp, O_Z // GW))
    d_qn, d_kn, d_vg, d_betax, d_gx = _rec_bwd("gdn_rec_bwd", _gdn_chunk, rec_gd, d_ogd, s_gd, GW // LANES)
    gpre_cts = [[(d_qn, GW, 0)], [(d_kn, GW, 0)], [(d_vg, GW, 0)], [(d_betax, GW, 0)], [(d_gx, GW, 0)]]
    (d_qkvc, dp), (d_alog_p, d_dt_p), _ = _rows_bwd(
        "gdn_pre_bwd", f_gdn_pre, gpre_rows, [alog_p, dt_p], gpre_cts, 64,
        [(0, 0), (0, GW), (0, 2 * GW), (1, 0)], [3 * GW, LANES], out_dtypes=[F32, BF16], alias=(1, dp, O_BA // LANES))
    dp, d_conv8 = _conv_bwd(d_qkvc, p, conv8, dp, O_GQ // LANES)
    small_g = [_col_shards(d_wup_p[:LW]), _col_shards(d_aup_p[:LA]), _col_shards(d_gup), _col_shards(d_conv8[:4])]
    xstart('small', jnp.stack([pack_small([g[s] for g in small_g]) for s in range(4)]))
    dw_p = _mm("mm_dw_in", u, dp, ta=True, out_dtype=BF16)
    xstart('w_in', _col_shards(unpad_layout(dw_p.astype(BF16))))
    d_u = _mm("mm_d_u", dp, w_p, tb=True)
    (grad_x,), (G['mix_norm_pre'],), _ = _rows_bwd(
        "rms_pre_bwd", _f_rms, [(xs, D, 0)], [W['mix_norm_pre']], [[(d_u, D, 0)]], 128, [(0, 0)], [D],
        adds=[(0, (dx_a, D, 0))])

    G['rwkv_shift_mix'] = xm_unlayout(d_mix_x)
    G['rwkv_r_k'] = d_rk
    G['gdn_a_log'] = d_alog_p[:, NH:2 * NH]
    G['gdn_dt_bias'] = d_dt_p[:, NH:2 * NH]
    G['gdn_norm_w'] = jnp.sum(d_nw_t.reshape(GW // GDN_HEAD, GDN_HEAD), axis=0, keepdims=True)

    res = {}
    for n in xstarted:
        started, own = xstarted[n]
        _, land = _send3_wait("grad_wait_" + n, started, grad_x)
        part = _sum4("sum4_" + n, own, land)
        other = _swap_sibling("swap_" + n, part)
        if n == 'small':
            for sn, pa, pb in zip(SMALL_SHARDED, unpack_small(part), unpack_small(other)):
                res[sn] = _adam_big("adam_" + sn, W[sn], Mo[sn], Vo[sn], pa, pb)
        else:
            res[n] = _adam_big("adam_" + n, W[n], Mo[n], Vo[n], part, other)

    sizes = [-(-W[n].size // LANES) * LANES for n in REPLICATED]
    total = -(-sum(sizes) // (8 * LANES)) * (8 * LANES)

    def pack(vals):
        parts = [jnp.pad(vv.reshape(-1), (0, sz - vv.size)) for vv, sz in zip(vals, sizes)]
        flat = jnp.concatenate(parts)
        return jnp.pad(flat, (0, total - flat.shape[0])).reshape(8, total // 8)

    g8 = _gather8(pack([G[n] for n in REPLICATED]))
    small = _adam_small(pack([W[n] for n in REPLICATED]), pack([Mo[n] for n in REPLICATED]),
                        pack([Vo[n] for n in REPLICATED]), g8)
    small = [s.reshape(-1) for s in small]
    off = 0
    for n, sz in zip(REPLICATED, sizes):
        res[n] = [s[off:off + W[n].size].reshape(W[n].shape) for s in small]
        off += sz

    loss = lax.psum(loss_acc[0, 0], ("x", "y", "c"))
    outs = [loss, grad_x[None]]
    for j in range(4):
        outs += [res[n][j].reshape(given[n].shape) for n in WEIGHTS]
    return tuple(outs)
```

```python
import functools
import math

import jax
import jax.numpy as jnp
from jax import lax
from jax.experimental import pallas as pl
from jax.experimental.pallas import tpu as pltpu

F32 = jnp.float32
BF16 = jnp.bfloat16
MESH = pl.DeviceIdType.MESH

LANES = 128
VMEM_LIMIT = 56 * 1024 * 1024
CHUNK = 64
TIME_BLOCK = 128
REC_GROUP = 16
REC_MODE = '1'
INV_MODE = '1'

NORM_EPS = 1e-6
L2_EPS = 1e-6
RWKV_GN_EPS = 64e-5
RWKV_HEAD = 64
GDN_HEAD = 128
XA_HEAD = 128
ADAM_LR, ADAM_B1, ADAM_B2, ADAM_EPS, ADAM_WD, ADAM_STEP = 0.001, 0.9, 0.999, 1e-08, 0.01, 10

WEIGHTS = ['mix_norm_pre', 'mix_norm_post', 'w_in', 'rwkv_shift_mix', 'rwkv_w0', 'rwkv_w_up', 'rwkv_a0', 'rwkv_a_up',
           'rwkv_g_up', 'rwkv_k_k', 'rwkv_k_a', 'rwkv_r_k', 'rwkv_gn_w', 'rwkv_gn_b', 'gdn_conv_w', 'gdn_a_log',
           'gdn_dt_bias', 'gdn_norm_w', 'w_branch_rwkv', 'w_branch_gdn', 'w_mix_out', 'xa_norm_pre', 'xa_norm_mem',
           'xa_norm_post', 'xa_w_q', 'xa_w_kv', 'xa_w_o', 'mlp_norm_pre', 'mlp_norm_post', 'mlp_w_up', 'mlp_w_down']
COL_SHARDED = ['w_in', 'rwkv_w_up', 'rwkv_a_up', 'rwkv_g_up', 'gdn_conv_w', 'w_branch_rwkv', 'w_branch_gdn', 'xa_w_o',
               'mlp_w_up']
ROW_SHARDED = ['w_mix_out', 'xa_w_q', 'xa_w_kv', 'mlp_w_down']
SHARDED = COL_SHARDED + ROW_SHARDED
SMALL_SHARDED = ['rwkv_w_up', 'rwkv_a_up', 'rwkv_g_up', 'gdn_conv_w']
BIG_SHARDED = [n for n in SHARDED if n not in SMALL_SHARDED]
REPLICATED = [n for n in WEIGHTS if n not in SHARDED]


ANY = pl.BlockSpec(memory_space=pl.ANY)
_PENDING = []


def _pcall(body, **kw):
    deps = tuple(_PENDING) if "in_specs" in kw else ()
    if not deps:
        return pl.pallas_call(body, **kw)
    del _PENDING[:]
    n_in = len(kw["in_specs"])
    kw["in_specs"] = list(kw["in_specs"]) + [ANY] * len(deps)

    def with_deps(*refs):
        return body(*refs[:n_in], *refs[n_in + len(deps):])

    call = pl.pallas_call(with_deps, **kw)
    return lambda *args: call(*args, *deps)


def _params(sem=None):
    return pltpu.CompilerParams(dimension_semantics=sem, vmem_limit_bytes=VMEM_LIMIT)


def _pick(n, pref, mult=LANES):
    t = (min(n, pref) // mult) * mult
    while t >= mult:
        if n % t == 0:
            return t
        t -= mult
    return n


def _dg(a, b, ca, cb):
    if a.ndim == 3:
        return lax.dot_general(a, b, (((ca + 1,), (cb + 1,)), ((0,), (0,))), preferred_element_type=F32)
    return lax.dot_general(a, b, (((ca,), (cb,)), ((), ())), preferred_element_type=F32)


def _split(x):
    hi = x.astype(BF16)
    lo = (x - hi.astype(F32)).astype(BF16)
    return hi, lo


def _dot_raw(a, b, ca, cb, mode):
    if mode == '1':
        return _dg(a.astype(BF16), b.astype(BF16), ca, cb)
    if mode == 'cb':
        ah, al = _split(a)
        bb = b.astype(BF16)
        return _dg(ah, bb, ca, cb) + _dg(al, bb, ca, cb)
    if mode == 'ca':
        bh, bl = _split(b)
        ab = a.astype(BF16)
        return _dg(ab, bh, ca, cb) + _dg(ab, bl, ca, cb)
    ah, al = _split(a)
    bh, bl = _split(b)
    return _dg(ah, bh, ca, cb) + (_dg(ah, bl, ca, cb) + _dg(al, bh, ca, cb))


@functools.partial(jax.custom_vjp, nondiff_argnums=(2, 3, 4))
def _dot_diff(a, b, ca, cb, mode):
    return _dot_raw(a, b, ca, cb, mode)


def _dot_diff_fwd(a, b, ca, cb, mode):
    return _dot_raw(a, b, ca, cb, mode), (a, b)


def _dot_diff_bwd(ca, cb, mode, res, ct):
    a, b = res
    if mode == 'ca':
        da = jnp.zeros_like(a)
    else:
        bmode = {'1': '1', '3': '3', 'cb': 'cb'}[mode]
        if ca == 1:
            da = _dot_raw(ct, b, 1, 1 if cb == 0 else 0, bmode)
        else:
            da = _dot_raw(b, ct, 1 if cb == 0 else 0, 1, 'ca' if bmode == 'cb' else bmode)
    if mode == 'cb':
        db = jnp.zeros_like(b)
    else:
        amode = {'1': '1', '3': '3', 'ca': 'ca'}[mode]
        if cb == 0:
            db = _dot_raw(a, ct, 0 if ca == 1 else 1, 0, amode)
        else:
            db = _dot_raw(ct, a, 0, 0 if ca == 1 else 1, 'cb' if amode == 'ca' else amode)
    return da, db


_dot_diff.defvjp(_dot_diff_fwd, _dot_diff_bwd)


class _Ops:
    def __init__(self, diff):
        self.diff = diff

    def dot(self, a, b, ca=1, cb=0, mode='1'):
        if self.diff:
            return _dot_diff(a, b, ca, cb, mode)
        return _dot_raw(a, b, ca, cb, mode)


_RAW = _Ops(False)
_DIFF = _Ops(True)


def _sigmoid(x):
    return 1.0 / (1.0 + jnp.exp(-x))


def _softplus(x):
    return jnp.maximum(x, 0.0) + jnp.log(1.0 + jnp.exp(-jnp.abs(x)))


def _rms(x, g, eps=NORM_EPS):
    return x * lax.rsqrt(jnp.mean(x * x, axis=-1, keepdims=True) + eps) * g


def _head_matrix(width, head):
    ch = lax.broadcasted_iota(jnp.int32, (width, LANES), 0)
    hh = lax.broadcasted_iota(jnp.int32, (width, LANES), 1)
    return (lax.shift_right_logical(ch, int(math.log2(head))) == hh).astype(BF16)


def _headsum(ops, x, head):
    e = _head_matrix(x.shape[-1], head)
    s = ops.dot(x, e, 1, 0, 'cb')
    return ops.dot(s, e, 1, 1, 'cb')


def _f_rms(ops, x, g):
    return (_rms(x, g),)


def _f_resid(ops, h, y, g_post, g_next):
    h1 = h + _rms(y, g_post)
    return h1, _rms(h1, g_next)


def _f_final(ops, h, y, tgt, g_post):
    e = h + _rms(y, g_post) - tgt
    return (0.5 * jnp.mean(e * e, axis=-1, keepdims=True),)


def _f_merge(ops, g_rw, g_gdn, b_rw, b_gdn):
    return (_sigmoid(g_rw) * b_rw + _sigmoid(g_gdn) * b_gdn,)


def _f_rwkv_pre(ops, xr, xk, xv, xlw, xla, xlg, w0, a0, k_k, k_a, w_up, a_up, g_up):
    z = w0 + ops.dot(jnp.tanh(xlw), w_up)
    lwd = -jnp.exp(-_softplus(-z) - 0.5)
    a = _sigmoid(a0 + ops.dot(xla, a_up))
    gate = ops.dot(_sigmoid(xlg), g_up)
    kk = xk * k_k
    kk = kk * lax.rsqrt(_headsum(ops, kk * kk, RWKV_HEAD) + L2_EPS)
    k2 = xk * (1.0 + (a - 1.0) * k_a)
    return xr, lwd, k2, xv, -kk, kk * a, gate


def _f_rwkv_post(ops, y, r, k2, v, gate, r_k, gn_w, gn_b):
    inv = 1.0 / RWKV_HEAD
    yc = y - _headsum(ops, y, RWKV_HEAD) * inv
    var = _headsum(ops, yc * yc, RWKV_HEAD) * inv
    yn = yc * lax.rsqrt(var + RWKV_GN_EPS) * gn_w + gn_b
    bonus = _headsum(ops, r * k2 * r_k, RWKV_HEAD) * v
    return ((yn + bonus) * gate,)


def _f_gdn_pre(ops, qc, kc, vc, ba, alog_p, dt_p, *, heads):
    lane = lax.broadcasted_iota(jnp.int32, (1, LANES), 1)
    beta = _sigmoid(ba)
    g = -jnp.exp(alog_p) * _softplus(ba + dt_p)
    gb = jnp.where(lane < heads, beta, g)
    width = qc.shape[-1]
    row = lax.broadcasted_iota(jnp.int32, (LANES, width), 0)
    col_head = lax.shift_right_logical(lax.broadcasted_iota(jnp.int32, (LANES, width), 1), int(math.log2(GDN_HEAD)))
    beta_x = ops.dot(gb, (row == col_head).astype(BF16), 1, 0, 'cb')
    g_x = ops.dot(gb, (row == col_head + heads).astype(BF16), 1, 0, 'cb')
    qn = qc * lax.rsqrt(_headsum(ops, qc * qc, GDN_HEAD) + L2_EPS)
    kn = kc * lax.rsqrt(_headsum(ops, kc * kc, GDN_HEAD) + L2_EPS)
    return qn, kn, vc, beta_x, g_x


def _f_gdn_post(ops, o, z, nw):
    ms = _headsum(ops, o * o, GDN_HEAD) * (1.0 / GDN_HEAD)
    return (o * lax.rsqrt(ms + NORM_EPS) * nw * (z * _sigmoid(z)),)


def _f_xattn(ops, q, k, v):
    width = q.shape[-1]
    lane = lax.broadcasted_iota(jnp.int32, (1, width), 1)
    out = jnp.zeros_like(q)
    for h in range(width // XA_HEAD):
        m = ((lane >= h * XA_HEAD) & (lane < (h + 1) * XA_HEAD)).astype(F32)
        s = ops.dot(q * m, k, 1, 1) * (XA_HEAD ** -0.5)
        s = s - jnp.max(s, axis=-1, keepdims=True)
        e = jnp.exp(s)
        pr = e / jnp.sum(e, axis=-1, keepdims=True)
        out = out + ops.dot(pr, v) * m
    return (out,)


def _inv_unit_lower(ops, a):
    n = a.shape[-1]
    ii = lax.broadcasted_iota(jnp.int32, (n, n), 0)
    jj = lax.broadcasted_iota(jnp.int32, (n, n), 1)
    p = (ii == jj).astype(F32) + a
    q = a
    k = 2
    while k < n:
        q = ops.dot(q, q, 1, 0, INV_MODE)
        p = p + ops.dot(p, q, 1, 0, INV_MODE)
        k *= 2
    return p


def _const(mask, batch):
    return jnp.broadcast_to(mask.astype(BF16), (batch,) + mask.shape)


def _rwkv_chunk(ops, s, r, lw, k, v, a, b):
    g, c, _ = r.shape
    ii = lax.broadcasted_iota(jnp.int32, (c, c), 0)
    jj = lax.broadcasted_iota(jnp.int32, (c, c), 1)
    incl = ii >= jj
    strict = ii > jj
    cum = ops.dot(_const(incl, g), lw, 1, 0, 'ca')
    tot = ops.dot(jnp.ones((g, LANES, c), BF16), lw, 1, 0, 'ca')
    w_inv = jnp.exp(-cum)
    at = a * jnp.exp(cum - lw)
    bt = b * w_inv
    kt = k * w_inv
    rt = r * jnp.exp(cum)
    lane = lax.broadcasted_iota(jnp.int32, (2, 1, 1, LANES), 3)
    head = lax.broadcasted_iota(jnp.int32, (2, 1, 1, LANES), 0)
    hm = ((lane >= RWKV_HEAD) == (head == 1)).astype(F32)

    def per_head(x):
        return (x[None] * hm).reshape(2 * g, c, x.shape[-1])

    def both(x):
        return jnp.broadcast_to(x[None], (2,) + x.shape).reshape((2 * g,) + x.shape[1:])

    def merge(x2):
        return jnp.sum(x2.reshape(2, g, c, LANES) * hm, axis=0)

    bt2, kt2, v2 = both(bt), both(kt), both(v)
    atm, rtm = per_head(at), per_head(rt)
    base = ops.dot(at, s, 1, 1, REC_MODE)
    a_ab = jnp.where(strict, ops.dot(atm, bt2, 1, 1, REC_MODE), 0.0)
    a_ak = jnp.where(strict, ops.dot(atm, kt2, 1, 1, REC_MODE), 0.0)
    a_rb = jnp.where(incl, ops.dot(rtm, bt2, 1, 1, REC_MODE), 0.0)
    a_rk = jnp.where(incl, ops.dot(rtm, kt2, 1, 1, REC_MODE), 0.0)
    t = _inv_unit_lower(ops, a_ab)
    u = merge(ops.dot(t, both(base) + ops.dot(a_ak, v2, 1, 0, REC_MODE), 1, 0, REC_MODE))
    y = ops.dot(rt, s, 1, 1, REC_MODE) + merge(ops.dot(a_rb, both(u), 1, 0, REC_MODE) + ops.dot(a_rk, v2, 1, 0, REC_MODE))
    bi = lax.broadcasted_iota(jnp.int32, (LANES, LANES), 0) >= RWKV_HEAD
    bj = lax.broadcasted_iota(jnp.int32, (LANES, LANES), 1) >= RWKV_HEAD
    upd = jnp.where(bi == bj, ops.dot(u, bt, 0, 0, REC_MODE) + ops.dot(v, kt, 0, 0, REC_MODE), 0.0)
    return (s + upd) * jnp.exp(tot), y


def _gdn_chunk(ops, s, q, k, v, beta, g):
    nb, c, _ = q.shape
    ii = lax.broadcasted_iota(jnp.int32, (c, c), 0)
    jj = lax.broadcasted_iota(jnp.int32, (c, c), 1)
    causal = ii >= jj
    strict = ii > jj
    q = q * (GDN_HEAD ** -0.5)
    gc = ops.dot(_const(causal, nb), g, 1, 0, 'ca')
    ones = jnp.ones((nb, c, LANES), BF16)
    gc_col = ops.dot(gc, ones, 1, 1, 'cb') * (1.0 / LANES)
    gc_row = ops.dot(ones, gc, 1, 1, 'ca') * (1.0 / LANES)
    decay = jnp.where(causal, jnp.exp(jnp.where(causal, gc_col - gc_row, 0.0)), 0.0)
    kb = k * beta
    lmat = jnp.where(strict, ops.dot(kb, k, 1, 1, REC_MODE) * decay, 0.0)
    tinv = _inv_unit_lower(ops, -lmat)
    egc = jnp.exp(gc)
    u = ops.dot(tinv, v * beta, 1, 0, REC_MODE)
    w = ops.dot(tinv, kb * egc, 1, 0, REC_MODE)
    a_intra = jnp.where(causal, ops.dot(q, k, 1, 1, REC_MODE) * decay, 0.0)
    g_tot = ops.dot(jnp.ones((nb, c, c), BF16), g, 1, 0, 'ca')
    k_dec = k * jnp.exp(g_tot - gc)
    v_new = u - ops.dot(w, s, 1, 0, REC_MODE)
    o = ops.dot(q * egc, s, 1, 0, REC_MODE) + ops.dot(a_intra, v_new, 1, 0, REC_MODE)
    g_full = ops.dot(jnp.ones((nb, LANES, c), BF16), g, 1, 0, 'ca')
    s_new = s * jnp.exp(g_full) + ops.dot(k_dec, v_new, 0, 0, REC_MODE)
    return s_new, o


def _mm(name, a, b, ta=False, tb=False, out_dtype=F32, tm=1024, tn=1024, tk=4096, epilogue=None, extra=(),
        b_shards=False, out_shards=False):
    m = a.shape[1] if ta else a.shape[0]
    kd = a.shape[0] if ta else a.shape[1]
    ns = b.shape[2] if b_shards else None
    b_rows, b_cols = (b.shape[1], 4 * ns) if b_shards else b.shape
    n = b_rows if tb else b_cols
    assert kd == (b_cols if tb else b_rows), (name, a.shape, b.shape)
    n_unit = ns if (b_shards and not tb) else (n // 4 if out_shards else n)
    k_unit = ns if (b_shards and tb) else kd
    tm, tn, tk = _pick(m, tm), _pick(n_unit, tn), _pick(k_unit, tk)
    nk = kd // tk
    n_x = len(extra)
    out_dtypes = out_dtype if epilogue is not None else (out_dtype,)

    def body(a_ref, b_ref, *rest):
        x_refs, o_refs, acc = rest[:n_x], rest[n_x:-1], rest[-1]
        kk = pl.program_id(2)
        part = _dg(a_ref[...].astype(BF16), b_ref[...].astype(BF16), 0 if ta else 1, 1 if tb else 0)

        def finish(z):
            vals = (z,) if epilogue is None else epilogue(z, *[xr[...] for xr in x_refs])
            for o_ref, val in zip(o_refs, vals):
                o_ref[...] = val.astype(o_ref.dtype)

        if nk == 1:
            finish(part)
            return

        @pl.when(kk == 0)
        def _():
            acc[...] = part

        @pl.when((kk > 0) & (kk < nk - 1))
        def _():
            acc[...] += part

        @pl.when(kk == nk - 1)
        def _():
            finish(acc[...] + part)

    a_spec = pl.BlockSpec((tk, tm), lambda i, j, k: (k, i)) if ta else pl.BlockSpec((tm, tk), lambda i, j, k: (i, k))
    if not b_shards:
        b_spec = pl.BlockSpec((tn, tk), lambda i, j, k: (j, k)) if tb else pl.BlockSpec((tk, tn), lambda i, j, k: (k, j))
    elif tb:
        per_k = ns // tk
        b_spec = pl.BlockSpec((None, tn, tk), lambda i, j, k: (k // per_k, j, k % per_k))
    else:
        per_n = ns // tn
        b_spec = pl.BlockSpec((None, tk, tn), lambda i, j, k: (j // per_n, k, j % per_n))
    o_spec = pl.BlockSpec((tm, tn), lambda i, j, k: (i, j))
    if out_shards:
        per_o = (n // 4) // tn
        res_spec = [pl.BlockSpec((None, tm, tn), lambda i, j, k: (j // per_o, i, j % per_o))]
        res_shape = [jax.ShapeDtypeStruct((4, m, n // 4), out_dtypes[0])]
    else:
        res_spec = [o_spec] * len(out_dtypes)
        res_shape = [jax.ShapeDtypeStruct((m, n), dt) for dt in out_dtypes]
    res = _pcall(
        body, name=name, grid=(m // tm, n // tn, nk), in_specs=[a_spec, b_spec] + [o_spec] * n_x,
        out_specs=res_spec, out_shape=res_shape,
        scratch_shapes=[pltpu.VMEM((tm, tn), F32)],
        compiler_params=_params(("parallel", "parallel", "arbitrary")))(a, b, *extra)
    return res if epilogue is not None else res[0]


def _row_spec(tm, width, cb):
    return pl.BlockSpec((tm, width), lambda i: (i, cb))


def _rows_fwd(name, f, rows, params, outs, tm):
    t = rows[0][0].shape[0]
    tm = min(tm, t)
    n_in = len(rows) + len(params)
    kept = [o for o in outs if o is not None]

    def body(*refs):
        vals = f(_RAW, *[r[...].astype(F32) for r in refs[:n_in]])
        o_refs = refs[n_in:]
        j = 0
        for val, o in zip(vals, outs):
            if o is not None:
                o_refs[j][...] = val.astype(o_refs[j].dtype)
                j += 1

    in_specs = [_row_spec(tm, w, cb) for (_, w, cb) in rows]
    in_specs += [pl.BlockSpec(p.shape, lambda i: (0, 0)) for p in params]
    res = _pcall(
        body, name=name, grid=(t // tm,), in_specs=in_specs,
        out_specs=[_row_spec(tm, w, 0) for (w, _) in kept],
        out_shape=[jax.ShapeDtypeStruct((t, w), dt) for (w, dt) in kept],
        compiler_params=_params(("parallel",)))(*[r[0] for r in rows], *params)
    return res


def _rows_bwd(name, f, rows, params, cts, tm, row_out, out_widths, adds=(), loss=False, out_dtypes=None, alias=None):
    t = rows[0][0].shape[0]
    tm = min(tm, t)
    n_r, n_p = len(rows), len(params)
    ct_flat = [c for cl in cts for c in cl]
    n_ct, n_add = len(ct_flat), len(adds)
    n_out = len(out_widths)

    def body(*refs):
        i = pl.program_id(0)
        ins = [r[...].astype(F32) for r in refs[:n_r + n_p]]
        ct_refs = refs[n_r + n_p:n_r + n_p + n_ct]
        add_refs = refs[n_r + n_p + n_ct:n_r + n_p + n_ct + n_add]
        o_refs = refs[n_r + n_p + n_ct + n_add + (alias is not None):]
        outs, vjp = jax.vjp(lambda *a: f(_DIFF, *a), *ins)
        ctv, pos = [], 0
        for o, cl in zip(outs, cts):
            if loss:
                ctv.append(jnp.ones_like(o))
            elif not cl:
                ctv.append(jnp.zeros_like(o))
            else:
                acc = ct_refs[pos][...].astype(F32)
                for q in range(1, len(cl)):
                    acc = acc + ct_refs[pos + q][...].astype(F32)
                ctv.append(acc)
            pos += len(cl)
        grads = vjp(tuple(ctv))
        for ri, ro in enumerate(row_out):
            if ro is not None:
                oi, off = ro
                val = grads[ri]
                for ai, (aoi, _) in enumerate(adds):
                    if aoi == oi:
                        val = val + add_refs[ai][...].astype(F32)
                o_refs[oi][:, off:off + val.shape[1]] = val.astype(o_refs[oi].dtype)
        p_refs = o_refs[n_out:n_out + n_p]

        @pl.when(i == 0)
        def _():
            for pr in p_refs:
                pr[...] = jnp.zeros_like(pr)
            if loss:
                o_refs[n_out + n_p][...] = jnp.zeros((8, LANES), F32)

        for pi, pr in enumerate(p_refs):
            pr[...] += grads[n_r + pi]
        if loss:
            o_refs[n_out + n_p][...] += jnp.sum(outs[0])

    in_specs = [_row_spec(tm, w, cb) for (_, w, cb) in rows]
    in_specs += [pl.BlockSpec(p.shape, lambda i: (0, 0)) for p in params]
    in_specs += [_row_spec(tm, w, cb) for (_, w, cb) in ct_flat]
    in_specs += [_row_spec(tm, w, cb) for (_, (_, w, cb)) in adds]
    out_specs = [_row_spec(tm, w, 0) for w in out_widths]
    out_specs += [pl.BlockSpec(p.shape, lambda i: (0, 0)) for p in params]
    out_dtypes = out_dtypes or [F32] * n_out
    out_shape = [jax.ShapeDtypeStruct((t, w), dt) for w, dt in zip(out_widths, out_dtypes)]
    out_shape += [jax.ShapeDtypeStruct(p.shape, F32) for p in params]
    if loss:
        out_specs.append(pl.BlockSpec((8, LANES), lambda i: (0, 0)))
        out_shape.append(jax.ShapeDtypeStruct((8, LANES), F32))
    extra, aliases = [], {}
    if alias is not None:
        oi, buf, cb = alias
        in_specs.append(ANY)
        out_specs[oi] = _row_spec(tm, out_widths[oi], cb)
        out_shape[oi] = jax.ShapeDtypeStruct(buf.shape, buf.dtype)
        extra, aliases = [buf], {n_r + n_p + n_ct + n_add: oi}
    res = _pcall(
        body, name=name, grid=(t // tm,), in_specs=in_specs, out_specs=out_specs, out_shape=out_shape,
        input_output_aliases=aliases, compiler_params=_params(("arbitrary",)))(
            *[r[0] for r in rows], *params, *[c[0] for c in ct_flat], *[a[1][0] for a in adds], *extra)
    return res[:n_out], res[n_out:n_out + n_p], res[n_out + n_p:]


def _shift_down(x, k):
    row = lax.broadcasted_iota(jnp.int32, x.shape, 0)
    return jnp.where(row >= k, pltpu.roll(x, k, 0), 0.0)


def _shift_up(x, k):
    t = x.shape[0]
    row = lax.broadcasted_iota(jnp.int32, x.shape, 0)
    return jnp.where(row < t - k, pltpu.roll(x, t - k, 0), 0.0)


def _tokshift_fwd(p, mix, n_main, main_cb, lora_cb):
    t = p.shape[0]
    nblk = mix.shape[1] // LANES

    def src(j):
        return (0, jnp.where(j < n_main, j + main_cb, j - n_main + lora_cb))

    def body(p_ref, mix_ref, o_ref):
        xv = p_ref[...].astype(F32)
        o_ref[...] = xv + (_shift_down(xv, 1) - xv) * mix_ref[...]

    return _pcall(
        body, name="tokshift_fwd", grid=(nblk,),
        in_specs=[pl.BlockSpec((t, LANES), src), pl.BlockSpec((1, LANES), lambda j: (0, j))],
        out_specs=pl.BlockSpec((t, LANES), lambda j: (0, j)),
        out_shape=jax.ShapeDtypeStruct((t, mix.shape[1]), F32), compiler_params=_params(("parallel",)))(p, mix)


def _tokshift_bwd(dxm, p, mix, dp_buf, n_main, main_cb, lora_cb):
    t = p.shape[0]
    nblk = mix.shape[1] // LANES

    def src(j):
        return (0, jnp.where(j < n_main, j + main_cb, j - n_main + lora_cb))

    def body(d_ref, p_ref, mix_ref, buf_ref, dp_ref, dmix_ref):
        xv, d, mx = p_ref[...].astype(F32), d_ref[...], mix_ref[...]
        dp_ref[...] = (d * (1.0 - mx) + _shift_up(d * mx, 1)).astype(dp_ref.dtype)
        dmix_ref[...] = jnp.sum(d * (_shift_down(xv, 1) - xv), axis=0, keepdims=True)

    return _pcall(
        body, name="tokshift_bwd", grid=(nblk,),
        in_specs=[pl.BlockSpec((t, LANES), lambda j: (0, j)), pl.BlockSpec((t, LANES), src),
                  pl.BlockSpec((1, LANES), lambda j: (0, j)), ANY],
        out_specs=[pl.BlockSpec((t, LANES), src), pl.BlockSpec((1, LANES), lambda j: (0, j))],
        out_shape=[jax.ShapeDtypeStruct(dp_buf.shape, dp_buf.dtype), jax.ShapeDtypeStruct((1, mix.shape[1]), F32)],
        input_output_aliases={3: 0},
        compiler_params=_params(("parallel",)))(dxm, p, mix, dp_buf)


def _conv_pre(xv, w):
    return (w[3:4, :] * xv + w[2:3, :] * _shift_down(xv, 1) + w[1:2, :] * _shift_down(xv, 2)
            + w[0:1, :] * _shift_down(xv, 3))


def _conv_fwd(p, w8, cb0):
    t = p.shape[0]
    width = w8.shape[1]

    def body(p_ref, w_ref, o_ref):
        c = _conv_pre(p_ref[...].astype(F32), w_ref[...])
        o_ref[...] = c * _sigmoid(c)

    return _pcall(
        body, name="conv_fwd", grid=(width // LANES,),
        in_specs=[pl.BlockSpec((t, LANES), lambda j: (0, j + cb0)), pl.BlockSpec((8, LANES), lambda j: (0, j))],
        out_specs=pl.BlockSpec((t, LANES), lambda j: (0, j)),
        out_shape=jax.ShapeDtypeStruct((t, width), F32), compiler_params=_params(("parallel",)))(p, w8)


def _conv_bwd(dy, p, w8, dp_buf, cb0):
    t = p.shape[0]
    width = w8.shape[1]

    def body(d_ref, p_ref, w_ref, buf_ref, dx_ref, dw_ref):
        xv, w = p_ref[...].astype(F32), w_ref[...]
        c = _conv_pre(xv, w)
        sg = _sigmoid(c)
        dc = d_ref[...] * (sg * (1.0 + c * (1.0 - sg)))
        dx_ref[...] = (w[3:4, :] * dc + w[2:3, :] * _shift_up(dc, 1) + w[1:2, :] * _shift_up(dc, 2)
                       + w[0:1, :] * _shift_up(dc, 3)).astype(dx_ref.dtype)
        row =lax.broadcasted_iota(jnp.int32, (8, LANES), 0)
        dw = jnp.zeros((8, LANES), F32)
        for j in range(4):
            sj = jnp.sum(dc * (xv if j == 3 else _shift_down(xv, 3 - j)), axis=0, keepdims=True)
            dw = dw + jnp.where(row == j, sj, 0.0)
        dw_ref[...] = dw

    return _pcall(
        body, name="conv_bwd", grid=(width // LANES,),
        in_specs=[pl.BlockSpec((t, LANES), lambda j: (0, j)), pl.BlockSpec((t, LANES), lambda j: (0, j + cb0)),
                  pl.BlockSpec((8, LANES), lambda j: (0, j)), ANY],
        out_specs=[pl.BlockSpec((t, LANES), lambda j: (0, j + cb0)), pl.BlockSpec((8, LANES), lambda j: (0, j))],
        out_shape=[jax.ShapeDtypeStruct(dp_buf.shape, dp_buf.dtype), jax.ShapeDtypeStruct((8, width), F32)],
        input_output_aliases={3: 0},
        compiler_params=_params(("parallel",)))(dy, p, w8, dp_buf)


def _rec_fwd(name, chunk_fn, ins, nblocks):
    t = ins[0][0].shape[0]
    tb = min(TIME_BLOCK, t)
    nt, ncb = t // tb, tb // CHUNK
    n_in = len(ins)
    grp = REC_GROUP if nblocks % REC_GROUP == 0 and all(c0 % REC_GROUP == 0 for (_, c0) in ins) else 1
    width = grp * LANES

    def body(*refs):
        in_refs, y_ref, sall_ref, s_scr = refs[:n_in], refs[n_in], refs[n_in + 1], refs[n_in + 2]

        @pl.when(pl.program_id(1) == 0)
        def _():
            s_scr[...] = jnp.zeros_like(s_scr)

        def step(ci, s):
            sl = pl.ds(pl.multiple_of(ci * CHUNK, CHUNK), CHUNK)
            lanes = [slice(gi * LANES, (gi + 1) * LANES) for gi in range(grp)]
            s1, y = chunk_fn(_RAW, s, *[jnp.stack([r[sl, ln] for ln in lanes]) for r in in_refs])
            for gi, ln in enumerate(lanes):
                sall_ref[gi, ci] = s[gi]
                y_ref[sl, ln] = y[gi]
            return s1

        s_scr[...] = lax.fori_loop(0, ncb, step, s_scr[...])

    in_specs = [pl.BlockSpec((tb, width), functools.partial(lambda h, tt, cb: (tt, cb + h), cb=c0 // grp))
                for (_, c0) in ins]
    return _pcall(
        body, name=name, grid=(nblocks // grp, nt), in_specs=in_specs,
        out_specs=[pl.BlockSpec((tb, width), lambda h, tt: (tt, h)),
                   pl.BlockSpec((grp, ncb, LANES, LANES), lambda h, tt: (h, tt, 0, 0))],
        out_shape=[jax.ShapeDtypeStruct((t, nblocks * LANES), F32),
                   jax.ShapeDtypeStruct((nblocks, t // CHUNK, LANES, LANES), F32)],
        scratch_shapes=[pltpu.VMEM((grp, LANES, LANES), F32)],
        compiler_params=_params(("parallel", "arbitrary")))(*[a for (a, _) in ins])


def _rec_bwd(name, chunk_fn, ins, dy, sall, nblocks):
    t = ins[0][0].shape[0]
    tb = min(TIME_BLOCK, t)
    nt, ncb = t // tb, tb // CHUNK
    n_in = len(ins)
    grp = REC_GROUP if nblocks % REC_GROUP == 0 and all(c0 % REC_GROUP == 0 for (_, c0) in ins) else 1
    width = grp * LANES

    def body(*refs):
        in_refs, dy_ref, sall_ref = refs[:n_in], refs[n_in], refs[n_in + 1]
        d_refs, ds_scr = refs[n_in + 2:2 * n_in + 2], refs[2 * n_in + 2]

        @pl.when(pl.program_id(1) == 0)
        def _():
            ds_scr[...] = jnp.zeros_like(ds_scr)

        def step(j, ds):
            ci = ncb - 1 - j
            sl = pl.ds(pl.multiple_of(ci * CHUNK, CHUNK), CHUNK)
            lanes = [slice(gi * LANES, (gi + 1) * LANES) for gi in range(grp)]
            s0 = jnp.stack([sall_ref[gi, ci] for gi in range(grp)])
            vals = [jnp.stack([r[sl, ln] for ln in lanes]) for r in in_refs]
            _, vjp = jax.vjp(lambda *a: chunk_fn(_DIFF, *a), s0, *vals)
            grads = vjp((ds, jnp.stack([dy_ref[sl, ln] for ln in lanes])))
            for dr, gval in zip(d_refs, grads[1:]):
                for gi, ln in enumerate(lanes):
                    dr[sl, ln] = gval[gi]
            return grads[0]

        ds_scr[...] = lax.fori_loop(0, ncb, step, ds_scr[...])

    in_specs = [pl.BlockSpec((tb, width), functools.partial(lambda h, tt, cb: (nt - 1 - tt, cb + h), cb=c0 // grp))
                for (_, c0) in ins]
    in_specs.append(pl.BlockSpec((tb, width), lambda h, tt: (nt - 1 - tt, h)))
    in_specs.append(pl.BlockSpec((grp, ncb, LANES, LANES), lambda h, tt: (h, nt - 1 - tt, 0, 0)))
    return _pcall(
        body, name=name, grid=(nblocks // grp, nt), in_specs=in_specs,
        out_specs=[pl.BlockSpec((tb, width), lambda h, tt: (nt - 1 - tt, h)) for _ in ins],
        out_shape=[jax.ShapeDtypeStruct((t, nblocks * LANES), F32) for _ in ins],
        scratch_shapes=[pltpu.VMEM((grp, LANES, LANES), F32)],
        compiler_params=_params(("parallel", "arbitrary")))(*[a for (a, _) in ins], dy, sall)


def _pick_rows(r, c):
    tr = max(16, ((1 << 18) // c) // 16 * 16)
    while tr > 16 and r % tr:
        tr -= 16
    return tr if r % tr == 0 else r


def _sum4(name, own, r3):
    _, r, c = r3.shape
    tr = _pick_rows(r, c)

    def body(own_ref, r_ref, o_ref):
        total = ((own_ref[...].astype(F32) + r_ref[0].astype(F32)) + r_ref[1].astype(F32)) + r_ref[2].astype(F32)
        o_ref[...] = total.astype(o_ref.dtype)

    return _pcall(
        body, name=name, grid=(r // tr,),
        in_specs=[pl.BlockSpec((tr, c), lambda i: (i, 0)), pl.BlockSpec((3, tr, c), lambda i: (0, i, 0))],
        out_specs=pl.BlockSpec((tr, c), lambda i: (i, 0)), out_shape=jax.ShapeDtypeStruct((r, c), BF16),
        compiler_params=_params(("parallel",)))(own, r3)


def _adam_math(w, g, m, v):
    m = ADAM_B1 * m + (1.0 - ADAM_B1) * g
    v = ADAM_B2 * v + (1.0 - ADAM_B2) * (g * g)
    m_hat = m / (1.0 - ADAM_B1 ** ADAM_STEP)
    v_hat = v / (1.0 - ADAM_B2 ** ADAM_STEP)
    return -ADAM_LR * (m_hat / (jnp.sqrt(v_hat) + ADAM_EPS) + ADAM_WD * w), m, v


def _adam_big(name, w, m, v, ga, gb):
    r, c = w.shape
    tr = _pick_rows(r, c)

    def body(w_ref, m_ref, v_ref, ga_ref, gb_ref, g_ref, d_ref, mo_ref, vo_ref):
        g = ga_ref[...].astype(F32) + gb_ref[...].astype(F32)
        g_ref[...] = g
        d_ref[...], mo_ref[...], vo_ref[...] = _adam_math(w_ref[...], g, m_ref[...], v_ref[...])

    spec = pl.BlockSpec((tr, c), lambda i: (i, 0))
    return _pcall(
        body, name=name, grid=(r // tr,), in_specs=[spec] * 5, out_specs=[spec] * 4,
        out_shape=[jax.ShapeDtypeStruct((r, c), F32)] * 4, compiler_params=_params(("parallel",)))(w, m, v, ga, gb)


def _adam_small(w, m, v, g8):
    shape = w.shape

    def body(w_ref, m_ref, v_ref, g8_ref, g_ref, d_ref, mo_ref, vo_ref):
        g = g8_ref[0]
        for dev in range(1, 8):
            g = g + g8_ref[dev]
        g_ref[...] = g
        d_ref[...], mo_ref[...], vo_ref[...] = _adam_math(w_ref[...], g, m_ref[...], v_ref[...])

    return _pcall(
        body, name="adam_small", out_shape=[jax.ShapeDtypeStruct(shape, F32)] * 4,
        compiler_params=_params())(w, m, v, g8)


def _place():
    x, y, c = lax.axis_index("x"), lax.axis_index("y"), lax.axis_index("c")
    return x, y, c, [(1 - x, y), (x, 1 - y), (1 - x, 1 - y)]


def _remote(src, dst, send_sems, recv_sems, k, dev):
    return pltpu.make_async_remote_copy(src_ref=src, dst_ref=dst, send_sem=send_sems.at[k], recv_sem=recv_sems.at[k],
                                        device_id=dev, device_id_type=MESH)


HBM = pl.BlockSpec(memory_space=pltpu.HBM)
SEM = pl.BlockSpec(memory_space=pltpu.SEMAPHORE)


def _send3_start(name, src, land, mode):
    land_arr = land if mode == 'forward_half' else lax.empty(land, src.dtype)
    half = land_arr.shape[1] // 2

    def body(src_ref, land_ref, send_sem, recv_sem, src_thru, land_thru, token):
        x, y, c, chips = _place()
        rows = pl.ds(pl.multiple_of(c * half, half), half)
        for k, (px, py) in enumerate(chips):
            dev = (px, py, c)
            if mode == 'gather':
                s_k, d_k = src_ref, land_ref.at[2 * x + y]
            elif mode == 'gather_half':
                s_k, d_k = src_ref.at[rows], land_ref.at[2 * x + y, rows]
            elif mode == 'forward_half':
                s_k = d_k = land_ref.at[2 * px + py, rows]
                dev = (x, y, 1 - c)
            else:
                s_k, d_k = src_ref.at[2 * px + py], land_ref.at[k]
            pltpu.make_async_remote_copy(src_ref=s_k, dst_ref=d_k, send_sem=send_sem, recv_sem=recv_sem,
                                         device_id=dev, device_id_type=MESH).start()
        token[...] = jnp.zeros_like(token)

    return pl.pallas_call(
        body, name=name,
        out_shape=(pltpu.SemaphoreType.DMA(()), pltpu.SemaphoreType.DMA(()), pltpu.HBM(src.shape, src.dtype),
                   pltpu.HBM(land_arr.shape, src.dtype), jax.ShapeDtypeStruct((8, LANES), F32)),
        in_specs=(HBM, HBM), out_specs=(SEM, SEM, HBM, HBM, pl.BlockSpec(memory_space=pltpu.VMEM)),
        input_output_aliases={0: 2, 1: 3},
        compiler_params=pltpu.CompilerParams(has_side_effects=pltpu.SideEffectType.DATAFLOW_SIDE_EFFECTING),
    )(pltpu.with_memory_space_constraint(src, pltpu.HBM), pltpu.with_memory_space_constraint(land_arr, pltpu.HBM))


def _send3_wait(name, started, after, halves=False):
    send_sem, recv_sem, src_thru, land_thru, _ = started
    half = land_thru.shape[1] // 2

    def body(src_ref, land_ref, send_sem, recv_sem, after_ref, src_dead, got_ref):
        x, y, c, chips = _place()
        three = land_ref.at[pl.ds(0, 3), pl.ds(0, half)] if halves else land_ref.at[pl.ds(0, 3)]
        cp = pltpu.make_async_remote_copy(src_ref=three, dst_ref=three, send_sem=send_sem, recv_sem=recv_sem,
                                          device_id=(chips[0][0], chips[0][1], c), device_id_type=MESH)
        cp.wait_send()
        cp.wait_recv()

    return pl.pallas_call(
        body, name=name,
        out_shape=(pltpu.HBM(src_thru.shape, src_thru.dtype), pltpu.HBM(land_thru.shape, land_thru.dtype)),
        in_specs=(HBM, HBM, SEM, SEM, ANY), out_specs=(HBM, HBM), input_output_aliases={0: 0, 1: 1},
        compiler_params=pltpu.CompilerParams(has_side_effects=pltpu.SideEffectType.DATAFLOW_SIDE_EFFECTING),
    )(src_thru, land_thru, send_sem, recv_sem, after)


def _swap_sibling(name, part):
    def body(src, out, send_sems, recv_sems):
        x, y, c, _ = _place()
        cp = _remote(src, out, send_sems, recv_sems, 0, (x, y, 1 - c))
        cp.start()
        cp.wait_recv()
        cp.wait_send()

    return _pcall(
        body, name=name, in_specs=[ANY], out_specs=ANY, out_shape=jax.ShapeDtypeStruct(part.shape, part.dtype),
        scratch_shapes=[pltpu.SemaphoreType.DMA((1,)), pltpu.SemaphoreType.DMA((1,))])(part)


def _gather8(flat):
    def body(src, out, send_sems, recv_sems, local_sem):
        x, y, c, _ = _place()
        own = pltpu.make_async_copy(src, out.at[4 * x + 2 * y + c], local_sem)
        own.start()
        peers = []
        for mask in range(1, 8):
            px = x ^ ((mask >> 2) & 1)
            py = y ^ ((mask >> 1) & 1)
            pc = c ^ (mask & 1)
            peers.append((px, py, pc))
        sends = [_remote(src, out.at[4 * x + 2 * y + c], send_sems, recv_sems, k, dev) for k, dev in enumerate(peers)]
        for cp in sends:
            cp.start()
        for k, (px, py, pc) in enumerate(peers):
            _remote(src, out.at[4 * px + 2 * py + pc], send_sems, recv_sems, k, (px, py, pc)).wait_recv()
        for cp in sends:
            cp.wait_send()
        own.wait()

    return _pcall(
        body, name="gather8_small", in_specs=[ANY], out_specs=ANY,
        out_shape=jax.ShapeDtypeStruct((8,) + flat.shape, flat.dtype),
        scratch_shapes=[pltpu.SemaphoreType.DMA((7,)), pltpu.SemaphoreType.DMA((7,)), pltpu.SemaphoreType.DMA(())],
    )(flat)


def _pad_cols(a, width):
    return jnp.pad(a, ((0, 0), (0, width - a.shape[1])))


def _col_shards(full):
    r, c = full.shape
    return full.reshape(r, 4, c // 4).transpose(1, 0, 2)


def kernel(x, mem, mix_norm_pre, mix_norm_post, w_in, rwkv_shift_mix, rwkv_w0, rwkv_w_up, rwkv_a0, rwkv_a_up, rwkv_g_up, rwkv_k_k, rwkv_k_a, rwkv_r_k, rwkv_gn_w, rwkv_gn_b, gdn_conv_w, gdn_a_log, gdn_dt_bias, gdn_norm_w, w_branch_rwkv, w_branch_gdn, w_mix_out, xa_norm_pre, xa_norm_mem, xa_norm_post, xa_w_q, xa_w_kv, xa_w_o, mlp_norm_pre, mlp_norm_post, mlp_w_up, mlp_w_down, loss_target, m_mix_norm_pre, m_mix_norm_post, m_w_in, m_rwkv_shift_mix, m_rwkv_w0, m_rwkv_w_up, m_rwkv_a0, m_rwkv_a_up, m_rwkv_g_up, m_rwkv_k_k, m_rwkv_k_a, m_rwkv_r_k, m_rwkv_gn_w, m_rwkv_gn_b, m_gdn_conv_w, m_gdn_a_log, m_gdn_dt_bias, m_gdn_norm_w, m_w_branch_rwkv, m_w_branch_gdn, m_w_mix_out, m_xa_norm_pre, m_xa_norm_mem, m_xa_norm_post, m_xa_w_q, m_xa_w_kv, m_xa_w_o, m_mlp_norm_pre, m_mlp_norm_post, m_mlp_w_up, m_mlp_w_down, v_mix_norm_pre, v_mix_norm_post, v_w_in, v_rwkv_shift_mix, v_rwkv_w0, v_rwkv_w_up, v_rwkv_a0, v_rwkv_a_up, v_rwkv_g_up, v_rwkv_k_k, v_rwkv_k_a, v_rwkv_r_k, v_rwkv_gn_w, v_rwkv_gn_b, v_gdn_conv_w, v_gdn_a_log, v_gdn_dt_bias, v_gdn_norm_w, v_w_branch_rwkv, v_w_branch_gdn, v_w_mix_out, v_xa_norm_pre, v_xa_norm_mem, v_xa_norm_post, v_xa_w_q, v_xa_w_kv, v_xa_w_o, v_mlp_norm_pre, v_mlp_norm_post, v_mlp_w_up, v_mlp_w_down):
    given = dict(locals())
    W = {n: given[n] if given[n].ndim == 2 else given[n][0] for n in WEIGHTS}
    Mo = {n: given['m_' + n].reshape(W[n].shape) for n in WEIGHTS}
    Vo = {n: given['v_' + n].reshape(W[n].shape) for n in WEIGHTS}
    xs, mems, tgt = x[0], mem[0], loss_target[0]
    T, D = xs.shape
    RW = W['rwkv_w0'].shape[-1]
    GW = W['gdn_conv_w'].shape[-1] * 4 // 3
    NH = W['gdn_a_log'].shape[-1]
    LW, LA, LG = W['rwkv_w_up'].shape[0], W['rwkv_a_up'].shape[0], W['rwkv_g_up'].shape[0]
    XW = W['xa_w_q'].shape[-1]
    assert LW <= LANES and LA <= LANES and LG % LANES == 0 and 2 * NH <= LANES

    del _PENDING[:]
    me = 2 * lax.axis_index("x") + lax.axis_index("y")
    small_sizes = [W[n].size for n in SMALL_SHARDED]
    small_len = -(-sum(small_sizes) // 1024) * 1024

    def pack_small(vals):
        flat = jnp.concatenate([vv.reshape(-1).astype(F32) for vv in vals])
        return jnp.pad(flat, (0, small_len - flat.shape[0])).reshape(small_len // 1024, 1024)

    def unpack_small(packed):
        flat, out, off = packed.reshape(-1), [], 0
        for n, sz in zip(SMALL_SHARDED, small_sizes):
            out.append(flat[off:off + sz].reshape(W[n].shape))
            off += sz
        return out

    gstarted = {}

    def gstart(n, after=None):
        local = pack_small([W[s] for s in SMALL_SHARDED]) if n == 'small' else W[n].astype(BF16)
        if after is not None:
            local, _ = lax.optimization_barrier((local, after))
        mode = 'gather_half' if n in HALVED else 'gather'
        gstarted[n] = _send3_start("gather_start_" + n, local, (4,) + local.shape, mode)
        _PENDING.append(gstarted[n][4])

    HALVED = ('w_in', 'mlp_w_up', 'mlp_w_down')
    gstart('w_in')
    gstart('small')

    def gathered(n, after):
        src, land = _send3_wait("gather_wait_" + n, gstarted[n], after, halves=n in HALVED)
        if n in HALVED:
            fwd = _send3_start("gather_forward_" + n, src, land, 'forward_half')
            src, land = _send3_wait("gather_forward_wait_" + n, fwd, fwd[4], halves=True)
        return lax.dynamic_update_index_in_dim(land, src, me, 0)

    def gathered_full(n, after):
        g = gathered(n, after)
        if n == 'w_in':
            return jnp.concatenate([g[0], g[1], g[2], g[3]], axis=1)
        if n in COL_SHARDED:
            return g
        return g.reshape((4 * g.shape[1],) + g.shape[2:])

    RC = 3 * RW + LW + LA + LG
    o_z_raw = RC + 3 * GW
    o_b_raw = o_z_raw + GW
    o_grw_raw = o_b_raw + 2 * NH
    O_GRW = 0
    O_GGDN = D
    O_R = 2 * D
    O_GQ = O_R + 3 * RW
    O_Z = O_GQ + 3 * GW
    O_LW = O_Z + GW
    O_LA = O_LW + LANES
    O_LG = O_LA + LANES
    O_BA = O_LG + LG
    END = O_BA + LANES
    NP = -(-END // 1024) * 1024
    XM = 3 * RW + 2 * LANES + LG
    assert O_R % RW == 0 and O_GQ % GW == 0 and O_Z % GW == 0 and O_LG % LG == 0 and (3 * RW + 2 * LANES) % LG == 0

    def pad_layout(raw, lead):
        return jnp.concatenate([
            raw[:, o_grw_raw:o_grw_raw + 2 * D], raw[:, :3 * RW], raw[:, RC:RC + 3 * GW], raw[:, o_z_raw:o_z_raw + GW],
            _pad_cols(raw[:, 3 * RW:3 * RW + LW], LANES), _pad_cols(raw[:, 3 * RW + LW:3 * RW + LW + LA], LANES),
            raw[:, 3 * RW + LW + LA:RC], _pad_cols(raw[:, o_b_raw:o_b_raw + 2 * NH], LANES),
            jnp.zeros((lead, NP - END), raw.dtype)], axis=1)

    def unpad_layout(pd):
        return jnp.concatenate([
            pd[:, O_R:O_R + 3 * RW], pd[:, O_LW:O_LW + LW], pd[:, O_LA:O_LA + LA], pd[:, O_LG:O_LG + LG],
            pd[:, O_GQ:O_GQ + 3 * GW], pd[:, O_Z:O_Z + GW], pd[:, O_BA:O_BA + 2 * NH], pd[:, O_GRW:O_GRW + 2 * D]], axis=1)

    def xm_layout(raw):
        return jnp.concatenate([raw[:, :3 * RW], _pad_cols(raw[:, 3 * RW:3 * RW + LW], LANES),
                                _pad_cols(raw[:, 3 * RW + LW:3 * RW + LW + LA], LANES), raw[:, 3 * RW + LW + LA:]], axis=1)

    def xm_unlayout(pd):
        return jnp.concatenate([pd[:, :3 * RW], pd[:, 3 * RW:3 * RW + LW], pd[:, 3 * RW + LANES:3 * RW + LANES + LA],
                                pd[:, 3 * RW + 2 * LANES:]], axis=1)

    mix_x = xm_layout(W['rwkv_shift_mix'])
    r_k = W['rwkv_r_k'].reshape(1, RW)
    alog_p = jnp.pad(W['gdn_a_log'], ((0, 0), (NH, LANES - 2 * NH)))
    dt_p = jnp.pad(W['gdn_dt_bias'], ((0, 0), (NH, LANES - 2 * NH)))
    nw_t = jnp.tile(W['gdn_norm_w'], (1, GW // GDN_HEAD))
    n_main, main_cb, lora_cb = 3 * RW // LANES, O_R // LANES, O_LW // LANES
    f_gdn_pre = functools.partial(_f_gdn_pre, heads=NH)

    (u,) = _rows_fwd("rms_pre", _f_rms, [(xs, D, 0)], [W['mix_norm_pre']], [(D, BF16)], 256)
    full = {'w_in': gathered_full('w_in', u)}
    w_p = pad_layout(full['w_in'], D)
    small4 = gathered('small', w_p)
    for n in ['w_branch_rwkv', 'w_branch_gdn', 'w_mix_out', 'xa_w_q', 'xa_w_kv', 'xa_w_o', 'mlp_w_up']:
        gstart(n, small4)
    p = _mm("mm_in", u, w_p, out_dtype=BF16)
    small_parts = [unpack_small(small4[s]) for s in range(4)]
    for j, n in enumerate(SMALL_SHARDED):
        full[n] = jnp.concatenate([small_parts[s][j] for s in range(4)], axis=1)
    w_up_p = jnp.pad(full['rwkv_w_up'], ((0, LANES - LW), (0, 0)))
    a_up_p = jnp.pad(full['rwkv_a_up'], ((0, LANES - LA), (0, 0)))
    g_up = full['rwkv_g_up']
    conv8 = jnp.pad(full['gdn_conv_w'], ((0, 4), (0, 0)))
    rwkv_par = [W['rwkv_w0'], W['rwkv_a0'], W['rwkv_k_k'], W['rwkv_k_a'], w_up_p, a_up_p, g_up]
    xm = _tokshift_fwd(p, mix_x, n_main, main_cb, lora_cb)
    pre_rows = [(xm, RW, 0), (xm, RW, 1), (xm, RW, 2), (xm, LANES, n_main), (xm, LANES, n_main + 1),
                (xm, LG, (3 * RW + 2 * LANES) // LG)]
    lwd, k2, av, bv, gate = _rows_fwd("rwkv_pre", _f_rwkv_pre, pre_rows, rwkv_par,
                                      [None, (RW, F32), (RW, F32), None, (RW, F32), (RW, F32), (RW, F32)], 128)
    rec_rw = [(xm, 0), (lwd, 0), (k2, 0), (xm, 2 * RW // LANES), (av, 0), (bv, 0)]
    y_rec, s_rw = _rec_fwd("rwkv_rec_fwd", _rwkv_chunk, rec_rw, RW // LANES)
    post_rows = [(y_rec, RW, 0), (xm, RW, 0), (k2, RW, 0), (xm, RW, 2), (gate, RW, 0)]
    post_par = [r_k, W['rwkv_gn_w'], W['rwkv_gn_b']]
    (y_rw,) = _rows_fwd("rwkv_post", _f_rwkv_post, post_rows, post_par, [(RW, BF16)], 128)

    qkvc = _conv_fwd(p, conv8, O_GQ // LANES)
    gpre_rows = [(qkvc, GW, 0), (qkvc, GW, 1), (qkvc, GW, 2), (p, LANES, O_BA // LANES)]
    qn, kn, beta_x, g_x = _rows_fwd("gdn_pre", f_gdn_pre, gpre_rows, [alog_p, dt_p],
                                    [(GW, F32), (GW, F32), None, (GW, F32), (GW, F32)], 128)
    rec_gd = [(qn, 0), (kn, 0), (qkvc, 2 * GW // LANES), (beta_x, 0), (g_x, 0)]
    o_gd, s_gd = _rec_fwd("gdn_rec_fwd", _gdn_chunk, rec_gd, GW // LANES)
    gpost_rows = [(o_gd, GW, 0), (p, GW, O_Z // GW)]
    (y_gdn,) = _rows_fwd("gdn_post", _f_gdn_post, gpost_rows, [nw_t], [(GW, BF16)], 128)

    full['w_branch_rwkv'] = gathered_full('w_branch_rwkv', y_gdn)
    gstart('mlp_w_down', full['w_branch_rwkv'])
    br_rw = _mm("mm_br_rw", y_rw, full['w_branch_rwkv'], b_shards=True)
    full['w_branch_gdn'] = gathered_full('w_branch_gdn', y_gdn)
    br_gdn = _mm("mm_br_gdn", y_gdn, full['w_branch_gdn'], b_shards=True)
    merge_rows = [(p, D, O_GRW // D), (p, D, O_GGDN // D), (br_rw, D, 0), (br_gdn, D, 0)]
    (merged,) = _rows_fwd("merge", _f_merge, merge_rows, [], [(D, BF16)], 128)
    full['w_mix_out'] = gathered_full('w_mix_out', merged)
    ymix = _mm("mm_mix_out", merged, full['w_mix_out'])
    r1_par = [W['mix_norm_post'], W['xa_norm_pre']]
    h1, cq = _rows_fwd("resid1", _f_resid, [(xs, D, 0), (ymix, D, 0)], r1_par, [(D, F32), (D, BF16)], 128)
    full['xa_w_q'] = gathered_full('xa_w_q', cq)
    q = _mm("mm_xa_q", cq, full['xa_w_q'])
    (mn,) = _rows_fwd("mem_norm", _f_rms, [(mems, D, 0)], [W['xa_norm_mem']], [(D, BF16)], 128)
    full['xa_w_kv'] = gathered_full('xa_w_kv', mn)
    kv = _mm("mm_xa_kv", mn, full['xa_w_kv'])
    k_x, v_x = kv[:, :XW], kv[:, XW:]
    (o_x,) = _rows_fwd("xattn", _f_xattn, [(q, XW, 0)], [k_x, v_x], [(XW, BF16)], 256)
    full['xa_w_o'] = gathered_full('xa_w_o', o_x)
    xo = _mm("mm_xa_o", o_x, full['xa_w_o'], b_shards=True)
    r2_par = [W['xa_norm_post'], W['mlp_norm_pre']]
    h2, fm = _rows_fwd("resid2", _f_resid, [(h1, D, 0), (xo, D, 0)], r2_par, [(D, F32), (D, BF16)], 128)
    full['mlp_w_up'] = gathered_full('mlp_w_up', fm)
    up, act = _mm("mm_mlp_up", fm, full['mlp_w_up'], out_dtype=(F32, BF16), b_shards=True,
                  epilogue=lambda z: (z, jnp.square(jnp.maximum(z, 0.0))))
    DFF = up.shape[1]
    full['mlp_w_down'] = gathered_full('mlp_w_down', act)
    down = _mm("mm_mlp_down", act, full['mlp_w_down'])

    G = {}
    xstarted = {}

    def xstart(n, g4):
        own = lax.dynamic_index_in_dim(g4, me, 0, keepdims=False)
        xstarted[n] = (_send3_start("grad_start_" + n, g4, (3,) + g4.shape[1:], 'exchange'), own)
        _PENDING.append(xstarted[n][0][4])

    (dh2, ddown), (G['mlp_norm_post'],), (loss_acc,) = _rows_bwd(
        "final", _f_final, [(h2, D, 0), (down, D, 0), (tgt, D, 0)], [W['mlp_norm_post']], [[]], 128,
        [(0, 0), (1, 0), None], [D, D], loss=True, out_dtypes=[F32, BF16])
    (d_up,) = _mm("mm_d_act", ddown, full['mlp_w_down'], tb=True, out_dtype=(BF16,), extra=(up,),
                  epilogue=lambda z, upt: (z * (2.0 * jnp.maximum(upt, 0.0)),))
    dw_down = _mm("mm_dw_down", act, ddown, ta=True, out_dtype=BF16)
    xstart('mlp_w_down', dw_down.astype(BF16).reshape(4, DFF // 4, D))
    d_fm = _mm("mm_d_fm", d_up, full['mlp_w_up'], tb=True, b_shards=True)
    xstart('mlp_w_up', _mm("mm_dw_up", fm, d_up, ta=True, out_dtype=BF16, out_shards=True))
    (dh1, dxo), (G['xa_norm_post'], G['mlp_norm_pre']), _ = _rows_bwd(
        "resid2_bwd", _f_resid, [(h1, D, 0), (xo, D, 0)], r2_par, [[(dh2, D, 0)], [(d_fm, D, 0)]], 128,
        [(0, 0), (1, 0)], [D, D], out_dtypes=[F32, BF16])
    d_ox = _mm("mm_d_ox", dxo, full['xa_w_o'], tb=True, b_shards=True)
    xstart('xa_w_o', _mm("mm_dw_o", o_x, dxo, ta=True, out_dtype=BF16, out_shards=True))
    (dq,), (dk_x, dv_x), _ = _rows_bwd("xattn_bwd", _f_xattn, [(q, XW, 0)], [k_x, v_x], [[(d_ox, XW, 0)]], 128,
                                       [(0, 0)], [XW], out_dtypes=[BF16])
    d_cq = _mm("mm_d_cq", dq, full['xa_w_q'], tb=True)
    dw_q = _mm("mm_dw_q", cq, dq, ta=True, out_dtype=BF16)
    xstart('xa_w_q', dw_q.astype(BF16).reshape(4, D // 4, XW))
    dkv = jnp.concatenate([dk_x, dv_x], axis=1)
    d_mn = _mm("mm_d_mn", dkv, full['xa_w_kv'], tb=True)
    dw_kv = _mm("mm_dw_kv", mn, dkv, ta=True, out_dtype=BF16)
    xstart('xa_w_kv', dw_kv.astype(BF16).reshape(4, D // 4, 2 * XW))
    _, (G['xa_norm_mem'],), _ = _rows_bwd("mem_norm_bwd", _f_rms, [(mems, D, 0)], [W['xa_norm_mem']],
                                          [[(d_mn, D, 0)]], 128, [None], [])
    (dx_a, dymix), (G['mix_norm_post'], G['xa_norm_pre']), _ = _rows_bwd(
        "resid1_bwd", _f_resid, [(xs, D, 0), (ymix, D, 0)], r1_par, [[(dh1, D, 0)], [(d_cq, D, 0)]], 128,
        [(0, 0), (1, 0)], [D, D], out_dtypes=[F32, BF16])
    d_merged = _mm("mm_d_merged", dymix, full['w_mix_out'], tb=True)
    dw_out = _mm("mm_dw_out", merged, dymix, ta=True, out_dtype=BF16)
    xstart('w_mix_out', dw_out.astype(BF16).reshape(4, D // 4, D))
    dp = jnp.zeros((T, NP), BF16)
    (dp, d_brw, d_bgdn), _, _ = _rows_bwd(
        "merge_bwd", _f_merge, merge_rows, [], [[(d_merged, D, 0)]], 128, [(0, 0), (0, D), (1, 0), (2, 0)], [2 * D, D, D],
        out_dtypes=[BF16] * 3, alias=(0, dp, O_GRW // (2 * D)))
    d_yrw = _mm("mm_d_yrw", d_brw, full['w_branch_rwkv'], tb=True, b_shards=True)
    xstart('w_branch_rwkv', _mm("mm_dw_brw", y_rw, d_brw, ta=True, out_dtype=BF16, out_shards=True))
    d_ygdn = _mm("mm_d_ygdn", d_bgdn, full['w_branch_gdn'], tb=True, b_shards=True)
    xstart('w_branch_gdn', _mm("mm_dw_bgdn", y_gdn, d_bgdn, ta=True, out_dtype=BF16, out_shards=True))

    (d_y, d_r1, d_k2a, d_v1, d_gate), (d_rk, G['rwkv_gn_w'], G['rwkv_gn_b']), _ = _rows_bwd(
        "rwkv_post_bwd", _f_rwkv_post, post_rows, post_par, [[(d_yrw, RW, 0)]], 64,
        [(0, 0), (1, 0), (2, 0), (3, 0), (4, 0)], [RW] * 5)
    d_r2, d_lwd, d_k2b, d_v2, d_av, d_bv = _rec_bwd("rwkv_rec_bwd", _rwkv_chunk, rec_rw, d_y, s_rw, RW // LANES)
    pre_cts = [[(d_r1, RW, 0), (d_r2, RW, 0)], [(d_lwd, RW, 0)], [(d_k2a, RW, 0), (d_k2b, RW, 0)],
               [(d_v1, RW, 0), (d_v2, RW, 0)], [(d_av, RW, 0)], [(d_bv, RW, 0)], [(d_gate, RW, 0)]]
    pre_out = [(0, 0), (0, RW), (0, 2 * RW), (0, 3 * RW), (0, 3 * RW + LANES), (0, 3 * RW + 2 * LANES)]
    (d_xm,), pre_pg, _ = _rows_bwd("rwkv_pre_bwd", _f_rwkv_pre, pre_rows, rwkv_par, pre_cts, 64, pre_out, [XM])
    G['rwkv_w0'], G['rwkv_a0'], G['rwkv_k_k'], G['rwkv_k_a'], d_wup_p, d_aup_p, d_gup = pre_pg
    dp, d_mix_x = _tokshift_bwd(d_xm, p, mix_x, dp, n_main, main_cb, lora_cb)

    (d_ogd, dp), (d_nw_t,), _ = _rows_bwd("gdn_post_bwd", _f_gdn_post, gpost_rows, [nw_t], [[(d_ygdn, GW, 0)]], 64,
                                          [(0, 0), (1, 0)], [GW, GW], out_dtypes=[F32, BF16], alias=(1, dp, O_Z // GW))
    d_qn, d_kn, d_vg, d_betax, d_gx = _rec_bwd("gdn_rec_bwd", _gdn_chunk, rec_gd, d_ogd, s_gd, GW // LANES)
    gpre_cts = [[(d_qn, GW, 0)], [(d_kn, GW, 0)], [(d_vg, GW, 0)], [(d_betax, GW, 0)], [(d_gx, GW, 0)]]
    (d_qkvc, dp), (d_alog_p, d_dt_p), _ = _rows_bwd(
        "gdn_pre_bwd", f_gdn_pre, gpre_rows, [alog_p, dt_p], gpre_cts, 64,
        [(0, 0), (0, GW), (0, 2 * GW), (1, 0)], [3 * GW, LANES], out_dtypes=[F32, BF16], alias=(1, dp, O_BA // LANES))
    dp, d_conv8 = _conv_bwd(d_qkvc, p, conv8, dp, O_GQ // LANES)
    small_g = [_col_shards(d_wup_p[:LW]), _col_shards(d_aup_p[:LA]), _col_shards(d_gup), _col_shards(d_conv8[:4])]
    xstart('small', jnp.stack([pack_small([g[s] for g in small_g]) for s in range(4)]))
    dw_p = _mm("mm_dw_in", u, dp, ta=True, out_dtype=BF16)
    xstart('w_in', _col_shards(unpad_layout(dw_p.astype(BF16))))
    d_u = _mm("mm_d_u", dp, w_p, tb=True)
    (grad_x,), (G['mix_norm_pre'],), _ = _rows_bwd(
        "rms_pre_bwd", _f_rms, [(xs, D, 0)], [W['mix_norm_pre']], [[(d_u, D, 0)]], 128, [(0, 0)], [D],
        adds=[(0, (dx_a, D, 0))])

    G['rwkv_shift_mix'] = xm_unlayout(d_mix_x)
    G['rwkv_r_k'] = d_rk
    G['gdn_a_log'] = d_alog_p[:, NH:2 * NH]
    G['gdn_dt_bias'] = d_dt_p[:, NH:2 * NH]
    G['gdn_norm_w'] = jnp.sum(d_nw_t.reshape(GW // GDN_HEAD, GDN_HEAD), axis=0, keepdims=True)

    res = {}
    for n in xstarted:
        started, own = xstarted[n]
        _, land = _send3_wait("grad_wait_" + n, started, grad_x)
        part = _sum4("sum4_" + n, own, land)
        other = _swap_sibling("swap_" + n, part)
        if n == 'small':
            for sn, pa, pb in zip(SMALL_SHARDED, unpack_small(part), unpack_small(other)):
                res[sn] = _adam_big("adam_" + sn, W[sn], Mo[sn], Vo[sn], pa, pb)
        else:
            res[n] = _adam_big("adam_" + n, W[n], Mo[n], Vo[n], part, other)

    sizes = [-(-W[n].size // LANES) * LANES for n in REPLICATED]
    total = -(-sum(sizes) // (8 * LANES)) * (8 * LANES)

    def pack(vals):
        parts = [jnp.pad(vv.reshape(-1), (0, sz - vv.size)) for vv, sz in zip(vals, sizes)]
        flat = jnp.concatenate(parts)
        return jnp.pad(flat, (0, total - flat.shape[0])).reshape(8, total // 8)

    g8 = _gather8(pack([G[n] for n in REPLICATED]))
    small = _adam_small(pack([W[n] for n in REPLICATED]), pack([Mo[n] for n in REPLICATED]),
                        pack([Vo[n] for n in REPLICATED]), g8)
    small = [s.reshape(-1) for s in small]
    off = 0
    for n, sz in zip(REPLICATED, sizes):
        res[n] = [s[off:off + W[n].size].reshape(W[n].shape) for s in small]
        off += sz

    loss = lax.psum(loss_acc[0, 0], ("x", "y", "c"))
    outs = [loss, grad_x[None]]
    for j in range(4):
        outs += [res[n][j].reshape(given[n].shape) for n in WEIGHTS]
    return tuple(outs)
```
